```python
import math
import jax
import jax.numpy as jnp
from jax import lax
import numpy as np

D_MODEL = 1024
BATCH = 16
SEQ = 4096
DEPTH = 1
DEC_BATCH = 128
DEC_SEQ = 8
PAST_LEN = 8192
PAGE_SIZE = 128

HEAD_DIM = 64
A_HEADS = 8
A_KV_HEADS = 2
IDX_HEADS = 4
IDX_DIM = 64
A_TOPK = 256
B_HEADS = 8
B_KV_HEADS = 2
NSA_BLOCK = 64
NSA_TOPN = 16
NSA_FORCE = 8.0
CMP_HIDDEN = 64
WINDOW = 512
REL_BUCKETS = 32
REL_MAX_EXACT = 16
REL_MAX_DIST = 128
PEER_HEADS = 8
PEER_NKEYS = 128
PEER_EXPERTS = PEER_NKEYS * PEER_NKEYS
PEER_QDIM = 256
PEER_TOPK = 16
Q_BLOCK = 32
TOKEN_BLOCK = 256
LN_EPS = 1e-5
DN_ALPHA = (2 * DEPTH) ** 0.25
DN_BETA = (8 * DEPTH) ** -0.25

PROJ_LAYOUT = (
    ('a_q', A_HEADS * HEAD_DIM),
    ('a_kv', 2 * A_KV_HEADS * HEAD_DIM),
    ('a_iq', IDX_HEADS * IDX_DIM),
    ('a_iw', IDX_HEADS),
    ('a_ik', IDX_DIM),
    ('b_q', B_HEADS * HEAD_DIM),
    ('b_cmp_kv', 2 * B_KV_HEADS * HEAD_DIM),
    ('b_sel_kv', 2 * B_KV_HEADS * HEAD_DIM),
    ('b_win_kv', 2 * B_KV_HEADS * HEAD_DIM),
    ('b_gate', B_HEADS * 3),
    ('merge_gate', 2 * D_MODEL),
)
PROJ_WIDTH = sum(w for _, w in PROJ_LAYOUT)

kernel_name = 'hybrid_dsa_nsa_peer_step'


def layer_norm(x, g, b):
    xf = x.astype(jnp.float32)
    mu = jnp.mean(xf, axis=-1, keepdims=True)
    var = jnp.mean(jnp.square(xf - mu), axis=-1, keepdims=True)
    return ((xf - mu) * lax.rsqrt(var + LN_EPS) * g.astype(jnp.float32) + b.astype(jnp.float32)).astype(x.dtype)


def masked_softmax(logits, mask):
    l = jnp.where(mask, logits.astype(jnp.float32), -jnp.inf)
    m = jnp.max(l, axis=-1, keepdims=True)
    m = jnp.where(jnp.isfinite(m), m, 0.0)
    e = jnp.exp(l - m)
    s = jnp.sum(e, axis=-1, keepdims=True)
    return e / jnp.where(s > 0, s, 1.0)


def rel_bucket(d):
    n = jnp.maximum(d, 0)
    nf = jnp.maximum(n, 1).astype(jnp.float32)
    large = REL_MAX_EXACT + (jnp.log(nf / REL_MAX_EXACT) / math.log(REL_MAX_DIST / REL_MAX_EXACT)
                             * (REL_BUCKETS - REL_MAX_EXACT)).astype(jnp.int32)
    return jnp.where(n < REL_MAX_EXACT, n, jnp.minimum(large, REL_BUCKETS - 1))


def rel_bias(table, d):
    return table[rel_bucket(d)]


def project(x, w_in):
    B, T, _ = x.shape
    p = jnp.einsum('btd,dn->btn', x, w_in)
    parts = {}
    off = 0
    for name, width in PROJ_LAYOUT:
        parts[name] = p[..., off:off + width]
        off += width
    kv_a = (B, T, 2, A_KV_HEADS, HEAD_DIM)
    kv_b = (B, T, 2, B_KV_HEADS, HEAD_DIM)
    return {
        'a_q': parts['a_q'].reshape(B, T, A_HEADS, HEAD_DIM),
        'a_kv': parts['a_kv'].reshape(kv_a),
        'a_iq': parts['a_iq'].reshape(B, T, IDX_HEADS, IDX_DIM),
        'a_iw': parts['a_iw'] * IDX_HEADS ** -0.5,
        'a_ik': parts['a_ik'],
        'b_q': parts['b_q'].reshape(B, T, B_HEADS, HEAD_DIM),
        'b_cmp_kv': parts['b_cmp_kv'].reshape(kv_b),
        'b_sel_kv': parts['b_sel_kv'].reshape(kv_b),
        'b_win_kv': parts['b_win_kv'].reshape(kv_b),
        'b_gate': jax.nn.sigmoid(parts['b_gate']).reshape(B, T, B_HEADS, 3),
        'merge_gate': jax.nn.sigmoid(parts['merge_gate']).reshape(B, T, 2, D_MODEL),
    }


def dsa_attend(q, iq, iw, q_pos, ik_all, fetch_kv, table_a):
    B, T = q.shape[:2]
    R = A_HEADS // A_KV_HEADS
    L = ik_all.shape[1]
    k_sel = min(A_TOPK, L // 4)
    k_pos = jnp.arange(L, dtype=jnp.int32)
    rel = jax.nn.relu(jnp.einsum('bthd,bsd->bths', iq, ik_all) * IDX_DIM ** -0.5)
    score = jnp.einsum('bths,bth->bts', rel, iw).astype(jnp.float32)
    score = jnp.where((k_pos[None, :] <= q_pos[:, None])[None], score, -jnp.inf)
    _, sel = lax.top_k(score, k_sel)
    kv = fetch_kv(sel)
    qg = q.reshape(B, T, A_KV_HEADS, R, HEAD_DIM)
    logits = jnp.einsum('btgrd,btkgd->btgrk', qg, kv[..., 0, :, :]) * HEAD_DIM ** -0.5
    bias = jnp.moveaxis(rel_bias(table_a, q_pos[None, :, None] - sel), -1, 2).reshape(B, T, A_KV_HEADS, R, k_sel)
    valid = (sel <= q_pos[None, :, None])[:, :, None, None, :]
    p = masked_softmax(logits + bias, valid)
    o = jnp.einsum('btgrk,btkgd->btgrd', p.astype(kv.dtype), kv[..., 1, :, :])
    return o.reshape(B, T, A_HEADS * HEAD_DIM)


def nsa_compress(kv, pe, w1, w2):
    B, L = kv.shape[:2]
    nb = -(-L // NSA_BLOCK)
    kv = jnp.pad(kv, ((0, 0), (0, nb * NSA_BLOCK - L), (0, 0), (0, 0), (0, 0)))
    blocks = kv.reshape(B, nb, NSA_BLOCK, 2, B_KV_HEADS, HEAD_DIM) + pe[None, None, :, :, None, :]
    h = jax.nn.gelu(jnp.einsum('bnlcgd,lcde->bncge', blocks, w1))
    return jnp.einsum('bncge,cef->bncgf', h, w2)


def nsa_attend(q, gate, q_pos, cmp, fetch_sel, win_kv, win_pos, table_b):
    B, T = q.shape[:2]
    G = B_KV_HEADS
    R = B_HEADS // B_KV_HEADS
    nb = cmp.shape[1]
    scale = HEAD_DIM ** -0.5
    qg = q.reshape(B, T, G, R, HEAD_DIM)
    blk = jnp.arange(nb, dtype=jnp.int32)
    blk_end = (blk + 1) * NSA_BLOCK - 1
    lc = jnp.einsum('btgrd,bngd->btgrn', qg, cmp[:, :, 0]) * scale
    bias_c = jnp.moveaxis(rel_bias(table_b, q_pos[:, None] - blk_end[None, :]), -1, 1).reshape(T, G, R, nb)
    p_cmp = masked_softmax(lc + bias_c[None], (blk_end[None, :] <= q_pos[:, None])[None, :, None, None, :])
    o_cmp = jnp.einsum('btgrn,bngd->btgrd', p_cmp.astype(q.dtype), cmp[:, :, 1])
    importance = jnp.sum(p_cmp, axis=3)
    cur = q_pos // NSA_BLOCK
    eligible = blk[None, :] <= cur[:, None]
    forced = (blk[None, :] == 0) | (blk[None, :] == cur[:, None]) | (blk[None, :] == cur[:, None] - 1)
    score = jnp.where(forced[None, :, None, :], NSA_FORCE, importance)
    score = jnp.where(eligible[None, :, None, :], score, -1.0)
    n_sel = min(NSA_TOPN, nb)
    _, sel = lax.top_k(score, n_sel)
    pos = (sel[..., None] * NSA_BLOCK + jnp.arange(NSA_BLOCK, dtype=jnp.int32)).reshape(B, T, G, n_sel * NSA_BLOCK)
    kv = fetch_sel(pos)
    ls = jnp.einsum('btgrd,btgkd->btgrk', qg, kv[..., 0, :]) * scale
    tab = table_b.reshape(REL_BUCKETS, G, R)
    bias_s = tab[rel_bucket(q_pos[None, :, None, None] - pos), jnp.arange(G)[None, None, :, None]]
    bias_s = jnp.moveaxis(bias_s, -1, 3)
    p_slc = masked_softmax(ls + bias_s, (pos <= q_pos[None, :, None, None])[:, :, :, None, :])
    o_slc = jnp.einsum('btgrk,btgkd->btgrd', p_slc.astype(kv.dtype), kv[..., 1, :])
    Kw = win_kv.shape[1]
    lw = jnp.einsum('btgrd,bkgd->btgrk', qg, win_kv[:, :, 0]) * scale
    dw = q_pos[:, None] - win_pos[None, :]
    bias_w = jnp.moveaxis(rel_bias(table_b, dw), -1, 1).reshape(T, G, R, Kw)
    valid_w = (dw >= 0) & (dw < WINDOW) & (win_pos[None, :] >= 0)
    p_win = masked_softmax(lw + bias_w[None], valid_w[None, :, None, None, :])
    o_win = jnp.einsum('btgrk,bkgd->btgrd', p_win.astype(win_kv.dtype), win_kv[:, :, 1])
    g = gate.reshape(B, T, G, R, 3)
    o = g[..., 0:1] * o_cmp + g[..., 1:2] * o_slc + g[..., 2:3] * o_win
    return o.reshape(B, T, B_HEADS * HEAD_DIM)


def mix_prompt(pr, rel_table, cmp_pe, cmp_w1, cmp_w2):
    B, S = pr['a_q'].shape[:2]
    n_qb = S // Q_BLOCK
    a_kv, ik, sel_kv = pr['a_kv'], pr['a_ik'], pr['b_sel_kv']
    cmp = nsa_compress(pr['b_cmp_kv'], cmp_pe, cmp_w1, cmp_w2)
    win_pad = jnp.pad(pr['b_win_kv'], ((0, 0), (WINDOW, 0), (0, 0), (0, 0), (0, 0)))
    b_idx = jnp.arange(B)
    g_idx = jnp.arange(B_KV_HEADS)[None, None, :, None]
    table_a, table_b = rel_table[:, :A_HEADS], rel_table[:, A_HEADS:]

    def fetch_a(pos):
        return a_kv[b_idx[:, None, None], pos]

    def fetch_b(pos):
        return sel_kv[b_idx[:, None, None, None], pos, :, g_idx]

    def block(xs):
        q_a, iq, iw, q_b, gate, q0 = xs
        q_pos = q0 + jnp.arange(Q_BLOCK, dtype=jnp.int32)
        o_a = dsa_attend(q_a, iq, iw, q_pos, ik, fetch_a, table_a)
        win = lax.dynamic_slice_in_dim(win_pad, q0, WINDOW + Q_BLOCK, axis=1)
        win_pos = q0 - WINDOW + jnp.arange(WINDOW + Q_BLOCK, dtype=jnp.int32)
        o_b = nsa_attend(q_b, gate, q_pos, cmp, fetch_b, win, win_pos, table_b)
        return o_a, o_b

    def to_blocks(t):
        return t.reshape((B, n_qb, Q_BLOCK) + t.shape[2:]).swapaxes(0, 1)

    xs = (to_blocks(pr['a_q']), to_blocks(pr['a_iq']), to_blocks(pr['a_iw']),
          to_blocks(pr['b_q']), to_blocks(pr['b_gate']), jnp.arange(n_qb, dtype=jnp.int32) * Q_BLOCK)
    o_a, o_b = lax.map(block, xs)
    o_a = o_a.swapaxes(0, 1).reshape(B, S, A_HEADS * HEAD_DIM)
    o_b = o_b.swapaxes(0, 1).reshape(B, S, B_HEADS * HEAD_DIM)
    wb = min(WINDOW, S)
    new_state = (pr['a_kv'], pr['a_ik'], pr['b_cmp_kv'], pr['b_sel_kv'], pr['b_win_kv'][:, S - wb:])
    return o_a, o_b, new_state


def gather_pages(pool, page_table):
    rows = pool[page_table]
    return rows.reshape((page_table.shape[0], page_table.shape[1] * PAGE_SIZE) + pool.shape[2:])


def paged_fetch(pool, new_rows, page_table, grouped):
    DB = page_table.shape[0]
    past_len = page_table.shape[1] * PAGE_SIZE
    n_new = new_rows.shape[1]

    def fetch(pos):
        b = jnp.arange(DB).reshape((DB,) + (1,) * (pos.ndim - 1))
        in_past = pos < past_len
        pp = jnp.where(in_past, pos, 0)
        page = page_table[b, pp // PAGE_SIZE]
        off = pp % PAGE_SIZE
        jn = jnp.clip(pos - past_len, 0, n_new - 1)
        if grouped:
            g = jnp.arange(B_KV_HEADS)[None, None, :, None]
            old = pool[page, off, :, g]
            new = new_rows[b, jn, :, g]
        else:
            old = pool[page, off]
            new = new_rows[b, jn]
        mask = in_past.reshape(in_past.shape + (1,) * (old.ndim - pos.ndim))
        return jnp.where(mask, old, new)

    return fetch


def mix_sample(pr, cache_a_kv, cache_a_idx, cache_b_cmp_kv, cache_b_sel_kv, state_b_win_kv, page_table,
               rel_table, cmp_pe, cmp_w1, cmp_w2):
    T = pr['a_q'].shape[1]
    past_len = page_table.shape[1] * PAGE_SIZE
    q_pos = past_len + jnp.arange(T, dtype=jnp.int32)
    ik_all = jnp.concatenate([gather_pages(cache_a_idx, page_table), pr['a_ik']], axis=1)
    cmp_rows = jnp.concatenate([gather_pages(cache_b_cmp_kv, page_table), pr['b_cmp_kv']], axis=1)
    cmp = nsa_compress(cmp_rows, cmp_pe, cmp_w1, cmp_w2)
    fetch_a = paged_fetch(cache_a_kv, pr['a_kv'], page_table, grouped=False)
    fetch_b = paged_fetch(cache_b_sel_kv, pr['b_sel_kv'], page_table, grouped=True)
    win_all = jnp.concatenate([state_b_win_kv, pr['b_win_kv']], axis=1)
    wb = state_b_win_kv.shape[1]
    win_pos = past_len - wb + jnp.arange(wb + T, dtype=jnp.int32)
    o_a = dsa_attend(pr['a_q'], pr['a_iq'], pr['a_iw'], q_pos, ik_all, fetch_a, rel_table[:, :A_HEADS])
    o_b = nsa_attend(pr['b_q'], pr['b_gate'], q_pos, cmp, fetch_b, win_all, win_pos, rel_table[:, A_HEADS:])
    new_state = (pr['a_kv'], pr['a_ik'], pr['b_cmp_kv'], pr['b_sel_kv'], win_all[:, T:])
    return o_a, o_b, new_state


def peer(x, w_pq, sub_keys, u_table, v_table):
    B, T, D = x.shape
    flat = x.reshape(B * T, D)
    n = flat.shape[0]
    n_blk = -(-n // TOKEN_BLOCK)
    flat = jnp.pad(flat, ((0, n_blk * TOKEN_BLOCK - n), (0, 0))).reshape(n_blk, TOKEN_BLOCK, D)

    def block(xb):
        q = jnp.einsum('nd,dhc->nhc', xb, w_pq).reshape(TOKEN_BLOCK, PEER_HEADS, 2, PEER_QDIM // 2)
        s = jnp.einsum('nhpc,hpkc->nhpk', q, sub_keys).astype(jnp.float32)
        vals, idx = lax.top_k(s, PEER_TOPK)
        cand = (vals[..., 0, :, None] + vals[..., 1, None, :]).reshape(TOKEN_BLOCK, PEER_HEADS, PEER_TOPK * PEER_TOPK)
        cv, ci = lax.top_k(cand, PEER_TOPK)
        i1 = jnp.take_along_axis(idx[..., 0, :], ci // PEER_TOPK, axis=-1)
        i2 = jnp.take_along_axis(idx[..., 1, :], ci % PEER_TOPK, axis=-1)
        expert = i1 * PEER_NKEYS + i2
        g = jax.nn.softmax(cv, axis=-1)
        act = jax.nn.gelu(jnp.einsum('nd,nhkd->nhk', xb, u_table[expert]))
        return jnp.einsum('nhk,nhkd->nd', (g * act).astype(xb.dtype), v_table[expert])

    out = lax.map(block, flat).reshape(n_blk * TOKEN_BLOCK, D)[:n]
    return out.reshape(B, T, D)


def merge_and_channel_mix(x, merge_gate, o_a, o_b, w_up_a, w_up_b, w_out, ln1_g, ln1_b,
                          w_pq, sub_keys, u_table, v_table, ln2_g, ln2_b):
    y_a = jnp.einsum('btc,cd->btd', o_a, w_up_a)
    y_b = jnp.einsum('btc,cd->btd', o_b, w_up_b)
    mixed = jnp.einsum('btd,de->bte', merge_gate[:, :, 0] * y_a + merge_gate[:, :, 1] * y_b, w_out)
    h = layer_norm(DN_ALPHA * x + mixed, ln1_g, ln1_b)
    return layer_norm(DN_ALPHA * h + peer(h, w_pq, sub_keys, u_table, v_table), ln2_g, ln2_b)


def setup_inputs(seed: int = 0) -> dict:
    key = jax.random.key(seed)
    ks = jax.random.split(key, 26)
    f32 = jnp.float32
    n_pages = PAST_LEN // PAGE_SIZE
    n_used = DEC_BATCH * n_pages
    n_pool = (n_used * 5) // 4
    win_buf = min(WINDOW, PAST_LEN)

    def nrm(k, shape, scale=1.0):
        return jax.random.normal(k, shape, f32) * scale

    page_table = jax.random.permutation(ks[0], n_pool)[:n_used].reshape(DEC_BATCH, n_pages).astype(jnp.int32)
    return {
        'x_prompt': nrm(ks[1], (BATCH, SEQ, D_MODEL)),
        'x_sample': nrm(ks[2], (DEC_BATCH, DEC_SEQ, D_MODEL)),
        'cache_a_kv': nrm(ks[3], (DEPTH, n_pool, PAGE_SIZE, 2, A_KV_HEADS, HEAD_DIM)),
        'cache_a_idx': nrm(ks[4], (DEPTH, n_pool, PAGE_SIZE, IDX_DIM)),
        'cache_b_cmp_kv': nrm(ks[5], (DEPTH, n_pool, PAGE_SIZE, 2, B_KV_HEADS, HEAD_DIM)),
        'cache_b_sel_kv': nrm(ks[6], (DEPTH, n_pool, PAGE_SIZE, 2, B_KV_HEADS, HEAD_DIM)),
        'state_b_win_kv': nrm(ks[7], (DEPTH, DEC_BATCH, win_buf, 2, B_KV_HEADS, HEAD_DIM)),
        'page_table': page_table,
        'w_in': nrm(ks[8], (DEPTH, D_MODEL, PROJ_WIDTH), D_MODEL ** -0.5),
        'rel_bias_table': nrm(ks[9], (REL_BUCKETS, A_HEADS + B_HEADS), 0.5),
        'cmp_pe': nrm(ks[10], (DEPTH, NSA_BLOCK, 2, HEAD_DIM), 0.1),
        'cmp_w1': nrm(ks[11], (DEPTH, NSA_BLOCK, 2, HEAD_DIM, CMP_HIDDEN), (NSA_BLOCK * HEAD_DIM) ** -0.5),
        'cmp_w2': nrm(ks[12], (DEPTH, 2, CMP_HIDDEN, HEAD_DIM), CMP_HIDDEN ** -0.5),
        'w_up_a': nrm(ks[13], (DEPTH, A_HEADS * HEAD_DIM, D_MODEL), (A_HEADS * HEAD_DIM) ** -0.5),
        'w_up_b': nrm(ks[14], (DEPTH, B_HEADS * HEAD_DIM, D_MODEL), (B_HEADS * HEAD_DIM) ** -0.5),
        'w_out': nrm(ks[15], (DEPTH, D_MODEL, D_MODEL), DN_BETA * D_MODEL ** -0.5),
        'ln1_g': 1.0 + nrm(ks[16], (DEPTH, D_MODEL), 0.02),
        'ln1_b': nrm(ks[17], (DEPTH, D_MODEL), 0.02),
        'w_pq': nrm(ks[18], (DEPTH, D_MODEL, PEER_HEADS, PEER_QDIM), D_MODEL ** -0.5),
        'peer_sub_keys': nrm(ks[19], (DEPTH, PEER_HEADS, 2, PEER_NKEYS, PEER_QDIM // 2), (PEER_QDIM // 2) ** -0.5),
        'peer_u': nrm(ks[20], (DEPTH, PEER_EXPERTS, D_MODEL), D_MODEL ** -0.5),
        'peer_v': nrm(ks[21], (DEPTH, PEER_EXPERTS, D_MODEL), DN_BETA),
        'ln2_g': 1.0 + nrm(ks[22], (DEPTH, D_MODEL), 0.02),
        'ln2_b': nrm(ks[23], (DEPTH, D_MODEL), 0.02),
    }


def reference(x_prompt, x_sample, cache_a_kv, cache_a_idx, cache_b_cmp_kv, cache_b_sel_kv, state_b_win_kv,
              page_table, w_in, rel_bias_table, cmp_pe, cmp_w1, cmp_w2, w_up_a, w_up_b, w_out, ln1_g, ln1_b,
              w_pq, peer_sub_keys, peer_u, peer_v, ln2_g, ln2_b):
    xp, xs = x_prompt, x_sample
    st_p, st_s = [], []
    for l in range(DEPTH):
        pr = project(xp, w_in[l])
        o_a, o_b, new_p = mix_prompt(pr, rel_bias_table, cmp_pe[l], cmp_w1[l], cmp_w2[l])
        xp = merge_and_channel_mix(xp, pr['merge_gate'], o_a, o_b, w_up_a[l], w_up_b[l], w_out[l], ln1_g[l], ln1_b[l],
                                   w_pq[l], peer_sub_keys[l], peer_u[l], peer_v[l], ln2_g[l], ln2_b[l])
        ps = project(xs, w_in[l])
        o_a, o_b, new_s = mix_sample(ps, cache_a_kv[l], cache_a_idx[l], cache_b_cmp_kv[l], cache_b_sel_kv[l],
                                     state_b_win_kv[l], page_table, rel_bias_table, cmp_pe[l], cmp_w1[l], cmp_w2[l])
        xs = merge_and_channel_mix(xs, ps['merge_gate'], o_a, o_b, w_up_a[l], w_up_b[l], w_out[l], ln1_g[l], ln1_b[l],
                                   w_pq[l], peer_sub_keys[l], peer_u[l], peer_v[l], ln2_g[l], ln2_b[l])
        st_p.append(new_p)
        st_s.append(new_s)
    y_prompt, y_sample = xp, xs
    new_a_kv_p = jnp.stack([s[0] for s in st_p])
    new_a_idx_p = jnp.stack([s[1] for s in st_p])
    new_cmp_kv_p = jnp.stack([s[2] for s in st_p])
    new_sel_kv_p = jnp.stack([s[3] for s in st_p])
    new_win_kv_p = jnp.stack([s[4] for s in st_p])
    new_a_kv_s = jnp.stack([s[0] for s in st_s])
    new_a_idx_s = jnp.stack([s[1] for s in st_s])
    new_cmp_kv_s = jnp.stack([s[2] for s in st_s])
    new_sel_kv_s = jnp.stack([s[3] for s in st_s])
    new_win_kv_s = jnp.stack([s[4] for s in st_s])
    return (y_prompt, y_sample, new_a_kv_p, new_a_idx_p, new_cmp_kv_p, new_sel_kv_p, new_win_kv_p,
            new_a_kv_s, new_a_idx_s, new_cmp_kv_s, new_sel_kv_s, new_win_kv_s)
```

```python
import functools
import math

import jax
import jax.numpy as jnp
import numpy as np
from jax import lax
from jax.experimental import pallas as pl
from jax.experimental.pallas import tpu as pltpu

D_MODEL = 1024
DEPTH = 1
PAGE_SIZE = 128
HEAD_DIM = 64
A_HEADS = 8
A_KV_HEADS = 2
IDX_HEADS = 4
IDX_DIM = 64
A_TOPK = 256
B_HEADS = 8
B_KV_HEADS = 2
NSA_BLOCK = 64
NSA_TOPN = 16
NSA_FORCE = 8.0
CMP_HIDDEN = 64
WINDOW = 512
REL_BUCKETS = 32
REL_MAX_EXACT = 16
REL_MAX_DIST = 128
PEER_HEADS = 8
PEER_NKEYS = 128
PEER_QDIM = 256
PEER_TOPK = 16
Q_BLOCK = 32
TOKEN_BLOCK = 256
LN_EPS = 1e-5
DN_ALPHA = (2 * DEPTH) ** 0.25

PROJ_LAYOUT = (
    ('a_q', A_HEADS * HEAD_DIM),
    ('a_kv', 2 * A_KV_HEADS * HEAD_DIM),
    ('a_iq', IDX_HEADS * IDX_DIM),
    ('a_iw', IDX_HEADS),
    ('a_ik', IDX_DIM),
    ('b_q', B_HEADS * HEAD_DIM),
    ('b_cmp_kv', 2 * B_KV_HEADS * HEAD_DIM),
    ('b_sel_kv', 2 * B_KV_HEADS * HEAD_DIM),
    ('b_win_kv', 2 * B_KV_HEADS * HEAD_DIM),
    ('b_gate', B_HEADS * 3),
    ('merge_gate', 2 * D_MODEL),
)

LANES = 128
VMEM_LIMIT = 56 * 1024 * 1024
MISC_GATE_OFF = IDX_HEADS

F32 = jnp.float32
BF16 = jnp.bfloat16


_PROJ_OUT = ('a_q', 'a_kv', 'a_iq', 'a_ik', 'misc', 'b_q', 'b_cmp_kv', 'b_sel_kv', 'b_win_kv', 'merge_gate')


def _proj_kernel(x_ref, *refs):
    n = len(_PROJ_OUT)
    w_refs, o_refs = refs[:n], refs[n:]
    x = x_ref[...].astype(BF16)
    for name, w_ref, o_ref in zip(_PROJ_OUT, w_refs, o_refs):
        r = jnp.dot(x, w_ref[...], preferred_element_type=F32)
        if name == 'misc':
            lane = lax.broadcasted_iota(jnp.int32, r.shape, 1)
            r = jnp.where(lane < MISC_GATE_OFF, r * IDX_HEADS ** -0.5, jax.nn.sigmoid(r))
        elif name == 'merge_gate':
            r = jax.nn.sigmoid(r)
        o_ref[...] = r


def _split_w_in(w_in):
    parts = {}
    off = 0
    for name, width in PROJ_LAYOUT:
        parts[name] = w_in[:, off:off + width]
        off += width
    misc = jnp.concatenate([parts['a_iw'], parts['b_gate']], axis=1)
    parts['misc'] = jnp.pad(misc, ((0, 0), (0, LANES - misc.shape[1])))
    return [parts[name].astype(BF16) for name in _PROJ_OUT]


def _project(x2d, ws, tm):
    n = x2d.shape[0]
    widths = [w.shape[1] for w in ws]
    return pl.pallas_call(
        _proj_kernel,
        grid=(n // tm,),
        in_specs=[pl.BlockSpec((tm, D_MODEL), lambda i: (i, 0))]
        + [pl.BlockSpec((D_MODEL, wd), lambda i: (0, 0)) for wd in widths],
        out_specs=[pl.BlockSpec((tm, wd), lambda i: (i, 0)) for wd in widths],
        out_shape=[jax.ShapeDtypeStruct((n, wd), F32) for wd in widths],
        compiler_params=pltpu.CompilerParams(dimension_semantics=('arbitrary',), vmem_limit_bytes=VMEM_LIMIT),
        name='project',
    )(x2d, *ws)


def _unpack_proj(outs, B, T):
    d = dict(zip(_PROJ_OUT, outs))
    misc = d.pop('misc')
    kv = (B, T, 2, A_KV_HEADS, HEAD_DIM)
    return {
        'a_q': d['a_q'].reshape(B, T, A_HEADS, HEAD_DIM),
        'a_kv': d['a_kv'].reshape(kv),
        'a_iq': d['a_iq'].reshape(B, T, IDX_HEADS, IDX_DIM),
        'a_iw': misc[:, :IDX_HEADS].reshape(B, T, IDX_HEADS),
        'a_ik': d['a_ik'].reshape(B, T, IDX_DIM),
        'b_q': d['b_q'].reshape(B, T, B_HEADS, HEAD_DIM),
        'b_cmp_kv': d['b_cmp_kv'].reshape(kv),
        'b_sel_kv': d['b_sel_kv'].reshape(kv),
        'b_win_kv': d['b_win_kv'].reshape(kv),
        'b_gate': misc[:, MISC_GATE_OFF:MISC_GATE_OFF + 3 * B_HEADS].reshape(B, T, B_HEADS, 3),
        'merge_gate': d['merge_gate'],
    }


def _layer_norm(x, g, b):
    mu = jnp.mean(x, axis=-1, keepdims=True)
    xc = x - mu
    var = jnp.mean(xc * xc, axis=-1, keepdims=True)
    return xc * lax.rsqrt(var + LN_EPS) * g + b


def _merge_kernel(x_ref, oa_ref, ob_ref, mg_ref, wa_ref, wb_ref, wo_ref, g_ref, b_ref, h_ref):
    ya = jnp.dot(oa_ref[...].astype(BF16), wa_ref[...], preferred_element_type=F32)
    yb = jnp.dot(ob_ref[...].astype(BF16), wb_ref[...], preferred_element_type=F32)
    mg = mg_ref[...]
    mix = mg[:, :D_MODEL] * ya + mg[:, D_MODEL:] * yb
    mixed = jnp.dot(mix.astype(BF16), wo_ref[...], preferred_element_type=F32)
    h_ref[...] = _layer_norm(DN_ALPHA * x_ref[...] + mixed, g_ref[...], b_ref[...])


def _merge(x2d, o_a, o_b, mg, w_up_a, w_up_b, w_out, ln_g, ln_b, tm):
    n = x2d.shape[0]
    ca, cb = o_a.shape[1], o_b.shape[1]
    row = lambda i: (i, 0)
    fixed = lambda i: (0, 0)
    return pl.pallas_call(
        _merge_kernel,
        grid=(n // tm,),
        in_specs=[pl.BlockSpec((tm, D_MODEL), row), pl.BlockSpec((tm, ca), row), pl.BlockSpec((tm, cb), row),
                  pl.BlockSpec((tm, 2 * D_MODEL), row), pl.BlockSpec((ca, D_MODEL), fixed),
                  pl.BlockSpec((cb, D_MODEL), fixed), pl.BlockSpec((D_MODEL, D_MODEL), fixed),
                  pl.BlockSpec((1, D_MODEL), fixed), pl.BlockSpec((1, D_MODEL), fixed)],
        out_specs=pl.BlockSpec((tm, D_MODEL), row),
        out_shape=jax.ShapeDtypeStruct((n, D_MODEL), F32),
        compiler_params=pltpu.CompilerParams(dimension_semantics=('arbitrary',), vmem_limit_bytes=VMEM_LIMIT),
        name='merge',
    )(x2d, o_a, o_b, mg, w_up_a.astype(BF16), w_up_b.astype(BF16), w_out.astype(BF16),
      ln_g.reshape(1, D_MODEL), ln_b.reshape(1, D_MODEL))


def _masked_softmax(logits, mask):
    l = jnp.where(mask, logits.astype(F32), -jnp.inf)
    m = jnp.max(l, axis=-1, keepdims=True)
    m = jnp.where(jnp.isfinite(m), m, 0.0)
    e = jnp.exp(l - m)
    s = jnp.sum(e, axis=-1, keepdims=True)
    return e / jnp.where(s > 0, s, 1.0)


def _rel_bucket(d):
    n = jnp.maximum(d, 0)
    nf = jnp.maximum(n, 1).astype(F32)
    large = REL_MAX_EXACT + (jnp.log(nf / REL_MAX_EXACT) / math.log(REL_MAX_DIST / REL_MAX_EXACT)
                             * (REL_BUCKETS - REL_MAX_EXACT)).astype(jnp.int32)
    return jnp.where(n < REL_MAX_EXACT, n, jnp.minimum(large, REL_BUCKETS - 1))


def _rel_bias(table, d):
    return table[_rel_bucket(d)]


def _dsa_attend(q, iq, iw, q_pos, ik_all, fetch_kv, table_a):
    B, T = q.shape[:2]
    R = A_HEADS // A_KV_HEADS
    L = ik_all.shape[1]
    k_sel = min(A_TOPK, L // 4)
    k_pos = jnp.arange(L, dtype=jnp.int32)
    rel = jax.nn.relu(jnp.einsum('bthd,bsd->bths', iq, ik_all) * IDX_DIM ** -0.5)
    score = jnp.einsum('bths,bth->bts', rel, iw).astype(F32)
    score = jnp.where((k_pos[None, :] <= q_pos[:, None])[None], score, -jnp.inf)
    _, sel = lax.top_k(score, k_sel)
    kv = fetch_kv(sel)
    qg = q.reshape(B, T, A_KV_HEADS, R, HEAD_DIM)
    logits = jnp.einsum('btgrd,btkgd->btgrk', qg, kv[..., 0, :, :]) * HEAD_DIM ** -0.5
    bias = jnp.moveaxis(_rel_bias(table_a, q_pos[None, :, None] - sel), -1, 2).reshape(B, T, A_KV_HEADS, R, k_sel)
    valid = (sel <= q_pos[None, :, None])[:, :, None, None, :]
    p = _masked_softmax(logits + bias, valid)
    o = jnp.einsum('btgrk,btkgd->btgrd', p.astype(kv.dtype), kv[..., 1, :, :])
    return o.reshape(B, T, A_HEADS * HEAD_DIM)


def _nsa_compress(kv, pe, w1, w2):
    B, L = kv.shape[:2]
    nb = -(-L // NSA_BLOCK)
    kv = jnp.pad(kv, ((0, 0), (0, nb * NSA_BLOCK - L), (0, 0), (0, 0), (0, 0)))
    blocks = kv.reshape(B, nb, NSA_BLOCK, 2, B_KV_HEADS, HEAD_DIM) + pe[None, None, :, :, None, :]
    h = jax.nn.gelu(jnp.einsum('bnlcgd,lcde->bncge', blocks, w1))
    return jnp.einsum('bncge,cef->bncgf', h, w2)


def _nsa_attend(q, gate, q_pos, cmp, fetch_sel, win_kv, win_pos, table_b):
    B, T = q.shape[:2]
    G = B_KV_HEADS
    R = B_HEADS // B_KV_HEADS
    nb = cmp.shape[1]
    scale = HEAD_DIM ** -0.5
    qg = q.reshape(B, T, G, R, HEAD_DIM)
    blk = jnp.arange(nb, dtype=jnp.int32)
    blk_end = (blk + 1) * NSA_BLOCK - 1
    lc = jnp.einsum('btgrd,bngd->btgrn', qg, cmp[:, :, 0]) * scale
    bias_c = jnp.moveaxis(_rel_bias(table_b, q_pos[:, None] - blk_end[None, :]), -1, 1).reshape(T, G, R, nb)
    p_cmp = _masked_softmax(lc + bias_c[None], (blk_end[None, :] <= q_pos[:, None])[None, :, None, None, :])
    o_cmp = jnp.einsum('btgrn,bngd->btgrd', p_cmp.astype(q.dtype), cmp[:, :, 1])
    importance = jnp.sum(p_cmp, axis=3)
    cur = q_pos // NSA_BLOCK
    eligible = blk[None, :] <= cur[:, None]
    forced = (blk[None, :] == 0) | (blk[None, :] == cur[:, None]) | (blk[None, :] == cur[:, None] - 1)
    score = jnp.where(forced[None, :, None, :], NSA_FORCE, importance)
    score = jnp.where(eligible[None, :, None, :], score, -1.0)
    n_sel = min(NSA_TOPN, nb)
    _, sel = lax.top_k(score, n_sel)
    pos = (sel[..., None] * NSA_BLOCK + jnp.arange(NSA_BLOCK, dtype=jnp.int32)).reshape(B, T, G, n_sel * NSA_BLOCK)
    kv = fetch_sel(pos)
    ls = jnp.einsum('btgrd,btgkd->btgrk', qg, kv[..., 0, :]) * scale
    tab = table_b.reshape(REL_BUCKETS, G, R)
    bias_s = tab[_rel_bucket(q_pos[None, :, None, None] - pos), jnp.arange(G)[None, None, :, None]]
    bias_s = jnp.moveaxis(bias_s, -1, 3)
    p_slc = _masked_softmax(ls + bias_s, (pos <= q_pos[None, :, None, None])[:, :, :, None, :])
    o_slc = jnp.einsum('btgrk,btgkd->btgrd', p_slc.astype(kv.dtype), kv[..., 1, :])
    Kw = win_kv.shape[1]
    lw = jnp.einsum('btgrd,bkgd->btgrk', qg, win_kv[:, :, 0]) * scale
    dw = q_pos[:, None] - win_pos[None, :]
    bias_w = jnp.moveaxis(_rel_bias(table_b, dw), -1, 1).reshape(T, G, R, Kw)
    valid_w = (dw >= 0) & (dw < WINDOW) & (win_pos[None, :] >= 0)
    p_win = _masked_softmax(lw + bias_w[None], valid_w[None, :, None, None, :])
    o_win = jnp.einsum('btgrk,bkgd->btgrd', p_win.astype(win_kv.dtype), win_kv[:, :, 1])
    g = gate.reshape(B, T, G, R, 3)
    o = g[..., 0:1] * o_cmp + g[..., 1:2] * o_slc + g[..., 2:3] * o_win
    return o.reshape(B, T, B_HEADS * HEAD_DIM)


def _mix_prompt(pr, rel_table, cmp_pe, cmp_w1, cmp_w2):
    B, S = pr['a_q'].shape[:2]
    n_qb = S // Q_BLOCK
    a_kv, ik, sel_kv = pr['a_kv'], pr['a_ik'], pr['b_sel_kv']
    cmp = _nsa_compress(pr['b_cmp_kv'], cmp_pe, cmp_w1, cmp_w2)
    win_pad = jnp.pad(pr['b_win_kv'], ((0, 0), (WINDOW, 0), (0, 0), (0, 0), (0, 0)))
    b_idx = jnp.arange(B)
    g_idx = jnp.arange(B_KV_HEADS)[None, None, :, None]
    table_a, table_b = rel_table[:, :A_HEADS], rel_table[:, A_HEADS:]

    def fetch_a(pos):
        return a_kv[b_idx[:, None, None], pos]

    def fetch_b(pos):
        return sel_kv[b_idx[:, None, None, None], pos, :, g_idx]

    def block(xs):
        q_a, iq, iw, q_b, gate, q0 = xs
        q_pos = q0 + jnp.arange(Q_BLOCK, dtype=jnp.int32)
        o_a = _dsa_attend(q_a, iq, iw, q_pos, ik, fetch_a, table_a)
        win = lax.dynamic_slice_in_dim(win_pad, q0, WINDOW + Q_BLOCK, axis=1)
        win_pos = q0 - WINDOW + jnp.arange(WINDOW + Q_BLOCK, dtype=jnp.int32)
        o_b = _nsa_attend(q_b, gate, q_pos, cmp, fetch_b, win, win_pos, table_b)
        return o_a, o_b

    def to_blocks(t):
        return t.reshape((B, n_qb, Q_BLOCK) + t.shape[2:]).swapaxes(0, 1)

    xs = (to_blocks(pr['a_q']), to_blocks(pr['a_iq']), to_blocks(pr['a_iw']),
          to_blocks(pr['b_q']), to_blocks(pr['b_gate']), jnp.arange(n_qb, dtype=jnp.int32) * Q_BLOCK)
    o_a, o_b = lax.map(block, xs)
    o_a = o_a.swapaxes(0, 1).reshape(B, S, A_HEADS * HEAD_DIM)
    o_b = o_b.swapaxes(0, 1).reshape(B, S, B_HEADS * HEAD_DIM)
    wb = min(WINDOW, S)
    new_state = (pr['a_kv'], pr['a_ik'], pr['b_cmp_kv'], pr['b_sel_kv'], pr['b_win_kv'][:, S - wb:])
    return o_a, o_b, new_state


def _gather_pages(pool, page_table):
    rows = pool[page_table]
    return rows.reshape((page_table.shape[0], page_table.shape[1] * PAGE_SIZE) + pool.shape[2:])


def _paged_fetch(pool, new_rows, page_table, grouped):
    DB = page_table.shape[0]
    past_len = page_table.shape[1] * PAGE_SIZE
    n_new = new_rows.shape[1]

    def fetch(pos):
        b = jnp.arange(DB).reshape((DB,) + (1,) * (pos.ndim - 1))
        in_past = pos < past_len
        pp = jnp.where(in_past, pos, 0)
        page = page_table[b, pp // PAGE_SIZE]
        off = pp % PAGE_SIZE
        jn = jnp.clip(pos - past_len, 0, n_new - 1)
        if grouped:
            g = jnp.arange(B_KV_HEADS)[None, None, :, None]
            old = pool[page, off, :, g]
            new = new_rows[b, jn, :, g]
        else:
            old = pool[page, off]
            new = new_rows[b, jn]
        mask = in_past.reshape(in_past.shape + (1,) * (old.ndim - pos.ndim))
        return jnp.where(mask, old, new)

    return fetch


def _mix_sample(pr, cache_a_kv, cache_a_idx, cache_b_cmp_kv, cache_b_sel_kv, state_b_win_kv, page_table,
                rel_table, cmp_pe, cmp_w1, cmp_w2):
    T = pr['a_q'].shape[1]
    past_len = page_table.shape[1] * PAGE_SIZE
    q_pos = past_len + jnp.arange(T, dtype=jnp.int32)
    ik_all = jnp.concatenate([_gather_pages(cache_a_idx, page_table), pr['a_ik']], axis=1)
    cmp_rows = jnp.concatenate([_gather_pages(cache_b_cmp_kv, page_table), pr['b_cmp_kv']], axis=1)
    cmp = _nsa_compress(cmp_rows, cmp_pe, cmp_w1, cmp_w2)
    fetch_a = _paged_fetch(cache_a_kv, pr['a_kv'], page_table, grouped=False)
    fetch_b = _paged_fetch(cache_b_sel_kv, pr['b_sel_kv'], page_table, grouped=True)
    win_all = jnp.concatenate([state_b_win_kv, pr['b_win_kv']], axis=1)
    wb = state_b_win_kv.shape[1]
    win_pos = past_len - wb + jnp.arange(wb + T, dtype=jnp.int32)
    o_a = _dsa_attend(pr['a_q'], pr['a_iq'], pr['a_iw'], q_pos, ik_all, fetch_a, rel_table[:, :A_HEADS])
    o_b = _nsa_attend(pr['b_q'], pr['b_gate'], q_pos, cmp, fetch_b, win_all, win_pos, rel_table[:, A_HEADS:])
    new_state = (pr['a_kv'], pr['a_ik'], pr['b_cmp_kv'], pr['b_sel_kv'], win_all[:, T:])
    return o_a, o_b, new_state


def _peer(x, w_pq, sub_keys, u_table, v_table):
    B, T, D = x.shape
    flat = x.reshape(B * T, D)
    n = flat.shape[0]
    n_blk = -(-n // TOKEN_BLOCK)
    flat = jnp.pad(flat, ((0, n_blk * TOKEN_BLOCK - n), (0, 0))).reshape(n_blk, TOKEN_BLOCK, D)

    def block(xb):
        q = jnp.einsum('nd,dhc->nhc', xb, w_pq).reshape(TOKEN_BLOCK, PEER_HEADS, 2, PEER_QDIM // 2)
        s = jnp.einsum('nhpc,hpkc->nhpk', q, sub_keys).astype(F32)
        vals, idx = lax.top_k(s, PEER_TOPK)
        cand = (vals[..., 0, :, None] + vals[..., 1, None, :]).reshape(TOKEN_BLOCK, PEER_HEADS, PEER_TOPK * PEER_TOPK)
        cv, ci = lax.top_k(cand, PEER_TOPK)
        i1 = jnp.take_along_axis(idx[..., 0, :], ci // PEER_TOPK, axis=-1)
        i2 = jnp.take_along_axis(idx[..., 1, :], ci % PEER_TOPK, axis=-1)
        expert = i1 * PEER_NKEYS + i2
        g = jax.nn.softmax(cv, axis=-1)
        act = jax.nn.gelu(jnp.einsum('nd,nhkd->nhk', xb, u_table[expert]))
        return jnp.einsum('nhk,nhkd->nd', (g * act).astype(xb.dtype), v_table[expert])

    out = lax.map(block, flat).reshape(n_blk * TOKEN_BLOCK, D)[:n]
    return out.reshape(B, T, D)


def _ln_ref(x, g, b):
    mu = jnp.mean(x, axis=-1, keepdims=True)
    var = jnp.mean(jnp.square(x - mu), axis=-1, keepdims=True)
    return (x - mu) * lax.rsqrt(var + LN_EPS) * g + b


def _group_step(x, mixer, ws, w_up_a, w_up_b, w_out, ln1_g, ln1_b, w_pq, sub_keys, peer_u, peer_v, ln2_g, ln2_b, tm):
    B, T, _ = x.shape
    x2d = x.reshape(B * T, D_MODEL)
    pr = _unpack_proj(_project(x2d, ws, tm), B, T)
    o_a, o_b, new_state = mixer(pr)
    h = _merge(x2d, o_a.reshape(B * T, -1), o_b.reshape(B * T, -1), pr['merge_gate'],
               w_up_a, w_up_b, w_out, ln1_g, ln1_b, tm).reshape(B, T, D_MODEL)
    y = _ln_ref(DN_ALPHA * h + _peer(h, w_pq, sub_keys, peer_u, peer_v), ln2_g, ln2_b)
    return y, new_state


def kernel(x_prompt, x_sample, cache_a_kv, cache_a_idx, cache_b_cmp_kv, cache_b_sel_kv, state_b_win_kv, page_table, w_in, rel_bias_table, cmp_pe, cmp_w1, cmp_w2, w_up_a, w_up_b, w_out, ln1_g, ln1_b, w_pq, peer_sub_keys, peer_u, peer_v, ln2_g, ln2_b):
    l = 0
    ws = _split_w_in(w_in[l])
    tail = (w_up_a[l], w_up_b[l], w_out[l], ln1_g[l], ln1_b[l], w_pq[l], peer_sub_keys[l], peer_u[l], peer_v[l],
            ln2_g[l], ln2_b[l])
    mix_p = functools.partial(_mix_prompt, rel_table=rel_bias_table, cmp_pe=cmp_pe[l], cmp_w1=cmp_w1[l],
                              cmp_w2=cmp_w2[l])
    y_p, st_p = _group_step(x_prompt, mix_p, ws, *tail, tm=256)
    mix_s = functools.partial(_mix_sample, cache_a_kv=cache_a_kv[l], cache_a_idx=cache_a_idx[l],
                              cache_b_cmp_kv=cache_b_cmp_kv[l], cache_b_sel_kv=cache_b_sel_kv[l],
                              state_b_win_kv=state_b_win_kv[l], page_table=page_table, rel_table=rel_bias_table,
                              cmp_pe=cmp_pe[l], cmp_w1=cmp_w1[l], cmp_w2=cmp_w2[l])
    y_s, st_s = _group_step(x_sample, mix_s, ws, *tail, tm=256)
    return (y_p, y_s) + tuple(s[None] for s in st_p) + tuple(s[None] for s in st_s)
```

```python
import functools
import math

import jax
import jax.numpy as jnp
import numpy as np
from jax import lax
from jax.experimental import pallas as pl
from jax.experimental.pallas import tpu as pltpu

D_MODEL = 1024
DEPTH = 1
PAGE_SIZE = 128
HEAD_DIM = 64
A_HEADS = 8
A_KV_HEADS = 2
IDX_HEADS = 4
IDX_DIM = 64
A_TOPK = 256
B_HEADS = 8
B_KV_HEADS = 2
NSA_BLOCK = 64
NSA_TOPN = 16
NSA_FORCE = 8.0
CMP_HIDDEN = 64
WINDOW = 512
REL_BUCKETS = 32
REL_MAX_EXACT = 16
REL_MAX_DIST = 128
PEER_HEADS = 8
PEER_NKEYS = 128
PEER_QDIM = 256
PEER_TOPK = 16
Q_BLOCK = 32
TOKEN_BLOCK = 256
LN_EPS = 1e-5
DN_ALPHA = (2 * DEPTH) ** 0.25

PROJ_LAYOUT = (
    ('a_q', A_HEADS * HEAD_DIM),
    ('a_kv', 2 * A_KV_HEADS * HEAD_DIM),
    ('a_iq', IDX_HEADS * IDX_DIM),
    ('a_iw', IDX_HEADS),
    ('a_ik', IDX_DIM),
    ('b_q', B_HEADS * HEAD_DIM),
    ('b_cmp_kv', 2 * B_KV_HEADS * HEAD_DIM),
    ('b_sel_kv', 2 * B_KV_HEADS * HEAD_DIM),
    ('b_win_kv', 2 * B_KV_HEADS * HEAD_DIM),
    ('b_gate', B_HEADS * 3),
    ('merge_gate', 2 * D_MODEL),
)

LANES = 128
VMEM_LIMIT = 56 * 1024 * 1024
MISC_GATE_OFF = IDX_HEADS

F32 = jnp.float32
BF16 = jnp.bfloat16
MXU = jnp.bfloat16


_PROJ_OUT = ('a_q', 'a_kv', 'a_iq', 'a_ik', 'misc', 'b_q', 'b_cmp_kv', 'b_sel_kv', 'b_win_kv', 'merge_gate')


def _proj_kernel(x_ref, *refs):
    n = len(_PROJ_OUT)
    w_refs, o_refs = refs[:n], refs[n:]
    x = x_ref[...].astype(BF16)
    for name, w_ref, o_ref in zip(_PROJ_OUT, w_refs, o_refs):
        r = jnp.dot(x, w_ref[...], preferred_element_type=F32)
        if name == 'misc':
            lane = lax.broadcasted_iota(jnp.int32, r.shape, 1)
            r = jnp.where(lane < MISC_GATE_OFF, r * IDX_HEADS ** -0.5, jax.nn.sigmoid(r))
        elif name == 'merge_gate':
            r = jax.nn.sigmoid(r)
        o_ref[...] = r


def _split_w_in(w_in):
    parts = {}
    off = 0
    for name, width in PROJ_LAYOUT:
        parts[name] = w_in[:, off:off + width]
        off += width
    misc = jnp.concatenate([parts['a_iw'], parts['b_gate']], axis=1)
    parts['misc'] = jnp.pad(misc, ((0, 0), (0, LANES - misc.shape[1])))
    return [parts[name].astype(BF16) for name in _PROJ_OUT]


def _project(x2d, ws, tm):
    n = x2d.shape[0]
    widths = [w.shape[1] for w in ws]
    return pl.pallas_call(
        _proj_kernel,
        grid=(n // tm,),
        in_specs=[pl.BlockSpec((tm, D_MODEL), lambda i: (i, 0))]
        + [pl.BlockSpec((D_MODEL, wd), lambda i: (0, 0)) for wd in widths],
        out_specs=[pl.BlockSpec((tm, wd), lambda i: (i, 0)) for wd in widths],
        out_shape=[jax.ShapeDtypeStruct((n, wd), F32) for wd in widths],
        compiler_params=pltpu.CompilerParams(dimension_semantics=('arbitrary',), vmem_limit_bytes=VMEM_LIMIT),
        name='project',
    )(x2d, *ws)


def _unpack_proj(outs, B, T):
    d = dict(zip(_PROJ_OUT, outs))
    misc = d['misc']
    kv = (B, T, 2, A_KV_HEADS, HEAD_DIM)
    return {
        'a_q': d['a_q'].reshape(B, T, A_HEADS, HEAD_DIM),
        'a_kv': d['a_kv'].reshape(kv),
        'a_iq': d['a_iq'].reshape(B, T, IDX_HEADS, IDX_DIM),
        'a_iw': misc[:, :IDX_HEADS].reshape(B, T, IDX_HEADS),
        'a_ik': d['a_ik'].reshape(B, T, IDX_DIM),
        'b_q': d['b_q'].reshape(B, T, B_HEADS, HEAD_DIM),
        'b_cmp_kv': d['b_cmp_kv'].reshape(kv),
        'b_sel_kv': d['b_sel_kv'].reshape(kv),
        'b_win_kv': d['b_win_kv'].reshape(kv),
        'b_gate': misc[:, MISC_GATE_OFF:MISC_GATE_OFF + 3 * B_HEADS].reshape(B, T, B_HEADS, 3),
        'merge_gate': d['merge_gate'],
        'raw': d,
    }


def _layer_norm(x, g, b):
    mu = jnp.mean(x, axis=-1, keepdims=True)
    xc = x - mu
    var = jnp.mean(xc * xc, axis=-1, keepdims=True)
    return xc * lax.rsqrt(var + LN_EPS) * g + b


def _merge_kernel(x_ref, oa_ref, ob_ref, mg_ref, wa_ref, wb_ref, wo_ref, g_ref, b_ref, h_ref):
    ya = jnp.dot(oa_ref[...].astype(BF16), wa_ref[...], preferred_element_type=F32)
    yb = jnp.dot(ob_ref[...].astype(BF16), wb_ref[...], preferred_element_type=F32)
    mg = mg_ref[...]
    mix = mg[:, :D_MODEL] * ya + mg[:, D_MODEL:] * yb
    mixed = jnp.dot(mix.astype(BF16), wo_ref[...], preferred_element_type=F32)
    h_ref[...] = _layer_norm(DN_ALPHA * x_ref[...] + mixed, g_ref[...], b_ref[...])


def _merge(x2d, o_a, o_b, mg, w_up_a, w_up_b, w_out, ln_g, ln_b, tm):
    n = x2d.shape[0]
    ca, cb = o_a.shape[1], o_b.shape[1]
    row = lambda i: (i, 0)
    fixed = lambda i: (0, 0)
    return pl.pallas_call(
        _merge_kernel,
        grid=(n // tm,),
        in_specs=[pl.BlockSpec((tm, D_MODEL), row), pl.BlockSpec((tm, ca), row), pl.BlockSpec((tm, cb), row),
                  pl.BlockSpec((tm, 2 * D_MODEL), row), pl.BlockSpec((ca, D_MODEL), fixed),
                  pl.BlockSpec((cb, D_MODEL), fixed), pl.BlockSpec((D_MODEL, D_MODEL), fixed),
                  pl.BlockSpec((1, D_MODEL), fixed), pl.BlockSpec((1, D_MODEL), fixed)],
        out_specs=pl.BlockSpec((tm, D_MODEL), row),
        out_shape=jax.ShapeDtypeStruct((n, D_MODEL), F32),
        compiler_params=pltpu.CompilerParams(dimension_semantics=('arbitrary',), vmem_limit_bytes=VMEM_LIMIT),
        name='merge',
    )(x2d, o_a, o_b, mg, w_up_a.astype(BF16), w_up_b.astype(BF16), w_out.astype(BF16),
      ln_g.reshape(1, D_MODEL), ln_b.reshape(1, D_MODEL))


def _masked_softmax(logits, mask):
    l = jnp.where(mask, logits.astype(F32), -jnp.inf)
    m = jnp.max(l, axis=-1, keepdims=True)
    m = jnp.where(jnp.isfinite(m), m, 0.0)
    e = jnp.exp(l - m)
    s = jnp.sum(e, axis=-1, keepdims=True)
    return e / jnp.where(s > 0, s, 1.0)


def _rel_bucket(d):
    n = jnp.maximum(d, 0)
    nf = jnp.maximum(n, 1).astype(F32)
    large = REL_MAX_EXACT + (jnp.log(nf / REL_MAX_EXACT) / math.log(REL_MAX_DIST / REL_MAX_EXACT)
                             * (REL_BUCKETS - REL_MAX_EXACT)).astype(jnp.int32)
    return jnp.where(n < REL_MAX_EXACT, n, jnp.minimum(large, REL_BUCKETS - 1))


def _rel_bias(table, d):
    return table[_rel_bucket(d)]


def _dsa_attend(q, iq, iw, q_pos, ik_all, fetch_kv, table_a):
    B, T = q.shape[:2]
    R = A_HEADS // A_KV_HEADS
    L = ik_all.shape[1]
    k_sel = min(A_TOPK, L // 4)
    k_pos = jnp.arange(L, dtype=jnp.int32)
    rel = jax.nn.relu(jnp.einsum('bthd,bsd->bths', iq, ik_all) * IDX_DIM ** -0.5)
    score = jnp.einsum('bths,bth->bts', rel, iw).astype(F32)
    score = jnp.where((k_pos[None, :] <= q_pos[:, None])[None], score, -jnp.inf)
    _, sel = lax.top_k(score, k_sel)
    kv = fetch_kv(sel)
    qg = q.reshape(B, T, A_KV_HEADS, R, HEAD_DIM)
    logits = jnp.einsum('btgrd,btkgd->btgrk', qg, kv[..., 0, :, :]) * HEAD_DIM ** -0.5
    bias = jnp.moveaxis(_rel_bias(table_a, q_pos[None, :, None] - sel), -1, 2).reshape(B, T, A_KV_HEADS, R, k_sel)
    valid = (sel <= q_pos[None, :, None])[:, :, None, None, :]
    p = _masked_softmax(logits + bias, valid)
    o = jnp.einsum('btgrk,btkgd->btgrd', p.astype(kv.dtype), kv[..., 1, :, :])
    return o.reshape(B, T, A_HEADS * HEAD_DIM)


def _nsa_compress(kv, pe, w1, w2):
    B, L = kv.shape[:2]
    nb = -(-L // NSA_BLOCK)
    kv = jnp.pad(kv, ((0, 0), (0, nb * NSA_BLOCK - L), (0, 0), (0, 0), (0, 0)))
    blocks = kv.reshape(B, nb, NSA_BLOCK, 2, B_KV_HEADS, HEAD_DIM) + pe[None, None, :, :, None, :]
    h = jax.nn.gelu(jnp.einsum('bnlcgd,lcde->bncge', blocks, w1))
    return jnp.einsum('bncge,cef->bncgf', h, w2)


def _nsa_attend(q, gate, q_pos, cmp, fetch_sel, win_kv, win_pos, table_b):
    B, T = q.shape[:2]
    G = B_KV_HEADS
    R = B_HEADS // B_KV_HEADS
    nb = cmp.shape[1]
    scale = HEAD_DIM ** -0.5
    qg = q.reshape(B, T, G, R, HEAD_DIM)
    blk = jnp.arange(nb, dtype=jnp.int32)
    blk_end = (blk + 1) * NSA_BLOCK - 1
    lc = jnp.einsum('btgrd,bngd->btgrn', qg, cmp[:, :, 0]) * scale
    bias_c = jnp.moveaxis(_rel_bias(table_b, q_pos[:, None] - blk_end[None, :]), -1, 1).reshape(T, G, R, nb)
    p_cmp = _masked_softmax(lc + bias_c[None], (blk_end[None, :] <= q_pos[:, None])[None, :, None, None, :])
    o_cmp = jnp.einsum('btgrn,bngd->btgrd', p_cmp.astype(q.dtype), cmp[:, :, 1])
    importance = jnp.sum(p_cmp, axis=3)
    cur = q_pos // NSA_BLOCK
    eligible = blk[None, :] <= cur[:, None]
    forced = (blk[None, :] == 0) | (blk[None, :] == cur[:, None]) | (blk[None, :] == cur[:, None] - 1)
    score = jnp.where(forced[None, :, None, :], NSA_FORCE, importance)
    score = jnp.where(eligible[None, :, None, :], score, -1.0)
    n_sel = min(NSA_TOPN, nb)
    _, sel = lax.top_k(score, n_sel)
    pos = (sel[..., None] * NSA_BLOCK + jnp.arange(NSA_BLOCK, dtype=jnp.int32)).reshape(B, T, G, n_sel * NSA_BLOCK)
    kv = fetch_sel(pos)
    ls = jnp.einsum('btgrd,btgkd->btgrk', qg, kv[..., 0, :]) * scale
    tab = table_b.reshape(REL_BUCKETS, G, R)
    bias_s = tab[_rel_bucket(q_pos[None, :, None, None] - pos), jnp.arange(G)[None, None, :, None]]
    bias_s = jnp.moveaxis(bias_s, -1, 3)
    p_slc = _masked_softmax(ls + bias_s, (pos <= q_pos[None, :, None, None])[:, :, :, None, :])
    o_slc = jnp.einsum('btgrk,btgkd->btgrd', p_slc.astype(kv.dtype), kv[..., 1, :])
    Kw = win_kv.shape[1]
    lw = jnp.einsum('btgrd,bkgd->btgrk', qg, win_kv[:, :, 0]) * scale
    dw = q_pos[:, None] - win_pos[None, :]
    bias_w = jnp.moveaxis(_rel_bias(table_b, dw), -1, 1).reshape(T, G, R, Kw)
    valid_w = (dw >= 0) & (dw < WINDOW) & (win_pos[None, :] >= 0)
    p_win = _masked_softmax(lw + bias_w[None], valid_w[None, :, None, None, :])
    o_win = jnp.einsum('btgrk,bkgd->btgrd', p_win.astype(win_kv.dtype), win_kv[:, :, 1])
    g = gate.reshape(B, T, G, R, 3)
    o = g[..., 0:1] * o_cmp + g[..., 1:2] * o_slc + g[..., 2:3] * o_win
    return o.reshape(B, T, B_HEADS * HEAD_DIM)


TQ = 256
NEG = -1e30
INT_MIN = -2 ** 31
COUNT_ROWS = 64
_NT = (((1,), (1,)), ((), ()))
_TN = (((0,), (0,)), ((), ()))


def _bucket_np(d):
    n = np.maximum(d, 0)
    nf = np.maximum(n, 1).astype(np.float64)
    large = REL_MAX_EXACT + (np.log(nf / REL_MAX_EXACT) / math.log(REL_MAX_DIST / REL_MAX_EXACT)
                             * (REL_BUCKETS - REL_MAX_EXACT)).astype(np.int64)
    return np.where(n < REL_MAX_EXACT, n, np.minimum(large, REL_BUCKETS - 1)).astype(np.int32)


def _band_bias(table):
    d = TQ + np.arange(TQ)[None, :] - np.arange(2 * TQ)[:, None]
    band = jnp.take(table, jnp.asarray(_bucket_np(d)), axis=0)
    return jnp.moveaxis(band - table[REL_BUCKETS - 1], -1, 0)


def _block_bias(table, S):
    nb = S // NSA_BLOCK
    d = np.arange(S)[None, :] - ((np.arange(nb) + 1) * NSA_BLOCK - 1)[:, None]
    return jnp.moveaxis(jnp.take(table, jnp.asarray(_bucket_np(d)), axis=0), -1, 0)


def _ordered_keys(x):
    x = jnp.where(x == 0.0, 0.0, x)
    b = lax.bitcast_convert_type(x, jnp.int32)
    return b ^ ((b >> 31) & jnp.int32(0x7FFFFFFF))


def _softmax_reset(m_sc, l_sc, acc_sc):
    m_sc[...] = jnp.full(m_sc.shape, NEG, F32)
    l_sc[...] = jnp.zeros(l_sc.shape, F32)
    acc_sc[...] = jnp.zeros(acc_sc.shape, F32)


def _softmax_update(h, logits, v_c, m_sc, l_sc, acc_sc):
    m_old = m_sc[h:h + 1, :]
    m_new = jnp.maximum(m_old, jnp.max(logits, axis=0, keepdims=True))
    alpha = jnp.exp(m_old - m_new)
    p = jnp.exp(logits - m_new)
    l_sc[h:h + 1, :] = alpha * l_sc[h:h + 1, :] + jnp.sum(p, axis=0, keepdims=True)
    pv = lax.dot_general(v_c, p.astype(MXU), _TN, preferred_element_type=F32)
    acc_sc[h] = alpha * acc_sc[h] + pv
    m_sc[h:h + 1, :] = m_new


def _attend_chunk(kv_c, q_heads, madd_of_group, band_ref, band_row0, m_sc, l_sc, acc_sc):
    n_g = B_KV_HEADS
    rep = len(q_heads) // n_g
    for g in range(n_g):
        k_c = kv_c[:, g * HEAD_DIM:(g + 1) * HEAD_DIM]
        v_c = kv_c[:, (n_g + g) * HEAD_DIM:(n_g + g + 1) * HEAD_DIM]
        madd = madd_of_group(g)
        for r in range(rep):
            h = g * rep + r
            logits = lax.dot_general(k_c, q_heads[h], _NT, preferred_element_type=F32) + madd
            if band_row0 is not None:
                logits = logits + band_ref[h, band_row0:band_row0 + TQ, :]
            _softmax_update(h, logits, v_c, m_sc, l_sc, acc_sc)


def _dsa_kernel(iq_ref, misc_ref, q_ref, ik_ref, kv_ref, band_ref, o_ref, key_sc, m_sc, l_sc, acc_sc, out_sc,
                *, k_sel, pos_bits):
    i = pl.program_id(1)
    t0 = i * TQ
    n_chunks = i + 1
    lane_t = t0 + lax.broadcasted_iota(jnp.int32, (1, TQ), 1)
    sub_iota = lax.broadcasted_iota(jnp.int32, (TQ, TQ), 0)
    misc_t = misc_ref[...].T
    iq = iq_ref[...].astype(MXU)
    iq_heads = [iq[:, h * IDX_DIM:(h + 1) * IDX_DIM] for h in range(IDX_HEADS)]

    def score_chunk(j, carry):
        r0 = pl.multiple_of(j * TQ, TQ)
        ik_c = ik_ref[pl.ds(r0, TQ), :].astype(MXU)
        sc = jnp.zeros((TQ, TQ), F32)
        for h in range(IDX_HEADS):
            rel = lax.dot_general(ik_c, iq_heads[h], _NT, preferred_element_type=F32)
            sc = sc + jnp.maximum(rel * IDX_DIM ** -0.5, 0.0) * misc_t[h:h + 1, :]
        sc = jnp.where(r0 + sub_iota <= lane_t, sc, -jnp.inf)
        key_sc[pl.ds(r0, TQ), :] = _ordered_keys(sc)
        return carry

    lax.fori_loop(0, n_chunks, score_chunk, 0)

    def count(pred):
        def body(jj, acc):
            r0 = pl.multiple_of(jj * COUNT_ROWS, COUNT_ROWS)
            blk = key_sc[pl.ds(r0, COUNT_ROWS), :]
            pos = r0 + lax.broadcasted_iota(jnp.int32, (COUNT_ROWS, TQ), 0)
            hit = jnp.where(pred(blk, pos), 1, 0)
            return acc + jnp.sum(hit.reshape(COUNT_ROWS // 8, 8, TQ), axis=0)

        acc = lax.fori_loop(0, n_chunks * (TQ // COUNT_ROWS), body, jnp.zeros((8, TQ), jnp.int32))
        return jnp.sum(acc, axis=0, keepdims=True)

    v = jnp.full((1, TQ), INT_MIN, jnp.int32)
    v = jnp.where(count(lambda blk, pos: blk >= 0) >= k_sel, 0, v)

    def value_bit(it, v):
        cand = v | jnp.left_shift(jnp.int32(1), 30 - it)
        return jnp.where(count(lambda blk, pos: blk >= cand) >= k_sel, cand, v)

    v = lax.fori_loop(0, 31, value_bit, v)
    need = k_sel - count(lambda blk, pos: blk > v)
    n_ge = count(lambda blk, pos: blk >= v)

    def tie_search():
        def pos_bit(it, jm):
            cand = jm | jnp.left_shift(jnp.int32(1), pos_bits - 1 - it)
            return jnp.where(count(lambda blk, pos: (blk == v) & (pos < cand)) < need, cand, jm)

        return lax.fori_loop(0, pos_bits, pos_bit, jnp.zeros((1, TQ), jnp.int32))

    j_max = lax.cond(jnp.max(n_ge) > k_sel, tie_search, lambda: jnp.full((1, TQ), 2 ** 30, jnp.int32))

    q = (q_ref[...] * HEAD_DIM ** -0.5).astype(MXU)
    q_heads = [q[:, h * HEAD_DIM:(h + 1) * HEAD_DIM] for h in range(A_HEADS)]
    _softmax_reset(m_sc, l_sc, acc_sc)

    def chunk(j, band_row0, causal):
        r0 = pl.multiple_of(j * TQ, TQ)
        key = key_sc[pl.ds(r0, TQ), :]
        pos = r0 + sub_iota
        sel = (key > v) | ((key == v) & (pos <= j_max))
        if causal:
            sel = sel & (pos <= lane_t)
        madd = jnp.where(sel, 0.0, NEG)
        kv_c = kv_ref[pl.ds(r0, TQ), :].astype(MXU)
        _attend_chunk(kv_c, q_heads, lambda g: madd, band_ref, band_row0, m_sc, l_sc, acc_sc)

    def far_chunk(j, carry):
        chunk(j, None, False)
        return carry

    lax.fori_loop(0, jnp.maximum(i - 1, 0), far_chunk, 0)

    @pl.when(i >= 1)
    def _():
        chunk(i - 1, 0, False)

    chunk(i, TQ, True)
    for h in range(A_HEADS):
        out_sc[h * HEAD_DIM:(h + 1) * HEAD_DIM, :] = acc_sc[h] / l_sc[h:h + 1, :]
    o_ref[...] = out_sc[...].T


def _dsa_prompt(raw, band_a, B, S):
    nq = S // TQ
    k_sel = min(A_TOPK, S // 4)
    assert S % TQ == 0 and TQ >= k_sel
    tile = lambda b, i: (b * nq + i, 0)
    seq = lambda b, i: (b, 0)
    kern = functools.partial(_dsa_kernel, k_sel=k_sel, pos_bits=(S - 1).bit_length())
    return pl.pallas_call(
        kern,
        grid=(B, nq),
        in_specs=[pl.BlockSpec((TQ, IDX_HEADS * IDX_DIM), tile), pl.BlockSpec((TQ, LANES), tile),
                  pl.BlockSpec((TQ, A_HEADS * HEAD_DIM), tile), pl.BlockSpec((S, IDX_DIM), seq),
                  pl.BlockSpec((S, 2 * A_KV_HEADS * HEAD_DIM), seq),
                  pl.BlockSpec((A_HEADS, 2 * TQ, TQ), lambda b, i: (0, 0, 0))],
        out_specs=pl.BlockSpec((TQ, A_HEADS * HEAD_DIM), tile),
        out_shape=jax.ShapeDtypeStruct((B * S, A_HEADS * HEAD_DIM), F32),
        scratch_shapes=[pltpu.VMEM((S, TQ), jnp.int32), pltpu.VMEM((A_HEADS, TQ), F32), pltpu.VMEM((A_HEADS, TQ), F32),
                        pltpu.VMEM((A_HEADS, HEAD_DIM, TQ), F32), pltpu.VMEM((A_HEADS * HEAD_DIM, TQ), F32)],
        compiler_params=pltpu.CompilerParams(dimension_semantics=('arbitrary', 'arbitrary'),
                                             vmem_limit_bytes=VMEM_LIMIT),
        name='dsa_prompt',
    )(raw['a_iq'], raw['misc'], raw['a_q'], raw['a_ik'], raw['a_kv'], band_a)


def _compress_kernel(x_ref, pe_ref, w1_ref, w2_ref, o_ref):
    x = (x_ref[...] + pe_ref[...]).astype(MXU)
    h = jax.nn.gelu(jnp.dot(x, w1_ref[...], preferred_element_type=F32))
    o_ref[...] = jnp.dot(h.astype(MXU), w2_ref[...], preferred_element_type=F32)


def _compress_weights(pe, w1, w2):
    eye_c = jnp.eye(2, dtype=F32)
    eye_g = jnp.eye(B_KV_HEADS, dtype=F32)
    w1_big = jnp.einsum('lcde,cC,gG->lcgdCGe', w1, eye_c, eye_g)
    w1_big = w1_big.reshape(NSA_BLOCK * 2 * B_KV_HEADS * HEAD_DIM, 2 * B_KV_HEADS * CMP_HIDDEN)
    w2_big = jnp.einsum('cef,cC,gG->cgeCGf', w2, eye_c, eye_g)
    w2_big = w2_big.reshape(2 * B_KV_HEADS * CMP_HIDDEN, 2 * B_KV_HEADS * HEAD_DIM)
    pe_flat = jnp.broadcast_to(pe[:, :, None, :], (NSA_BLOCK, 2, B_KV_HEADS, HEAD_DIM)).reshape(1, -1)
    return pe_flat, w1_big.astype(MXU), w2_big.astype(MXU)


def _compress(blocks2d, pe_flat, w1_big, w2_big, tm):
    n, width = blocks2d.shape
    fixed = lambda i: (0, 0)
    return pl.pallas_call(
        _compress_kernel,
        grid=(n // tm,),
        in_specs=[pl.BlockSpec((tm, width), lambda i: (i, 0)), pl.BlockSpec((1, width), fixed),
                  pl.BlockSpec(w1_big.shape, fixed), pl.BlockSpec(w2_big.shape, fixed)],
        out_specs=pl.BlockSpec((tm, w2_big.shape[1]), lambda i: (i, 0)),
        out_shape=jax.ShapeDtypeStruct((n, w2_big.shape[1]), F32),
        compiler_params=pltpu.CompilerParams(dimension_semantics=('arbitrary',), vmem_limit_bytes=VMEM_LIMIT),
        name='nsa_compress',
    )(blocks2d, pe_flat, w1_big, w2_big)


def _nsa_kernel(q_ref, misc_ref, cmp_ref, selkv_ref, winkv_ref, band_ref, biasc_ref, o_ref,
                sel_sc, m_sc, l_sc, acc_sc, out_sc, *, nb, n_sel):
    i = pl.program_id(1)
    t0 = i * TQ
    n_g = B_KV_HEADS
    rep = B_HEADS // n_g
    lane_t = t0 + lax.broadcasted_iota(jnp.int32, (1, TQ), 1)
    sub_iota = lax.broadcasted_iota(jnp.int32, (TQ, TQ), 0)
    lane_iota = lax.broadcasted_iota(jnp.int32, (TQ, TQ), 1)
    misc_t = misc_ref[...].T
    gate = lambda h, k: misc_t[MISC_GATE_OFF + 3 * h + k:MISC_GATE_OFF + 3 * h + k + 1, :]
    q = (q_ref[...] * HEAD_DIM ** -0.5).astype(MXU)
    q_heads = [q[:, h * HEAD_DIM:(h + 1) * HEAD_DIM] for h in range(B_HEADS)]

    cmp = cmp_ref[...].astype(MXU)
    blk = lax.broadcasted_iota(jnp.int32, (nb, TQ), 0)
    visible = (blk + 1) * NSA_BLOCK - 1 <= lane_t
    cur = lax.shift_right_logical(lane_t, NSA_BLOCK.bit_length() - 1)
    forced = (blk == 0) | (blk == cur) | (blk == cur - 1)
    for g in range(n_g):
        k_c = cmp[:, g * HEAD_DIM:(g + 1) * HEAD_DIM]
        v_c = cmp[:, (n_g + g) * HEAD_DIM:(n_g + g + 1) * HEAD_DIM]
        importance = jnp.zeros((nb, TQ), F32)
        for r in range(rep):
            h = g * rep + r
            lc = lax.dot_general(k_c, q_heads[h], _NT, preferred_element_type=F32) + biasc_ref[h]
            lc = jnp.where(visible, lc, -jnp.inf)
            m = jnp.max(lc, axis=0, keepdims=True)
            m = jnp.where(m > -jnp.inf, m, 0.0)
            e = jnp.exp(lc - m)
            s = jnp.sum(e, axis=0, keepdims=True)
            p = e / jnp.where(s > 0, s, 1.0)
            importance = importance + p
            o_cmp = lax.dot_general(v_c, p.astype(MXU), _TN, preferred_element_type=F32)
            out_sc[h * HEAD_DIM:(h + 1) * HEAD_DIM, :] = gate(h, 0) * o_cmp
        score = jnp.where(forced, NSA_FORCE, importance)
        score = jnp.where(blk <= cur, score, -1.0)
        rank = jnp.zeros((nb, TQ), jnp.int32)
        for n in range(nb):
            row = score[n:n + 1, :]
            beats = (row > score) | ((row == score) & (blk > n))
            rank = rank + jnp.where(beats, 1, 0)
        sel_sc[g] = jnp.where(rank < n_sel, 0.0, NEG)

    def finish_branch(k):
        for h in range(B_HEADS):
            rows = slice(h * HEAD_DIM, (h + 1) * HEAD_DIM)
            out_sc[rows, :] = out_sc[rows, :] + gate(h, k) * (acc_sc[h] / l_sc[h:h + 1, :])

    _softmax_reset(m_sc, l_sc, acc_sc)
    blocks_per_chunk = TQ // NSA_BLOCK

    def sel_chunk(j, band_row0, causal):
        r0 = pl.multiple_of(j * TQ, TQ)
        kv_c = selkv_ref[pl.ds(r0, TQ), :].astype(MXU)

        def madd_of_group(g):
            rows = [sel_sc[g, pl.ds(j * blocks_per_chunk + b, 1), :] for b in range(blocks_per_chunk)]
            madd = jnp.concatenate([jnp.broadcast_to(row, (NSA_BLOCK, TQ)) for row in rows], axis=0)
            if causal:
                madd = jnp.where(sub_iota <= lane_iota, madd, NEG)
            return madd

        _attend_chunk(kv_c, q_heads, madd_of_group, band_ref, band_row0, m_sc, l_sc, acc_sc)

    def far_chunk(j, carry):
        sel_chunk(j, None, False)
        return carry

    lax.fori_loop(0, jnp.maximum(i - 1, 0), far_chunk, 0)

    @pl.when(i >= 1)
    def _():
        sel_chunk(i - 1, 0, False)

    sel_chunk(i, TQ, True)
    finish_branch(1)

    _softmax_reset(m_sc, l_sc, acc_sc)
    for back in range(WINDOW // TQ, -1, -1):
        dist = back * TQ + lane_iota - sub_iota
        madd = jnp.where((dist >= 0) & (dist < WINDOW), 0.0, NEG)
        band_row0 = {0: TQ, 1: 0}.get(back)

        def win_chunk(back=back, madd=madd, band_row0=band_row0):
            r0 = pl.multiple_of((i - back) * TQ, TQ)
            kv_c = winkv_ref[pl.ds(r0, TQ), :].astype(MXU)
            _attend_chunk(kv_c, q_heads, lambda g: madd, band_ref, band_row0, m_sc, l_sc, acc_sc)

        if back == 0:
            win_chunk()
        else:
            pl.when(i >= back)(win_chunk)
    finish_branch(2)
    o_ref[...] = out_sc[...].T


def _nsa_prompt(raw, cmp, band_b, bias_c, B, S):
    nq = S // TQ
    nb = S // NSA_BLOCK
    assert S % TQ == 0 and TQ % NSA_BLOCK == 0
    tile = lambda b, i: (b * nq + i, 0)
    seq = lambda b, i: (b, 0)
    kv_w = 2 * B_KV_HEADS * HEAD_DIM
    kern = functools.partial(_nsa_kernel, nb=nb, n_sel=min(NSA_TOPN, nb))
    return pl.pallas_call(
        kern,
        grid=(B, nq),
        in_specs=[pl.BlockSpec((TQ, B_HEADS * HEAD_DIM), tile), pl.BlockSpec((TQ, LANES), tile),
                  pl.BlockSpec((nb, kv_w), seq), pl.BlockSpec((S, kv_w), seq), pl.BlockSpec((S, kv_w), seq),
                  pl.BlockSpec((B_HEADS, 2 * TQ, TQ), lambda b, i: (0, 0, 0)),
                  pl.BlockSpec((B_HEADS, nb, TQ), lambda b, i: (0, 0, i))],
        out_specs=pl.BlockSpec((TQ, B_HEADS * HEAD_DIM), tile),
        out_shape=jax.ShapeDtypeStruct((B * S, B_HEADS * HEAD_DIM), F32),
        scratch_shapes=[pltpu.VMEM((B_KV_HEADS, nb, TQ), F32), pltpu.VMEM((B_HEADS, TQ), F32),
                        pltpu.VMEM((B_HEADS, TQ), F32), pltpu.VMEM((B_HEADS, HEAD_DIM, TQ), F32),
                        pltpu.VMEM((B_HEADS * HEAD_DIM, TQ), F32)],
        compiler_params=pltpu.CompilerParams(dimension_semantics=('arbitrary', 'arbitrary'),
                                             vmem_limit_bytes=VMEM_LIMIT),
        name='nsa_prompt',
    )(raw['b_q'], raw['misc'], cmp, raw['b_sel_kv'], raw['b_win_kv'], band_b, bias_c)


def _mix_prompt(pr, rel_table, cmp_pe, cmp_w1, cmp_w2):
    B, S = pr['a_q'].shape[:2]
    raw = pr['raw']
    table_a, table_b = rel_table[:, :A_HEADS], rel_table[:, A_HEADS:]
    pe_flat, w1_big, w2_big = _compress_weights(cmp_pe, cmp_w1, cmp_w2)
    n_blocks = B * S // NSA_BLOCK
    cmp = _compress(raw['b_cmp_kv'].reshape(n_blocks, -1), pe_flat, w1_big, w2_big, tm=min(128, n_blocks))
    o_a = _dsa_prompt(raw, _band_bias(table_a), B, S)
    o_b = _nsa_prompt(raw, cmp, _band_bias(table_b), _block_bias(table_b, S), B, S)
    wb = min(WINDOW, S)
    new_state = (pr['a_kv'], pr['a_ik'], pr['b_cmp_kv'], pr['b_sel_kv'], pr['b_win_kv'][:, S - wb:])
    return o_a, o_b, new_state


def _gather_pages(pool, page_table):
    rows = pool[page_table]
    return rows.reshape((page_table.shape[0], page_table.shape[1] * PAGE_SIZE) + pool.shape[2:])


def _paged_fetch(pool, new_rows, page_table, grouped):
    DB = page_table.shape[0]
    past_len = page_table.shape[1] * PAGE_SIZE
    n_new = new_rows.shape[1]

    def fetch(pos):
        b = jnp.arange(DB).reshape((DB,) + (1,) * (pos.ndim - 1))
        in_past = pos < past_len
        pp = jnp.where(in_past, pos, 0)
        page = page_table[b, pp // PAGE_SIZE]
        off = pp % PAGE_SIZE
        jn = jnp.clip(pos - past_len, 0, n_new - 1)
        if grouped:
            g = jnp.arange(B_KV_HEADS)[None, None, :, None]
            old = pool[page, off, :, g]
            new = new_rows[b, jn, :, g]
        else:
            old = pool[page, off]
            new = new_rows[b, jn]
        mask = in_past.reshape(in_past.shape + (1,) * (old.ndim - pos.ndim))
        return jnp.where(mask, old, new)

    return fetch


def _mix_sample(pr, cache_a_kv, cache_a_idx, cache_b_cmp_kv, cache_b_sel_kv, state_b_win_kv, page_table,
                rel_table, cmp_pe, cmp_w1, cmp_w2):
    T = pr['a_q'].shape[1]
    past_len = page_table.shape[1] * PAGE_SIZE
    q_pos = past_len + jnp.arange(T, dtype=jnp.int32)
    ik_all = jnp.concatenate([_gather_pages(cache_a_idx, page_table), pr['a_ik']], axis=1)
    cmp_rows = jnp.concatenate([_gather_pages(cache_b_cmp_kv, page_table), pr['b_cmp_kv']], axis=1)
    cmp = _nsa_compress(cmp_rows, cmp_pe, cmp_w1, cmp_w2)
    fetch_a = _paged_fetch(cache_a_kv, pr['a_kv'], page_table, grouped=False)
    fetch_b = _paged_fetch(cache_b_sel_kv, pr['b_sel_kv'], page_table, grouped=True)
    win_all = jnp.concatenate([state_b_win_kv, pr['b_win_kv']], axis=1)
    wb = state_b_win_kv.shape[1]
    win_pos = past_len - wb + jnp.arange(wb + T, dtype=jnp.int32)
    o_a = _dsa_attend(pr['a_q'], pr['a_iq'], pr['a_iw'], q_pos, ik_all, fetch_a, rel_table[:, :A_HEADS])
    o_b = _nsa_attend(pr['b_q'], pr['b_gate'], q_pos, cmp, fetch_b, win_all, win_pos, rel_table[:, A_HEADS:])
    new_state = (pr['a_kv'], pr['a_ik'], pr['b_cmp_kv'], pr['b_sel_kv'], win_all[:, T:])
    return o_a, o_b, new_state


def _peer(x, w_pq, sub_keys, u_table, v_table):
    B, T, D = x.shape
    flat = x.reshape(B * T, D)
    n = flat.shape[0]
    n_blk = -(-n // TOKEN_BLOCK)
    flat = jnp.pad(flat, ((0, n_blk * TOKEN_BLOCK - n), (0, 0))).reshape(n_blk, TOKEN_BLOCK, D)

    def block(xb):
        q = jnp.einsum('nd,dhc->nhc', xb, w_pq).reshape(TOKEN_BLOCK, PEER_HEADS, 2, PEER_QDIM // 2)
        s = jnp.einsum('nhpc,hpkc->nhpk', q, sub_keys).astype(F32)
        vals, idx = lax.top_k(s, PEER_TOPK)
        cand = (vals[..., 0, :, None] + vals[..., 1, None, :]).reshape(TOKEN_BLOCK, PEER_HEADS, PEER_TOPK * PEER_TOPK)
        cv, ci = lax.top_k(cand, PEER_TOPK)
        i1 = jnp.take_along_axis(idx[..., 0, :], ci // PEER_TOPK, axis=-1)
        i2 = jnp.take_along_axis(idx[..., 1, :], ci % PEER_TOPK, axis=-1)
        expert = i1 * PEER_NKEYS + i2
        g = jax.nn.softmax(cv, axis=-1)
        act = jax.nn.gelu(jnp.einsum('nd,nhkd->nhk', xb, u_table[expert]))
        return jnp.einsum('nhk,nhkd->nd', (g * act).astype(xb.dtype), v_table[expert])

    out = lax.map(block, flat).reshape(n_blk * TOKEN_BLOCK, D)[:n]
    return out.reshape(B, T, D)


def _ln_ref(x, g, b):
    mu = jnp.mean(x, axis=-1, keepdims=True)
    var = jnp.mean(jnp.square(x - mu), axis=-1, keepdims=True)
    return (x - mu) * lax.rsqrt(var + LN_EPS) * g + b


def _group_step(x, mixer, ws, w_up_a, w_up_b, w_out, ln1_g, ln1_b, w_pq, sub_keys, peer_u, peer_v, ln2_g, ln2_b, tm):
    B, T, _ = x.shape
    x2d = x.reshape(B * T, D_MODEL)
    pr = _unpack_proj(_project(x2d, ws, tm), B, T)
    o_a, o_b, new_state = mixer(pr)
    h = _merge(x2d, o_a.reshape(B * T, -1), o_b.reshape(B * T, -1), pr['merge_gate'],
               w_up_a, w_up_b, w_out, ln1_g, ln1_b, tm).reshape(B, T, D_MODEL)
    y = _ln_ref(DN_ALPHA * h + _peer(h, w_pq, sub_keys, peer_u, peer_v), ln2_g, ln2_b)
    return y, new_state


def kernel(x_prompt, x_sample, cache_a_kv, cache_a_idx, cache_b_cmp_kv, cache_b_sel_kv, state_b_win_kv, page_table, w_in, rel_bias_table, cmp_pe, cmp_w1, cmp_w2, w_up_a, w_up_b, w_out, ln1_g, ln1_b, w_pq, peer_sub_keys, peer_u, peer_v, ln2_g, ln2_b):
    l = 0
    ws = _split_w_in(w_in[l])
    tail = (w_up_a[l], w_up_b[l], w_out[l], ln1_g[l], ln1_b[l], w_pq[l], peer_sub_keys[l], peer_u[l], peer_v[l],
            ln2_g[l], ln2_b[l])
    mix_p = functools.partial(_mix_prompt, rel_table=rel_bias_table, cmp_pe=cmp_pe[l], cmp_w1=cmp_w1[l],
                              cmp_w2=cmp_w2[l])
    y_p, st_p = _group_step(x_prompt, mix_p, ws, *tail, tm=256)
    mix_s = functools.partial(_mix_sample, cache_a_kv=cache_a_kv[l], cache_a_idx=cache_a_idx[l],
                              cache_b_cmp_kv=cache_b_cmp_kv[l], cache_b_sel_kv=cache_b_sel_kv[l],
                              state_b_win_kv=state_b_win_kv[l], page_table=page_table, rel_table=rel_bias_table,
                              cmp_pe=cmp_pe[l], cmp_w1=cmp_w1[l], cmp_w2=cmp_w2[l])
    y_s, st_s = _group_step(x_sample, mix_s, ws, *tail, tm=256)
    return (y_p, y_s) + tuple(s[None] for s in st_p) + tuple(s[None] for s in st_s)
```

```python
import functools
import math

import jax
import jax.numpy as jnp
import numpy as np
from jax import lax
from jax.experimental import pallas as pl
from jax.experimental.pallas import tpu as pltpu

D_MODEL = 1024
DEPTH = 1
PAGE_SIZE = 128
HEAD_DIM = 64
A_HEADS = 8
A_KV_HEADS = 2
IDX_HEADS = 4
IDX_DIM = 64
A_TOPK = 256
B_HEADS = 8
B_KV_HEADS = 2
NSA_BLOCK = 64
NSA_TOPN = 16
NSA_FORCE = 8.0
CMP_HIDDEN = 64
WINDOW = 512
REL_BUCKETS = 32
REL_MAX_EXACT = 16
REL_MAX_DIST = 128
PEER_HEADS = 8
PEER_NKEYS = 128
PEER_QDIM = 256
PEER_TOPK = 16
Q_BLOCK = 32
TOKEN_BLOCK = 256
LN_EPS = 1e-5
DN_ALPHA = (2 * DEPTH) ** 0.25

PROJ_LAYOUT = (
    ('a_q', A_HEADS * HEAD_DIM),
    ('a_kv', 2 * A_KV_HEADS * HEAD_DIM),
    ('a_iq', IDX_HEADS * IDX_DIM),
    ('a_iw', IDX_HEADS),
    ('a_ik', IDX_DIM),
    ('b_q', B_HEADS * HEAD_DIM),
    ('b_cmp_kv', 2 * B_KV_HEADS * HEAD_DIM),
    ('b_sel_kv', 2 * B_KV_HEADS * HEAD_DIM),
    ('b_win_kv', 2 * B_KV_HEADS * HEAD_DIM),
    ('b_gate', B_HEADS * 3),
    ('merge_gate', 2 * D_MODEL),
)

LANES = 128
VMEM_LIMIT = 56 * 1024 * 1024
MISC_GATE_OFF = IDX_HEADS

F32 = jnp.float32
BF16 = jnp.bfloat16
MXU = jnp.bfloat16


_PROJ_OUT = ('a_q', 'a_kv', 'a_iq', 'a_ik', 'misc', 'b_q', 'b_cmp_kv', 'b_sel_kv', 'b_win_kv', 'merge_gate')


def _proj_kernel(x_ref, *refs):
    n = len(_PROJ_OUT)
    w_refs, o_refs = refs[:n], refs[n:]
    x = x_ref[...].astype(BF16)
    for name, w_ref, o_ref in zip(_PROJ_OUT, w_refs, o_refs):
        r = jnp.dot(x, w_ref[...], preferred_element_type=F32)
        if name == 'misc':
            lane = lax.broadcasted_iota(jnp.int32, r.shape, 1)
            r = jnp.where(lane < MISC_GATE_OFF, r * IDX_HEADS ** -0.5, jax.nn.sigmoid(r))
        elif name == 'merge_gate':
            r = jax.nn.sigmoid(r)
        o_ref[...] = r


def _split_w_in(w_in):
    parts = {}
    off = 0
    for name, width in PROJ_LAYOUT:
        parts[name] = w_in[:, off:off + width]
        off += width
    misc = jnp.concatenate([parts['a_iw'], parts['b_gate']], axis=1)
    parts['misc'] = jnp.pad(misc, ((0, 0), (0, LANES - misc.shape[1])))
    return [parts[name].astype(BF16) for name in _PROJ_OUT]


def _project(x2d, ws, tm):
    n = x2d.shape[0]
    widths = [w.shape[1] for w in ws]
    return pl.pallas_call(
        _proj_kernel,
        grid=(n // tm,),
        in_specs=[pl.BlockSpec((tm, D_MODEL), lambda i: (i, 0))]
        + [pl.BlockSpec((D_MODEL, wd), lambda i: (0, 0)) for wd in widths],
        out_specs=[pl.BlockSpec((tm, wd), lambda i: (i, 0)) for wd in widths],
        out_shape=[jax.ShapeDtypeStruct((n, wd), F32) for wd in widths],
        compiler_params=pltpu.CompilerParams(dimension_semantics=('arbitrary',), vmem_limit_bytes=VMEM_LIMIT),
        name='project',
    )(x2d, *ws)


def _unpack_proj(outs, B, T):
    d = dict(zip(_PROJ_OUT, outs))
    misc = d['misc']
    kv = (B, T, 2, A_KV_HEADS, HEAD_DIM)
    return {
        'a_q': d['a_q'].reshape(B, T, A_HEADS, HEAD_DIM),
        'a_kv': d['a_kv'].reshape(kv),
        'a_iq': d['a_iq'].reshape(B, T, IDX_HEADS, IDX_DIM),
        'a_iw': misc[:, :IDX_HEADS].reshape(B, T, IDX_HEADS),
        'a_ik': d['a_ik'].reshape(B, T, IDX_DIM),
        'b_q': d['b_q'].reshape(B, T, B_HEADS, HEAD_DIM),
        'b_cmp_kv': d['b_cmp_kv'].reshape(kv),
        'b_sel_kv': d['b_sel_kv'].reshape(kv),
        'b_win_kv': d['b_win_kv'].reshape(kv),
        'b_gate': misc[:, MISC_GATE_OFF:MISC_GATE_OFF + 3 * B_HEADS].reshape(B, T, B_HEADS, 3),
        'merge_gate': d['merge_gate'],
        'raw': d,
    }


def _layer_norm(x, g, b):
    mu = jnp.mean(x, axis=-1, keepdims=True)
    xc = x - mu
    var = jnp.mean(xc * xc, axis=-1, keepdims=True)
    return xc * lax.rsqrt(var + LN_EPS) * g + b


def _merge_kernel(x_ref, oa_ref, ob_ref, mg_ref, wa_ref, wb_ref, wo_ref, g_ref, b_ref, h_ref):
    ya = jnp.dot(oa_ref[...].astype(BF16), wa_ref[...], preferred_element_type=F32)
    yb = jnp.dot(ob_ref[...].astype(BF16), wb_ref[...], preferred_element_type=F32)
    mg = mg_ref[...]
    mix = mg[:, :D_MODEL] * ya + mg[:, D_MODEL:] * yb
    mixed = jnp.dot(mix.astype(BF16), wo_ref[...], preferred_element_type=F32)
    h_ref[...] = _layer_norm(DN_ALPHA * x_ref[...] + mixed, g_ref[...], b_ref[...])


def _merge(x2d, o_a, o_b, mg, w_up_a, w_up_b, w_out, ln_g, ln_b, tm):
    n = x2d.shape[0]
    ca, cb = o_a.shape[1], o_b.shape[1]
    row = lambda i: (i, 0)
    fixed = lambda i: (0, 0)
    return pl.pallas_call(
        _merge_kernel,
        grid=(n // tm,),
        in_specs=[pl.BlockSpec((tm, D_MODEL), row), pl.BlockSpec((tm, ca), row), pl.BlockSpec((tm, cb), row),
                  pl.BlockSpec((tm, 2 * D_MODEL), row), pl.BlockSpec((ca, D_MODEL), fixed),
                  pl.BlockSpec((cb, D_MODEL), fixed), pl.BlockSpec((D_MODEL, D_MODEL), fixed),
                  pl.BlockSpec((1, D_MODEL), fixed), pl.BlockSpec((1, D_MODEL), fixed)],
        out_specs=pl.BlockSpec((tm, D_MODEL), row),
        out_shape=jax.ShapeDtypeStruct((n, D_MODEL), F32),
        compiler_params=pltpu.CompilerParams(dimension_semantics=('arbitrary',), vmem_limit_bytes=VMEM_LIMIT),
        name='merge',
    )(x2d, o_a, o_b, mg, w_up_a.astype(BF16), w_up_b.astype(BF16), w_out.astype(BF16),
      ln_g.reshape(1, D_MODEL), ln_b.reshape(1, D_MODEL))


def _masked_softmax(logits, mask):
    l = jnp.where(mask, logits.astype(F32), -jnp.inf)
    m = jnp.max(l, axis=-1, keepdims=True)
    m = jnp.where(jnp.isfinite(m), m, 0.0)
    e = jnp.exp(l - m)
    s = jnp.sum(e, axis=-1, keepdims=True)
    return e / jnp.where(s > 0, s, 1.0)


def _rel_bucket(d):
    n = jnp.maximum(d, 0)
    nf = jnp.maximum(n, 1).astype(F32)
    large = REL_MAX_EXACT + (jnp.log(nf / REL_MAX_EXACT) / math.log(REL_MAX_DIST / REL_MAX_EXACT)
                             * (REL_BUCKETS - REL_MAX_EXACT)).astype(jnp.int32)
    return jnp.where(n < REL_MAX_EXACT, n, jnp.minimum(large, REL_BUCKETS - 1))


def _rel_bias(table, d):
    return table[_rel_bucket(d)]


def _dsa_attend(q, iq, iw, q_pos, ik_all, fetch_kv, table_a):
    B, T = q.shape[:2]
    R = A_HEADS // A_KV_HEADS
    L = ik_all.shape[1]
    k_sel = min(A_TOPK, L // 4)
    k_pos = jnp.arange(L, dtype=jnp.int32)
    rel = jax.nn.relu(jnp.einsum('bthd,bsd->bths', iq, ik_all) * IDX_DIM ** -0.5)
    score = jnp.einsum('bths,bth->bts', rel, iw).astype(F32)
    score = jnp.where((k_pos[None, :] <= q_pos[:, None])[None], score, -jnp.inf)
    _, sel = lax.top_k(score, k_sel)
    kv = fetch_kv(sel)
    qg = q.reshape(B, T, A_KV_HEADS, R, HEAD_DIM)
    logits = jnp.einsum('btgrd,btkgd->btgrk', qg, kv[..., 0, :, :]) * HEAD_DIM ** -0.5
    bias = jnp.moveaxis(_rel_bias(table_a, q_pos[None, :, None] - sel), -1, 2).reshape(B, T, A_KV_HEADS, R, k_sel)
    valid = (sel <= q_pos[None, :, None])[:, :, None, None, :]
    p = _masked_softmax(logits + bias, valid)
    o = jnp.einsum('btgrk,btkgd->btgrd', p.astype(kv.dtype), kv[..., 1, :, :])
    return o.reshape(B, T, A_HEADS * HEAD_DIM)


def _nsa_compress(kv, pe, w1, w2):
    B, L = kv.shape[:2]
    nb = -(-L // NSA_BLOCK)
    kv = jnp.pad(kv, ((0, 0), (0, nb * NSA_BLOCK - L), (0, 0), (0, 0), (0, 0)))
    blocks = kv.reshape(B, nb, NSA_BLOCK, 2, B_KV_HEADS, HEAD_DIM) + pe[None, None, :, :, None, :]
    h = jax.nn.gelu(jnp.einsum('bnlcgd,lcde->bncge', blocks, w1))
    return jnp.einsum('bncge,cef->bncgf', h, w2)


def _nsa_attend(q, gate, q_pos, cmp, fetch_sel, win_kv, win_pos, table_b):
    B, T = q.shape[:2]
    G = B_KV_HEADS
    R = B_HEADS // B_KV_HEADS
    nb = cmp.shape[1]
    scale = HEAD_DIM ** -0.5
    qg = q.reshape(B, T, G, R, HEAD_DIM)
    blk = jnp.arange(nb, dtype=jnp.int32)
    blk_end = (blk + 1) * NSA_BLOCK - 1
    lc = jnp.einsum('btgrd,bngd->btgrn', qg, cmp[:, :, 0]) * scale
    bias_c = jnp.moveaxis(_rel_bias(table_b, q_pos[:, None] - blk_end[None, :]), -1, 1).reshape(T, G, R, nb)
    p_cmp = _masked_softmax(lc + bias_c[None], (blk_end[None, :] <= q_pos[:, None])[None, :, None, None, :])
    o_cmp = jnp.einsum('btgrn,bngd->btgrd', p_cmp.astype(q.dtype), cmp[:, :, 1])
    importance = jnp.sum(p_cmp, axis=3)
    cur = q_pos // NSA_BLOCK
    eligible = blk[None, :] <= cur[:, None]
    forced = (blk[None, :] == 0) | (blk[None, :] == cur[:, None]) | (blk[None, :] == cur[:, None] - 1)
    score = jnp.where(forced[None, :, None, :], NSA_FORCE, importance)
    score = jnp.where(eligible[None, :, None, :], score, -1.0)
    n_sel = min(NSA_TOPN, nb)
    _, sel = lax.top_k(score, n_sel)
    pos = (sel[..., None] * NSA_BLOCK + jnp.arange(NSA_BLOCK, dtype=jnp.int32)).reshape(B, T, G, n_sel * NSA_BLOCK)
    kv = fetch_sel(pos)
    ls = jnp.einsum('btgrd,btgkd->btgrk', qg, kv[..., 0, :]) * scale
    tab = table_b.reshape(REL_BUCKETS, G, R)
    bias_s = tab[_rel_bucket(q_pos[None, :, None, None] - pos), jnp.arange(G)[None, None, :, None]]
    bias_s = jnp.moveaxis(bias_s, -1, 3)
    p_slc = _masked_softmax(ls + bias_s, (pos <= q_pos[None, :, None, None])[:, :, :, None, :])
    o_slc = jnp.einsum('btgrk,btgkd->btgrd', p_slc.astype(kv.dtype), kv[..., 1, :])
    Kw = win_kv.shape[1]
    lw = jnp.einsum('btgrd,bkgd->btgrk', qg, win_kv[:, :, 0]) * scale
    dw = q_pos[:, None] - win_pos[None, :]
    bias_w = jnp.moveaxis(_rel_bias(table_b, dw), -1, 1).reshape(T, G, R, Kw)
    valid_w = (dw >= 0) & (dw < WINDOW) & (win_pos[None, :] >= 0)
    p_win = _masked_softmax(lw + bias_w[None], valid_w[None, :, None, None, :])
    o_win = jnp.einsum('btgrk,bkgd->btgrd', p_win.astype(win_kv.dtype), win_kv[:, :, 1])
    g = gate.reshape(B, T, G, R, 3)
    o = g[..., 0:1] * o_cmp + g[..., 1:2] * o_slc + g[..., 2:3] * o_win
    return o.reshape(B, T, B_HEADS * HEAD_DIM)


TQ = 256
NEG = -1e30
INT_MIN = -2 ** 31
COUNT_ROWS = 64
_NT = (((1,), (1,)), ((), ()))
_TN = (((0,), (0,)), ((), ()))


def _bucket_np(d):
    n = np.maximum(d, 0)
    nf = np.maximum(n, 1).astype(np.float64)
    large = REL_MAX_EXACT + (np.log(nf / REL_MAX_EXACT) / math.log(REL_MAX_DIST / REL_MAX_EXACT)
                             * (REL_BUCKETS - REL_MAX_EXACT)).astype(np.int64)
    return np.where(n < REL_MAX_EXACT, n, np.minimum(large, REL_BUCKETS - 1)).astype(np.int32)


def _band_bias(table):
    d = TQ + np.arange(TQ)[None, :] - np.arange(2 * TQ)[:, None]
    band = jnp.take(table, jnp.asarray(_bucket_np(d)), axis=0)
    return jnp.moveaxis(band - table[REL_BUCKETS - 1], -1, 0)


def _block_bias(table, S):
    nb = S // NSA_BLOCK
    d = np.arange(S)[None, :] - ((np.arange(nb) + 1) * NSA_BLOCK - 1)[:, None]
    return jnp.moveaxis(jnp.take(table, jnp.asarray(_bucket_np(d)), axis=0), -1, 0)


def _ordered_keys(x):
    x = jnp.where(x == 0.0, 0.0, x)
    b = lax.bitcast_convert_type(x, jnp.int32)
    return b ^ ((b >> 31) & jnp.int32(0x7FFFFFFF))


def _softmax_reset(m_sc, l_sc, acc_sc):
    m_sc[...] = jnp.full(m_sc.shape, NEG, F32)
    l_sc[...] = jnp.zeros(l_sc.shape, F32)
    acc_sc[...] = jnp.zeros(acc_sc.shape, F32)


def _softmax_update(h, logits, v_c, m_sc, l_sc, acc_sc):
    m_old = m_sc[h:h + 1, :]
    m_new = jnp.maximum(m_old, jnp.max(logits, axis=0, keepdims=True))
    alpha = jnp.exp(m_old - m_new)
    p = jnp.exp(logits - m_new)
    l_sc[h:h + 1, :] = alpha * l_sc[h:h + 1, :] + jnp.sum(p, axis=0, keepdims=True)
    pv = lax.dot_general(v_c, p.astype(MXU), _TN, preferred_element_type=F32)
    acc_sc[h] = alpha * acc_sc[h] + pv
    m_sc[h:h + 1, :] = m_new


def _attend_chunk(kv_c, q_heads, madd_of_group, band_ref, band_row0, m_sc, l_sc, acc_sc):
    n_g = B_KV_HEADS
    rep = len(q_heads) // n_g
    for g in range(n_g):
        k_c = kv_c[:, g * HEAD_DIM:(g + 1) * HEAD_DIM]
        v_c = kv_c[:, (n_g + g) * HEAD_DIM:(n_g + g + 1) * HEAD_DIM]
        madd = madd_of_group(g)
        for r in range(rep):
            h = g * rep + r
            logits = lax.dot_general(k_c, q_heads[h], _NT, preferred_element_type=F32) + madd
            if band_row0 is not None:
                logits = logits + band_ref[h, band_row0:band_row0 + TQ, :]
            _softmax_update(h, logits, v_c, m_sc, l_sc, acc_sc)


def _dsa_kernel(iq_ref, misc_ref, q_ref, ik_ref, kv_ref, band_ref, o_ref, key_sc, m_sc, l_sc, acc_sc, out_sc,
                *, k_sel, pos_bits):
    i = pl.program_id(1)
    t0 = i * TQ
    n_chunks = i + 1
    lane_t = t0 + lax.broadcasted_iota(jnp.int32, (1, TQ), 1)
    sub_iota = lax.broadcasted_iota(jnp.int32, (TQ, TQ), 0)
    misc_t = misc_ref[...].T
    iq = iq_ref[...].astype(MXU)
    iq_heads = [iq[:, h * IDX_DIM:(h + 1) * IDX_DIM] for h in range(IDX_HEADS)]

    def score_chunk(j, carry):
        r0 = pl.multiple_of(j * TQ, TQ)
        ik_c = ik_ref[pl.ds(r0, TQ), :].astype(MXU)
        sc = jnp.zeros((TQ, TQ), F32)
        for h in range(IDX_HEADS):
            rel = lax.dot_general(ik_c, iq_heads[h], _NT, preferred_element_type=F32)
            sc = sc + jnp.maximum(rel * IDX_DIM ** -0.5, 0.0) * misc_t[h:h + 1, :]
        sc = jnp.where(r0 + sub_iota <= lane_t, sc, -jnp.inf)
        key_sc[pl.ds(r0, TQ), :] = _ordered_keys(sc)
        return carry

    lax.fori_loop(0, n_chunks, score_chunk, 0)

    def count(pred):
        def body(jj, acc):
            r0 = pl.multiple_of(jj * COUNT_ROWS, COUNT_ROWS)
            blk = key_sc[pl.ds(r0, COUNT_ROWS), :]
            pos = r0 + lax.broadcasted_iota(jnp.int32, (COUNT_ROWS, TQ), 0)
            hit = jnp.where(pred(blk, pos), 1, 0)
            return acc + jnp.sum(hit.reshape(COUNT_ROWS // 8, 8, TQ), axis=0)

        acc = lax.fori_loop(0, n_chunks * (TQ // COUNT_ROWS), body, jnp.zeros((8, TQ), jnp.int32))
        return jnp.sum(acc, axis=0, keepdims=True)

    v = jnp.full((1, TQ), INT_MIN, jnp.int32)
    v = jnp.where(count(lambda blk, pos: blk >= 0) >= k_sel, 0, v)

    def value_bit(it, v):
        cand = v | jnp.left_shift(jnp.int32(1), 30 - it)
        return jnp.where(count(lambda blk, pos: blk >= cand) >= k_sel, cand, v)

    v = lax.fori_loop(0, 31, value_bit, v)
    need = k_sel - count(lambda blk, pos: blk > v)
    n_ge = count(lambda blk, pos: blk >= v)

    def tie_search():
        def pos_bit(it, jm):
            cand = jm | jnp.left_shift(jnp.int32(1), pos_bits - 1 - it)
            return jnp.where(count(lambda blk, pos: (blk == v) & (pos < cand)) < need, cand, jm)

        return lax.fori_loop(0, pos_bits, pos_bit, jnp.zeros((1, TQ), jnp.int32))

    j_max = lax.cond(jnp.max(n_ge) > k_sel, tie_search, lambda: jnp.full((1, TQ), 2 ** 30, jnp.int32))

    q = (q_ref[...] * HEAD_DIM ** -0.5).astype(MXU)
    q_heads = [q[:, h * HEAD_DIM:(h + 1) * HEAD_DIM] for h in range(A_HEADS)]
    _softmax_reset(m_sc, l_sc, acc_sc)

    def chunk(j, band_row0, causal):
        r0 = pl.multiple_of(j * TQ, TQ)
        key = key_sc[pl.ds(r0, TQ), :]
        pos = r0 + sub_iota
        sel = (key > v) | ((key == v) & (pos <= j_max))
        if causal:
            sel = sel & (pos <= lane_t)
        madd = jnp.where(sel, 0.0, NEG)
        kv_c = kv_ref[pl.ds(r0, TQ), :].astype(MXU)
        _attend_chunk(kv_c, q_heads, lambda g: madd, band_ref, band_row0, m_sc, l_sc, acc_sc)

    def far_chunk(j, carry):
        chunk(j, None, False)
        return carry

    lax.fori_loop(0, jnp.maximum(i - 1, 0), far_chunk, 0)

    @pl.when(i >= 1)
    def _():
        chunk(i - 1, 0, False)

    chunk(i, TQ, True)
    for h in range(A_HEADS):
        out_sc[h * HEAD_DIM:(h + 1) * HEAD_DIM, :] = acc_sc[h] / l_sc[h:h + 1, :]
    o_ref[...] = out_sc[...].T


def _dsa_prompt(raw, band_a, B, S):
    nq = S // TQ
    k_sel = min(A_TOPK, S // 4)
    assert S % TQ == 0 and TQ >= k_sel
    tile = lambda b, i: (b * nq + i, 0)
    seq = lambda b, i: (b, 0)
    kern = functools.partial(_dsa_kernel, k_sel=k_sel, pos_bits=(S - 1).bit_length())
    return pl.pallas_call(
        kern,
        grid=(B, nq),
        in_specs=[pl.BlockSpec((TQ, IDX_HEADS * IDX_DIM), tile), pl.BlockSpec((TQ, LANES), tile),
                  pl.BlockSpec((TQ, A_HEADS * HEAD_DIM), tile), pl.BlockSpec((S, IDX_DIM), seq),
                  pl.BlockSpec((S, 2 * A_KV_HEADS * HEAD_DIM), seq),
                  pl.BlockSpec((A_HEADS, 2 * TQ, TQ), lambda b, i: (0, 0, 0))],
        out_specs=pl.BlockSpec((TQ, A_HEADS * HEAD_DIM), tile),
        out_shape=jax.ShapeDtypeStruct((B * S, A_HEADS * HEAD_DIM), F32),
        scratch_shapes=[pltpu.VMEM((S, TQ), jnp.int32), pltpu.VMEM((A_HEADS, TQ), F32), pltpu.VMEM((A_HEADS, TQ), F32),
                        pltpu.VMEM((A_HEADS, HEAD_DIM, TQ), F32), pltpu.VMEM((A_HEADS * HEAD_DIM, TQ), F32)],
        compiler_params=pltpu.CompilerParams(dimension_semantics=('arbitrary', 'arbitrary'),
                                             vmem_limit_bytes=VMEM_LIMIT),
        name='dsa_prompt',
    )(raw['a_iq'], raw['misc'], raw['a_q'], raw['a_ik'], raw['a_kv'], band_a)


def _compress_kernel(x_ref, pe_ref, w1_ref, w2_ref, o_ref):
    x = (x_ref[...] + pe_ref[...]).astype(MXU)
    h = jax.nn.gelu(jnp.dot(x, w1_ref[...], preferred_element_type=F32))
    o_ref[...] = jnp.dot(h.astype(MXU), w2_ref[...], preferred_element_type=F32)


def _compress_weights(pe, w1, w2):
    eye_c = jnp.eye(2, dtype=F32)
    eye_g = jnp.eye(B_KV_HEADS, dtype=F32)
    w1_big = jnp.einsum('lcde,cC,gG->lcgdCGe', w1, eye_c, eye_g)
    w1_big = w1_big.reshape(NSA_BLOCK * 2 * B_KV_HEADS * HEAD_DIM, 2 * B_KV_HEADS * CMP_HIDDEN)
    w2_big = jnp.einsum('cef,cC,gG->cgeCGf', w2, eye_c, eye_g)
    w2_big = w2_big.reshape(2 * B_KV_HEADS * CMP_HIDDEN, 2 * B_KV_HEADS * HEAD_DIM)
    pe_flat = jnp.broadcast_to(pe[:, :, None, :], (NSA_BLOCK, 2, B_KV_HEADS, HEAD_DIM)).reshape(1, -1)
    return pe_flat, w1_big.astype(MXU), w2_big.astype(MXU)


def _compress(blocks2d, pe_flat, w1_big, w2_big, tm):
    n, width = blocks2d.shape
    fixed = lambda i: (0, 0)
    return pl.pallas_call(
        _compress_kernel,
        grid=(n // tm,),
        in_specs=[pl.BlockSpec((tm, width), lambda i: (i, 0)), pl.BlockSpec((1, width), fixed),
                  pl.BlockSpec(w1_big.shape, fixed), pl.BlockSpec(w2_big.shape, fixed)],
        out_specs=pl.BlockSpec((tm, w2_big.shape[1]), lambda i: (i, 0)),
        out_shape=jax.ShapeDtypeStruct((n, w2_big.shape[1]), F32),
        compiler_params=pltpu.CompilerParams(dimension_semantics=('arbitrary',), vmem_limit_bytes=VMEM_LIMIT),
        name='nsa_compress',
    )(blocks2d, pe_flat, w1_big, w2_big)


def _nsa_kernel(q_ref, misc_ref, cmp_ref, selkv_ref, winkv_ref, band_ref, biasc_ref, o_ref,
                sel_sc, m_sc, l_sc, acc_sc, out_sc, *, nb, n_sel):
    i = pl.program_id(1)
    t0 = i * TQ
    n_g = B_KV_HEADS
    rep = B_HEADS // n_g
    lane_t = t0 + lax.broadcasted_iota(jnp.int32, (1, TQ), 1)
    sub_iota = lax.broadcasted_iota(jnp.int32, (TQ, TQ), 0)
    lane_iota = lax.broadcasted_iota(jnp.int32, (TQ, TQ), 1)
    misc_t = misc_ref[...].T
    gate = lambda h, k: misc_t[MISC_GATE_OFF + 3 * h + k:MISC_GATE_OFF + 3 * h + k + 1, :]
    q = (q_ref[...] * HEAD_DIM ** -0.5).astype(MXU)
    q_heads = [q[:, h * HEAD_DIM:(h + 1) * HEAD_DIM] for h in range(B_HEADS)]

    cmp = cmp_ref[...].astype(MXU)
    blk = lax.broadcasted_iota(jnp.int32, (nb, TQ), 0)
    visible = (blk + 1) * NSA_BLOCK - 1 <= lane_t
    cur = lax.shift_right_logical(lane_t, NSA_BLOCK.bit_length() - 1)
    forced = (blk == 0) | (blk == cur) | (blk == cur - 1)
    for g in range(n_g):
        k_c = cmp[:, g * HEAD_DIM:(g + 1) * HEAD_DIM]
        v_c = cmp[:, (n_g + g) * HEAD_DIM:(n_g + g + 1) * HEAD_DIM]
        importance = jnp.zeros((nb, TQ), F32)
        for r in range(rep):
            h = g * rep + r
            lc = lax.dot_general(k_c, q_heads[h], _NT, preferred_element_type=F32) + biasc_ref[h]
            lc = jnp.where(visible, lc, -jnp.inf)
            m = jnp.max(lc, axis=0, keepdims=True)
            m = jnp.where(m > -jnp.inf, m, 0.0)
            e = jnp.exp(lc - m)
            s = jnp.sum(e, axis=0, keepdims=True)
            p = e / jnp.where(s > 0, s, 1.0)
            importance = importance + p
            o_cmp = lax.dot_general(v_c, p.astype(MXU), _TN, preferred_element_type=F32)
            out_sc[h * HEAD_DIM:(h + 1) * HEAD_DIM, :] = gate(h, 0) * o_cmp
        score = jnp.where(forced, NSA_FORCE, importance)
        score = jnp.where(blk <= cur, score, -1.0)
        rank = jnp.zeros((nb, TQ), jnp.int32)
        for n in range(nb):
            row = score[n:n + 1, :]
            beats = (row > score) | ((row == score) & (blk > n))
            rank = rank + jnp.where(beats, 1, 0)
        sel_sc[g] = jnp.where(rank < n_sel, 0.0, NEG)

    def finish_branch(k):
        for h in range(B_HEADS):
            rows = slice(h * HEAD_DIM, (h + 1) * HEAD_DIM)
            out_sc[rows, :] = out_sc[rows, :] + gate(h, k) * (acc_sc[h] / l_sc[h:h + 1, :])

    _softmax_reset(m_sc, l_sc, acc_sc)
    blocks_per_chunk = TQ // NSA_BLOCK

    def sel_chunk(j, band_row0, causal):
        r0 = pl.multiple_of(j * TQ, TQ)
        kv_c = selkv_ref[pl.ds(r0, TQ), :].astype(MXU)

        def madd_of_group(g):
            rows = [sel_sc[g, pl.ds(j * blocks_per_chunk + b, 1), :] for b in range(blocks_per_chunk)]
            madd = jnp.concatenate([jnp.broadcast_to(row, (NSA_BLOCK, TQ)) for row in rows], axis=0)
            if causal:
                madd = jnp.where(sub_iota <= lane_iota, madd, NEG)
            return madd

        _attend_chunk(kv_c, q_heads, madd_of_group, band_ref, band_row0, m_sc, l_sc, acc_sc)

    def far_chunk(j, carry):
        sel_chunk(j, None, False)
        return carry

    lax.fori_loop(0, jnp.maximum(i - 1, 0), far_chunk, 0)

    @pl.when(i >= 1)
    def _():
        sel_chunk(i - 1, 0, False)

    sel_chunk(i, TQ, True)
    finish_branch(1)

    _softmax_reset(m_sc, l_sc, acc_sc)
    for back in range(WINDOW // TQ, -1, -1):
        dist = back * TQ + lane_iota - sub_iota
        madd = jnp.where((dist >= 0) & (dist < WINDOW), 0.0, NEG)
        band_row0 = {0: TQ, 1: 0}.get(back)

        def win_chunk(back=back, madd=madd, band_row0=band_row0):
            r0 = pl.multiple_of((i - back) * TQ, TQ)
            kv_c = winkv_ref[pl.ds(r0, TQ), :].astype(MXU)
            _attend_chunk(kv_c, q_heads, lambda g: madd, band_ref, band_row0, m_sc, l_sc, acc_sc)

        if back == 0:
            win_chunk()
        else:
            pl.when(i >= back)(win_chunk)
    finish_branch(2)
    o_ref[...] = out_sc[...].T


def _nsa_prompt(raw, cmp, band_b, bias_c, B, S):
    nq = S // TQ
    nb = S // NSA_BLOCK
    assert S % TQ == 0 and TQ % NSA_BLOCK == 0
    tile = lambda b, i: (b * nq + i, 0)
    seq = lambda b, i: (b, 0)
    kv_w = 2 * B_KV_HEADS * HEAD_DIM
    kern = functools.partial(_nsa_kernel, nb=nb, n_sel=min(NSA_TOPN, nb))
    return pl.pallas_call(
        kern,
        grid=(B, nq),
        in_specs=[pl.BlockSpec((TQ, B_HEADS * HEAD_DIM), tile), pl.BlockSpec((TQ, LANES), tile),
                  pl.BlockSpec((nb, kv_w), seq), pl.BlockSpec((S, kv_w), seq), pl.BlockSpec((S, kv_w), seq),
                  pl.BlockSpec((B_HEADS, 2 * TQ, TQ), lambda b, i: (0, 0, 0)),
                  pl.BlockSpec((B_HEADS, nb, TQ), lambda b, i: (0, 0, i))],
        out_specs=pl.BlockSpec((TQ, B_HEADS * HEAD_DIM), tile),
        out_shape=jax.ShapeDtypeStruct((B * S, B_HEADS * HEAD_DIM), F32),
        scratch_shapes=[pltpu.VMEM((B_KV_HEADS, nb, TQ), F32), pltpu.VMEM((B_HEADS, TQ), F32),
                        pltpu.VMEM((B_HEADS, TQ), F32), pltpu.VMEM((B_HEADS, HEAD_DIM, TQ), F32),
                        pltpu.VMEM((B_HEADS * HEAD_DIM, TQ), F32)],
        compiler_params=pltpu.CompilerParams(dimension_semantics=('arbitrary', 'arbitrary'),
                                             vmem_limit_bytes=VMEM_LIMIT),
        name='nsa_prompt',
    )(raw['b_q'], raw['misc'], cmp, raw['b_sel_kv'], raw['b_win_kv'], band_b, bias_c)


def _mix_prompt(pr, rel_table, cmp_pe, cmp_w1, cmp_w2):
    B, S = pr['a_q'].shape[:2]
    raw = pr['raw']
    table_a, table_b = rel_table[:, :A_HEADS], rel_table[:, A_HEADS:]
    pe_flat, w1_big, w2_big = _compress_weights(cmp_pe, cmp_w1, cmp_w2)
    n_blocks = B * S // NSA_BLOCK
    cmp = _compress(raw['b_cmp_kv'].reshape(n_blocks, -1), pe_flat, w1_big, w2_big, tm=min(128, n_blocks))
    o_a = _dsa_prompt(raw, _band_bias(table_a), B, S)
    o_b = _nsa_prompt(raw, cmp, _band_bias(table_b), _block_bias(table_b, S), B, S)
    wb = min(WINDOW, S)
    new_state = (pr['a_kv'], pr['a_ik'], pr['b_cmp_kv'], pr['b_sel_kv'], pr['b_win_kv'][:, S - wb:])
    return o_a, o_b, new_state


def _gather_pages(pool, page_table):
    rows = pool[page_table]
    return rows.reshape((page_table.shape[0], page_table.shape[1] * PAGE_SIZE) + pool.shape[2:])


def _paged_fetch(pool, new_rows, page_table, grouped):
    DB = page_table.shape[0]
    past_len = page_table.shape[1] * PAGE_SIZE
    n_new = new_rows.shape[1]

    def fetch(pos):
        b = jnp.arange(DB).reshape((DB,) + (1,) * (pos.ndim - 1))
        in_past = pos < past_len
        pp = jnp.where(in_past, pos, 0)
        page = page_table[b, pp // PAGE_SIZE]
        off = pp % PAGE_SIZE
        jn = jnp.clip(pos - past_len, 0, n_new - 1)
        if grouped:
            g = jnp.arange(B_KV_HEADS)[None, None, :, None]
            old = pool[page, off, :, g]
            new = new_rows[b, jn, :, g]
        else:
            old = pool[page, off]
            new = new_rows[b, jn]
        mask = in_past.reshape(in_past.shape + (1,) * (old.ndim - pos.ndim))
        return jnp.where(mask, old, new)

    return fetch


def _mix_sample(pr, cache_a_kv, cache_a_idx, cache_b_cmp_kv, cache_b_sel_kv, state_b_win_kv, page_table,
                rel_table, cmp_pe, cmp_w1, cmp_w2):
    T = pr['a_q'].shape[1]
    past_len = page_table.shape[1] * PAGE_SIZE
    q_pos = past_len + jnp.arange(T, dtype=jnp.int32)
    ik_all = jnp.concatenate([_gather_pages(cache_a_idx, page_table), pr['a_ik']], axis=1)
    cmp_rows = jnp.concatenate([_gather_pages(cache_b_cmp_kv, page_table), pr['b_cmp_kv']], axis=1)
    cmp = _nsa_compress(cmp_rows, cmp_pe, cmp_w1, cmp_w2)
    fetch_a = _paged_fetch(cache_a_kv, pr['a_kv'], page_table, grouped=False)
    fetch_b = _paged_fetch(cache_b_sel_kv, pr['b_sel_kv'], page_table, grouped=True)
    win_all = jnp.concatenate([state_b_win_kv, pr['b_win_kv']], axis=1)
    wb = state_b_win_kv.shape[1]
    win_pos = past_len - wb + jnp.arange(wb + T, dtype=jnp.int32)
    o_a = _dsa_attend(pr['a_q'], pr['a_iq'], pr['a_iw'], q_pos, ik_all, fetch_a, rel_table[:, :A_HEADS])
    o_b = _nsa_attend(pr['b_q'], pr['b_gate'], q_pos, cmp, fetch_b, win_all, win_pos, rel_table[:, A_HEADS:])
    new_state = (pr['a_kv'], pr['a_ik'], pr['b_cmp_kv'], pr['b_sel_kv'], win_all[:, T:])
    return o_a, o_b, new_state


PEER_SLOTS = PEER_HEADS * PEER_TOPK
ROUTE_TOKENS = 256
EXPERT_TOKENS = 128
GATHER_DEPTH = 8


def _take_topk(cur, pos_iota, k, payload=None):
    n = cur.shape[0]
    vals, picks = [], []
    for _ in range(k):
        m = jnp.max(cur, axis=0, keepdims=True)
        pos = jnp.min(jnp.where(cur == m, pos_iota, n), axis=0, keepdims=True)
        hit = pos_iota == pos
        vals.append(m)
        picks.append(pos if payload is None else jnp.sum(jnp.where(hit, payload, 0), axis=0, keepdims=True))
        cur = jnp.where(hit, -jnp.inf, cur)
    return vals, picks


def _peer_route_kernel(h_ref, wpq_ref, keys_ref, ids_ref, g_ref):
    tn = h_ref.shape[0]
    half = PEER_QDIM // 2
    q = jnp.dot(h_ref[...].astype(MXU), wpq_ref[...], preferred_element_type=F32).astype(MXU)
    key_iota = lax.broadcasted_iota(jnp.int32, (PEER_NKEYS, tn), 0)
    pair_iota = lax.broadcasted_iota(jnp.int32, (PEER_TOPK * PEER_TOPK, tn), 0)
    id_rows, g_rows = [], []
    for hd in range(PEER_HEADS):
        vals, idxs = [], []
        for p in range(2):
            c0 = (hd * 2 + p) * half
            s = lax.dot_general(keys_ref[hd * 2 + p], q[:, c0:c0 + half], _NT, preferred_element_type=F32)
            v_rows, i_rows = _take_topk(s, key_iota, PEER_TOPK)
            vals.append(v_rows)
            idxs.append(i_rows)
        v1 = jnp.concatenate(vals[1], axis=0)
        i1 = jnp.concatenate(idxs[1], axis=0)
        cand = jnp.concatenate([vals[0][a] + v1 for a in range(PEER_TOPK)], axis=0)
        expert = jnp.concatenate([idxs[0][a] * PEER_NKEYS + i1 for a in range(PEER_TOPK)], axis=0)
        cv, ce = _take_topk(cand, pair_iota, PEER_TOPK, payload=expert)
        cv = jnp.concatenate(cv, axis=0)
        e = jnp.exp(cv - cv[0:1, :])
        g_rows.append(e / jnp.sum(e, axis=0, keepdims=True))
        id_rows.append(jnp.concatenate(ce, axis=0))
    ids_ref[...] = jnp.concatenate(id_rows, axis=0).astype(F32).T.astype(jnp.int32)
    g_ref[...] = jnp.concatenate(g_rows, axis=0).T


def _peer_route(h2d, w_pq, sub_keys):
    n = h2d.shape[0]
    tn = ROUTE_TOKENS
    wpq = w_pq.reshape(D_MODEL, PEER_HEADS * PEER_QDIM).astype(MXU)
    keys = sub_keys.reshape(PEER_HEADS * 2, PEER_NKEYS, PEER_QDIM // 2).astype(MXU)
    return pl.pallas_call(
        _peer_route_kernel,
        grid=(n // tn,),
        in_specs=[pl.BlockSpec((tn, D_MODEL), lambda i: (i, 0)), pl.BlockSpec(wpq.shape, lambda i: (0, 0)),
                  pl.BlockSpec(keys.shape, lambda i: (0, 0, 0))],
        out_specs=[pl.BlockSpec((tn, PEER_SLOTS), lambda i: (i, 0)), pl.BlockSpec((tn, PEER_SLOTS), lambda i: (i, 0))],
        out_shape=[jax.ShapeDtypeStruct((n, PEER_SLOTS), jnp.int32), jax.ShapeDtypeStruct((n, PEER_SLOTS), F32)],
        compiler_params=pltpu.CompilerParams(dimension_semantics=('arbitrary',), vmem_limit_bytes=VMEM_LIMIT),
        name='peer_route',
    )(h2d, wpq, keys)


def _peer_expert_kernel(ids_ref, g_ref, h_ref, lng_ref, lnb_ref, uv_hbm, y_ref, buf, sem, gt_sc, out_sc):
    tb = h_ref.shape[0]

    def row_copy(t, slot, k):
        return pltpu.make_async_copy(uv_hbm.at[pl.ds(ids_ref[t, k], 1), :], buf.at[slot, pl.ds(k, 1), :],
                                     sem.at[slot])

    def issue(t, slot):
        for k in range(PEER_SLOTS):
            row_copy(t, slot, k).start(priority=k % 2)

    def wait(t, slot):
        for k in range(PEER_SLOTS):
            row_copy(t, slot, k).wait()

    gt_sc[...] = g_ref[...].T
    lane = lax.broadcasted_iota(jnp.int32, (PEER_SLOTS, tb), 1)
    for t in range(GATHER_DEPTH - 1):
        issue(t, t)

    def body(t, carry):
        nxt = t + GATHER_DEPTH - 1

        @pl.when(nxt < tb)
        def _():
            issue(nxt, nxt % GATHER_DEPTH)

        slot = t % GATHER_DEPTH
        wait(t, slot)
        x = h_ref[pl.ds(t, 1), :]
        act = jnp.sum(buf[slot, :, :D_MODEL] * x, axis=1, keepdims=True)
        g_col = jnp.sum(jnp.where(lane == t, gt_sc[...], 0.0), axis=1, keepdims=True)
        coef = g_col * jax.nn.gelu(act)
        out_sc[pl.ds(t, 1), :] = jnp.sum(buf[slot, :, D_MODEL:] * coef, axis=0, keepdims=True)
        return carry

    lax.fori_loop(0, tb, body, 0)
    y_ref[...] = _layer_norm(DN_ALPHA * h_ref[...] + out_sc[...], lng_ref[...], lnb_ref[...])


def _peer_experts(h2d, ids, g, uv, ln_g, ln_b):
    n = h2d.shape[0]
    tb = EXPERT_TOKENS
    row = lambda i: (i, 0)
    fixed = lambda i: (0, 0)
    return pl.pallas_call(
        _peer_expert_kernel,
        grid=(n // tb,),
        in_specs=[pl.BlockSpec((tb, PEER_SLOTS), row, memory_space=pltpu.SMEM), pl.BlockSpec((tb, PEER_SLOTS), row),
                  pl.BlockSpec((tb, D_MODEL), row), pl.BlockSpec((1, D_MODEL), fixed),
                  pl.BlockSpec((1, D_MODEL), fixed), pl.BlockSpec(memory_space=pl.ANY)],
        out_specs=pl.BlockSpec((tb, D_MODEL), row),
        out_shape=jax.ShapeDtypeStruct((n, D_MODEL), F32),
        scratch_shapes=[pltpu.VMEM((GATHER_DEPTH, PEER_SLOTS, 2 * D_MODEL), F32),
                        pltpu.SemaphoreType.DMA((GATHER_DEPTH,)), pltpu.VMEM((PEER_SLOTS, tb), F32),
                        pltpu.VMEM((tb, D_MODEL), F32)],
        compiler_params=pltpu.CompilerParams(dimension_semantics=('arbitrary',), vmem_limit_bytes=VMEM_LIMIT),
        name='peer_experts',
    )(ids, g, h2d, ln_g.reshape(1, D_MODEL), ln_b.reshape(1, D_MODEL), uv)


def _group_step(x, mixer, ws, w_up_a, w_up_b, w_out, ln1_g, ln1_b, w_pq, sub_keys, uv, ln2_g, ln2_b, tm):
    B, T, _ = x.shape
    x2d = x.reshape(B * T, D_MODEL)
    pr = _unpack_proj(_project(x2d, ws, tm), B, T)
    o_a, o_b, new_state = mixer(pr)
    h = _merge(x2d, o_a.reshape(B * T, -1), o_b.reshape(B * T, -1), pr['merge_gate'],
               w_up_a, w_up_b, w_out, ln1_g, ln1_b, tm)
    ids, g = _peer_route(h, w_pq, sub_keys)
    y = _peer_experts(h, ids, g, uv, ln2_g, ln2_b)
    return y.reshape(B, T, D_MODEL), new_state


def kernel(x_prompt, x_sample, cache_a_kv, cache_a_idx, cache_b_cmp_kv, cache_b_sel_kv, state_b_win_kv, page_table, w_in, rel_bias_table, cmp_pe, cmp_w1, cmp_w2, w_up_a, w_up_b, w_out, ln1_g, ln1_b, w_pq, peer_sub_keys, peer_u, peer_v, ln2_g, ln2_b):
    l = 0
    ws = _split_w_in(w_in[l])
    uv = jnp.concatenate([peer_u[l], peer_v[l]], axis=1)
    tail = (w_up_a[l], w_up_b[l], w_out[l], ln1_g[l], ln1_b[l], w_pq[l], peer_sub_keys[l], uv, ln2_g[l], ln2_b[l])
    mix_p = functools.partial(_mix_prompt, rel_table=rel_bias_table, cmp_pe=cmp_pe[l], cmp_w1=cmp_w1[l],
                              cmp_w2=cmp_w2[l])
    y_p, st_p = _group_step(x_prompt, mix_p, ws, *tail, tm=256)
    mix_s = functools.partial(_mix_sample, cache_a_kv=cache_a_kv[l], cache_a_idx=cache_a_idx[l],
                              cache_b_cmp_kv=cache_b_cmp_kv[l], cache_b_sel_kv=cache_b_sel_kv[l],
                              state_b_win_kv=state_b_win_kv[l], page_table=page_table, rel_table=rel_bias_table,
                              cmp_pe=cmp_pe[l], cmp_w1=cmp_w1[l], cmp_w2=cmp_w2[l])
    y_s, st_s = _group_step(x_sample, mix_s, ws, *tail, tm=256)
    return (y_p, y_s) + tuple(s[None] for s in st_p) + tuple(s[None] for s in st_s)
```

```python
import functools
import math

import jax
import jax.numpy as jnp
import numpy as np
from jax import lax
from jax.experimental import pallas as pl
from jax.experimental.pallas import tpu as pltpu

D_MODEL = 1024
DEPTH = 1
PAGE_SIZE = 128
HEAD_DIM = 64
A_HEADS = 8
A_KV_HEADS = 2
IDX_HEADS = 4
IDX_DIM = 64
A_TOPK = 256
B_HEADS = 8
B_KV_HEADS = 2
NSA_BLOCK = 64
NSA_TOPN = 16
NSA_FORCE = 8.0
CMP_HIDDEN = 64
WINDOW = 512
REL_BUCKETS = 32
REL_MAX_EXACT = 16
REL_MAX_DIST = 128
PEER_HEADS = 8
PEER_NKEYS = 128
PEER_QDIM = 256
PEER_TOPK = 16
Q_BLOCK = 32
TOKEN_BLOCK = 256
LN_EPS = 1e-5
DN_ALPHA = (2 * DEPTH) ** 0.25

PROJ_LAYOUT = (
    ('a_q', A_HEADS * HEAD_DIM),
    ('a_kv', 2 * A_KV_HEADS * HEAD_DIM),
    ('a_iq', IDX_HEADS * IDX_DIM),
    ('a_iw', IDX_HEADS),
    ('a_ik', IDX_DIM),
    ('b_q', B_HEADS * HEAD_DIM),
    ('b_cmp_kv', 2 * B_KV_HEADS * HEAD_DIM),
    ('b_sel_kv', 2 * B_KV_HEADS * HEAD_DIM),
    ('b_win_kv', 2 * B_KV_HEADS * HEAD_DIM),
    ('b_gate', B_HEADS * 3),
    ('merge_gate', 2 * D_MODEL),
)

LANES = 128
VMEM_LIMIT = 56 * 1024 * 1024
MISC_GATE_OFF = IDX_HEADS

F32 = jnp.float32
BF16 = jnp.bfloat16
MXU = jnp.bfloat16


_PROJ_OUT = ('a_q', 'a_kv', 'a_iq', 'a_ik', 'misc', 'b_q', 'b_cmp_kv', 'b_sel_kv', 'b_win_kv', 'merge_gate')


def _proj_kernel(x_ref, *refs):
    n = len(_PROJ_OUT)
    w_refs, o_refs = refs[:n], refs[n:]
    x = x_ref[...].astype(BF16)
    for name, w_ref, o_ref in zip(_PROJ_OUT, w_refs, o_refs):
        r = jnp.dot(x, w_ref[...], preferred_element_type=F32)
        if name == 'misc':
            lane = lax.broadcasted_iota(jnp.int32, r.shape, 1)
            r = jnp.where(lane < MISC_GATE_OFF, r * IDX_HEADS ** -0.5, jax.nn.sigmoid(r))
        elif name == 'merge_gate':
            r = jax.nn.sigmoid(r)
        o_ref[...] = r


def _split_w_in(w_in):
    parts = {}
    off = 0
    for name, width in PROJ_LAYOUT:
        parts[name] = w_in[:, off:off + width]
        off += width
    misc = jnp.concatenate([parts['a_iw'], parts['b_gate']], axis=1)
    parts['misc'] = jnp.pad(misc, ((0, 0), (0, LANES - misc.shape[1])))
    return [parts[name].astype(BF16) for name in _PROJ_OUT]


def _project(x2d, ws, tm):
    n = x2d.shape[0]
    widths = [w.shape[1] for w in ws]
    return pl.pallas_call(
        _proj_kernel,
        grid=(n // tm,),
        in_specs=[pl.BlockSpec((tm, D_MODEL), lambda i: (i, 0))]
        + [pl.BlockSpec((D_MODEL, wd), lambda i: (0, 0)) for wd in widths],
        out_specs=[pl.BlockSpec((tm, wd), lambda i: (i, 0)) for wd in widths],
        out_shape=[jax.ShapeDtypeStruct((n, wd), F32) for wd in widths],
        compiler_params=pltpu.CompilerParams(dimension_semantics=('arbitrary',), vmem_limit_bytes=VMEM_LIMIT),
        name='project',
    )(x2d, *ws)


def _unpack_proj(outs, B, T):
    d = dict(zip(_PROJ_OUT, outs))
    misc = d['misc']
    kv = (B, T, 2, A_KV_HEADS, HEAD_DIM)
    return {
        'a_q': d['a_q'].reshape(B, T, A_HEADS, HEAD_DIM),
        'a_kv': d['a_kv'].reshape(kv),
        'a_iq': d['a_iq'].reshape(B, T, IDX_HEADS, IDX_DIM),
        'a_iw': misc[:, :IDX_HEADS].reshape(B, T, IDX_HEADS),
        'a_ik': d['a_ik'].reshape(B, T, IDX_DIM),
        'b_q': d['b_q'].reshape(B, T, B_HEADS, HEAD_DIM),
        'b_cmp_kv': d['b_cmp_kv'].reshape(kv),
        'b_sel_kv': d['b_sel_kv'].reshape(kv),
        'b_win_kv': d['b_win_kv'].reshape(kv),
        'b_gate': misc[:, MISC_GATE_OFF:MISC_GATE_OFF + 3 * B_HEADS].reshape(B, T, B_HEADS, 3),
        'merge_gate': d['merge_gate'],
        'raw': d,
    }


def _layer_norm(x, g, b):
    mu = jnp.mean(x, axis=-1, keepdims=True)
    xc = x - mu
    var = jnp.mean(xc * xc, axis=-1, keepdims=True)
    return xc * lax.rsqrt(var + LN_EPS) * g + b


def _merge_kernel(x_ref, oa_ref, ob_ref, mg_ref, wa_ref, wb_ref, wo_ref, g_ref, b_ref, h_ref):
    ya = jnp.dot(oa_ref[...].astype(BF16), wa_ref[...], preferred_element_type=F32)
    yb = jnp.dot(ob_ref[...].astype(BF16), wb_ref[...], preferred_element_type=F32)
    mg = mg_ref[...]
    mix = mg[:, :D_MODEL] * ya + mg[:, D_MODEL:] * yb
    mixed = jnp.dot(mix.astype(BF16), wo_ref[...], preferred_element_type=F32)
    h_ref[...] = _layer_norm(DN_ALPHA * x_ref[...] + mixed, g_ref[...], b_ref[...])


def _merge(x2d, o_a, o_b, mg, w_up_a, w_up_b, w_out, ln_g, ln_b, tm):
    n = x2d.shape[0]
    ca, cb = o_a.shape[1], o_b.shape[1]
    row = lambda i: (i, 0)
    fixed = lambda i: (0, 0)
    return pl.pallas_call(
        _merge_kernel,
        grid=(n // tm,),
        in_specs=[pl.BlockSpec((tm, D_MODEL), row), pl.BlockSpec((tm, ca), row), pl.BlockSpec((tm, cb), row),
                  pl.BlockSpec((tm, 2 * D_MODEL), row), pl.BlockSpec((ca, D_MODEL), fixed),
                  pl.BlockSpec((cb, D_MODEL), fixed), pl.BlockSpec((D_MODEL, D_MODEL), fixed),
                  pl.BlockSpec((1, D_MODEL), fixed), pl.BlockSpec((1, D_MODEL), fixed)],
        out_specs=pl.BlockSpec((tm, D_MODEL), row),
        out_shape=jax.ShapeDtypeStruct((n, D_MODEL), F32),
        compiler_params=pltpu.CompilerParams(dimension_semantics=('arbitrary',), vmem_limit_bytes=VMEM_LIMIT),
        name='merge',
    )(x2d, o_a, o_b, mg, w_up_a.astype(BF16), w_up_b.astype(BF16), w_out.astype(BF16),
      ln_g.reshape(1, D_MODEL), ln_b.reshape(1, D_MODEL))


def _masked_softmax(logits, mask):
    l = jnp.where(mask, logits.astype(F32), -jnp.inf)
    m = jnp.max(l, axis=-1, keepdims=True)
    m = jnp.where(jnp.isfinite(m), m, 0.0)
    e = jnp.exp(l - m)
    s = jnp.sum(e, axis=-1, keepdims=True)
    return e / jnp.where(s > 0, s, 1.0)


def _rel_bucket(d):
    n = jnp.maximum(d, 0)
    nf = jnp.maximum(n, 1).astype(F32)
    large = REL_MAX_EXACT + (jnp.log(nf / REL_MAX_EXACT) / math.log(REL_MAX_DIST / REL_MAX_EXACT)
                             * (REL_BUCKETS - REL_MAX_EXACT)).astype(jnp.int32)
    return jnp.where(n < REL_MAX_EXACT, n, jnp.minimum(large, REL_BUCKETS - 1))


def _rel_bias(table, d):
    return table[_rel_bucket(d)]


def _dsa_attend(q, iq, iw, q_pos, ik_all, fetch_kv, table_a):
    B, T = q.shape[:2]
    R = A_HEADS // A_KV_HEADS
    L = ik_all.shape[1]
    k_sel = min(A_TOPK, L // 4)
    k_pos = jnp.arange(L, dtype=jnp.int32)
    rel = jax.nn.relu(jnp.einsum('bthd,bsd->bths', iq, ik_all) * IDX_DIM ** -0.5)
    score = jnp.einsum('bths,bth->bts', rel, iw).astype(F32)
    score = jnp.where((k_pos[None, :] <= q_pos[:, None])[None], score, -jnp.inf)
    _, sel = lax.top_k(score, k_sel)
    kv = fetch_kv(sel)
    qg = q.reshape(B, T, A_KV_HEADS, R, HEAD_DIM)
    logits = jnp.einsum('btgrd,btkgd->btgrk', qg, kv[..., 0, :, :]) * HEAD_DIM ** -0.5
    bias = jnp.moveaxis(_rel_bias(table_a, q_pos[None, :, None] - sel), -1, 2).reshape(B, T, A_KV_HEADS, R, k_sel)
    valid = (sel <= q_pos[None, :, None])[:, :, None, None, :]
    p = _masked_softmax(logits + bias, valid)
    o = jnp.einsum('btgrk,btkgd->btgrd', p.astype(kv.dtype), kv[..., 1, :, :])
    return o.reshape(B, T, A_HEADS * HEAD_DIM)


def _nsa_compress(kv, pe, w1, w2):
    B, L = kv.shape[:2]
    nb = -(-L // NSA_BLOCK)
    kv = jnp.pad(kv, ((0, 0), (0, nb * NSA_BLOCK - L), (0, 0), (0, 0), (0, 0)))
    blocks = kv.reshape(B, nb, NSA_BLOCK, 2, B_KV_HEADS, HEAD_DIM) + pe[None, None, :, :, None, :]
    h = jax.nn.gelu(jnp.einsum('bnlcgd,lcde->bncge', blocks, w1))
    return jnp.einsum('bncge,cef->bncgf', h, w2)


def _nsa_attend(q, gate, q_pos, cmp, fetch_sel, win_kv, win_pos, table_b):
    B, T = q.shape[:2]
    G = B_KV_HEADS
    R = B_HEADS // B_KV_HEADS
    nb = cmp.shape[1]
    scale = HEAD_DIM ** -0.5
    qg = q.reshape(B, T, G, R, HEAD_DIM)
    blk = jnp.arange(nb, dtype=jnp.int32)
    blk_end = (blk + 1) * NSA_BLOCK - 1
    lc = jnp.einsum('btgrd,bngd->btgrn', qg, cmp[:, :, 0]) * scale
    bias_c = jnp.moveaxis(_rel_bias(table_b, q_pos[:, None] - blk_end[None, :]), -1, 1).reshape(T, G, R, nb)
    p_cmp = _masked_softmax(lc + bias_c[None], (blk_end[None, :] <= q_pos[:, None])[None, :, None, None, :])
    o_cmp = jnp.einsum('btgrn,bngd->btgrd', p_cmp.astype(q.dtype), cmp[:, :, 1])
    importance = jnp.sum(p_cmp, axis=3)
    cur = q_pos // NSA_BLOCK
    eligible = blk[None, :] <= cur[:, None]
    forced = (blk[None, :] == 0) | (blk[None, :] == cur[:, None]) | (blk[None, :] == cur[:, None] - 1)
    score = jnp.where(forced[None, :, None, :], NSA_FORCE, importance)
    score = jnp.where(eligible[None, :, None, :], score, -1.0)
    n_sel = min(NSA_TOPN, nb)
    _, sel = lax.top_k(score, n_sel)
    pos = (sel[..., None] * NSA_BLOCK + jnp.arange(NSA_BLOCK, dtype=jnp.int32)).reshape(B, T, G, n_sel * NSA_BLOCK)
    kv = fetch_sel(pos)
    ls = jnp.einsum('btgrd,btgkd->btgrk', qg, kv[..., 0, :]) * scale
    tab = table_b.reshape(REL_BUCKETS, G, R)
    bias_s = tab[_rel_bucket(q_pos[None, :, None, None] - pos), jnp.arange(G)[None, None, :, None]]
    bias_s = jnp.moveaxis(bias_s, -1, 3)
    p_slc = _masked_softmax(ls + bias_s, (pos <= q_pos[None, :, None, None])[:, :, :, None, :])
    o_slc = jnp.einsum('btgrk,btgkd->btgrd', p_slc.astype(kv.dtype), kv[..., 1, :])
    Kw = win_kv.shape[1]
    lw = jnp.einsum('btgrd,bkgd->btgrk', qg, win_kv[:, :, 0]) * scale
    dw = q_pos[:, None] - win_pos[None, :]
    bias_w = jnp.moveaxis(_rel_bias(table_b, dw), -1, 1).reshape(T, G, R, Kw)
    valid_w = (dw >= 0) & (dw < WINDOW) & (win_pos[None, :] >= 0)
    p_win = _masked_softmax(lw + bias_w[None], valid_w[None, :, None, None, :])
    o_win = jnp.einsum('btgrk,bkgd->btgrd', p_win.astype(win_kv.dtype), win_kv[:, :, 1])
    g = gate.reshape(B, T, G, R, 3)
    o = g[..., 0:1] * o_cmp + g[..., 1:2] * o_slc + g[..., 2:3] * o_win
    return o.reshape(B, T, B_HEADS * HEAD_DIM)


TQ = 256
NEG = -1e30
INT_MIN = -2 ** 31
COUNT_ROWS = 64
_NT = (((1,), (1,)), ((), ()))
_TN = (((0,), (0,)), ((), ()))


def _bucket_np(d):
    n = np.maximum(d, 0)
    nf = np.maximum(n, 1).astype(np.float64)
    large = REL_MAX_EXACT + (np.log(nf / REL_MAX_EXACT) / math.log(REL_MAX_DIST / REL_MAX_EXACT)
                             * (REL_BUCKETS - REL_MAX_EXACT)).astype(np.int64)
    return np.where(n < REL_MAX_EXACT, n, np.minimum(large, REL_BUCKETS - 1)).astype(np.int32)


def _band_bias(table):
    d = TQ + np.arange(TQ)[None, :] - np.arange(2 * TQ)[:, None]
    band = jnp.take(table, jnp.asarray(_bucket_np(d)), axis=0)
    return jnp.moveaxis(band - table[REL_BUCKETS - 1], -1, 0)


def _block_bias(table, S):
    nb = S // NSA_BLOCK
    d = np.arange(S)[None, :] - ((np.arange(nb) + 1) * NSA_BLOCK - 1)[:, None]
    return jnp.moveaxis(jnp.take(table, jnp.asarray(_bucket_np(d)), axis=0), -1, 0)


def _ordered_keys(x):
    x = jnp.where(x == 0.0, 0.0, x)
    b = lax.bitcast_convert_type(x, jnp.int32)
    return b ^ ((b >> 31) & jnp.int32(0x7FFFFFFF))


def _softmax_reset(m_sc, l_sc, acc_sc):
    m_sc[...] = jnp.full(m_sc.shape, NEG, F32)
    l_sc[...] = jnp.zeros(l_sc.shape, F32)
    acc_sc[...] = jnp.zeros(acc_sc.shape, F32)


def _softmax_update(h, logits, v_c, m_sc, l_sc, acc_sc):
    m_old = m_sc[h:h + 1, :]
    m_new = jnp.maximum(m_old, jnp.max(logits, axis=0, keepdims=True))
    alpha = jnp.exp(m_old - m_new)
    p = jnp.exp(logits - m_new)
    l_sc[h:h + 1, :] = alpha * l_sc[h:h + 1, :] + jnp.sum(p, axis=0, keepdims=True)
    pv = lax.dot_general(v_c, p.astype(MXU), _TN, preferred_element_type=F32)
    acc_sc[h] = alpha * acc_sc[h] + pv
    m_sc[h:h + 1, :] = m_new


def _attend_chunk(kv_c, q_heads, madd_of_group, band_ref, band_row0, m_sc, l_sc, acc_sc):
    n_g = B_KV_HEADS
    rep = len(q_heads) // n_g
    for g in range(n_g):
        k_c = kv_c[:, g * HEAD_DIM:(g + 1) * HEAD_DIM]
        v_c = kv_c[:, (n_g + g) * HEAD_DIM:(n_g + g + 1) * HEAD_DIM]
        madd = madd_of_group(g)
        for r in range(rep):
            h = g * rep + r
            logits = lax.dot_general(k_c, q_heads[h], _NT, preferred_element_type=F32) + madd
            if band_row0 is not None:
                logits = logits + band_ref[h, band_row0:band_row0 + TQ, :]
            _softmax_update(h, logits, v_c, m_sc, l_sc, acc_sc)


def _dsa_kernel(iq_ref, misc_ref, q_ref, ik_ref, kv_ref, band_ref, o_ref, key_sc, m_sc, l_sc, acc_sc, out_sc,
                *, k_sel, pos_bits):
    i = pl.program_id(1)
    t0 = i * TQ
    n_chunks = i + 1
    lane_t = t0 + lax.broadcasted_iota(jnp.int32, (1, TQ), 1)
    sub_iota = lax.broadcasted_iota(jnp.int32, (TQ, TQ), 0)
    misc_t = misc_ref[...].T
    iq = iq_ref[...].astype(MXU)
    iq_heads = [iq[:, h * IDX_DIM:(h + 1) * IDX_DIM] for h in range(IDX_HEADS)]

    def score_chunk(j, carry):
        r0 = pl.multiple_of(j * TQ, TQ)
        ik_c = ik_ref[pl.ds(r0, TQ), :].astype(MXU)
        sc = jnp.zeros((TQ, TQ), F32)
        for h in range(IDX_HEADS):
            rel = lax.dot_general(ik_c, iq_heads[h], _NT, preferred_element_type=F32)
            sc = sc + jnp.maximum(rel * IDX_DIM ** -0.5, 0.0) * misc_t[h:h + 1, :]
        sc = jnp.where(r0 + sub_iota <= lane_t, sc, -jnp.inf)
        key_sc[pl.ds(r0, TQ), :] = _ordered_keys(sc)
        return carry

    lax.fori_loop(0, n_chunks, score_chunk, 0)

    def count(pred):
        def body(jj, acc):
            r0 = pl.multiple_of(jj * COUNT_ROWS, COUNT_ROWS)
            blk = key_sc[pl.ds(r0, COUNT_ROWS), :]
            pos = r0 + lax.broadcasted_iota(jnp.int32, (COUNT_ROWS, TQ), 0)
            hit = jnp.where(pred(blk, pos), 1, 0)
            return acc + jnp.sum(hit.reshape(COUNT_ROWS // 8, 8, TQ), axis=0)

        acc = lax.fori_loop(0, n_chunks * (TQ // COUNT_ROWS), body, jnp.zeros((8, TQ), jnp.int32))
        return jnp.sum(acc, axis=0, keepdims=True)

    v = jnp.full((1, TQ), INT_MIN, jnp.int32)
    v = jnp.where(count(lambda blk, pos: blk >= 0) >= k_sel, 0, v)

    def value_bit(it, v):
        cand = v | jnp.left_shift(jnp.int32(1), 30 - it)
        return jnp.where(count(lambda blk, pos: blk >= cand) >= k_sel, cand, v)

    v = lax.fori_loop(0, 31, value_bit, v)
    need = k_sel - count(lambda blk, pos: blk > v)
    n_ge = count(lambda blk, pos: blk >= v)

    def tie_search():
        def pos_bit(it, jm):
            cand = jm | jnp.left_shift(jnp.int32(1), pos_bits - 1 - it)
            return jnp.where(count(lambda blk, pos: (blk == v) & (pos < cand)) < need, cand, jm)

        return lax.fori_loop(0, pos_bits, pos_bit, jnp.zeros((1, TQ), jnp.int32))

    j_max = lax.cond(jnp.max(n_ge) > k_sel, tie_search, lambda: jnp.full((1, TQ), 2 ** 30, jnp.int32))

    q = (q_ref[...] * HEAD_DIM ** -0.5).astype(MXU)
    q_heads = [q[:, h * HEAD_DIM:(h + 1) * HEAD_DIM] for h in range(A_HEADS)]
    _softmax_reset(m_sc, l_sc, acc_sc)

    def chunk(j, band_row0, causal):
        r0 = pl.multiple_of(j * TQ, TQ)
        key = key_sc[pl.ds(r0, TQ), :]
        pos = r0 + sub_iota
        sel = (key > v) | ((key == v) & (pos <= j_max))
        if causal:
            sel = sel & (pos <= lane_t)
        madd = jnp.where(sel, 0.0, NEG)
        kv_c = kv_ref[pl.ds(r0, TQ), :].astype(MXU)
        _attend_chunk(kv_c, q_heads, lambda g: madd, band_ref, band_row0, m_sc, l_sc, acc_sc)

    def far_chunk(j, carry):
        chunk(j, None, False)
        return carry

    lax.fori_loop(0, jnp.maximum(i - 1, 0), far_chunk, 0)

    @pl.when(i >= 1)
    def _():
        chunk(i - 1, 0, False)

    chunk(i, TQ, True)
    for h in range(A_HEADS):
        out_sc[h * HEAD_DIM:(h + 1) * HEAD_DIM, :] = acc_sc[h] / l_sc[h:h + 1, :]
    o_ref[...] = out_sc[...].T


def _dsa_prompt(raw, band_a, B, S):
    nq = S // TQ
    k_sel = min(A_TOPK, S // 4)
    assert S % TQ == 0 and TQ >= k_sel
    tile = lambda b, i: (b * nq + i, 0)
    seq = lambda b, i: (b, 0)
    kern = functools.partial(_dsa_kernel, k_sel=k_sel, pos_bits=(S - 1).bit_length())
    return pl.pallas_call(
        kern,
        grid=(B, nq),
        in_specs=[pl.BlockSpec((TQ, IDX_HEADS * IDX_DIM), tile), pl.BlockSpec((TQ, LANES), tile),
                  pl.BlockSpec((TQ, A_HEADS * HEAD_DIM), tile), pl.BlockSpec((S, IDX_DIM), seq),
                  pl.BlockSpec((S, 2 * A_KV_HEADS * HEAD_DIM), seq),
                  pl.BlockSpec((A_HEADS, 2 * TQ, TQ), lambda b, i: (0, 0, 0))],
        out_specs=pl.BlockSpec((TQ, A_HEADS * HEAD_DIM), tile),
        out_shape=jax.ShapeDtypeStruct((B * S, A_HEADS * HEAD_DIM), F32),
        scratch_shapes=[pltpu.VMEM((S, TQ), jnp.int32), pltpu.VMEM((A_HEADS, TQ), F32), pltpu.VMEM((A_HEADS, TQ), F32),
                        pltpu.VMEM((A_HEADS, HEAD_DIM, TQ), F32), pltpu.VMEM((A_HEADS * HEAD_DIM, TQ), F32)],
        compiler_params=pltpu.CompilerParams(dimension_semantics=('arbitrary', 'arbitrary'),
                                             vmem_limit_bytes=VMEM_LIMIT),
        name='dsa_prompt',
    )(raw['a_iq'], raw['misc'], raw['a_q'], raw['a_ik'], raw['a_kv'], band_a)


def _compress_kernel(x_ref, pe_ref, w1_ref, w2_ref, o_ref):
    x = (x_ref[...] + pe_ref[...]).astype(MXU)
    h = jax.nn.gelu(jnp.dot(x, w1_ref[...], preferred_element_type=F32))
    o_ref[...] = jnp.dot(h.astype(MXU), w2_ref[...], preferred_element_type=F32)


def _compress_weights(pe, w1, w2):
    eye_c = jnp.eye(2, dtype=F32)
    eye_g = jnp.eye(B_KV_HEADS, dtype=F32)
    w1_big = jnp.einsum('lcde,cC,gG->lcgdCGe', w1, eye_c, eye_g)
    w1_big = w1_big.reshape(NSA_BLOCK * 2 * B_KV_HEADS * HEAD_DIM, 2 * B_KV_HEADS * CMP_HIDDEN)
    w2_big = jnp.einsum('cef,cC,gG->cgeCGf', w2, eye_c, eye_g)
    w2_big = w2_big.reshape(2 * B_KV_HEADS * CMP_HIDDEN, 2 * B_KV_HEADS * HEAD_DIM)
    pe_flat = jnp.broadcast_to(pe[:, :, None, :], (NSA_BLOCK, 2, B_KV_HEADS, HEAD_DIM)).reshape(1, -1)
    return pe_flat, w1_big.astype(MXU), w2_big.astype(MXU)


def _compress(blocks2d, pe_flat, w1_big, w2_big, tm):
    n, width = blocks2d.shape
    fixed = lambda i: (0, 0)
    return pl.pallas_call(
        _compress_kernel,
        grid=(n // tm,),
        in_specs=[pl.BlockSpec((tm, width), lambda i: (i, 0)), pl.BlockSpec((1, width), fixed),
                  pl.BlockSpec(w1_big.shape, fixed), pl.BlockSpec(w2_big.shape, fixed)],
        out_specs=pl.BlockSpec((tm, w2_big.shape[1]), lambda i: (i, 0)),
        out_shape=jax.ShapeDtypeStruct((n, w2_big.shape[1]), F32),
        compiler_params=pltpu.CompilerParams(dimension_semantics=('arbitrary',), vmem_limit_bytes=VMEM_LIMIT),
        name='nsa_compress',
    )(blocks2d, pe_flat, w1_big, w2_big)


def _nsa_kernel(q_ref, misc_ref, cmp_ref, selkv_ref, winkv_ref, band_ref, biasc_ref, o_ref,
                sel_sc, m_sc, l_sc, acc_sc, out_sc, *, nb, n_sel):
    i = pl.program_id(1)
    t0 = i * TQ
    n_g = B_KV_HEADS
    rep = B_HEADS // n_g
    lane_t = t0 + lax.broadcasted_iota(jnp.int32, (1, TQ), 1)
    sub_iota = lax.broadcasted_iota(jnp.int32, (TQ, TQ), 0)
    lane_iota = lax.broadcasted_iota(jnp.int32, (TQ, TQ), 1)
    misc_t = misc_ref[...].T
    gate = lambda h, k: misc_t[MISC_GATE_OFF + 3 * h + k:MISC_GATE_OFF + 3 * h + k + 1, :]
    q = (q_ref[...] * HEAD_DIM ** -0.5).astype(MXU)
    q_heads = [q[:, h * HEAD_DIM:(h + 1) * HEAD_DIM] for h in range(B_HEADS)]

    cmp = cmp_ref[...].astype(MXU)
    blk = lax.broadcasted_iota(jnp.int32, (nb, TQ), 0)
    visible = (blk + 1) * NSA_BLOCK - 1 <= lane_t
    cur = lax.shift_right_logical(lane_t, NSA_BLOCK.bit_length() - 1)
    forced = (blk == 0) | (blk == cur) | (blk == cur - 1)
    for g in range(n_g):
        k_c = cmp[:, g * HEAD_DIM:(g + 1) * HEAD_DIM]
        v_c = cmp[:, (n_g + g) * HEAD_DIM:(n_g + g + 1) * HEAD_DIM]
        importance = jnp.zeros((nb, TQ), F32)
        for r in range(rep):
            h = g * rep + r
            lc = lax.dot_general(k_c, q_heads[h], _NT, preferred_element_type=F32) + biasc_ref[h]
            lc = jnp.where(visible, lc, -jnp.inf)
            m = jnp.max(lc, axis=0, keepdims=True)
            m = jnp.where(m > -jnp.inf, m, 0.0)
            e = jnp.exp(lc - m)
            s = jnp.sum(e, axis=0, keepdims=True)
            p = e / jnp.where(s > 0, s, 1.0)
            importance = importance + p
            o_cmp = lax.dot_general(v_c, p.astype(MXU), _TN, preferred_element_type=F32)
            out_sc[h * HEAD_DIM:(h + 1) * HEAD_DIM, :] = gate(h, 0) * o_cmp
        score = jnp.where(forced, NSA_FORCE, importance)
        score = jnp.where(blk <= cur, score, -1.0)
        rank = jnp.zeros((nb, TQ), jnp.int32)
        for n in range(nb):
            row = score[n:n + 1, :]
            beats = (row > score) | ((row == score) & (blk > n))
            rank = rank + jnp.where(beats, 1, 0)
        sel_sc[g] = jnp.where(rank < n_sel, 0.0, NEG)

    def finish_branch(k):
        for h in range(B_HEADS):
            rows = slice(h * HEAD_DIM, (h + 1) * HEAD_DIM)
            out_sc[rows, :] = out_sc[rows, :] + gate(h, k) * (acc_sc[h] / l_sc[h:h + 1, :])

    _softmax_reset(m_sc, l_sc, acc_sc)
    blocks_per_chunk = TQ // NSA_BLOCK

    def sel_chunk(j, band_row0, causal):
        r0 = pl.multiple_of(j * TQ, TQ)
        kv_c = selkv_ref[pl.ds(r0, TQ), :].astype(MXU)

        def madd_of_group(g):
            rows = [sel_sc[g, pl.ds(j * blocks_per_chunk + b, 1), :] for b in range(blocks_per_chunk)]
            madd = jnp.concatenate([jnp.broadcast_to(row, (NSA_BLOCK, TQ)) for row in rows], axis=0)
            if causal:
                madd = jnp.where(sub_iota <= lane_iota, madd, NEG)
            return madd

        _attend_chunk(kv_c, q_heads, madd_of_group, band_ref, band_row0, m_sc, l_sc, acc_sc)

    def far_chunk(j, carry):
        sel_chunk(j, None, False)
        return carry

    lax.fori_loop(0, jnp.maximum(i - 1, 0), far_chunk, 0)

    @pl.when(i >= 1)
    def _():
        sel_chunk(i - 1, 0, False)

    sel_chunk(i, TQ, True)
    finish_branch(1)

    _softmax_reset(m_sc, l_sc, acc_sc)
    for back in range(WINDOW // TQ, -1, -1):
        dist = back * TQ + lane_iota - sub_iota
        madd = jnp.where((dist >= 0) & (dist < WINDOW), 0.0, NEG)
        band_row0 = {0: TQ, 1: 0}.get(back)

        def win_chunk(back=back, madd=madd, band_row0=band_row0):
            r0 = pl.multiple_of((i - back) * TQ, TQ)
            kv_c = winkv_ref[pl.ds(r0, TQ), :].astype(MXU)
            _attend_chunk(kv_c, q_heads, lambda g: madd, band_ref, band_row0, m_sc, l_sc, acc_sc)

        if back == 0:
            win_chunk()
        else:
            pl.when(i >= back)(win_chunk)
    finish_branch(2)
    o_ref[...] = out_sc[...].T


def _nsa_prompt(raw, cmp, band_b, bias_c, B, S):
    nq = S // TQ
    nb = S // NSA_BLOCK
    assert S % TQ == 0 and TQ % NSA_BLOCK == 0
    tile = lambda b, i: (b * nq + i, 0)
    seq = lambda b, i: (b, 0)
    kv_w = 2 * B_KV_HEADS * HEAD_DIM
    kern = functools.partial(_nsa_kernel, nb=nb, n_sel=min(NSA_TOPN, nb))
    return pl.pallas_call(
        kern,
        grid=(B, nq),
        in_specs=[pl.BlockSpec((TQ, B_HEADS * HEAD_DIM), tile), pl.BlockSpec((TQ, LANES), tile),
                  pl.BlockSpec((nb, kv_w), seq), pl.BlockSpec((S, kv_w), seq), pl.BlockSpec((S, kv_w), seq),
                  pl.BlockSpec((B_HEADS, 2 * TQ, TQ), lambda b, i: (0, 0, 0)),
                  pl.BlockSpec((B_HEADS, nb, TQ), lambda b, i: (0, 0, i))],
        out_specs=pl.BlockSpec((TQ, B_HEADS * HEAD_DIM), tile),
        out_shape=jax.ShapeDtypeStruct((B * S, B_HEADS * HEAD_DIM), F32),
        scratch_shapes=[pltpu.VMEM((B_KV_HEADS, nb, TQ), F32), pltpu.VMEM((B_HEADS, TQ), F32),
                        pltpu.VMEM((B_HEADS, TQ), F32), pltpu.VMEM((B_HEADS, HEAD_DIM, TQ), F32),
                        pltpu.VMEM((B_HEADS * HEAD_DIM, TQ), F32)],
        compiler_params=pltpu.CompilerParams(dimension_semantics=('arbitrary', 'arbitrary'),
                                             vmem_limit_bytes=VMEM_LIMIT),
        name='nsa_prompt',
    )(raw['b_q'], raw['misc'], cmp, raw['b_sel_kv'], raw['b_win_kv'], band_b, bias_c)


def _mix_prompt(pr, rel_table, cmp_pe, cmp_w1, cmp_w2):
    B, S = pr['a_q'].shape[:2]
    raw = pr['raw']
    table_a, table_b = rel_table[:, :A_HEADS], rel_table[:, A_HEADS:]
    pe_flat, w1_big, w2_big = _compress_weights(cmp_pe, cmp_w1, cmp_w2)
    n_blocks = B * S // NSA_BLOCK
    cmp = _compress(raw['b_cmp_kv'].reshape(n_blocks, -1), pe_flat, w1_big, w2_big, tm=min(128, n_blocks))
    o_a = _dsa_prompt(raw, _band_bias(table_a), B, S)
    o_b = _nsa_prompt(raw, cmp, _band_bias(table_b), _block_bias(table_b, S), B, S)
    wb = min(WINDOW, S)
    new_state = (pr['a_kv'], pr['a_ik'], pr['b_cmp_kv'], pr['b_sel_kv'], pr['b_win_kv'][:, S - wb:])
    return o_a, o_b, new_state


def _gather_pages(pool, page_table):
    rows = pool[page_table]
    return rows.reshape((page_table.shape[0], page_table.shape[1] * PAGE_SIZE) + pool.shape[2:])


def _paged_fetch(pool, new_rows, page_table, grouped):
    DB = page_table.shape[0]
    past_len = page_table.shape[1] * PAGE_SIZE
    n_new = new_rows.shape[1]

    def fetch(pos):
        b = jnp.arange(DB).reshape((DB,) + (1,) * (pos.ndim - 1))
        in_past = pos < past_len
        pp = jnp.where(in_past, pos, 0)
        page = page_table[b, pp // PAGE_SIZE]
        off = pp % PAGE_SIZE
        jn = jnp.clip(pos - past_len, 0, n_new - 1)
        if grouped:
            g = jnp.arange(B_KV_HEADS)[None, None, :, None]
            old = pool[page, off, :, g]
            new = new_rows[b, jn, :, g]
        else:
            old = pool[page, off]
            new = new_rows[b, jn]
        mask = in_past.reshape(in_past.shape + (1,) * (old.ndim - pos.ndim))
        return jnp.where(mask, old, new)

    return fetch


PAGES_PER_STEP = 8
KEY_PAD = 128


def _paged_specs(width):
    def spec(j):
        return pl.BlockSpec((1, PAGE_SIZE, width), lambda s, p, pt: (pt[s, p * PAGES_PER_STEP + j], 0, 0))
    return [spec(j) for j in range(PAGES_PER_STEP)]


def _stage_pages(page_refs, dst_sc):
    p = pl.program_id(1)
    for j, ref in enumerate(page_refs):
        r0 = pl.multiple_of((p * PAGES_PER_STEP + j) * PAGE_SIZE, PAGE_SIZE)
        dst_sc[pl.ds(r0, PAGE_SIZE), :] = ref[0]


def _stage_new_rows(new_ref, dst_sc, past):
    t = new_ref.shape[0]
    dst_sc[past:past + t, :] = new_ref[...]
    dst_sc[past + t:, :] = jnp.zeros((dst_sc.shape[0] - past - t, dst_sc.shape[1]), F32)


def _rows_softmax_attend(q_rows, k, v, bias, madd):
    logits = lax.dot_general(q_rows, k, _NT, preferred_element_type=F32) + bias + madd
    m = jnp.max(logits, axis=1, keepdims=True)
    e = jnp.exp(logits - m)
    p = e / jnp.sum(e, axis=1, keepdims=True)
    return jnp.dot(p.astype(MXU), v, preferred_element_type=F32)


def _stack_heads(x, heads, width):
    return jnp.concatenate([x[:, h * width:(h + 1) * width] for h in heads], axis=0)


def _dsa_sample_kernel(pt_ref, *refs, past, k_sel):
    n = PAGES_PER_STEP
    idx_pages, kv_pages = refs[:n], refs[n:2 * n]
    ikn_ref, kvn_ref, iq_ref, misc_ref, q_ref, bias_ref, o_ref, ik_sc, kv_sc = refs[2 * n:]
    _stage_pages(idx_pages, ik_sc)
    _stage_pages(kv_pages, kv_sc)

    @pl.when(pl.program_id(1) == pl.num_programs(1) - 1)
    def _():
        T = iq_ref.shape[0]
        L = ik_sc.shape[0]
        _stage_new_rows(ikn_ref, ik_sc, past)
        _stage_new_rows(kvn_ref, kv_sc, past)
        pos = lax.broadcasted_iota(jnp.int32, (T, L), 1)
        valid = pos <= past + lax.broadcasted_iota(jnp.int32, (T, L), 0)
        misc = misc_ref[...]
        iq = _stack_heads(iq_ref[...].astype(MXU), range(IDX_HEADS), IDX_DIM)
        rel = lax.dot_general(iq, ik_sc[...].astype(MXU), _NT, preferred_element_type=F32)
        sc = jnp.zeros((T, L), F32)
        for h in range(IDX_HEADS):
            sc = sc + jnp.maximum(rel[h * T:(h + 1) * T] * IDX_DIM ** -0.5, 0.0) * misc[:, h:h + 1]
        key = _ordered_keys(jnp.where(valid, sc, -jnp.inf))

        def count(hit):
            return jnp.sum(jnp.where(hit, 1, 0), axis=1, keepdims=True)

        v = jnp.full((T, 1), INT_MIN, jnp.int32)
        v = jnp.where(count(key >= 0) >= k_sel, 0, v)

        def value_bit(it, v):
            cand = v | jnp.left_shift(jnp.int32(1), 30 - it)
            return jnp.where(count(key >= cand) >= k_sel, cand, v)

        v = lax.fori_loop(0, 31, value_bit, v)
        need = k_sel - count(key > v)
        pos_bits = (L - 1).bit_length()

        def tie_search():
            def pos_bit(it, jm):
                cand = jm | jnp.left_shift(jnp.int32(1), pos_bits - 1 - it)
                return jnp.where(count((key == v) & (pos < cand)) < need, cand, jm)

            return lax.fori_loop(0, pos_bits, pos_bit, jnp.zeros((T, 1), jnp.int32))

        j_max = lax.cond(jnp.max(count(key >= v)) > k_sel, tie_search,
                         lambda: jnp.full((T, 1), 2 ** 30, jnp.int32))
        sel = ((key > v) | ((key == v) & (pos <= j_max))) & valid
        madd = jnp.where(sel, 0.0, NEG)
        rep = A_HEADS // A_KV_HEADS
        madd = jnp.concatenate([madd] * rep, axis=0)
        q = (q_ref[...] * HEAD_DIM ** -0.5).astype(MXU)
        kv = kv_sc[...].astype(MXU)
        for g in range(A_KV_HEADS):
            q_rows = _stack_heads(q, range(g * rep, (g + 1) * rep), HEAD_DIM)
            o = _rows_softmax_attend(q_rows, kv[:, g * HEAD_DIM:(g + 1) * HEAD_DIM],
                                     kv[:, (A_KV_HEADS + g) * HEAD_DIM:(A_KV_HEADS + g + 1) * HEAD_DIM],
                                     bias_ref[g], madd)
            for r in range(rep):
                h = g * rep + r
                o_ref[:, h * HEAD_DIM:(h + 1) * HEAD_DIM] = o[r * T:(r + 1) * T, :]


def _sample_bias(table, T, past, n_keys, key_pos0, n_groups):
    d = (past + np.arange(T))[:, None] - (key_pos0 + np.arange(n_keys))[None, :]
    b = jnp.take(table, jnp.asarray(_bucket_np(d)), axis=0)
    b = jnp.moveaxis(b, -1, 0)
    return b.reshape(n_groups, -1, n_keys)


def _seq_rows(width, T):
    return pl.BlockSpec((T, width), lambda s, p, pt: (s, 0))


def _dsa_sample(raw, pool_idx, pool_kv, page_table, table_a, DB, T):
    n_pages = page_table.shape[1]
    past = n_pages * PAGE_SIZE
    L = past + KEY_PAD
    assert n_pages % PAGES_PER_STEP == 0 and T <= KEY_PAD
    kv_w = 2 * A_KV_HEADS * HEAD_DIM
    bias = _sample_bias(table_a, T, past, L, 0, A_KV_HEADS)
    fixed3 = lambda s, p, pt: (0, 0, 0)
    grid_spec = pltpu.PrefetchScalarGridSpec(
        num_scalar_prefetch=1,
        grid=(DB, n_pages // PAGES_PER_STEP),
        in_specs=_paged_specs(IDX_DIM) + _paged_specs(kv_w)
        + [_seq_rows(IDX_DIM, T), _seq_rows(kv_w, T), _seq_rows(IDX_HEADS * IDX_DIM, T), _seq_rows(LANES, T),
           _seq_rows(A_HEADS * HEAD_DIM, T), pl.BlockSpec(bias.shape, fixed3)],
        out_specs=_seq_rows(A_HEADS * HEAD_DIM, T),
        scratch_shapes=[pltpu.VMEM((L, IDX_DIM), F32), pltpu.VMEM((L, kv_w), F32)],
    )
    kern = functools.partial(_dsa_sample_kernel, past=past, k_sel=min(A_TOPK, (past + T) // 4))
    return pl.pallas_call(
        kern, grid_spec=grid_spec,
        out_shape=jax.ShapeDtypeStruct((DB * T, A_HEADS * HEAD_DIM), F32),
        compiler_params=pltpu.CompilerParams(dimension_semantics=('arbitrary', 'arbitrary'),
                                             vmem_limit_bytes=VMEM_LIMIT),
        name='dsa_sample',
    )(page_table, *([pool_idx] * PAGES_PER_STEP), *([pool_kv] * PAGES_PER_STEP),
      raw['a_ik'], raw['a_kv'], raw['a_iq'], raw['misc'], raw['a_q'], bias)


def _compress_sample_kernel(pt_ref, *refs, past):
    n = PAGES_PER_STEP
    pages = refs[:n]
    new_ref, pe_ref, w1_ref, w2_ref, o_ref, xk_sc, xv_sc = refs[n:]
    halves = (xk_sc, xv_sc)
    p = pl.program_id(1)
    for j, ref in enumerate(pages):
        r0 = pl.multiple_of((p * PAGES_PER_STEP + j) * PAGE_SIZE, PAGE_SIZE)
        for c, x_sc in enumerate(halves):
            x_sc[pl.ds(r0, PAGE_SIZE), :] = ref[0, :, c * LANES:(c + 1) * LANES]

    @pl.when(p == pl.num_programs(1) - 1)
    def _():
        nbp = o_ref.shape[1]
        t = new_ref.shape[0]
        for c, x_sc in enumerate(halves):
            x_sc[past:past + t, :] = new_ref[:, c * LANES:(c + 1) * LANES]
            x_sc[past + t:, :] = jnp.zeros((x_sc.shape[0] - past - t, LANES), F32)

        def body(l, accs):
            out = []
            for c, (x_sc, acc) in enumerate(zip(halves, accs)):
                xl = x_sc[pl.ds(l, nbp, stride=NSA_BLOCK), :] + pe_ref[c, pl.ds(l, 1), :]
                out.append(acc + jnp.dot(xl.astype(MXU), w1_ref[l, c], preferred_element_type=F32))
            return tuple(out)

        zero = jnp.zeros((nbp, LANES), F32)
        h = jnp.concatenate(lax.fori_loop(0, NSA_BLOCK, body, (zero, zero)), axis=1)
        o_ref[0] = jnp.dot(jax.nn.gelu(h).astype(MXU), w2_ref[...], preferred_element_type=F32)


def _padded_blocks(past, T):
    nb = -(-(past + T) // NSA_BLOCK)
    return nb, -(-nb // 8) * 8


def _compress_sample(raw, pool_cmp, page_table, pe_flat, w1_big, w2_big, DB, T):
    n_pages = page_table.shape[1]
    past = n_pages * PAGE_SIZE
    _, nbp = _padded_blocks(past, T)
    kv_w = 2 * B_KV_HEADS * HEAD_DIM
    assert kv_w == 2 * LANES and w1_big.shape[1] == 2 * LANES
    w1_l = w1_big.reshape(NSA_BLOCK, 2, LANES, 2, LANES)
    w1_halves = jnp.stack([w1_l[:, 0, :, 0, :], w1_l[:, 1, :, 1, :]], axis=1)
    fixed2 = lambda s, p, pt: (0, 0)
    grid_spec = pltpu.PrefetchScalarGridSpec(
        num_scalar_prefetch=1,
        grid=(DB, n_pages // PAGES_PER_STEP),
        in_specs=_paged_specs(kv_w) + [_seq_rows(kv_w, T),
                                       pl.BlockSpec((2, NSA_BLOCK, LANES), lambda s, p, pt: (0, 0, 0)),
                                       pl.BlockSpec(w1_halves.shape, lambda s, p, pt: (0, 0, 0, 0)),
                                       pl.BlockSpec(w2_big.shape, fixed2)],
        out_specs=pl.BlockSpec((1, nbp, kv_w), lambda s, p, pt: (s, 0, 0)),
        scratch_shapes=[pltpu.VMEM((nbp * NSA_BLOCK, LANES), F32), pltpu.VMEM((nbp * NSA_BLOCK, LANES), F32)],
    )
    return pl.pallas_call(
        functools.partial(_compress_sample_kernel, past=past), grid_spec=grid_spec,
        out_shape=jax.ShapeDtypeStruct((DB, nbp, kv_w), F32),
        compiler_params=pltpu.CompilerParams(dimension_semantics=('arbitrary', 'arbitrary'),
                                             vmem_limit_bytes=VMEM_LIMIT),
        name='nsa_compress_sample',
    )(page_table, *([pool_cmp] * PAGES_PER_STEP), raw['b_cmp_kv'],
      pe_flat.reshape(NSA_BLOCK, 2, LANES).swapaxes(0, 1), w1_halves, w2_big)


def _nsa_sample_kernel(pt_ref, *refs, past, nb, n_sel):
    n = PAGES_PER_STEP
    pages = refs[:n]
    (seln_ref, winn_ref, q_ref, misc_ref, cmp_ref, win_ref, expand_ref, bias_c_ref, bias_s_ref, bias_w_ref,
     o_ref, kv_sc, win_sc) = refs[n:]
    _stage_pages(pages, kv_sc)

    @pl.when(pl.program_id(1) == pl.num_programs(1) - 1)
    def _():
        T = q_ref.shape[0]
        L = kv_sc.shape[0]
        n_g = B_KV_HEADS
        rep = B_HEADS // n_g
        wb = win_ref.shape[1]
        _stage_new_rows(seln_ref, kv_sc, past)
        win_sc[0:wb, :] = win_ref[0]
        _stage_new_rows(winn_ref, win_sc, wb)
        misc = misc_ref[...]
        gate = lambda h, k: misc[:, MISC_GATE_OFF + 3 * h + k:MISC_GATE_OFF + 3 * h + k + 1]
        q = (q_ref[...] * HEAD_DIM ** -0.5).astype(MXU)
        cmp = cmp_ref[0].astype(MXU)
        nbp = cmp.shape[0]
        t_col = lax.broadcasted_iota(jnp.int32, (T, 1), 0)
        q_pos = past + t_col
        blk = lax.broadcasted_iota(jnp.int32, (T, nbp), 1)
        cur = lax.shift_right_logical(q_pos, NSA_BLOCK.bit_length() - 1)
        visible = (blk + 1) * NSA_BLOCK - 1 <= q_pos
        visible_rows = jnp.concatenate([visible] * rep, axis=0)
        forced = (blk == 0) | (blk == cur) | (blk == cur - 1)
        pos = lax.broadcasted_iota(jnp.int32, (T, L), 1)
        causal = pos <= q_pos
        kv = kv_sc[...].astype(MXU)
        win = win_sc[...].astype(MXU)
        wpos = lax.broadcasted_iota(jnp.int32, (T, win.shape[0]), 1)
        dw = wb + t_col - wpos
        madd_w = jnp.where((dw >= 0) & (dw < WINDOW), 0.0, NEG)
        madd_w = jnp.concatenate([madd_w] * rep, axis=0)
        outs = []
        for g in range(n_g):
            heads = range(g * rep, (g + 1) * rep)
            ks = slice(g * HEAD_DIM, (g + 1) * HEAD_DIM)
            vs = slice((n_g + g) * HEAD_DIM, (n_g + g + 1) * HEAD_DIM)
            q_rows = _stack_heads(q, heads, HEAD_DIM)
            lc = lax.dot_general(q_rows, cmp[:, ks], _NT, preferred_element_type=F32) + bias_c_ref[g]
            lc = jnp.where(visible_rows, lc, -jnp.inf)
            m = jnp.max(lc, axis=1, keepdims=True)
            m = jnp.where(m > -jnp.inf, m, 0.0)
            e = jnp.exp(lc - m)
            s = jnp.sum(e, axis=1, keepdims=True)
            p = e / jnp.where(s > 0, s, 1.0)
            o_cmp = jnp.dot(p.astype(MXU), cmp[:, vs], preferred_element_type=F32)
            importance = p[0:T]
            for r in range(1, rep):
                importance = importance + p[r * T:(r + 1) * T]
            score = jnp.where(forced, NSA_FORCE, importance)
            score = jnp.where(blk <= cur, score, -1.0)
            rank = jnp.zeros((T, nbp), jnp.int32)
            for b in range(nb):
                col = score[:, b:b + 1]
                rank = rank + jnp.where((col > score) | ((col == score) & (blk > b)), 1, 0)
            picked = jnp.where(rank < n_sel, 1.0, 0.0).astype(MXU)
            on_keys = jnp.dot(picked, expand_ref[...], preferred_element_type=F32)
            madd_s = jnp.where((on_keys > 0.5) & causal, 0.0, NEG)
            madd_s = jnp.concatenate([madd_s] * rep, axis=0)
            o_slc = _rows_softmax_attend(q_rows, kv[:, ks], kv[:, vs], bias_s_ref[g], madd_s)
            o_win = _rows_softmax_attend(q_rows, win[:, ks], win[:, vs], bias_w_ref[g], madd_w)
            for r, h in enumerate(heads):
                rows = slice(r * T, (r + 1) * T)
                outs.append(gate(h, 0) * o_cmp[rows] + gate(h, 1) * o_slc[rows] + gate(h, 2) * o_win[rows])
        o_ref[...] = jnp.concatenate(outs, axis=1)


def _nsa_sample(raw, cmp, pool_sel, win_state, page_table, table_b, DB, T):
    n_pages = page_table.shape[1]
    past = n_pages * PAGE_SIZE
    L = past + KEY_PAD
    nb, nbp = _padded_blocks(past, T)
    wb = win_state.shape[1]
    lw = wb + KEY_PAD
    kv_w = 2 * B_KV_HEADS * HEAD_DIM
    n_g = B_KV_HEADS
    expand = jnp.asarray(np.arange(L)[None, :] // NSA_BLOCK == np.arange(nbp)[:, None], dtype=MXU)
    blk_end = (np.arange(nbp) + 1) * NSA_BLOCK - 1
    d_c = (past + np.arange(T))[:, None] - blk_end[None, :]
    bias_c = jnp.moveaxis(jnp.take(table_b, jnp.asarray(_bucket_np(d_c)), axis=0), -1, 0).reshape(n_g, -1, nbp)
    bias_s = _sample_bias(table_b, T, past, L, 0, n_g)
    bias_w = _sample_bias(table_b, T, past, lw, past - wb, n_g)
    fixed2 = lambda s, p, pt: (0, 0)
    fixed3 = lambda s, p, pt: (0, 0, 0)
    per_seq3 = lambda s, p, pt: (s, 0, 0)
    grid_spec = pltpu.PrefetchScalarGridSpec(
        num_scalar_prefetch=1,
        grid=(DB, n_pages // PAGES_PER_STEP),
        in_specs=_paged_specs(kv_w)
        + [_seq_rows(kv_w, T), _seq_rows(kv_w, T), _seq_rows(B_HEADS * HEAD_DIM, T), _seq_rows(LANES, T),
           pl.BlockSpec((1, nbp, kv_w), per_seq3), pl.BlockSpec((1, wb, kv_w), per_seq3),
           pl.BlockSpec(expand.shape, fixed2), pl.BlockSpec(bias_c.shape, fixed3),
           pl.BlockSpec(bias_s.shape, fixed3), pl.BlockSpec(bias_w.shape, fixed3)],
        out_specs=_seq_rows(B_HEADS * HEAD_DIM, T),
        scratch_shapes=[pltpu.VMEM((L, kv_w), F32), pltpu.VMEM((lw, kv_w), F32)],
    )
    kern = functools.partial(_nsa_sample_kernel, past=past, nb=nb, n_sel=min(NSA_TOPN, nb))
    return pl.pallas_call(
        kern, grid_spec=grid_spec,
        out_shape=jax.ShapeDtypeStruct((DB * T, B_HEADS * HEAD_DIM), F32),
        compiler_params=pltpu.CompilerParams(dimension_semantics=('arbitrary', 'arbitrary'),
                                             vmem_limit_bytes=VMEM_LIMIT),
        name='nsa_sample',
    )(page_table, *([pool_sel] * PAGES_PER_STEP), raw['b_sel_kv'], raw['b_win_kv'], raw['b_q'], raw['misc'],
      cmp, win_state, expand, bias_c, bias_s, bias_w)


def _mix_sample(pr, cache_a_kv, cache_a_idx, cache_b_cmp_kv, cache_b_sel_kv, state_b_win_kv, page_table,
                rel_table, cmp_pe, cmp_w1, cmp_w2):
    DB, T = pr['a_q'].shape[:2]
    raw = pr['raw']
    n_pool = cache_a_kv.shape[0]
    kv_w = 2 * A_KV_HEADS * HEAD_DIM
    table_a, table_b = rel_table[:, :A_HEADS], rel_table[:, A_HEADS:]
    pe_flat, w1_big, w2_big = _compress_weights(cmp_pe, cmp_w1, cmp_w2)
    win_state = state_b_win_kv.reshape(DB, -1, kv_w)
    o_a = _dsa_sample(raw, cache_a_idx, cache_a_kv.reshape(n_pool, PAGE_SIZE, kv_w), page_table, table_a, DB, T)
    cmp = _compress_sample(raw, cache_b_cmp_kv.reshape(n_pool, PAGE_SIZE, kv_w), page_table, pe_flat, w1_big,
                           w2_big, DB, T)
    o_b = _nsa_sample(raw, cmp, cache_b_sel_kv.reshape(n_pool, PAGE_SIZE, kv_w), win_state, page_table, table_b,
                      DB, T)
    win_all = jnp.concatenate([state_b_win_kv, pr['b_win_kv']], axis=1)
    new_state = (pr['a_kv'], pr['a_ik'], pr['b_cmp_kv'], pr['b_sel_kv'], win_all[:, T:])
    return o_a, o_b, new_state


PEER_SLOTS = PEER_HEADS * PEER_TOPK
ROUTE_TOKENS = 256
EXPERT_TOKENS = 128
GATHER_DEPTH = 8


def _take_topk(cur, pos_iota, k, payload=None):
    n = cur.shape[0]
    vals, picks = [], []
    for _ in range(k):
        m = jnp.max(cur, axis=0, keepdims=True)
        pos = jnp.min(jnp.where(cur == m, pos_iota, n), axis=0, keepdims=True)
        hit = pos_iota == pos
        vals.append(m)
        picks.append(pos if payload is None else jnp.sum(jnp.where(hit, payload, 0), axis=0, keepdims=True))
        cur = jnp.where(hit, -jnp.inf, cur)
    return vals, picks


def _peer_route_kernel(h_ref, wpq_ref, keys_ref, ids_ref, g_ref):
    tn = h_ref.shape[0]
    half = PEER_QDIM // 2
    q = jnp.dot(h_ref[...].astype(MXU), wpq_ref[...], preferred_element_type=F32).astype(MXU)
    key_iota = lax.broadcasted_iota(jnp.int32, (PEER_NKEYS, tn), 0)
    pair_iota = lax.broadcasted_iota(jnp.int32, (PEER_TOPK * PEER_TOPK, tn), 0)
    id_rows, g_rows = [], []
    for hd in range(PEER_HEADS):
        vals, idxs = [], []
        for p in range(2):
            c0 = (hd * 2 + p) * half
            s = lax.dot_general(keys_ref[hd * 2 + p], q[:, c0:c0 + half], _NT, preferred_element_type=F32)
            v_rows, i_rows = _take_topk(s, key_iota, PEER_TOPK)
            vals.append(v_rows)
            idxs.append(i_rows)
        v1 = jnp.concatenate(vals[1], axis=0)
        i1 = jnp.concatenate(idxs[1], axis=0)
        cand = jnp.concatenate([vals[0][a] + v1 for a in range(PEER_TOPK)], axis=0)
        expert = jnp.concatenate([idxs[0][a] * PEER_NKEYS + i1 for a in range(PEER_TOPK)], axis=0)
        cv, ce = _take_topk(cand, pair_iota, PEER_TOPK, payload=expert)
        cv = jnp.concatenate(cv, axis=0)
        e = jnp.exp(cv - cv[0:1, :])
        g_rows.append(e / jnp.sum(e, axis=0, keepdims=True))
        id_rows.append(jnp.concatenate(ce, axis=0))
    ids_ref[...] = jnp.concatenate(id_rows, axis=0).astype(F32).T.astype(jnp.int32)
    g_ref[...] = jnp.concatenate(g_rows, axis=0).T


def _peer_route(h2d, w_pq, sub_keys):
    n = h2d.shape[0]
    tn = ROUTE_TOKENS
    wpq = w_pq.reshape(D_MODEL, PEER_HEADS * PEER_QDIM).astype(MXU)
    keys = sub_keys.reshape(PEER_HEADS * 2, PEER_NKEYS, PEER_QDIM // 2).astype(MXU)
    return pl.pallas_call(
        _peer_route_kernel,
        grid=(n // tn,),
        in_specs=[pl.BlockSpec((tn, D_MODEL), lambda i: (i, 0)), pl.BlockSpec(wpq.shape, lambda i: (0, 0)),
                  pl.BlockSpec(keys.shape, lambda i: (0, 0, 0))],
        out_specs=[pl.BlockSpec((tn, PEER_SLOTS), lambda i: (i, 0)), pl.BlockSpec((tn, PEER_SLOTS), lambda i: (i, 0))],
        out_shape=[jax.ShapeDtypeStruct((n, PEER_SLOTS), jnp.int32), jax.ShapeDtypeStruct((n, PEER_SLOTS), F32)],
        compiler_params=pltpu.CompilerParams(dimension_semantics=('arbitrary',), vmem_limit_bytes=VMEM_LIMIT),
        name='peer_route',
    )(h2d, wpq, keys)


def _peer_expert_kernel(ids_ref, g_ref, h_ref, lng_ref, lnb_ref, uv_hbm, y_ref, buf, sem, gt_sc, out_sc):
    tb = h_ref.shape[0]

    def row_copy(t, slot, k):
        return pltpu.make_async_copy(uv_hbm.at[pl.ds(ids_ref[t, k], 1), :], buf.at[slot, pl.ds(k, 1), :],
                                     sem.at[slot])

    def issue(t, slot):
        for k in range(PEER_SLOTS):
            row_copy(t, slot, k).start(priority=k % 2)

    def wait(t, slot):
        for k in range(PEER_SLOTS):
            row_copy(t, slot, k).wait()

    gt_sc[...] = g_ref[...].T
    lane = lax.broadcasted_iota(jnp.int32, (PEER_SLOTS, tb), 1)
    for t in range(GATHER_DEPTH - 1):
        issue(t, t)

    def body(t, carry):
        nxt = t + GATHER_DEPTH - 1

        @pl.when(nxt < tb)
        def _():
            issue(nxt, nxt % GATHER_DEPTH)

        slot = t % GATHER_DEPTH
        wait(t, slot)
        x = h_ref[pl.ds(t, 1), :]
        act = jnp.sum(buf[slot, :, :D_MODEL] * x, axis=1, keepdims=True)
        g_col = jnp.sum(jnp.where(lane == t, gt_sc[...], 0.0), axis=1, keepdims=True)
        coef = g_col * jax.nn.gelu(act)
        out_sc[pl.ds(t, 1), :] = jnp.sum(buf[slot, :, D_MODEL:] * coef, axis=0, keepdims=True)
        return carry

    lax.fori_loop(0, tb, body, 0)
    y_ref[...] = _layer_norm(DN_ALPHA * h_ref[...] + out_sc[...], lng_ref[...], lnb_ref[...])


def _peer_experts(h2d, ids, g, uv, ln_g, ln_b):
    n = h2d.shape[0]
    tb = EXPERT_TOKENS
    row = lambda i: (i, 0)
    fixed = lambda i: (0, 0)
    return pl.pallas_call(
        _peer_expert_kernel,
        grid=(n // tb,),
        in_specs=[pl.BlockSpec((tb, PEER_SLOTS), row, memory_space=pltpu.SMEM), pl.BlockSpec((tb, PEER_SLOTS), row),
                  pl.BlockSpec((tb, D_MODEL), row), pl.BlockSpec((1, D_MODEL), fixed),
                  pl.BlockSpec((1, D_MODEL), fixed), pl.BlockSpec(memory_space=pl.ANY)],
        out_specs=pl.BlockSpec((tb, D_MODEL), row),
        out_shape=jax.ShapeDtypeStruct((n, D_MODEL), F32),
        scratch_shapes=[pltpu.VMEM((GATHER_DEPTH, PEER_SLOTS, 2 * D_MODEL), F32),
                        pltpu.SemaphoreType.DMA((GATHER_DEPTH,)), pltpu.VMEM((PEER_SLOTS, tb), F32),
                        pltpu.VMEM((tb, D_MODEL), F32)],
        compiler_params=pltpu.CompilerParams(dimension_semantics=('arbitrary',), vmem_limit_bytes=VMEM_LIMIT),
        name='peer_experts',
    )(ids, g, h2d, ln_g.reshape(1, D_MODEL), ln_b.reshape(1, D_MODEL), uv)


def _group_step(x, mixer, ws, w_up_a, w_up_b, w_out, ln1_g, ln1_b, w_pq, sub_keys, uv, ln2_g, ln2_b, tm):
    B, T, _ = x.shape
    x2d = x.reshape(B * T, D_MODEL)
    pr = _unpack_proj(_project(x2d, ws, tm), B, T)
    o_a, o_b, new_state = mixer(pr)
    h = _merge(x2d, o_a.reshape(B * T, -1), o_b.reshape(B * T, -1), pr['merge_gate'],
               w_up_a, w_up_b, w_out, ln1_g, ln1_b, tm)
    ids, g = _peer_route(h, w_pq, sub_keys)
    y = _peer_experts(h, ids, g, uv, ln2_g, ln2_b)
    return y.reshape(B, T, D_MODEL), new_state


def kernel(x_prompt, x_sample, cache_a_kv, cache_a_idx, cache_b_cmp_kv, cache_b_sel_kv, state_b_win_kv, page_table, w_in, rel_bias_table, cmp_pe, cmp_w1, cmp_w2, w_up_a, w_up_b, w_out, ln1_g, ln1_b, w_pq, peer_sub_keys, peer_u, peer_v, ln2_g, ln2_b):
    l = 0
    ws = _split_w_in(w_in[l])
    uv = jnp.concatenate([peer_u[l], peer_v[l]], axis=1)
    tail = (w_up_a[l], w_up_b[l], w_out[l], ln1_g[l], ln1_b[l], w_pq[l], peer_sub_keys[l], uv, ln2_g[l], ln2_b[l])
    mix_p = functools.partial(_mix_prompt, rel_table=rel_bias_table, cmp_pe=cmp_pe[l], cmp_w1=cmp_w1[l],
                              cmp_w2=cmp_w2[l])
    y_p, st_p = _group_step(x_prompt, mix_p, ws, *tail, tm=256)
    mix_s = functools.partial(_mix_sample, cache_a_kv=cache_a_kv[l], cache_a_idx=cache_a_idx[l],
                              cache_b_cmp_kv=cache_b_cmp_kv[l], cache_b_sel_kv=cache_b_sel_kv[l],
                              state_b_win_kv=state_b_win_kv[l], page_table=page_table, rel_table=rel_bias_table,
                              cmp_pe=cmp_pe[l], cmp_w1=cmp_w1[l], cmp_w2=cmp_w2[l])
    y_s, st_s = _group_step(x_sample, mix_s, ws, *tail, tm=256)
    return (y_p, y_s) + tuple(s[None] for s in st_p) + tuple(s[None] for s in st_s)
```

```python
import functools
import math

import jax
import jax.numpy as jnp
import numpy as np
from jax import lax
from jax.experimental import pallas as pl
from jax.experimental.pallas import tpu as pltpu

D_MODEL = 1024
DEPTH = 1
PAGE_SIZE = 128
HEAD_DIM = 64
A_HEADS = 8
A_KV_HEADS = 2
IDX_HEADS = 4
IDX_DIM = 64
A_TOPK = 256
B_HEADS = 8
B_KV_HEADS = 2
NSA_BLOCK = 64
NSA_TOPN = 16
NSA_FORCE = 8.0
CMP_HIDDEN = 64
WINDOW = 512
REL_BUCKETS = 32
REL_MAX_EXACT = 16
REL_MAX_DIST = 128
PEER_HEADS = 8
PEER_NKEYS = 128
PEER_QDIM = 256
PEER_TOPK = 16
Q_BLOCK = 32
TOKEN_BLOCK = 256
LN_EPS = 1e-5
DN_ALPHA = (2 * DEPTH) ** 0.25

PROJ_LAYOUT = (
    ('a_q', A_HEADS * HEAD_DIM),
    ('a_kv', 2 * A_KV_HEADS * HEAD_DIM),
    ('a_iq', IDX_HEADS * IDX_DIM),
    ('a_iw', IDX_HEADS),
    ('a_ik', IDX_DIM),
    ('b_q', B_HEADS * HEAD_DIM),
    ('b_cmp_kv', 2 * B_KV_HEADS * HEAD_DIM),
    ('b_sel_kv', 2 * B_KV_HEADS * HEAD_DIM),
    ('b_win_kv', 2 * B_KV_HEADS * HEAD_DIM),
    ('b_gate', B_HEADS * 3),
    ('merge_gate', 2 * D_MODEL),
)

LANES = 128
VMEM_LIMIT = 56 * 1024 * 1024
MISC_GATE_OFF = IDX_HEADS

F32 = jnp.float32
BF16 = jnp.bfloat16
MXU = jnp.bfloat16


_PROJ_OUT = ('a_q', 'a_kv', 'a_iq', 'a_ik', 'misc', 'b_q', 'b_cmp_kv', 'b_sel_kv', 'b_win_kv', 'merge_gate')


def _proj_kernel(x_ref, *refs):
    n = len(_PROJ_OUT)
    w_refs, o_refs = refs[:n], refs[n:]
    x = x_ref[...].astype(BF16)
    for name, w_ref, o_ref in zip(_PROJ_OUT, w_refs, o_refs):
        r = jnp.dot(x, w_ref[...], preferred_element_type=F32)
        if name == 'misc':
            lane = lax.broadcasted_iota(jnp.int32, r.shape, 1)
            r = jnp.where(lane < MISC_GATE_OFF, r * IDX_HEADS ** -0.5, jax.nn.sigmoid(r))
        elif name == 'merge_gate':
            r = jax.nn.sigmoid(r)
        o_ref[...] = r


def _split_w_in(w_in):
    parts = {}
    off = 0
    for name, width in PROJ_LAYOUT:
        parts[name] = w_in[:, off:off + width]
        off += width
    misc = jnp.concatenate([parts['a_iw'], parts['b_gate']], axis=1)
    parts['misc'] = jnp.pad(misc, ((0, 0), (0, LANES - misc.shape[1])))
    return [parts[name].astype(BF16) for name in _PROJ_OUT]


def _project(x2d, ws, tm):
    n = x2d.shape[0]
    widths = [w.shape[1] for w in ws]
    return pl.pallas_call(
        _proj_kernel,
        grid=(n // tm,),
        in_specs=[pl.BlockSpec((tm, D_MODEL), lambda i: (i, 0))]
        + [pl.BlockSpec((D_MODEL, wd), lambda i: (0, 0)) for wd in widths],
        out_specs=[pl.BlockSpec((tm, wd), lambda i: (i, 0)) for wd in widths],
        out_shape=[jax.ShapeDtypeStruct((n, wd), F32) for wd in widths],
        compiler_params=pltpu.CompilerParams(dimension_semantics=('arbitrary',), vmem_limit_bytes=VMEM_LIMIT),
        name='project',
    )(x2d, *ws)


def _unpack_proj(outs, B, T):
    d = dict(zip(_PROJ_OUT, outs))
    misc = d['misc']
    kv = (B, T, 2, A_KV_HEADS, HEAD_DIM)
    return {
        'a_q': d['a_q'].reshape(B, T, A_HEADS, HEAD_DIM),
        'a_kv': d['a_kv'].reshape(kv),
        'a_iq': d['a_iq'].reshape(B, T, IDX_HEADS, IDX_DIM),
        'a_iw': misc[:, :IDX_HEADS].reshape(B, T, IDX_HEADS),
        'a_ik': d['a_ik'].reshape(B, T, IDX_DIM),
        'b_q': d['b_q'].reshape(B, T, B_HEADS, HEAD_DIM),
        'b_cmp_kv': d['b_cmp_kv'].reshape(kv),
        'b_sel_kv': d['b_sel_kv'].reshape(kv),
        'b_win_kv': d['b_win_kv'].reshape(kv),
        'b_gate': misc[:, MISC_GATE_OFF:MISC_GATE_OFF + 3 * B_HEADS].reshape(B, T, B_HEADS, 3),
        'merge_gate': d['merge_gate'],
        'raw': d,
    }


def _layer_norm(x, g, b):
    mu = jnp.mean(x, axis=-1, keepdims=True)
    xc = x - mu
    var = jnp.mean(xc * xc, axis=-1, keepdims=True)
    return xc * lax.rsqrt(var + LN_EPS) * g + b


def _merge_kernel(x_ref, oa_ref, ob_ref, mg_ref, wa_ref, wb_ref, wo_ref, g_ref, b_ref, h_ref):
    ya = jnp.dot(oa_ref[...].astype(BF16), wa_ref[...], preferred_element_type=F32)
    yb = jnp.dot(ob_ref[...].astype(BF16), wb_ref[...], preferred_element_type=F32)
    mg = mg_ref[...]
    mix = mg[:, :D_MODEL] * ya + mg[:, D_MODEL:] * yb
    mixed = jnp.dot(mix.astype(BF16), wo_ref[...], preferred_element_type=F32)
    h_ref[...] = _layer_norm(DN_ALPHA * x_ref[...] + mixed, g_ref[...], b_ref[...])


def _merge(x2d, o_a, o_b, mg, w_up_a, w_up_b, w_out, ln_g, ln_b, tm):
    n = x2d.shape[0]
    ca, cb = o_a.shape[1], o_b.shape[1]
    row = lambda i: (i, 0)
    fixed = lambda i: (0, 0)
    return pl.pallas_call(
        _merge_kernel,
        grid=(n // tm,),
        in_specs=[pl.BlockSpec((tm, D_MODEL), row), pl.BlockSpec((tm, ca), row), pl.BlockSpec((tm, cb), row),
                  pl.BlockSpec((tm, 2 * D_MODEL), row), pl.BlockSpec((ca, D_MODEL), fixed),
                  pl.BlockSpec((cb, D_MODEL), fixed), pl.BlockSpec((D_MODEL, D_MODEL), fixed),
                  pl.BlockSpec((1, D_MODEL), fixed), pl.BlockSpec((1, D_MODEL), fixed)],
        out_specs=pl.BlockSpec((tm, D_MODEL), row),
        out_shape=jax.ShapeDtypeStruct((n, D_MODEL), F32),
        compiler_params=pltpu.CompilerParams(dimension_semantics=('arbitrary',), vmem_limit_bytes=VMEM_LIMIT),
        name='merge',
    )(x2d, o_a, o_b, mg, w_up_a.astype(BF16), w_up_b.astype(BF16), w_out.astype(BF16),
      ln_g.reshape(1, D_MODEL), ln_b.reshape(1, D_MODEL))


TQ = 256
NEG = -1e30
INT_MIN = -2 ** 31
COUNT_ROWS = 256
_NT = (((1,), (1,)), ((), ()))
_TN = (((0,), (0,)), ((), ()))


def _bucket_np(d):
    n = np.maximum(d, 0)
    nf = np.maximum(n, 1).astype(np.float64)
    large = REL_MAX_EXACT + (np.log(nf / REL_MAX_EXACT) / math.log(REL_MAX_DIST / REL_MAX_EXACT)
                             * (REL_BUCKETS - REL_MAX_EXACT)).astype(np.int64)
    return np.where(n < REL_MAX_EXACT, n, np.minimum(large, REL_BUCKETS - 1)).astype(np.int32)


def _band_bias(table, n_groups):
    d = TQ + np.arange(TQ)[None, :] - np.arange(2 * TQ)[:, None]
    band = jnp.take(table, jnp.asarray(_bucket_np(d)), axis=0) - table[REL_BUCKETS - 1]
    band = jnp.moveaxis(band, -1, 1)
    return jnp.moveaxis(band.reshape(2 * TQ, n_groups, -1), 1, 0)


def _block_bias(table, S):
    nb = S // NSA_BLOCK
    d = np.arange(S)[None, :] - ((np.arange(nb) + 1) * NSA_BLOCK - 1)[:, None]
    return jnp.moveaxis(jnp.take(table, jnp.asarray(_bucket_np(d)), axis=0), -1, 0)


def _ordered_keys(x):
    x = jnp.where(x == 0.0, 0.0, x)
    b = lax.bitcast_convert_type(x, jnp.int32)
    return b ^ ((b >> 31) & jnp.int32(0x7FFFFFFF))


def _softmax_reset(m_sc, l_sc, acc_sc):
    m_sc[...] = jnp.full(m_sc.shape, NEG, F32)
    l_sc[...] = jnp.zeros(l_sc.shape, F32)
    acc_sc[...] = jnp.zeros(acc_sc.shape, F32)


def _group_queries(q, n_groups):
    rep = q.shape[1] // HEAD_DIM // n_groups
    return [_stack_heads(q, range(g * rep, (g + 1) * rep), HEAD_DIM) for g in range(n_groups)]


def _attend_chunk(kv_c, q_groups, madd_of_group, band_ref, band_row0, m_sc, l_sc, acc_sc):
    n_g = len(q_groups)
    rep = q_groups[0].shape[0] // TQ
    for g in range(n_g):
        k_c = kv_c[:, g * HEAD_DIM:(g + 1) * HEAD_DIM]
        v_c = kv_c[:, (n_g + g) * HEAD_DIM:(n_g + g + 1) * HEAD_DIM]
        add = jnp.concatenate([madd_of_group(g)] * rep, axis=1)
        if band_row0 is not None:
            add = add + band_ref[g, band_row0:band_row0 + TQ, :]
        logits = lax.dot_general(k_c, q_groups[g], _NT, preferred_element_type=F32) + add
        m_old = m_sc[g]
        m_new = jnp.maximum(m_old, jnp.max(logits, axis=0, keepdims=True))
        alpha = jnp.exp(m_old - m_new)
        p = jnp.exp(logits - m_new)
        l_sc[g] = alpha * l_sc[g] + jnp.sum(p, axis=0, keepdims=True)
        m_sc[g] = m_new
        p = p.astype(MXU)
        for r in range(rep):
            h = g * rep + r
            lanes = slice(r * TQ, (r + 1) * TQ)
            pv = lax.dot_general(v_c, p[:, lanes], _TN, preferred_element_type=F32)
            acc_sc[h] = alpha[:, lanes] * acc_sc[h] + pv


def _softmax_scratch(n_heads, n_groups):
    stats = pltpu.VMEM((n_groups, 1, n_heads // n_groups * TQ), F32)
    return [stats, stats, pltpu.VMEM((n_heads, HEAD_DIM, TQ), F32)]


def _attend_result(h, rep, l_sc, acc_sc):
    g, r = divmod(h, rep)
    return acc_sc[h] / l_sc[g][:, r * TQ:(r + 1) * TQ]


def _dsa_kernel(iq_ref, misc_ref, q_ref, ik_ref, kv_ref, band_ref, o_ref, key_sc, m_sc, l_sc, acc_sc, out_sc,
                *, k_sel, pos_bits):
    i = pl.program_id(1)
    t0 = i * TQ
    n_chunks = i + 1
    lane_t = t0 + lax.broadcasted_iota(jnp.int32, (1, TQ), 1)
    sub_iota = lax.broadcasted_iota(jnp.int32, (TQ, TQ), 0)
    misc_t = misc_ref[...].T
    iq = iq_ref[...].astype(MXU)
    iq_heads = [iq[:, h * IDX_DIM:(h + 1) * IDX_DIM] for h in range(IDX_HEADS)]

    def score_chunk(j, carry):
        r0 = pl.multiple_of(j * TQ, TQ)
        ik_c = ik_ref[pl.ds(r0, TQ), :].astype(MXU)
        sc = jnp.zeros((TQ, TQ), F32)
        for h in range(IDX_HEADS):
            rel = lax.dot_general(ik_c, iq_heads[h], _NT, preferred_element_type=F32)
            sc = sc + jnp.maximum(rel * IDX_DIM ** -0.5, 0.0) * misc_t[h:h + 1, :]
        sc = jnp.where(r0 + sub_iota <= lane_t, sc, -jnp.inf)
        key_sc[pl.ds(r0, TQ), :] = _ordered_keys(sc)
        return carry

    lax.fori_loop(0, n_chunks, score_chunk, 0)

    def count(pred):
        def body(jj, acc):
            r0 = pl.multiple_of(jj * COUNT_ROWS, COUNT_ROWS)
            blk = key_sc[pl.ds(r0, COUNT_ROWS), :]
            pos = r0 + lax.broadcasted_iota(jnp.int32, (COUNT_ROWS, TQ), 0)
            hit = jnp.where(pred(blk, pos), 1, 0)
            return acc + jnp.sum(hit.reshape(COUNT_ROWS // 8, 8, TQ), axis=0)

        acc = lax.fori_loop(0, n_chunks * (TQ // COUNT_ROWS), body, jnp.zeros((8, TQ), jnp.int32))
        return jnp.sum(acc, axis=0, keepdims=True)

    v = jnp.full((1, TQ), INT_MIN, jnp.int32)
    v = jnp.where(count(lambda blk, pos: blk >= 0) >= k_sel, 0, v)

    def value_bit(it, v):
        cand = v | jnp.left_shift(jnp.int32(1), 30 - it)
        return jnp.where(count(lambda blk, pos: blk >= cand) >= k_sel, cand, v)

    v = lax.fori_loop(0, 31, value_bit, v)
    need = k_sel - count(lambda blk, pos: blk > v)
    n_ge = count(lambda blk, pos: blk >= v)

    def tie_search():
        def pos_bit(it, jm):
            cand = jm | jnp.left_shift(jnp.int32(1), pos_bits - 1 - it)
            return jnp.where(count(lambda blk, pos: (blk == v) & (pos < cand)) < need, cand, jm)

        return lax.fori_loop(0, pos_bits, pos_bit, jnp.zeros((1, TQ), jnp.int32))

    j_max = lax.cond(jnp.max(n_ge) > k_sel, tie_search, lambda: jnp.full((1, TQ), 2 ** 30, jnp.int32))

    q = (q_ref[...] * HEAD_DIM ** -0.5).astype(MXU)
    q_groups = _group_queries(q, A_KV_HEADS)
    _softmax_reset(m_sc, l_sc, acc_sc)

    def chunk(j, band_row0, causal):
        r0 = pl.multiple_of(j * TQ, TQ)
        key = key_sc[pl.ds(r0, TQ), :]
        pos = r0 + sub_iota
        sel = (key > v) | ((key == v) & (pos <= j_max))
        if causal:
            sel = sel & (pos <= lane_t)
        madd = jnp.where(sel, 0.0, NEG)
        kv_c = kv_ref[pl.ds(r0, TQ), :].astype(MXU)
        _attend_chunk(kv_c, q_groups, lambda g: madd, band_ref, band_row0, m_sc, l_sc, acc_sc)

    def far_chunk(j, carry):
        chunk(j, None, False)
        return carry

    lax.fori_loop(0, jnp.maximum(i - 1, 0), far_chunk, 0)

    @pl.when(i >= 1)
    def _():
        chunk(i - 1, 0, False)

    chunk(i, TQ, True)
    for h in range(A_HEADS):
        out_sc[h * HEAD_DIM:(h + 1) * HEAD_DIM, :] = _attend_result(h, A_HEADS // A_KV_HEADS, l_sc, acc_sc)
    o_ref[...] = out_sc[...].T


def _dsa_prompt(raw, band_a, B, S):
    nq = S // TQ
    k_sel = min(A_TOPK, S // 4)
    assert S % TQ == 0 and TQ >= k_sel
    tile = lambda b, i: (b * nq + i, 0)
    seq = lambda b, i: (b, 0)
    kern = functools.partial(_dsa_kernel, k_sel=k_sel, pos_bits=(S - 1).bit_length())
    return pl.pallas_call(
        kern,
        grid=(B, nq),
        in_specs=[pl.BlockSpec((TQ, IDX_HEADS * IDX_DIM), tile), pl.BlockSpec((TQ, LANES), tile),
                  pl.BlockSpec((TQ, A_HEADS * HEAD_DIM), tile), pl.BlockSpec((S, IDX_DIM), seq),
                  pl.BlockSpec((S, 2 * A_KV_HEADS * HEAD_DIM), seq),
                  pl.BlockSpec(band_a.shape, lambda b, i: (0, 0, 0))],
        out_specs=pl.BlockSpec((TQ, A_HEADS * HEAD_DIM), tile),
        out_shape=jax.ShapeDtypeStruct((B * S, A_HEADS * HEAD_DIM), F32),
        scratch_shapes=[pltpu.VMEM((S, TQ), jnp.int32)] + _softmax_scratch(A_HEADS, A_KV_HEADS)
        + [pltpu.VMEM((A_HEADS * HEAD_DIM, TQ), F32)],
        compiler_params=pltpu.CompilerParams(dimension_semantics=('arbitrary', 'arbitrary'),
                                             vmem_limit_bytes=VMEM_LIMIT),
        name='dsa_prompt',
    )(raw['a_iq'], raw['misc'], raw['a_q'], raw['a_ik'], raw['a_kv'], band_a)


def _compress_kernel(x_ref, pe_ref, w1_ref, w2_ref, o_ref):
    x = (x_ref[...] + pe_ref[...]).astype(MXU)
    h = jax.nn.gelu(jnp.dot(x, w1_ref[...], preferred_element_type=F32))
    o_ref[...] = jnp.dot(h.astype(MXU), w2_ref[...], preferred_element_type=F32)


def _compress_weights(pe, w1, w2):
    eye_c = jnp.eye(2, dtype=F32)
    eye_g = jnp.eye(B_KV_HEADS, dtype=F32)
    w1_big = jnp.einsum('lcde,cC,gG->lcgdCGe', w1, eye_c, eye_g)
    w1_big = w1_big.reshape(NSA_BLOCK * 2 * B_KV_HEADS * HEAD_DIM, 2 * B_KV_HEADS * CMP_HIDDEN)
    w2_big = jnp.einsum('cef,cC,gG->cgeCGf', w2, eye_c, eye_g)
    w2_big = w2_big.reshape(2 * B_KV_HEADS * CMP_HIDDEN, 2 * B_KV_HEADS * HEAD_DIM)
    pe_flat = jnp.broadcast_to(pe[:, :, None, :], (NSA_BLOCK, 2, B_KV_HEADS, HEAD_DIM)).reshape(1, -1)
    return pe_flat, w1_big.astype(MXU), w2_big.astype(MXU)


def _compress(blocks2d, pe_flat, w1_big, w2_big, tm):
    n, width = blocks2d.shape
    fixed = lambda i: (0, 0)
    return pl.pallas_call(
        _compress_kernel,
        grid=(n // tm,),
        in_specs=[pl.BlockSpec((tm, width), lambda i: (i, 0)), pl.BlockSpec((1, width), fixed),
                  pl.BlockSpec(w1_big.shape, fixed), pl.BlockSpec(w2_big.shape, fixed)],
        out_specs=pl.BlockSpec((tm, w2_big.shape[1]), lambda i: (i, 0)),
        out_shape=jax.ShapeDtypeStruct((n, w2_big.shape[1]), F32),
        compiler_params=pltpu.CompilerParams(dimension_semantics=('arbitrary',), vmem_limit_bytes=VMEM_LIMIT),
        name='nsa_compress',
    )(blocks2d, pe_flat, w1_big, w2_big)


def _nsa_kernel(q_ref, misc_ref, cmp_ref, selkv_ref, winkv_ref, band_ref, biasc_ref, o_ref,
                sel_sc, m_sc, l_sc, acc_sc, out_sc, *, nb, n_sel):
    i = pl.program_id(1)
    t0 = i * TQ
    n_g = B_KV_HEADS
    rep = B_HEADS // n_g
    lane_t = t0 + lax.broadcasted_iota(jnp.int32, (1, TQ), 1)
    sub_iota = lax.broadcasted_iota(jnp.int32, (TQ, TQ), 0)
    lane_iota = lax.broadcasted_iota(jnp.int32, (TQ, TQ), 1)
    misc_t = misc_ref[...].T
    gate = lambda h, k: misc_t[MISC_GATE_OFF + 3 * h + k:MISC_GATE_OFF + 3 * h + k + 1, :]
    q = (q_ref[...] * HEAD_DIM ** -0.5).astype(MXU)
    q_heads = [q[:, h * HEAD_DIM:(h + 1) * HEAD_DIM] for h in range(B_HEADS)]
    q_groups = _group_queries(q, n_g)

    cmp = cmp_ref[...].astype(MXU)
    blk = lax.broadcasted_iota(jnp.int32, (nb, TQ), 0)
    visible = (blk + 1) * NSA_BLOCK - 1 <= lane_t
    cur = lax.shift_right_logical(lane_t, NSA_BLOCK.bit_length() - 1)
    forced = (blk == 0) | (blk == cur) | (blk == cur - 1)
    for g in range(n_g):
        k_c = cmp[:, g * HEAD_DIM:(g + 1) * HEAD_DIM]
        v_c = cmp[:, (n_g + g) * HEAD_DIM:(n_g + g + 1) * HEAD_DIM]
        importance = jnp.zeros((nb, TQ), F32)
        for r in range(rep):
            h = g * rep + r
            lc = lax.dot_general(k_c, q_heads[h], _NT, preferred_element_type=F32) + biasc_ref[h]
            lc = jnp.where(visible, lc, -jnp.inf)
            m = jnp.max(lc, axis=0, keepdims=True)
            m = jnp.where(m > -jnp.inf, m, 0.0)
            e = jnp.exp(lc - m)
            s = jnp.sum(e, axis=0, keepdims=True)
            p = e / jnp.where(s > 0, s, 1.0)
            importance = importance + p
            o_cmp = lax.dot_general(v_c, p.astype(MXU), _TN, preferred_element_type=F32)
            out_sc[h * HEAD_DIM:(h + 1) * HEAD_DIM, :] = gate(h, 0) * o_cmp
        score = jnp.where(forced, NSA_FORCE, importance)
        score = jnp.where(blk <= cur, score, -1.0)
        rank = jnp.zeros((nb, TQ), jnp.int32)
        for n in range(nb):
            row = score[n:n + 1, :]
            beats = (row > score) | ((row == score) & (blk > n))
            rank = rank + jnp.where(beats, 1, 0)
        sel_sc[g] = jnp.where(rank < n_sel, 0.0, NEG)

    def finish_branch(k):
        for h in range(B_HEADS):
            rows = slice(h * HEAD_DIM, (h + 1) * HEAD_DIM)
            out_sc[rows, :] = out_sc[rows, :] + gate(h, k) * _attend_result(h, rep, l_sc, acc_sc)

    _softmax_reset(m_sc, l_sc, acc_sc)
    blocks_per_chunk = TQ // NSA_BLOCK

    def sel_chunk(j, band_row0, causal):
        r0 = pl.multiple_of(j * TQ, TQ)
        kv_c = selkv_ref[pl.ds(r0, TQ), :].astype(MXU)

        def madd_of_group(g):
            rows = [sel_sc[g, pl.ds(j * blocks_per_chunk + b, 1), :] for b in range(blocks_per_chunk)]
            madd = jnp.concatenate([jnp.broadcast_to(row, (NSA_BLOCK, TQ)) for row in rows], axis=0)
            if causal:
                madd = jnp.where(sub_iota <= lane_iota, madd, NEG)
            return madd

        _attend_chunk(kv_c, q_groups, madd_of_group, band_ref, band_row0, m_sc, l_sc, acc_sc)

    def far_chunk(j, carry):
        sel_chunk(j, None, False)
        return carry

    lax.fori_loop(0, jnp.maximum(i - 1, 0), far_chunk, 0)

    @pl.when(i >= 1)
    def _():
        sel_chunk(i - 1, 0, False)

    sel_chunk(i, TQ, True)
    finish_branch(1)

    _softmax_reset(m_sc, l_sc, acc_sc)
    for back in range(WINDOW // TQ, -1, -1):
        dist = back * TQ + lane_iota - sub_iota
        madd = jnp.where((dist >= 0) & (dist < WINDOW), 0.0, NEG)
        band_row0 = {0: TQ, 1: 0}.get(back)

        def win_chunk(back=back, madd=madd, band_row0=band_row0):
            r0 = pl.multiple_of((i - back) * TQ, TQ)
            kv_c = winkv_ref[pl.ds(r0, TQ), :].astype(MXU)
            _attend_chunk(kv_c, q_groups, lambda g: madd, band_ref, band_row0, m_sc, l_sc, acc_sc)

        if back == 0:
            win_chunk()
        else:
            pl.when(i >= back)(win_chunk)
    finish_branch(2)
    o_ref[...] = out_sc[...].T


def _nsa_prompt(raw, cmp, band_b, bias_c, B, S):
    nq = S // TQ
    nb = S // NSA_BLOCK
    assert S % TQ == 0 and TQ % NSA_BLOCK == 0
    tile = lambda b, i: (b * nq + i, 0)
    seq = lambda b, i: (b, 0)
    kv_w = 2 * B_KV_HEADS * HEAD_DIM
    kern = functools.partial(_nsa_kernel, nb=nb, n_sel=min(NSA_TOPN, nb))
    return pl.pallas_call(
        kern,
        grid=(B, nq),
        in_specs=[pl.BlockSpec((TQ, B_HEADS * HEAD_DIM), tile), pl.BlockSpec((TQ, LANES), tile),
                  pl.BlockSpec((nb, kv_w), seq), pl.BlockSpec((S, kv_w), seq), pl.BlockSpec((S, kv_w), seq),
                  pl.BlockSpec(band_b.shape, lambda b, i: (0, 0, 0)),
                  pl.BlockSpec((B_HEADS, nb, TQ), lambda b, i: (0, 0, i))],
        out_specs=pl.BlockSpec((TQ, B_HEADS * HEAD_DIM), tile),
        out_shape=jax.ShapeDtypeStruct((B * S, B_HEADS * HEAD_DIM), F32),
        scratch_shapes=[pltpu.VMEM((B_KV_HEADS, nb, TQ), F32)] + _softmax_scratch(B_HEADS, B_KV_HEADS)
        + [pltpu.VMEM((B_HEADS * HEAD_DIM, TQ), F32)],
        compiler_params=pltpu.CompilerParams(dimension_semantics=('arbitrary', 'arbitrary'),
                                             vmem_limit_bytes=VMEM_LIMIT),
        name='nsa_prompt',
    )(raw['b_q'], raw['misc'], cmp, raw['b_sel_kv'], raw['b_win_kv'], band_b, bias_c)


def _mix_prompt(pr, rel_table, cmp_pe, cmp_w1, cmp_w2):
    B, S = pr['a_q'].shape[:2]
    raw = pr['raw']
    table_a, table_b = rel_table[:, :A_HEADS], rel_table[:, A_HEADS:]
    pe_flat, w1_big, w2_big = _compress_weights(cmp_pe, cmp_w1, cmp_w2)
    n_blocks = B * S // NSA_BLOCK
    cmp = _compress(raw['b_cmp_kv'].reshape(n_blocks, -1), pe_flat, w1_big, w2_big, tm=min(128, n_blocks))
    o_a = _dsa_prompt(raw, _band_bias(table_a, A_KV_HEADS), B, S)
    o_b = _nsa_prompt(raw, cmp, _band_bias(table_b, B_KV_HEADS), _block_bias(table_b, S), B, S)
    wb = min(WINDOW, S)
    new_state = (pr['a_kv'], pr['a_ik'], pr['b_cmp_kv'], pr['b_sel_kv'], pr['b_win_kv'][:, S - wb:])
    return o_a, o_b, new_state


PAGES_PER_STEP = 8
KEY_PAD = 128


def _paged_specs(width):
    def spec(j):
        return pl.BlockSpec((1, PAGE_SIZE, width), lambda s, p, pt: (pt[s, p * PAGES_PER_STEP + j], 0, 0))
    return [spec(j) for j in range(PAGES_PER_STEP)]


def _stage_pages(page_refs, dst_sc):
    p = pl.program_id(1)
    for j, ref in enumerate(page_refs):
        r0 = pl.multiple_of((p * PAGES_PER_STEP + j) * PAGE_SIZE, PAGE_SIZE)
        dst_sc[pl.ds(r0, PAGE_SIZE), :] = ref[0]


def _stage_new_rows(new_ref, dst_sc, past):
    t = new_ref.shape[0]
    dst_sc[past:past + t, :] = new_ref[...]
    dst_sc[past + t:, :] = jnp.zeros((dst_sc.shape[0] - past - t, dst_sc.shape[1]), F32)


def _rows_softmax_attend(q_rows, k, v, bias, madd):
    logits = lax.dot_general(q_rows, k, _NT, preferred_element_type=F32) + bias + madd
    m = jnp.max(logits, axis=1, keepdims=True)
    e = jnp.exp(logits - m)
    p = e / jnp.sum(e, axis=1, keepdims=True)
    return jnp.dot(p.astype(MXU), v, preferred_element_type=F32)


def _stack_heads(x, heads, width):
    return jnp.concatenate([x[:, h * width:(h + 1) * width] for h in heads], axis=0)


def _dsa_sample_kernel(pt_ref, *refs, past, k_sel):
    n = PAGES_PER_STEP
    idx_pages, kv_pages = refs[:n], refs[n:2 * n]
    ikn_ref, kvn_ref, iq_ref, misc_ref, q_ref, bias_ref, o_ref, ik_sc, kv_sc = refs[2 * n:]
    _stage_pages(idx_pages, ik_sc)
    _stage_pages(kv_pages, kv_sc)

    @pl.when(pl.program_id(1) == pl.num_programs(1) - 1)
    def _():
        T = iq_ref.shape[0]
        L = ik_sc.shape[0]
        _stage_new_rows(ikn_ref, ik_sc, past)
        _stage_new_rows(kvn_ref, kv_sc, past)
        pos = lax.broadcasted_iota(jnp.int32, (T, L), 1)
        valid = pos <= past + lax.broadcasted_iota(jnp.int32, (T, L), 0)
        misc = misc_ref[...]
        iq = _stack_heads(iq_ref[...].astype(MXU), range(IDX_HEADS), IDX_DIM)
        rel = lax.dot_general(iq, ik_sc[...].astype(MXU), _NT, preferred_element_type=F32)
        sc = jnp.zeros((T, L), F32)
        for h in range(IDX_HEADS):
            sc = sc + jnp.maximum(rel[h * T:(h + 1) * T] * IDX_DIM ** -0.5, 0.0) * misc[:, h:h + 1]
        key = _ordered_keys(jnp.where(valid, sc, -jnp.inf))

        def count(hit):
            return jnp.sum(jnp.where(hit, 1, 0), axis=1, keepdims=True)

        v = jnp.full((T, 1), INT_MIN, jnp.int32)
        v = jnp.where(count(key >= 0) >= k_sel, 0, v)

        def value_bit(it, v):
            cand = v | jnp.left_shift(jnp.int32(1), 30 - it)
            return jnp.where(count(key >= cand) >= k_sel, cand, v)

        v = lax.fori_loop(0, 31, value_bit, v)
        need = k_sel - count(key > v)
        pos_bits = (L - 1).bit_length()

        def tie_search():
            def pos_bit(it, jm):
                cand = jm | jnp.left_shift(jnp.int32(1), pos_bits - 1 - it)
                return jnp.where(count((key == v) & (pos < cand)) < need, cand, jm)

            return lax.fori_loop(0, pos_bits, pos_bit, jnp.zeros((T, 1), jnp.int32))

        j_max = lax.cond(jnp.max(count(key >= v)) > k_sel, tie_search,
                         lambda: jnp.full((T, 1), 2 ** 30, jnp.int32))
        sel = ((key > v) | ((key == v) & (pos <= j_max))) & valid
        madd = jnp.where(sel, 0.0, NEG)
        rep = A_HEADS // A_KV_HEADS
        madd = jnp.concatenate([madd] * rep, axis=0)
        q = (q_ref[...] * HEAD_DIM ** -0.5).astype(MXU)
        kv = kv_sc[...].astype(MXU)
        for g in range(A_KV_HEADS):
            q_rows = _stack_heads(q, range(g * rep, (g + 1) * rep), HEAD_DIM)
            o = _rows_softmax_attend(q_rows, kv[:, g * HEAD_DIM:(g + 1) * HEAD_DIM],
                                     kv[:, (A_KV_HEADS + g) * HEAD_DIM:(A_KV_HEADS + g + 1) * HEAD_DIM],
                                     bias_ref[g], madd)
            for r in range(rep):
                h = g * rep + r
                o_ref[:, h * HEAD_DIM:(h + 1) * HEAD_DIM] = o[r * T:(r + 1) * T, :]


def _sample_bias(table, T, past, n_keys, key_pos0, n_groups):
    d = (past + np.arange(T))[:, None] - (key_pos0 + np.arange(n_keys))[None, :]
    b = jnp.take(table, jnp.asarray(_bucket_np(d)), axis=0)
    b = jnp.moveaxis(b, -1, 0)
    return b.reshape(n_groups, -1, n_keys)


def _seq_rows(width, T):
    return pl.BlockSpec((T, width), lambda s, p, pt: (s, 0))


def _dsa_sample(raw, pool_idx, pool_kv, page_table, table_a, DB, T):
    n_pages = page_table.shape[1]
    past = n_pages * PAGE_SIZE
    L = past + KEY_PAD
    assert n_pages % PAGES_PER_STEP == 0 and T <= KEY_PAD
    kv_w = 2 * A_KV_HEADS * HEAD_DIM
    bias = _sample_bias(table_a, T, past, L, 0, A_KV_HEADS)
    fixed3 = lambda s, p, pt: (0, 0, 0)
    grid_spec = pltpu.PrefetchScalarGridSpec(
        num_scalar_prefetch=1,
        grid=(DB, n_pages // PAGES_PER_STEP),
        in_specs=_paged_specs(IDX_DIM) + _paged_specs(kv_w)
        + [_seq_rows(IDX_DIM, T), _seq_rows(kv_w, T), _seq_rows(IDX_HEADS * IDX_DIM, T), _seq_rows(LANES, T),
           _seq_rows(A_HEADS * HEAD_DIM, T), pl.BlockSpec(bias.shape, fixed3)],
        out_specs=_seq_rows(A_HEADS * HEAD_DIM, T),
        scratch_shapes=[pltpu.VMEM((L, IDX_DIM), F32), pltpu.VMEM((L, kv_w), F32)],
    )
    kern = functools.partial(_dsa_sample_kernel, past=past, k_sel=min(A_TOPK, (past + T) // 4))
    return pl.pallas_call(
        kern, grid_spec=grid_spec,
        out_shape=jax.ShapeDtypeStruct((DB * T, A_HEADS * HEAD_DIM), F32),
        compiler_params=pltpu.CompilerParams(dimension_semantics=('arbitrary', 'arbitrary'),
                                             vmem_limit_bytes=VMEM_LIMIT),
        name='dsa_sample',
    )(page_table, *([pool_idx] * PAGES_PER_STEP), *([pool_kv] * PAGES_PER_STEP),
      raw['a_ik'], raw['a_kv'], raw['a_iq'], raw['misc'], raw['a_q'], bias)


def _compress_sample_kernel(pt_ref, *refs, past):
    n = PAGES_PER_STEP
    pages = refs[:n]
    new_ref, pe_ref, w1_ref, w2_ref, o_ref, xk_sc, xv_sc = refs[n:]
    halves = (xk_sc, xv_sc)
    p = pl.program_id(1)
    for j, ref in enumerate(pages):
        r0 = pl.multiple_of((p * PAGES_PER_STEP + j) * PAGE_SIZE, PAGE_SIZE)
        for c, x_sc in enumerate(halves):
            x_sc[pl.ds(r0, PAGE_SIZE), :] = ref[0, :, c * LANES:(c + 1) * LANES]

    @pl.when(p == pl.num_programs(1) - 1)
    def _():
        nbp = o_ref.shape[1]
        t = new_ref.shape[0]
        for c, x_sc in enumerate(halves):
            x_sc[past:past + t, :] = new_ref[:, c * LANES:(c + 1) * LANES]
            x_sc[past + t:, :] = jnp.zeros((x_sc.shape[0] - past - t, LANES), F32)

        def body(l, accs):
            out = []
            for c, (x_sc, acc) in enumerate(zip(halves, accs)):
                xl = x_sc[pl.ds(l, nbp, stride=NSA_BLOCK), :] + pe_ref[c, pl.ds(l, 1), :]
                out.append(acc + jnp.dot(xl.astype(MXU), w1_ref[l, c], preferred_element_type=F32))
            return tuple(out)

        zero = jnp.zeros((nbp, LANES), F32)
        h = jnp.concatenate(lax.fori_loop(0, NSA_BLOCK, body, (zero, zero)), axis=1)
        o_ref[0] = jnp.dot(jax.nn.gelu(h).astype(MXU), w2_ref[...], preferred_element_type=F32)


def _padded_blocks(past, T):
    nb = -(-(past + T) // NSA_BLOCK)
    return nb, -(-nb // 8) * 8


def _compress_sample(raw, pool_cmp, page_table, pe_flat, w1_big, w2_big, DB, T):
    n_pages = page_table.shape[1]
    past = n_pages * PAGE_SIZE
    _, nbp = _padded_blocks(past, T)
    kv_w = 2 * B_KV_HEADS * HEAD_DIM
    assert kv_w == 2 * LANES and w1_big.shape[1] == 2 * LANES
    w1_l = w1_big.reshape(NSA_BLOCK, 2, LANES, 2, LANES)
    w1_halves = jnp.stack([w1_l[:, 0, :, 0, :], w1_l[:, 1, :, 1, :]], axis=1)
    fixed2 = lambda s, p, pt: (0, 0)
    grid_spec = pltpu.PrefetchScalarGridSpec(
        num_scalar_prefetch=1,
        grid=(DB, n_pages // PAGES_PER_STEP),
        in_specs=_paged_specs(kv_w) + [_seq_rows(kv_w, T),
                                       pl.BlockSpec((2, NSA_BLOCK, LANES), lambda s, p, pt: (0, 0, 0)),
                                       pl.BlockSpec(w1_halves.shape, lambda s, p, pt: (0, 0, 0, 0)),
                                       pl.BlockSpec(w2_big.shape, fixed2)],
        out_specs=pl.BlockSpec((1, nbp, kv_w), lambda s, p, pt: (s, 0, 0)),
        scratch_shapes=[pltpu.VMEM((nbp * NSA_BLOCK, LANES), F32), pltpu.VMEM((nbp * NSA_BLOCK, LANES), F32)],
    )
    return pl.pallas_call(
        functools.partial(_compress_sample_kernel, past=past), grid_spec=grid_spec,
        out_shape=jax.ShapeDtypeStruct((DB, nbp, kv_w), F32),
        compiler_params=pltpu.CompilerParams(dimension_semantics=('arbitrary', 'arbitrary'),
                                             vmem_limit_bytes=VMEM_LIMIT),
        name='nsa_compress_sample',
    )(page_table, *([pool_cmp] * PAGES_PER_STEP), raw['b_cmp_kv'],
      pe_flat.reshape(NSA_BLOCK, 2, LANES).swapaxes(0, 1), w1_halves, w2_big)


def _nsa_sample_kernel(pt_ref, *refs, past, nb, n_sel):
    n = PAGES_PER_STEP
    pages = refs[:n]
    (seln_ref, winn_ref, q_ref, misc_ref, cmp_ref, win_ref, expand_ref, bias_c_ref, bias_s_ref, bias_w_ref,
     o_ref, kv_sc, win_sc) = refs[n:]
    _stage_pages(pages, kv_sc)

    @pl.when(pl.program_id(1) == pl.num_programs(1) - 1)
    def _():
        T = q_ref.shape[0]
        L = kv_sc.shape[0]
        n_g = B_KV_HEADS
        rep = B_HEADS // n_g
        wb = win_ref.shape[1]
        _stage_new_rows(seln_ref, kv_sc, past)
        win_sc[0:wb, :] = win_ref[0]
        _stage_new_rows(winn_ref, win_sc, wb)
        misc = misc_ref[...]
        gate = lambda h, k: misc[:, MISC_GATE_OFF + 3 * h + k:MISC_GATE_OFF + 3 * h + k + 1]
        q = (q_ref[...] * HEAD_DIM ** -0.5).astype(MXU)
        cmp = cmp_ref[0].astype(MXU)
        nbp = cmp.shape[0]
        t_col = lax.broadcasted_iota(jnp.int32, (T, 1), 0)
        q_pos = past + t_col
        blk = lax.broadcasted_iota(jnp.int32, (T, nbp), 1)
        cur = lax.shift_right_logical(q_pos, NSA_BLOCK.bit_length() - 1)
        visible = (blk + 1) * NSA_BLOCK - 1 <= q_pos
        visible_rows = jnp.concatenate([visible] * rep, axis=0)
        forced = (blk == 0) | (blk == cur) | (blk == cur - 1)
        pos = lax.broadcasted_iota(jnp.int32, (T, L), 1)
        causal = pos <= q_pos
        kv = kv_sc[...].astype(MXU)
        win = win_sc[...].astype(MXU)
        wpos = lax.broadcasted_iota(jnp.int32, (T, win.shape[0]), 1)
        dw = wb + t_col - wpos
        madd_w = jnp.where((dw >= 0) & (dw < WINDOW), 0.0, NEG)
        madd_w = jnp.concatenate([madd_w] * rep, axis=0)
        outs = []
        for g in range(n_g):
            heads = range(g * rep, (g + 1) * rep)
            ks = slice(g * HEAD_DIM, (g + 1) * HEAD_DIM)
            vs = slice((n_g + g) * HEAD_DIM, (n_g + g + 1) * HEAD_DIM)
            q_rows = _stack_heads(q, heads, HEAD_DIM)
            lc = lax.dot_general(q_rows, cmp[:, ks], _NT, preferred_element_type=F32) + bias_c_ref[g]
            lc = jnp.where(visible_rows, lc, -jnp.inf)
            m = jnp.max(lc, axis=1, keepdims=True)
            m = jnp.where(m > -jnp.inf, m, 0.0)
            e = jnp.exp(lc - m)
            s = jnp.sum(e, axis=1, keepdims=True)
            p = e / jnp.where(s > 0, s, 1.0)
            o_cmp = jnp.dot(p.astype(MXU), cmp[:, vs], preferred_element_type=F32)
            importance = p[0:T]
            for r in range(1, rep):
                importance = importance + p[r * T:(r + 1) * T]
            score = jnp.where(forced, NSA_FORCE, importance)
            score = jnp.where(blk <= cur, score, -1.0)
            rank = jnp.zeros((T, nbp), jnp.int32)
            for b in range(nb):
                col = score[:, b:b + 1]
                rank = rank + jnp.where((col > score) | ((col == score) & (blk > b)), 1, 0)
            picked = jnp.where(rank < n_sel, 1.0, 0.0).astype(MXU)
            on_keys = jnp.dot(picked, expand_ref[...], preferred_element_type=F32)
            madd_s = jnp.where((on_keys > 0.5) & causal, 0.0, NEG)
            madd_s = jnp.concatenate([madd_s] * rep, axis=0)
            o_slc = _rows_softmax_attend(q_rows, kv[:, ks], kv[:, vs], bias_s_ref[g], madd_s)
            o_win = _rows_softmax_attend(q_rows, win[:, ks], win[:, vs], bias_w_ref[g], madd_w)
            for r, h in enumerate(heads):
                rows = slice(r * T, (r + 1) * T)
                outs.append(gate(h, 0) * o_cmp[rows] + gate(h, 1) * o_slc[rows] + gate(h, 2) * o_win[rows])
        o_ref[...] = jnp.concatenate(outs, axis=1)


def _nsa_sample(raw, cmp, pool_sel, win_state, page_table, table_b, DB, T):
    n_pages = page_table.shape[1]
    past = n_pages * PAGE_SIZE
    L = past + KEY_PAD
    nb, nbp = _padded_blocks(past, T)
    wb = win_state.shape[1]
    lw = wb + KEY_PAD
    kv_w = 2 * B_KV_HEADS * HEAD_DIM
    n_g = B_KV_HEADS
    expand = jnp.asarray(np.arange(L)[None, :] // NSA_BLOCK == np.arange(nbp)[:, None], dtype=MXU)
    blk_end = (np.arange(nbp) + 1) * NSA_BLOCK - 1
    d_c = (past + np.arange(T))[:, None] - blk_end[None, :]
    bias_c = jnp.moveaxis(jnp.take(table_b, jnp.asarray(_bucket_np(d_c)), axis=0), -1, 0).reshape(n_g, -1, nbp)
    bias_s = _sample_bias(table_b, T, past, L, 0, n_g)
    bias_w = _sample_bias(table_b, T, past, lw, past - wb, n_g)
    fixed2 = lambda s, p, pt: (0, 0)
    fixed3 = lambda s, p, pt: (0, 0, 0)
    per_seq3 = lambda s, p, pt: (s, 0, 0)
    grid_spec = pltpu.PrefetchScalarGridSpec(
        num_scalar_prefetch=1,
        grid=(DB, n_pages // PAGES_PER_STEP),
        in_specs=_paged_specs(kv_w)
        + [_seq_rows(kv_w, T), _seq_rows(kv_w, T), _seq_rows(B_HEADS * HEAD_DIM, T), _seq_rows(LANES, T),
           pl.BlockSpec((1, nbp, kv_w), per_seq3), pl.BlockSpec((1, wb, kv_w), per_seq3),
           pl.BlockSpec(expand.shape, fixed2), pl.BlockSpec(bias_c.shape, fixed3),
           pl.BlockSpec(bias_s.shape, fixed3), pl.BlockSpec(bias_w.shape, fixed3)],
        out_specs=_seq_rows(B_HEADS * HEAD_DIM, T),
        scratch_shapes=[pltpu.VMEM((L, kv_w), F32), pltpu.VMEM((lw, kv_w), F32)],
    )
    kern = functools.partial(_nsa_sample_kernel, past=past, nb=nb, n_sel=min(NSA_TOPN, nb))
    return pl.pallas_call(
        kern, grid_spec=grid_spec,
        out_shape=jax.ShapeDtypeStruct((DB * T, B_HEADS * HEAD_DIM), F32),
        compiler_params=pltpu.CompilerParams(dimension_semantics=('arbitrary', 'arbitrary'),
                                             vmem_limit_bytes=VMEM_LIMIT),
        name='nsa_sample',
    )(page_table, *([pool_sel] * PAGES_PER_STEP), raw['b_sel_kv'], raw['b_win_kv'], raw['b_q'], raw['misc'],
      cmp, win_state, expand, bias_c, bias_s, bias_w)


def _mix_sample(pr, cache_a_kv, cache_a_idx, cache_b_cmp_kv, cache_b_sel_kv, state_b_win_kv, page_table,
                rel_table, cmp_pe, cmp_w1, cmp_w2):
    DB, T = pr['a_q'].shape[:2]
    raw = pr['raw']
    n_pool = cache_a_kv.shape[0]
    kv_w = 2 * A_KV_HEADS * HEAD_DIM
    table_a, table_b = rel_table[:, :A_HEADS], rel_table[:, A_HEADS:]
    pe_flat, w1_big, w2_big = _compress_weights(cmp_pe, cmp_w1, cmp_w2)
    win_state = state_b_win_kv.reshape(DB, -1, kv_w)
    o_a = _dsa_sample(raw, cache_a_idx, cache_a_kv.reshape(n_pool, PAGE_SIZE, kv_w), page_table, table_a, DB, T)
    cmp = _compress_sample(raw, cache_b_cmp_kv.reshape(n_pool, PAGE_SIZE, kv_w), page_table, pe_flat, w1_big,
                           w2_big, DB, T)
    o_b = _nsa_sample(raw, cmp, cache_b_sel_kv.reshape(n_pool, PAGE_SIZE, kv_w), win_state, page_table, table_b,
                      DB, T)
    win_all = jnp.concatenate([state_b_win_kv, pr['b_win_kv']], axis=1)
    new_state = (pr['a_kv'], pr['a_ik'], pr['b_cmp_kv'], pr['b_sel_kv'], win_all[:, T:])
    return o_a, o_b, new_state


PEER_SLOTS = PEER_HEADS * PEER_TOPK
ROUTE_TOKENS = 256
EXPERT_TOKENS = 128
GATHER_DEPTH = 8
D_SUB = D_MODEL // LANES
EXPERT_ROWS = 2 * D_SUB


def _take_topk(cur, pos_iota, k, payload=None):
    n = cur.shape[0]
    vals, picks = [], []
    for _ in range(k):
        m = jnp.max(cur, axis=0, keepdims=True)
        pos = jnp.min(jnp.where(cur == m, pos_iota, n), axis=0, keepdims=True)
        hit = pos_iota == pos
        vals.append(m)
        picks.append(pos if payload is None else jnp.sum(jnp.where(hit, payload, 0), axis=0, keepdims=True))
        cur = jnp.where(hit, -jnp.inf, cur)
    return vals, picks


def _peer_route_kernel(h_ref, wpq_ref, keys_ref, ids_ref, g_ref):
    tn = h_ref.shape[0]
    half = PEER_QDIM // 2
    q = jnp.dot(h_ref[...].astype(MXU), wpq_ref[...], preferred_element_type=F32).astype(MXU)
    key_iota = lax.broadcasted_iota(jnp.int32, (PEER_NKEYS, tn), 0)
    pairs = [(a, b) for a in range(PEER_TOPK) for b in range(PEER_TOPK // (a + 1))]
    n_pairs = -(-len(pairs) // 8) * 8
    pair_iota = lax.broadcasted_iota(jnp.int32, (n_pairs, tn), 0)
    pad_v = [jnp.full((n_pairs - len(pairs), tn), -jnp.inf, F32)]
    pad_i = [jnp.zeros((n_pairs - len(pairs), tn), jnp.int32)]
    id_rows, g_rows = [], []
    for hd in range(PEER_HEADS):
        vals, idxs = [], []
        for p in range(2):
            c0 = (hd * 2 + p) * half
            s = lax.dot_general(keys_ref[hd * 2 + p], q[:, c0:c0 + half], _NT, preferred_element_type=F32)
            v_rows, i_rows = _take_topk(s, key_iota, PEER_TOPK)
            vals.append(v_rows)
            idxs.append(i_rows)
        cand = jnp.concatenate([vals[0][a] + vals[1][b] for a, b in pairs] + pad_v, axis=0)
        expert = jnp.concatenate([idxs[0][a] * PEER_NKEYS + idxs[1][b] for a, b in pairs] + pad_i, axis=0)
        cv, ce = _take_topk(cand, pair_iota, PEER_TOPK, payload=expert)
        cv = jnp.concatenate(cv, axis=0)
        e = jnp.exp(cv - cv[0:1, :])
        g_rows.append(e / jnp.sum(e, axis=0, keepdims=True))
        id_rows.append(jnp.concatenate(ce, axis=0))
    ids_ref[...] = jnp.concatenate(id_rows, axis=0).astype(F32).T.astype(jnp.int32)
    g_ref[...] = jnp.concatenate(g_rows, axis=0).T


def _peer_route(h2d, w_pq, sub_keys):
    n = h2d.shape[0]
    tn = ROUTE_TOKENS
    wpq = w_pq.reshape(D_MODEL, PEER_HEADS * PEER_QDIM).astype(MXU)
    keys = sub_keys.reshape(PEER_HEADS * 2, PEER_NKEYS, PEER_QDIM // 2).astype(MXU)
    return pl.pallas_call(
        _peer_route_kernel,
        grid=(n // tn,),
        in_specs=[pl.BlockSpec((tn, D_MODEL), lambda i: (i, 0)), pl.BlockSpec(wpq.shape, lambda i: (0, 0)),
                  pl.BlockSpec(keys.shape, lambda i: (0, 0, 0))],
        out_specs=[pl.BlockSpec((tn, PEER_SLOTS), lambda i: (i, 0)), pl.BlockSpec((tn, PEER_SLOTS), lambda i: (i, 0))],
        out_shape=[jax.ShapeDtypeStruct((n, PEER_SLOTS), jnp.int32), jax.ShapeDtypeStruct((n, PEER_SLOTS), F32)],
        compiler_params=pltpu.CompilerParams(dimension_semantics=('arbitrary',), vmem_limit_bytes=VMEM_LIMIT),
        name='peer_route',
    )(h2d, wpq, keys)


def _peer_expert_kernel(ids_ref, g_ref, h_ref, lng_ref, lnb_ref, uv_hbm, y_ref, buf, sem, gt_sc, out_sc):
    tb = h_ref.shape[0]
    ahead = GATHER_DEPTH - 1

    def row_copy(t, slot, k):
        src = pl.ds(pl.multiple_of(ids_ref[t, k] * EXPERT_ROWS, EXPERT_ROWS), EXPERT_ROWS)
        return pltpu.make_async_copy(uv_hbm.at[src, :], buf.at[slot, :, k, :], sem.at[slot])

    def issue(t, slot):
        for k in range(PEER_SLOTS):
            row_copy(t, slot, k).start(priority=k % 2)

    def consume(t, slot):
        for k in range(PEER_SLOTS):
            row_copy(t, slot, k).wait()
        x = h_ref[t]
        part = jnp.zeros((PEER_SLOTS, LANES), F32)
        for s in range(D_SUB):
            part = part + buf[slot, s] * x[s:s + 1, :]
        act = jnp.sum(part, axis=1, keepdims=True)
        g_col = jnp.sum(jnp.where(lane == t, gt_sc[...], 0.0), axis=1, keepdims=True)
        coef = jnp.broadcast_to(g_col * jax.nn.gelu(act), (PEER_SLOTS, LANES))
        rows = [jnp.sum(buf[slot, D_SUB + s] * coef, axis=0, keepdims=True) for s in range(D_SUB)]
        out_sc[t] = jnp.concatenate(rows, axis=0)

    gt_sc[...] = g_ref[...].T
    lane = lax.broadcasted_iota(jnp.int32, (PEER_SLOTS, tb), 1)
    for t in range(ahead):
        issue(t, t)

    def body(t, carry):
        issue(t + ahead, (t + ahead) % GATHER_DEPTH)
        consume(t, t % GATHER_DEPTH)
        return carry

    lax.fori_loop(0, tb - ahead, body, 0)
    for t in range(tb - ahead, tb):
        consume(t, t % GATHER_DEPTH)
    z = DN_ALPHA * h_ref[...] + out_sc[...]
    mean = lambda a: jnp.sum(jnp.sum(a, axis=2, keepdims=True), axis=1, keepdims=True) * (1.0 / D_MODEL)
    zc = z - mean(z)
    y_ref[...] = zc * lax.rsqrt(mean(zc * zc) + LN_EPS) * lng_ref[...] + lnb_ref[...]


def _peer_experts(h2d, ids, g, uv_rows, ln_g, ln_b):
    n = h2d.shape[0]
    tb = EXPERT_TOKENS
    row = lambda i: (i, 0)
    tok = lambda i: (i, 0, 0)
    y = pl.pallas_call(
        _peer_expert_kernel,
        grid=(n // tb,),
        in_specs=[pl.BlockSpec((tb, PEER_SLOTS), row, memory_space=pltpu.SMEM), pl.BlockSpec((tb, PEER_SLOTS), row),
                  pl.BlockSpec((tb, D_SUB, LANES), tok), pl.BlockSpec((D_SUB, LANES), lambda i: (0, 0)),
                  pl.BlockSpec((D_SUB, LANES), lambda i: (0, 0)), pl.BlockSpec(memory_space=pl.ANY)],
        out_specs=pl.BlockSpec((tb, D_SUB, LANES), tok),
        out_shape=jax.ShapeDtypeStruct((n, D_SUB, LANES), F32),
        scratch_shapes=[pltpu.VMEM((GATHER_DEPTH, EXPERT_ROWS, PEER_SLOTS, LANES), F32),
                        pltpu.SemaphoreType.DMA((GATHER_DEPTH,)), pltpu.VMEM((PEER_SLOTS, tb), F32),
                        pltpu.VMEM((tb, D_SUB, LANES), F32)],
        compiler_params=pltpu.CompilerParams(dimension_semantics=('arbitrary',), vmem_limit_bytes=VMEM_LIMIT),
        name='peer_experts',
    )(ids, g, h2d.reshape(n, D_SUB, LANES), ln_g.reshape(D_SUB, LANES), ln_b.reshape(D_SUB, LANES), uv_rows)
    return y.reshape(n, D_MODEL)


def _group_step(x, mixer, ws, w_up_a, w_up_b, w_out, ln1_g, ln1_b, w_pq, sub_keys, uv, ln2_g, ln2_b, tm):
    B, T, _ = x.shape
    x2d = x.reshape(B * T, D_MODEL)
    pr = _unpack_proj(_project(x2d, ws, tm), B, T)
    o_a, o_b, new_state = mixer(pr)
    h = _merge(x2d, o_a.reshape(B * T, -1), o_b.reshape(B * T, -1), pr['merge_gate'],
               w_up_a, w_up_b, w_out, ln1_g, ln1_b, tm)
    ids, g = _peer_route(h, w_pq, sub_keys)
    y = _peer_experts(h, ids, g, uv, ln2_g, ln2_b)
    return y.reshape(B, T, D_MODEL), new_state


def kernel(x_prompt, x_sample, cache_a_kv, cache_a_idx, cache_b_cmp_kv, cache_b_sel_kv, state_b_win_kv, page_table, w_in, rel_bias_table, cmp_pe, cmp_w1, cmp_w2, w_up_a, w_up_b, w_out, ln1_g, ln1_b, w_pq, peer_sub_keys, peer_u, peer_v, ln2_g, ln2_b):
    l = 0
    ws = _split_w_in(w_in[l])
    uv = jnp.concatenate([peer_u[l], peer_v[l]], axis=1).reshape(-1, LANES)
    tail = (w_up_a[l], w_up_b[l], w_out[l], ln1_g[l], ln1_b[l], w_pq[l], peer_sub_keys[l], uv, ln2_g[l], ln2_b[l])
    mix_p = functools.partial(_mix_prompt, rel_table=rel_bias_table, cmp_pe=cmp_pe[l], cmp_w1=cmp_w1[l],
                              cmp_w2=cmp_w2[l])
    y_p, st_p = _group_step(x_prompt, mix_p, ws, *tail, tm=256)
    mix_s = functools.partial(_mix_sample, cache_a_kv=cache_a_kv[l], cache_a_idx=cache_a_idx[l],
                              cache_b_cmp_kv=cache_b_cmp_kv[l], cache_b_sel_kv=cache_b_sel_kv[l],
                              state_b_win_kv=state_b_win_kv[l], page_table=page_table, rel_table=rel_bias_table,
                              cmp_pe=cmp_pe[l], cmp_w1=cmp_w1[l], cmp_w2=cmp_w2[l])
    y_s, st_s = _group_step(x_sample, mix_s, ws, *tail, tm=256)
    return (y_p, y_s) + tuple(s[None] for s in st_p) + tuple(s[None] for s in st_s)
```

```python
import functools
import math

import jax
import jax.numpy as jnp
import numpy as np
from jax import lax
from jax.experimental import pallas as pl
from jax.experimental.pallas import tpu as pltpu

D_MODEL = 1024
DEPTH = 1
PAGE_SIZE = 128
HEAD_DIM = 64
A_HEADS = 8
A_KV_HEADS = 2
IDX_HEADS = 4
IDX_DIM = 64
A_TOPK = 256
B_HEADS = 8
B_KV_HEADS = 2
NSA_BLOCK = 64
NSA_TOPN = 16
NSA_FORCE = 8.0
CMP_HIDDEN = 64
WINDOW = 512
REL_BUCKETS = 32
REL_MAX_EXACT = 16
REL_MAX_DIST = 128
PEER_HEADS = 8
PEER_NKEYS = 128
PEER_QDIM = 256
PEER_TOPK = 16
Q_BLOCK = 32
TOKEN_BLOCK = 256
LN_EPS = 1e-5
DN_ALPHA = (2 * DEPTH) ** 0.25

PROJ_LAYOUT = (
    ('a_q', A_HEADS * HEAD_DIM),
    ('a_kv', 2 * A_KV_HEADS * HEAD_DIM),
    ('a_iq', IDX_HEADS * IDX_DIM),
    ('a_iw', IDX_HEADS),
    ('a_ik', IDX_DIM),
    ('b_q', B_HEADS * HEAD_DIM),
    ('b_cmp_kv', 2 * B_KV_HEADS * HEAD_DIM),
    ('b_sel_kv', 2 * B_KV_HEADS * HEAD_DIM),
    ('b_win_kv', 2 * B_KV_HEADS * HEAD_DIM),
    ('b_gate', B_HEADS * 3),
    ('merge_gate', 2 * D_MODEL),
)

LANES = 128
VMEM_LIMIT = 56 * 1024 * 1024
MISC_GATE_OFF = IDX_HEADS

F32 = jnp.float32
BF16 = jnp.bfloat16
MXU = jnp.bfloat16


_PROJ_OUT = ('a_q', 'a_kv', 'a_iq', 'a_ik', 'misc', 'b_q', 'b_cmp_kv', 'b_sel_kv', 'b_win_kv', 'merge_gate')


def _proj_kernel(x_ref, *refs):
    n = len(_PROJ_OUT)
    w_refs, o_refs = refs[:n], refs[n:]
    x = x_ref[...].astype(BF16)
    for name, w_ref, o_ref in zip(_PROJ_OUT, w_refs, o_refs):
        r = jnp.dot(x, w_ref[...], preferred_element_type=F32)
        if name == 'misc':
            lane = lax.broadcasted_iota(jnp.int32, r.shape, 1)
            r = jnp.where(lane < MISC_GATE_OFF, r * IDX_HEADS ** -0.5, jax.nn.sigmoid(r))
        elif name == 'merge_gate':
            r = jax.nn.sigmoid(r)
        o_ref[...] = r


def _split_w_in(w_in):
    parts = {}
    off = 0
    for name, width in PROJ_LAYOUT:
        parts[name] = w_in[:, off:off + width]
        off += width
    misc = jnp.concatenate([parts['a_iw'], parts['b_gate']], axis=1)
    parts['misc'] = jnp.pad(misc, ((0, 0), (0, LANES - misc.shape[1])))
    return [parts[name].astype(BF16) for name in _PROJ_OUT]


def _project(x2d, ws, tm):
    n = x2d.shape[0]
    widths = [w.shape[1] for w in ws]
    return pl.pallas_call(
        _proj_kernel,
        grid=(n // tm,),
        in_specs=[pl.BlockSpec((tm, D_MODEL), lambda i: (i, 0))]
        + [pl.BlockSpec((D_MODEL, wd), lambda i: (0, 0)) for wd in widths],
        out_specs=[pl.BlockSpec((tm, wd), lambda i: (i, 0)) for wd in widths],
        out_shape=[jax.ShapeDtypeStruct((n, wd), F32) for wd in widths],
        compiler_params=pltpu.CompilerParams(dimension_semantics=('arbitrary',), vmem_limit_bytes=VMEM_LIMIT),
        name='project',
    )(x2d, *ws)


def _unpack_proj(outs, B, T):
    d = dict(zip(_PROJ_OUT, outs))
    misc = d['misc']
    kv = (B, T, 2, A_KV_HEADS, HEAD_DIM)
    return {
        'a_q': d['a_q'].reshape(B, T, A_HEADS, HEAD_DIM),
        'a_kv': d['a_kv'].reshape(kv),
        'a_iq': d['a_iq'].reshape(B, T, IDX_HEADS, IDX_DIM),
        'a_iw': misc[:, :IDX_HEADS].reshape(B, T, IDX_HEADS),
        'a_ik': d['a_ik'].reshape(B, T, IDX_DIM),
        'b_q': d['b_q'].reshape(B, T, B_HEADS, HEAD_DIM),
        'b_cmp_kv': d['b_cmp_kv'].reshape(kv),
        'b_sel_kv': d['b_sel_kv'].reshape(kv),
        'b_win_kv': d['b_win_kv'].reshape(kv),
        'b_gate': misc[:, MISC_GATE_OFF:MISC_GATE_OFF + 3 * B_HEADS].reshape(B, T, B_HEADS, 3),
        'merge_gate': d['merge_gate'],
        'raw': d,
    }


def _layer_norm(x, g, b):
    mu = jnp.mean(x, axis=-1, keepdims=True)
    xc = x - mu
    var = jnp.mean(xc * xc, axis=-1, keepdims=True)
    return xc * lax.rsqrt(var + LN_EPS) * g + b


def _merge_kernel(x_ref, oa_ref, ob_ref, mg_ref, wa_ref, wb_ref, wo_ref, g_ref, b_ref, h_ref):
    ya = jnp.dot(oa_ref[...].astype(BF16), wa_ref[...], preferred_element_type=F32)
    yb = jnp.dot(ob_ref[...].astype(BF16), wb_ref[...], preferred_element_type=F32)
    mg = mg_ref[...]
    mix = mg[:, :D_MODEL] * ya + mg[:, D_MODEL:] * yb
    mixed = jnp.dot(mix.astype(BF16), wo_ref[...], preferred_element_type=F32)
    h_ref[...] = _layer_norm(DN_ALPHA * x_ref[...] + mixed, g_ref[...], b_ref[...])


def _merge(x2d, o_a, o_b, mg, w_up_a, w_up_b, w_out, ln_g, ln_b, tm):
    n = x2d.shape[0]
    ca, cb = o_a.shape[1], o_b.shape[1]
    row = lambda i: (i, 0)
    fixed = lambda i: (0, 0)
    return pl.pallas_call(
        _merge_kernel,
        grid=(n // tm,),
        in_specs=[pl.BlockSpec((tm, D_MODEL), row), pl.BlockSpec((tm, ca), row), pl.BlockSpec((tm, cb), row),
                  pl.BlockSpec((tm, 2 * D_MODEL), row), pl.BlockSpec((ca, D_MODEL), fixed),
                  pl.BlockSpec((cb, D_MODEL), fixed), pl.BlockSpec((D_MODEL, D_MODEL), fixed),
                  pl.BlockSpec((1, D_MODEL), fixed), pl.BlockSpec((1, D_MODEL), fixed)],
        out_specs=pl.BlockSpec((tm, D_MODEL), row),
        out_shape=jax.ShapeDtypeStruct((n, D_MODEL), F32),
        compiler_params=pltpu.CompilerParams(dimension_semantics=('arbitrary',), vmem_limit_bytes=VMEM_LIMIT),
        name='merge',
    )(x2d, o_a, o_b, mg, w_up_a.astype(BF16), w_up_b.astype(BF16), w_out.astype(BF16),
      ln_g.reshape(1, D_MODEL), ln_b.reshape(1, D_MODEL))


TQ = 256
NEG = -1e30
INT_MIN = -2 ** 31
COUNT_ROWS = 256
_NT = (((1,), (1,)), ((), ()))
_TN = (((0,), (0,)), ((), ()))


def _bucket_np(d):
    n = np.maximum(d, 0)
    nf = np.maximum(n, 1).astype(np.float64)
    large = REL_MAX_EXACT + (np.log(nf / REL_MAX_EXACT) / math.log(REL_MAX_DIST / REL_MAX_EXACT)
                             * (REL_BUCKETS - REL_MAX_EXACT)).astype(np.int64)
    return np.where(n < REL_MAX_EXACT, n, np.minimum(large, REL_BUCKETS - 1)).astype(np.int32)


def _rel_bias_at(table, d):
    bucket = jnp.asarray(_bucket_np(d))[..., None]
    out = jnp.broadcast_to(table[REL_BUCKETS - 1], bucket.shape[:-1] + table.shape[1:])
    for b in range(REL_BUCKETS - 1):
        out = jnp.where(bucket == b, table[b], out)
    return out


def _band_bias(table, n_groups):
    d = TQ + np.arange(TQ)[None, :] - np.arange(2 * TQ)[:, None]
    band = _rel_bias_at(table, d) - table[REL_BUCKETS - 1]
    band = jnp.moveaxis(band, -1, 1)
    return jnp.moveaxis(band.reshape(2 * TQ, n_groups, -1), 1, 0)


def _block_bias(table, S):
    nb = S // NSA_BLOCK
    d = np.arange(S)[None, :] - ((np.arange(nb) + 1) * NSA_BLOCK - 1)[:, None]
    return jnp.moveaxis(_rel_bias_at(table, d), -1, 0)


def _ordered_keys(x):
    x = jnp.where(x == 0.0, 0.0, x)
    b = lax.bitcast_convert_type(x, jnp.int32)
    return b ^ ((b >> 31) & jnp.int32(0x7FFFFFFF))


def _softmax_reset(m_sc, l_sc, acc_sc):
    m_sc[...] = jnp.full(m_sc.shape, NEG, F32)
    l_sc[...] = jnp.zeros(l_sc.shape, F32)
    acc_sc[...] = jnp.zeros(acc_sc.shape, F32)


def _group_queries(q, n_groups):
    rep = q.shape[1] // HEAD_DIM // n_groups
    return [_stack_heads(q, range(g * rep, (g + 1) * rep), HEAD_DIM) for g in range(n_groups)]


def _attend_chunk(kv_c, q_groups, madd_of_group, band_ref, band_row0, m_sc, l_sc, acc_sc):
    n_g = len(q_groups)
    rep = q_groups[0].shape[0] // TQ
    for g in range(n_g):
        k_c = kv_c[:, g * HEAD_DIM:(g + 1) * HEAD_DIM]
        v_c = kv_c[:, (n_g + g) * HEAD_DIM:(n_g + g + 1) * HEAD_DIM]
        add = jnp.concatenate([madd_of_group(g)] * rep, axis=1)
        if band_row0 is not None:
            add = add + band_ref[g, band_row0:band_row0 + TQ, :]
        logits = lax.dot_general(k_c, q_groups[g], _NT, preferred_element_type=F32) + add
        m_old = m_sc[g]
        m_new = jnp.maximum(m_old, jnp.max(logits, axis=0, keepdims=True))
        alpha = jnp.exp(m_old - m_new)
        p = jnp.exp(logits - m_new)
        l_sc[g] = alpha * l_sc[g] + jnp.sum(p, axis=0, keepdims=True)
        m_sc[g] = m_new
        p = p.astype(MXU)
        for r in range(rep):
            h = g * rep + r
            lanes = slice(r * TQ, (r + 1) * TQ)
            pv = lax.dot_general(v_c, p[:, lanes], _TN, preferred_element_type=F32)
            acc_sc[h] = alpha[:, lanes] * acc_sc[h] + pv


def _softmax_scratch(n_heads, n_groups):
    stats = pltpu.VMEM((n_groups, 1, n_heads // n_groups * TQ), F32)
    return [stats, stats, pltpu.VMEM((n_heads, HEAD_DIM, TQ), F32)]


def _attend_result(h, rep, l_sc, acc_sc):
    g, r = divmod(h, rep)
    return acc_sc[h] / l_sc[g][:, r * TQ:(r + 1) * TQ]


def _dsa_kernel(iq_ref, misc_ref, q_ref, ik_ref, kv_ref, band_ref, o_ref, key_sc, m_sc, l_sc, acc_sc, out_sc,
                *, k_sel, pos_bits):
    i = pl.program_id(1)
    t0 = i * TQ
    n_chunks = i + 1
    lane_t = t0 + lax.broadcasted_iota(jnp.int32, (1, TQ), 1)
    sub_iota = lax.broadcasted_iota(jnp.int32, (TQ, TQ), 0)
    misc_t = misc_ref[...].T
    iq = iq_ref[...].astype(MXU)
    iq_heads = [iq[:, h * IDX_DIM:(h + 1) * IDX_DIM] for h in range(IDX_HEADS)]

    def score_chunk(j, carry):
        r0 = pl.multiple_of(j * TQ, TQ)
        ik_c = ik_ref[pl.ds(r0, TQ), :].astype(MXU)
        sc = jnp.zeros((TQ, TQ), F32)
        for h in range(IDX_HEADS):
            rel = lax.dot_general(ik_c, iq_heads[h], _NT, preferred_element_type=F32)
            sc = sc + jnp.maximum(rel * IDX_DIM ** -0.5, 0.0) * misc_t[h:h + 1, :]
        sc = jnp.where(r0 + sub_iota <= lane_t, sc, -jnp.inf)
        key_sc[pl.ds(r0, TQ), :] = _ordered_keys(sc)
        return carry

    lax.fori_loop(0, n_chunks, score_chunk, 0)

    def count(pred):
        def body(jj, acc):
            r0 = pl.multiple_of(jj * COUNT_ROWS, COUNT_ROWS)
            blk = key_sc[pl.ds(r0, COUNT_ROWS), :]
            pos = r0 + lax.broadcasted_iota(jnp.int32, (COUNT_ROWS, TQ), 0)
            hit = jnp.where(pred(blk, pos), 1, 0)
            return acc + jnp.sum(hit.reshape(COUNT_ROWS // 8, 8, TQ), axis=0)

        acc = lax.fori_loop(0, n_chunks * (TQ // COUNT_ROWS), body, jnp.zeros((8, TQ), jnp.int32))
        return jnp.sum(acc, axis=0, keepdims=True)

    v = jnp.full((1, TQ), INT_MIN, jnp.int32)
    v = jnp.where(count(lambda blk, pos: blk >= 0) >= k_sel, 0, v)

    def value_bit(it, v):
        cand = v | jnp.left_shift(jnp.int32(1), 30 - it)
        return jnp.where(count(lambda blk, pos: blk >= cand) >= k_sel, cand, v)

    v = lax.fori_loop(0, 31, value_bit, v)
    need = k_sel - count(lambda blk, pos: blk > v)
    n_ge = count(lambda blk, pos: blk >= v)

    def tie_search():
        def pos_bit(it, jm):
            cand = jm | jnp.left_shift(jnp.int32(1), pos_bits - 1 - it)
            return jnp.where(count(lambda blk, pos: (blk == v) & (pos < cand)) < need, cand, jm)

        return lax.fori_loop(0, pos_bits, pos_bit, jnp.zeros((1, TQ), jnp.int32))

    j_max = lax.cond(jnp.max(n_ge) > k_sel, tie_search, lambda: jnp.full((1, TQ), 2 ** 30, jnp.int32))

    q = (q_ref[...] * HEAD_DIM ** -0.5).astype(MXU)
    q_groups = _group_queries(q, A_KV_HEADS)
    _softmax_reset(m_sc, l_sc, acc_sc)

    def chunk(j, band_row0, causal):
        r0 = pl.multiple_of(j * TQ, TQ)
        key = key_sc[pl.ds(r0, TQ), :]
        pos = r0 + sub_iota
        sel = (key > v) | ((key == v) & (pos <= j_max))
        if causal:
            sel = sel & (pos <= lane_t)
        madd = jnp.where(sel, 0.0, NEG)
        kv_c = kv_ref[pl.ds(r0, TQ), :].astype(MXU)
        _attend_chunk(kv_c, q_groups, lambda g: madd, band_ref, band_row0, m_sc, l_sc, acc_sc)

    def far_chunk(j, carry):
        chunk(j, None, False)
        return carry

    lax.fori_loop(0, jnp.maximum(i - 1, 0), far_chunk, 0)

    @pl.when(i >= 1)
    def _():
        chunk(i - 1, 0, False)

    chunk(i, TQ, True)
    for h in range(A_HEADS):
        out_sc[h * HEAD_DIM:(h + 1) * HEAD_DIM, :] = _attend_result(h, A_HEADS // A_KV_HEADS, l_sc, acc_sc)
    o_ref[...] = out_sc[...].T


def _dsa_prompt(raw, band_a, B, S):
    nq = S // TQ
    k_sel = min(A_TOPK, S // 4)
    assert S % TQ == 0 and TQ >= k_sel
    tile = lambda b, i: (b * nq + i, 0)
    seq = lambda b, i: (b, 0)
    kern = functools.partial(_dsa_kernel, k_sel=k_sel, pos_bits=(S - 1).bit_length())
    return pl.pallas_call(
        kern,
        grid=(B, nq),
        in_specs=[pl.BlockSpec((TQ, IDX_HEADS * IDX_DIM), tile), pl.BlockSpec((TQ, LANES), tile),
                  pl.BlockSpec((TQ, A_HEADS * HEAD_DIM), tile), pl.BlockSpec((S, IDX_DIM), seq),
                  pl.BlockSpec((S, 2 * A_KV_HEADS * HEAD_DIM), seq),
                  pl.BlockSpec(band_a.shape, lambda b, i: (0, 0, 0))],
        out_specs=pl.BlockSpec((TQ, A_HEADS * HEAD_DIM), tile),
        out_shape=jax.ShapeDtypeStruct((B * S, A_HEADS * HEAD_DIM), F32),
        scratch_shapes=[pltpu.VMEM((S, TQ), jnp.int32)] + _softmax_scratch(A_HEADS, A_KV_HEADS)
        + [pltpu.VMEM((A_HEADS * HEAD_DIM, TQ), F32)],
        compiler_params=pltpu.CompilerParams(dimension_semantics=('arbitrary', 'arbitrary'),
                                             vmem_limit_bytes=VMEM_LIMIT),
        name='dsa_prompt',
    )(raw['a_iq'], raw['misc'], raw['a_q'], raw['a_ik'], raw['a_kv'], band_a)


def _compress_kernel(x_ref, pe_ref, w1_ref, w2_ref, o_ref):
    x = (x_ref[...] + pe_ref[...]).astype(MXU)
    h = jax.nn.gelu(jnp.dot(x, w1_ref[...], preferred_element_type=F32))
    o_ref[...] = jnp.dot(h.astype(MXU), w2_ref[...], preferred_element_type=F32)


def _compress_weights(pe, w1, w2):
    eye_c = jnp.eye(2, dtype=F32)
    eye_g = jnp.eye(B_KV_HEADS, dtype=F32)
    w1_big = jnp.einsum('lcde,cC,gG->lcgdCGe', w1, eye_c, eye_g)
    w1_big = w1_big.reshape(NSA_BLOCK * 2 * B_KV_HEADS * HEAD_DIM, 2 * B_KV_HEADS * CMP_HIDDEN)
    w2_big = jnp.einsum('cef,cC,gG->cgeCGf', w2, eye_c, eye_g)
    w2_big = w2_big.reshape(2 * B_KV_HEADS * CMP_HIDDEN, 2 * B_KV_HEADS * HEAD_DIM)
    pe_flat = jnp.broadcast_to(pe[:, :, None, :], (NSA_BLOCK, 2, B_KV_HEADS, HEAD_DIM)).reshape(1, -1)
    return pe_flat, w1_big.astype(MXU), w2_big.astype(MXU)


def _compress(blocks2d, pe_flat, w1_big, w2_big, tm):
    n, width = blocks2d.shape
    fixed = lambda i: (0, 0)
    return pl.pallas_call(
        _compress_kernel,
        grid=(n // tm,),
        in_specs=[pl.BlockSpec((tm, width), lambda i: (i, 0)), pl.BlockSpec((1, width), fixed),
                  pl.BlockSpec(w1_big.shape, fixed), pl.BlockSpec(w2_big.shape, fixed)],
        out_specs=pl.BlockSpec((tm, w2_big.shape[1]), lambda i: (i, 0)),
        out_shape=jax.ShapeDtypeStruct((n, w2_big.shape[1]), F32),
        compiler_params=pltpu.CompilerParams(dimension_semantics=('arbitrary',), vmem_limit_bytes=VMEM_LIMIT),
        name='nsa_compress',
    )(blocks2d, pe_flat, w1_big, w2_big)


def _nsa_kernel(q_ref, misc_ref, cmp_ref, selkv_ref, winkv_ref, band_ref, biasc_ref, o_ref,
                sel_sc, m_sc, l_sc, acc_sc, out_sc, *, nb, n_sel):
    i = pl.program_id(1)
    t0 = i * TQ
    n_g = B_KV_HEADS
    rep = B_HEADS // n_g
    lane_t = t0 + lax.broadcasted_iota(jnp.int32, (1, TQ), 1)
    sub_iota = lax.broadcasted_iota(jnp.int32, (TQ, TQ), 0)
    lane_iota = lax.broadcasted_iota(jnp.int32, (TQ, TQ), 1)
    misc_t = misc_ref[...].T
    gate = lambda h, k: misc_t[MISC_GATE_OFF + 3 * h + k:MISC_GATE_OFF + 3 * h + k + 1, :]
    q = (q_ref[...] * HEAD_DIM ** -0.5).astype(MXU)
    q_heads = [q[:, h * HEAD_DIM:(h + 1) * HEAD_DIM] for h in range(B_HEADS)]
    q_groups = _group_queries(q, n_g)

    cmp = cmp_ref[...].astype(MXU)
    blk = lax.broadcasted_iota(jnp.int32, (nb, TQ), 0)
    visible = (blk + 1) * NSA_BLOCK - 1 <= lane_t
    cur = lax.shift_right_logical(lane_t, NSA_BLOCK.bit_length() - 1)
    forced = (blk == 0) | (blk == cur) | (blk == cur - 1)
    for g in range(n_g):
        k_c = cmp[:, g * HEAD_DIM:(g + 1) * HEAD_DIM]
        v_c = cmp[:, (n_g + g) * HEAD_DIM:(n_g + g + 1) * HEAD_DIM]
        importance = jnp.zeros((nb, TQ), F32)
        for r in range(rep):
            h = g * rep + r
            lc = lax.dot_general(k_c, q_heads[h], _NT, preferred_element_type=F32) + biasc_ref[h]
            lc = jnp.where(visible, lc, -jnp.inf)
            m = jnp.max(lc, axis=0, keepdims=True)
            m = jnp.where(m > -jnp.inf, m, 0.0)
            e = jnp.exp(lc - m)
            s = jnp.sum(e, axis=0, keepdims=True)
            p = e / jnp.where(s > 0, s, 1.0)
            importance = importance + p
            o_cmp = lax.dot_general(v_c, p.astype(MXU), _TN, preferred_element_type=F32)
            out_sc[h * HEAD_DIM:(h + 1) * HEAD_DIM, :] = gate(h, 0) * o_cmp
        score = jnp.where(forced, NSA_FORCE, importance)
        score = jnp.where(blk <= cur, score, -1.0)
        rank = jnp.zeros((nb, TQ), jnp.int32)
        for n in range(nb):
            row = score[n:n + 1, :]
            beats = (row > score) | ((row == score) & (blk > n))
            rank = rank + jnp.where(beats, 1, 0)
        sel_sc[g] = jnp.where(rank < n_sel, 0.0, NEG)

    def finish_branch(k):
        for h in range(B_HEADS):
            rows = slice(h * HEAD_DIM, (h + 1) * HEAD_DIM)
            out_sc[rows, :] = out_sc[rows, :] + gate(h, k) * _attend_result(h, rep, l_sc, acc_sc)

    _softmax_reset(m_sc, l_sc, acc_sc)
    blocks_per_chunk = TQ // NSA_BLOCK

    def sel_chunk(j, band_row0, causal):
        r0 = pl.multiple_of(j * TQ, TQ)
        kv_c = selkv_ref[pl.ds(r0, TQ), :].astype(MXU)

        def madd_of_group(g):
            rows = [sel_sc[g, pl.ds(j * blocks_per_chunk + b, 1), :] for b in range(blocks_per_chunk)]
            madd = jnp.concatenate([jnp.broadcast_to(row, (NSA_BLOCK, TQ)) for row in rows], axis=0)
            if causal:
                madd = jnp.where(sub_iota <= lane_iota, madd, NEG)
            return madd

        _attend_chunk(kv_c, q_groups, madd_of_group, band_ref, band_row0, m_sc, l_sc, acc_sc)

    def far_chunk(j, carry):
        sel_chunk(j, None, False)
        return carry

    lax.fori_loop(0, jnp.maximum(i - 1, 0), far_chunk, 0)

    @pl.when(i >= 1)
    def _():
        sel_chunk(i - 1, 0, False)

    sel_chunk(i, TQ, True)
    finish_branch(1)

    _softmax_reset(m_sc, l_sc, acc_sc)
    for back in range(WINDOW // TQ, -1, -1):
        dist = back * TQ + lane_iota - sub_iota
        madd = jnp.where((dist >= 0) & (dist < WINDOW), 0.0, NEG)
        band_row0 = {0: TQ, 1: 0}.get(back)

        def win_chunk(back=back, madd=madd, band_row0=band_row0):
            r0 = pl.multiple_of((i - back) * TQ, TQ)
            kv_c = winkv_ref[pl.ds(r0, TQ), :].astype(MXU)
            _attend_chunk(kv_c, q_groups, lambda g: madd, band_ref, band_row0, m_sc, l_sc, acc_sc)

        if back == 0:
            win_chunk()
        else:
            pl.when(i >= back)(win_chunk)
    finish_branch(2)
    o_ref[...] = out_sc[...].T


def _nsa_prompt(raw, cmp, band_b, bias_c, B, S):
    nq = S // TQ
    nb = S // NSA_BLOCK
    assert S % TQ == 0 and TQ % NSA_BLOCK == 0
    tile = lambda b, i: (b * nq + i, 0)
    seq = lambda b, i: (b, 0)
    kv_w = 2 * B_KV_HEADS * HEAD_DIM
    kern = functools.partial(_nsa_kernel, nb=nb, n_sel=min(NSA_TOPN, nb))
    return pl.pallas_call(
        kern,
        grid=(B, nq),
        in_specs=[pl.BlockSpec((TQ, B_HEADS * HEAD_DIM), tile), pl.BlockSpec((TQ, LANES), tile),
                  pl.BlockSpec((nb, kv_w), seq), pl.BlockSpec((S, kv_w), seq), pl.BlockSpec((S, kv_w), seq),
                  pl.BlockSpec(band_b.shape, lambda b, i: (0, 0, 0)),
                  pl.BlockSpec((B_HEADS, nb, TQ), lambda b, i: (0, 0, i))],
        out_specs=pl.BlockSpec((TQ, B_HEADS * HEAD_DIM), tile),
        out_shape=jax.ShapeDtypeStruct((B * S, B_HEADS * HEAD_DIM), F32),
        scratch_shapes=[pltpu.VMEM((B_KV_HEADS, nb, TQ), F32)] + _softmax_scratch(B_HEADS, B_KV_HEADS)
        + [pltpu.VMEM((B_HEADS * HEAD_DIM, TQ), F32)],
        compiler_params=pltpu.CompilerParams(dimension_semantics=('arbitrary', 'arbitrary'),
                                             vmem_limit_bytes=VMEM_LIMIT),
        name='nsa_prompt',
    )(raw['b_q'], raw['misc'], cmp, raw['b_sel_kv'], raw['b_win_kv'], band_b, bias_c)


def _mix_prompt(pr, rel_table, cmp_pe, cmp_w1, cmp_w2):
    B, S = pr['a_q'].shape[:2]
    raw = pr['raw']
    table_a, table_b = rel_table[:, :A_HEADS], rel_table[:, A_HEADS:]
    pe_flat, w1_big, w2_big = _compress_weights(cmp_pe, cmp_w1, cmp_w2)
    n_blocks = B * S // NSA_BLOCK
    cmp = _compress(raw['b_cmp_kv'].reshape(n_blocks, -1), pe_flat, w1_big, w2_big, tm=min(128, n_blocks))
    o_a = _dsa_prompt(raw, _band_bias(table_a, A_KV_HEADS), B, S)
    o_b = _nsa_prompt(raw, cmp, _band_bias(table_b, B_KV_HEADS), _block_bias(table_b, S), B, S)
    wb = min(WINDOW, S)
    new_state = (pr['a_kv'], pr['a_ik'], pr['b_cmp_kv'], pr['b_sel_kv'], pr['b_win_kv'][:, S - wb:])
    return o_a, o_b, new_state


PAGES_PER_STEP = 8
KEY_PAD = 128


def _paged_specs(width):
    def spec(j):
        return pl.BlockSpec((1, PAGE_SIZE, width), lambda s, p, pt: (pt[s, p * PAGES_PER_STEP + j], 0, 0))
    return [spec(j) for j in range(PAGES_PER_STEP)]


def _stage_pages(page_refs, dst_sc):
    p = pl.program_id(1)
    for j, ref in enumerate(page_refs):
        r0 = pl.multiple_of((p * PAGES_PER_STEP + j) * PAGE_SIZE, PAGE_SIZE)
        dst_sc[pl.ds(r0, PAGE_SIZE), :] = ref[0]


def _stage_new_rows(new_ref, dst_sc, past):
    t = new_ref.shape[0]
    dst_sc[past:past + t, :] = new_ref[...]
    dst_sc[past + t:, :] = jnp.zeros((dst_sc.shape[0] - past - t, dst_sc.shape[1]), F32)


def _rows_softmax_attend(q_rows, k, v, bias, madd):
    logits = lax.dot_general(q_rows, k, _NT, preferred_element_type=F32) + bias + madd
    m = jnp.max(logits, axis=1, keepdims=True)
    e = jnp.exp(logits - m)
    p = e / jnp.sum(e, axis=1, keepdims=True)
    return jnp.dot(p.astype(MXU), v, preferred_element_type=F32)


def _stack_heads(x, heads, width):
    return jnp.concatenate([x[:, h * width:(h + 1) * width] for h in heads], axis=0)


def _dsa_sample_kernel(pt_ref, *refs, past, k_sel):
    n = PAGES_PER_STEP
    idx_pages, kv_pages = refs[:n], refs[n:2 * n]
    ikn_ref, kvn_ref, iq_ref, misc_ref, q_ref, bias_ref, o_ref, ik_sc, kv_sc = refs[2 * n:]
    _stage_pages(idx_pages, ik_sc)
    _stage_pages(kv_pages, kv_sc)

    @pl.when(pl.program_id(1) == pl.num_programs(1) - 1)
    def _():
        T = iq_ref.shape[0]
        L = ik_sc.shape[0]
        _stage_new_rows(ikn_ref, ik_sc, past)
        _stage_new_rows(kvn_ref, kv_sc, past)
        pos = lax.broadcasted_iota(jnp.int32, (T, L), 1)
        valid = pos <= past + lax.broadcasted_iota(jnp.int32, (T, L), 0)
        misc = misc_ref[...]
        iq = _stack_heads(iq_ref[...].astype(MXU), range(IDX_HEADS), IDX_DIM)
        rel = lax.dot_general(iq, ik_sc[...].astype(MXU), _NT, preferred_element_type=F32)
        sc = jnp.zeros((T, L), F32)
        for h in range(IDX_HEADS):
            sc = sc + jnp.maximum(rel[h * T:(h + 1) * T] * IDX_DIM ** -0.5, 0.0) * misc[:, h:h + 1]
        key = _ordered_keys(jnp.where(valid, sc, -jnp.inf))

        def count(hit):
            return jnp.sum(jnp.where(hit, 1, 0), axis=1, keepdims=True)

        v = jnp.full((T, 1), INT_MIN, jnp.int32)
        v = jnp.where(count(key >= 0) >= k_sel, 0, v)

        def value_bit(it, v):
            cand = v | jnp.left_shift(jnp.int32(1), 30 - it)
            return jnp.where(count(key >= cand) >= k_sel, cand, v)

        v = lax.fori_loop(0, 31, value_bit, v)
        need = k_sel - count(key > v)
        pos_bits = (L - 1).bit_length()

        def tie_search():
            def pos_bit(it, jm):
                cand = jm | jnp.left_shift(jnp.int32(1), pos_bits - 1 - it)
                return jnp.where(count((key == v) & (pos < cand)) < need, cand, jm)

            return lax.fori_loop(0, pos_bits, pos_bit, jnp.zeros((T, 1), jnp.int32))

        j_max = lax.cond(jnp.max(count(key >= v)) > k_sel, tie_search,
                         lambda: jnp.full((T, 1), 2 ** 30, jnp.int32))
        sel = ((key > v) | ((key == v) & (pos <= j_max))) & valid
        madd = jnp.where(sel, 0.0, NEG)
        rep = A_HEADS // A_KV_HEADS
        madd = jnp.concatenate([madd] * rep, axis=0)
        q = (q_ref[...] * HEAD_DIM ** -0.5).astype(MXU)
        kv = kv_sc[...].astype(MXU)
        for g in range(A_KV_HEADS):
            q_rows = _stack_heads(q, range(g * rep, (g + 1) * rep), HEAD_DIM)
            o = _rows_softmax_attend(q_rows, kv[:, g * HEAD_DIM:(g + 1) * HEAD_DIM],
                                     kv[:, (A_KV_HEADS + g) * HEAD_DIM:(A_KV_HEADS + g + 1) * HEAD_DIM],
                                     bias_ref[g], madd)
            for r in range(rep):
                h = g * rep + r
                o_ref[:, h * HEAD_DIM:(h + 1) * HEAD_DIM] = o[r * T:(r + 1) * T, :]


def _sample_bias(table, T, past, n_keys, key_pos0, n_groups):
    d = (past + np.arange(T))[:, None] - (key_pos0 + np.arange(n_keys))[None, :]
    b = _rel_bias_at(table, d)
    b = jnp.moveaxis(b, -1, 0)
    return b.reshape(n_groups, -1, n_keys)


def _seq_rows(width, T):
    return pl.BlockSpec((T, width), lambda s, p, pt: (s, 0))


def _dsa_sample(raw, pool_idx, pool_kv, page_table, table_a, DB, T):
    n_pages = page_table.shape[1]
    past = n_pages * PAGE_SIZE
    L = past + KEY_PAD
    assert n_pages % PAGES_PER_STEP == 0 and T <= KEY_PAD
    kv_w = 2 * A_KV_HEADS * HEAD_DIM
    bias = _sample_bias(table_a, T, past, L, 0, A_KV_HEADS)
    fixed3 = lambda s, p, pt: (0, 0, 0)
    grid_spec = pltpu.PrefetchScalarGridSpec(
        num_scalar_prefetch=1,
        grid=(DB, n_pages // PAGES_PER_STEP),
        in_specs=_paged_specs(IDX_DIM) + _paged_specs(kv_w)
        + [_seq_rows(IDX_DIM, T), _seq_rows(kv_w, T), _seq_rows(IDX_HEADS * IDX_DIM, T), _seq_rows(LANES, T),
           _seq_rows(A_HEADS * HEAD_DIM, T), pl.BlockSpec(bias.shape, fixed3)],
        out_specs=_seq_rows(A_HEADS * HEAD_DIM, T),
        scratch_shapes=[pltpu.VMEM((L, IDX_DIM), F32), pltpu.VMEM((L, kv_w), F32)],
    )
    kern = functools.partial(_dsa_sample_kernel, past=past, k_sel=min(A_TOPK, (past + T) // 4))
    return pl.pallas_call(
        kern, grid_spec=grid_spec,
        out_shape=jax.ShapeDtypeStruct((DB * T, A_HEADS * HEAD_DIM), F32),
        compiler_params=pltpu.CompilerParams(dimension_semantics=('arbitrary', 'arbitrary'),
                                             vmem_limit_bytes=VMEM_LIMIT),
        name='dsa_sample',
    )(page_table, *([pool_idx] * PAGES_PER_STEP), *([pool_kv] * PAGES_PER_STEP),
      raw['a_ik'], raw['a_kv'], raw['a_iq'], raw['misc'], raw['a_q'], bias)


def _compress_sample_kernel(pt_ref, *refs, past):
    n = PAGES_PER_STEP
    pages = refs[:n]
    new_ref, pe_ref, w1_ref, w2_ref, o_ref, xk_sc, xv_sc = refs[n:]
    halves = (xk_sc, xv_sc)
    p = pl.program_id(1)
    for j, ref in enumerate(pages):
        r0 = pl.multiple_of((p * PAGES_PER_STEP + j) * PAGE_SIZE, PAGE_SIZE)
        for c, x_sc in enumerate(halves):
            x_sc[pl.ds(r0, PAGE_SIZE), :] = ref[0, :, c * LANES:(c + 1) * LANES]

    @pl.when(p == pl.num_programs(1) - 1)
    def _():
        nbp = o_ref.shape[1]
        t = new_ref.shape[0]
        for c, x_sc in enumerate(halves):
            x_sc[past:past + t, :] = new_ref[:, c * LANES:(c + 1) * LANES]
            x_sc[past + t:, :] = jnp.zeros((x_sc.shape[0] - past - t, LANES), F32)

        def body(l, accs):
            out = []
            for c, (x_sc, acc) in enumerate(zip(halves, accs)):
                xl = x_sc[pl.ds(l, nbp, stride=NSA_BLOCK), :] + pe_ref[c, pl.ds(l, 1), :]
                out.append(acc + jnp.dot(xl.astype(MXU), w1_ref[l, c], preferred_element_type=F32))
            return tuple(out)

        zero = jnp.zeros((nbp, LANES), F32)
        h = jnp.concatenate(lax.fori_loop(0, NSA_BLOCK, body, (zero, zero)), axis=1)
        o_ref[0] = jnp.dot(jax.nn.gelu(h).astype(MXU), w2_ref[...], preferred_element_type=F32)


def _padded_blocks(past, T):
    nb = -(-(past + T) // NSA_BLOCK)
    return nb, -(-nb // 8) * 8


def _compress_sample(raw, pool_cmp, page_table, pe_flat, w1_big, w2_big, DB, T):
    n_pages = page_table.shape[1]
    past = n_pages * PAGE_SIZE
    _, nbp = _padded_blocks(past, T)
    kv_w = 2 * B_KV_HEADS * HEAD_DIM
    assert kv_w == 2 * LANES and w1_big.shape[1] == 2 * LANES
    w1_l = w1_big.reshape(NSA_BLOCK, 2, LANES, 2, LANES)
    w1_halves = jnp.stack([w1_l[:, 0, :, 0, :], w1_l[:, 1, :, 1, :]], axis=1)
    fixed2 = lambda s, p, pt: (0, 0)
    grid_spec = pltpu.PrefetchScalarGridSpec(
        num_scalar_prefetch=1,
        grid=(DB, n_pages // PAGES_PER_STEP),
        in_specs=_paged_specs(kv_w) + [_seq_rows(kv_w, T),
                                       pl.BlockSpec((2, NSA_BLOCK, LANES), lambda s, p, pt: (0, 0, 0)),
                                       pl.BlockSpec(w1_halves.shape, lambda s, p, pt: (0, 0, 0, 0)),
                                       pl.BlockSpec(w2_big.shape, fixed2)],
        out_specs=pl.BlockSpec((1, nbp, kv_w), lambda s, p, pt: (s, 0, 0)),
        scratch_shapes=[pltpu.VMEM((nbp * NSA_BLOCK, LANES), F32), pltpu.VMEM((nbp * NSA_BLOCK, LANES), F32)],
    )
    return pl.pallas_call(
        functools.partial(_compress_sample_kernel, past=past), grid_spec=grid_spec,
        out_shape=jax.ShapeDtypeStruct((DB, nbp, kv_w), F32),
        compiler_params=pltpu.CompilerParams(dimension_semantics=('arbitrary', 'arbitrary'),
                                             vmem_limit_bytes=VMEM_LIMIT),
        name='nsa_compress_sample',
    )(page_table, *([pool_cmp] * PAGES_PER_STEP), raw['b_cmp_kv'],
      pe_flat.reshape(NSA_BLOCK, 2, LANES).swapaxes(0, 1), w1_halves, w2_big)


def _nsa_sample_kernel(pt_ref, *refs, past, nb, n_sel):
    n = PAGES_PER_STEP
    pages = refs[:n]
    (seln_ref, winn_ref, q_ref, misc_ref, cmp_ref, win_ref, expand_ref, bias_c_ref, bias_s_ref, bias_w_ref,
     o_ref, kv_sc, win_sc) = refs[n:]
    _stage_pages(pages, kv_sc)

    @pl.when(pl.program_id(1) == pl.num_programs(1) - 1)
    def _():
        T = q_ref.shape[0]
        L = kv_sc.shape[0]
        n_g = B_KV_HEADS
        rep = B_HEADS // n_g
        wb = win_ref.shape[1]
        _stage_new_rows(seln_ref, kv_sc, past)
        win_sc[0:wb, :] = win_ref[0]
        _stage_new_rows(winn_ref, win_sc, wb)
        misc = misc_ref[...]
        gate = lambda h, k: misc[:, MISC_GATE_OFF + 3 * h + k:MISC_GATE_OFF + 3 * h + k + 1]
        q = (q_ref[...] * HEAD_DIM ** -0.5).astype(MXU)
        cmp = cmp_ref[0].astype(MXU)
        nbp = cmp.shape[0]
        t_col = lax.broadcasted_iota(jnp.int32, (T, 1), 0)
        q_pos = past + t_col
        blk = lax.broadcasted_iota(jnp.int32, (T, nbp), 1)
        cur = lax.shift_right_logical(q_pos, NSA_BLOCK.bit_length() - 1)
        visible = (blk + 1) * NSA_BLOCK - 1 <= q_pos
        visible_rows = jnp.concatenate([visible] * rep, axis=0)
        forced = (blk == 0) | (blk == cur) | (blk == cur - 1)
        pos = lax.broadcasted_iota(jnp.int32, (T, L), 1)
        causal = pos <= q_pos
        kv = kv_sc[...].astype(MXU)
        win = win_sc[...].astype(MXU)
        wpos = lax.broadcasted_iota(jnp.int32, (T, win.shape[0]), 1)
        dw = wb + t_col - wpos
        madd_w = jnp.where((dw >= 0) & (dw < WINDOW), 0.0, NEG)
        madd_w = jnp.concatenate([madd_w] * rep, axis=0)
        outs = []
        for g in range(n_g):
            heads = range(g * rep, (g + 1) * rep)
            ks = slice(g * HEAD_DIM, (g + 1) * HEAD_DIM)
            vs = slice((n_g + g) * HEAD_DIM, (n_g + g + 1) * HEAD_DIM)
            q_rows = _stack_heads(q, heads, HEAD_DIM)
            lc = lax.dot_general(q_rows, cmp[:, ks], _NT, preferred_element_type=F32) + bias_c_ref[g]
            lc = jnp.where(visible_rows, lc, -jnp.inf)
            m = jnp.max(lc, axis=1, keepdims=True)
            m = jnp.where(m > -jnp.inf, m, 0.0)
            e = jnp.exp(lc - m)
            s = jnp.sum(e, axis=1, keepdims=True)
            p = e / jnp.where(s > 0, s, 1.0)
            o_cmp = jnp.dot(p.astype(MXU), cmp[:, vs], preferred_element_type=F32)
            importance = p[0:T]
            for r in range(1, rep):
                importance = importance + p[r * T:(r + 1) * T]
            score = jnp.where(forced, NSA_FORCE, importance)
            score = jnp.where(blk <= cur, score, -1.0)
            rank = jnp.zeros((T, nbp), jnp.int32)
            for b in range(nb):
                col = score[:, b:b + 1]
                rank = rank + jnp.where((col > score) | ((col == score) & (blk > b)), 1, 0)
            picked = jnp.where(rank < n_sel, 1.0, 0.0).astype(MXU)
            on_keys = jnp.dot(picked, expand_ref[...], preferred_element_type=F32)
            madd_s = jnp.where((on_keys > 0.5) & causal, 0.0, NEG)
            madd_s = jnp.concatenate([madd_s] * rep, axis=0)
            o_slc = _rows_softmax_attend(q_rows, kv[:, ks], kv[:, vs], bias_s_ref[g], madd_s)
            o_win = _rows_softmax_attend(q_rows, win[:, ks], win[:, vs], bias_w_ref[g], madd_w)
            for r, h in enumerate(heads):
                rows = slice(r * T, (r + 1) * T)
                outs.append(gate(h, 0) * o_cmp[rows] + gate(h, 1) * o_slc[rows] + gate(h, 2) * o_win[rows])
        o_ref[...] = jnp.concatenate(outs, axis=1)


def _nsa_sample(raw, cmp, pool_sel, win_state, page_table, table_b, DB, T):
    n_pages = page_table.shape[1]
    past = n_pages * PAGE_SIZE
    L = past + KEY_PAD
    nb, nbp = _padded_blocks(past, T)
    wb = win_state.shape[1]
    lw = wb + KEY_PAD
    kv_w = 2 * B_KV_HEADS * HEAD_DIM
    n_g = B_KV_HEADS
    expand = jnp.asarray(np.arange(L)[None, :] // NSA_BLOCK == np.arange(nbp)[:, None], dtype=MXU)
    blk_end = (np.arange(nbp) + 1) * NSA_BLOCK - 1
    d_c = (past + np.arange(T))[:, None] - blk_end[None, :]
    bias_c = jnp.moveaxis(_rel_bias_at(table_b, d_c), -1, 0).reshape(n_g, -1, nbp)
    bias_s = _sample_bias(table_b, T, past, L, 0, n_g)
    bias_w = _sample_bias(table_b, T, past, lw, past - wb, n_g)
    fixed2 = lambda s, p, pt: (0, 0)
    fixed3 = lambda s, p, pt: (0, 0, 0)
    per_seq3 = lambda s, p, pt: (s, 0, 0)
    grid_spec = pltpu.PrefetchScalarGridSpec(
        num_scalar_prefetch=1,
        grid=(DB, n_pages // PAGES_PER_STEP),
        in_specs=_paged_specs(kv_w)
        + [_seq_rows(kv_w, T), _seq_rows(kv_w, T), _seq_rows(B_HEADS * HEAD_DIM, T), _seq_rows(LANES, T),
           pl.BlockSpec((1, nbp, kv_w), per_seq3), pl.BlockSpec((1, wb, kv_w), per_seq3),
           pl.BlockSpec(expand.shape, fixed2), pl.BlockSpec(bias_c.shape, fixed3),
           pl.BlockSpec(bias_s.shape, fixed3), pl.BlockSpec(bias_w.shape, fixed3)],
        out_specs=_seq_rows(B_HEADS * HEAD_DIM, T),
        scratch_shapes=[pltpu.VMEM((L, kv_w), F32), pltpu.VMEM((lw, kv_w), F32)],
    )
    kern = functools.partial(_nsa_sample_kernel, past=past, nb=nb, n_sel=min(NSA_TOPN, nb))
    return pl.pallas_call(
        kern, grid_spec=grid_spec,
        out_shape=jax.ShapeDtypeStruct((DB * T, B_HEADS * HEAD_DIM), F32),
        compiler_params=pltpu.CompilerParams(dimension_semantics=('arbitrary', 'arbitrary'),
                                             vmem_limit_bytes=VMEM_LIMIT),
        name='nsa_sample',
    )(page_table, *([pool_sel] * PAGES_PER_STEP), raw['b_sel_kv'], raw['b_win_kv'], raw['b_q'], raw['misc'],
      cmp, win_state, expand, bias_c, bias_s, bias_w)


def _mix_sample(pr, cache_a_kv, cache_a_idx, cache_b_cmp_kv, cache_b_sel_kv, state_b_win_kv, page_table,
                rel_table, cmp_pe, cmp_w1, cmp_w2):
    DB, T = pr['a_q'].shape[:2]
    raw = pr['raw']
    n_pool = cache_a_kv.shape[0]
    kv_w = 2 * A_KV_HEADS * HEAD_DIM
    table_a, table_b = rel_table[:, :A_HEADS], rel_table[:, A_HEADS:]
    pe_flat, w1_big, w2_big = _compress_weights(cmp_pe, cmp_w1, cmp_w2)
    win_state = state_b_win_kv.reshape(DB, -1, kv_w)
    o_a = _dsa_sample(raw, cache_a_idx, cache_a_kv.reshape(n_pool, PAGE_SIZE, kv_w), page_table, table_a, DB, T)
    cmp = _compress_sample(raw, cache_b_cmp_kv.reshape(n_pool, PAGE_SIZE, kv_w), page_table, pe_flat, w1_big,
                           w2_big, DB, T)
    o_b = _nsa_sample(raw, cmp, cache_b_sel_kv.reshape(n_pool, PAGE_SIZE, kv_w), win_state, page_table, table_b,
                      DB, T)
    win_all = jnp.concatenate([state_b_win_kv, pr['b_win_kv']], axis=1)
    new_state = (pr['a_kv'], pr['a_ik'], pr['b_cmp_kv'], pr['b_sel_kv'], win_all[:, T:])
    return o_a, o_b, new_state


PEER_SLOTS = PEER_HEADS * PEER_TOPK
ROUTE_TOKENS = 256
EXPERT_TOKENS = 128
GATHER_DEPTH = 8
D_SUB = D_MODEL // LANES
EXPERT_ROWS = 2 * D_SUB


def _take_topk(cur, pos_iota, k, payload=None):
    n = cur.shape[0]
    vals, picks = [], []
    for _ in range(k):
        m = jnp.max(cur, axis=0, keepdims=True)
        pos = jnp.min(jnp.where(cur == m, pos_iota, n), axis=0, keepdims=True)
        hit = pos_iota == pos
        vals.append(m)
        picks.append(pos if payload is None else jnp.sum(jnp.where(hit, payload, 0), axis=0, keepdims=True))
        cur = jnp.where(hit, -jnp.inf, cur)
    return vals, picks


def _peer_route_kernel(h_ref, wpq_ref, keys_ref, ids_ref, g_ref):
    tn = h_ref.shape[0]
    half = PEER_QDIM // 2
    q = jnp.dot(h_ref[...].astype(MXU), wpq_ref[...], preferred_element_type=F32).astype(MXU)
    key_iota = lax.broadcasted_iota(jnp.int32, (PEER_NKEYS, tn), 0)
    pairs = [(a, b) for a in range(PEER_TOPK) for b in range(PEER_TOPK // (a + 1))]
    n_pairs = -(-len(pairs) // 8) * 8
    pair_iota = lax.broadcasted_iota(jnp.int32, (n_pairs, tn), 0)
    pad_v = [jnp.full((n_pairs - len(pairs), tn), -jnp.inf, F32)]
    pad_i = [jnp.zeros((n_pairs - len(pairs), tn), jnp.int32)]
    id_rows, g_rows = [], []
    for hd in range(PEER_HEADS):
        vals, idxs = [], []
        for p in range(2):
            c0 = (hd * 2 + p) * half
            s = lax.dot_general(keys_ref[hd * 2 + p], q[:, c0:c0 + half], _NT, preferred_element_type=F32)
            v_rows, i_rows = _take_topk(s, key_iota, PEER_TOPK)
            vals.append(v_rows)
            idxs.append(i_rows)
        cand = jnp.concatenate([vals[0][a] + vals[1][b] for a, b in pairs] + pad_v, axis=0)
        expert = jnp.concatenate([idxs[0][a] * PEER_NKEYS + idxs[1][b] for a, b in pairs] + pad_i, axis=0)
        cv, ce = _take_topk(cand, pair_iota, PEER_TOPK, payload=expert)
        cv = jnp.concatenate(cv, axis=0)
        e = jnp.exp(cv - cv[0:1, :])
        g_rows.append(e / jnp.sum(e, axis=0, keepdims=True))
        id_rows.append(jnp.concatenate(ce, axis=0))
    ids_ref[...] = jnp.concatenate(id_rows, axis=0).astype(F32).T.astype(jnp.int32)
    g_ref[...] = jnp.concatenate(g_rows, axis=0).T


def _peer_route(h2d, w_pq, sub_keys):
    n = h2d.shape[0]
    tn = ROUTE_TOKENS
    wpq = w_pq.reshape(D_MODEL, PEER_HEADS * PEER_QDIM).astype(MXU)
    keys = sub_keys.reshape(PEER_HEADS * 2, PEER_NKEYS, PEER_QDIM // 2).astype(MXU)
    return pl.pallas_call(
        _peer_route_kernel,
        grid=(n // tn,),
        in_specs=[pl.BlockSpec((tn, D_MODEL), lambda i: (i, 0)), pl.BlockSpec(wpq.shape, lambda i: (0, 0)),
                  pl.BlockSpec(keys.shape, lambda i: (0, 0, 0))],
        out_specs=[pl.BlockSpec((tn, PEER_SLOTS), lambda i: (i, 0)), pl.BlockSpec((tn, PEER_SLOTS), lambda i: (i, 0))],
        out_shape=[jax.ShapeDtypeStruct((n, PEER_SLOTS), jnp.int32), jax.ShapeDtypeStruct((n, PEER_SLOTS), F32)],
        compiler_params=pltpu.CompilerParams(dimension_semantics=('arbitrary',), vmem_limit_bytes=VMEM_LIMIT),
        name='peer_route',
    )(h2d, wpq, keys)


def _peer_expert_kernel(ids_ref, g_ref, h_ref, lng_ref, lnb_ref, uv_hbm, y_ref, buf, sem, gt_sc, out_sc):
    tb = h_ref.shape[0]
    ahead = GATHER_DEPTH - 1

    def row_copy(t, slot, k):
        src = pl.ds(pl.multiple_of(ids_ref[t, k] * EXPERT_ROWS, EXPERT_ROWS), EXPERT_ROWS)
        return pltpu.make_async_copy(uv_hbm.at[src, :], buf.at[slot, :, k, :], sem.at[slot])

    def issue(t, slot, k0=0, k1=PEER_SLOTS):
        for k in range(k0, k1):
            row_copy(t, slot, k).start(priority=k % 2)

    def consume(t, slot, nxt=None):
        per_piece = PEER_SLOTS // EXPERT_ROWS

        def issue_piece(i):
            if nxt is not None:
                issue(nxt, nxt % GATHER_DEPTH, i * per_piece, (i + 1) * per_piece)

        for k in range(PEER_SLOTS):
            row_copy(t, slot, k).wait()
        x = h_ref[t]
        part = jnp.zeros((PEER_SLOTS, LANES), F32)
        for s in range(D_SUB):
            part = part + buf[slot, s] * x[s:s + 1, :]
            issue_piece(s)
        act = jnp.sum(part, axis=1, keepdims=True)
        g_col = jnp.sum(jnp.where(lane == t, gt_sc[...], 0.0), axis=1, keepdims=True)
        coef = jnp.broadcast_to(g_col * jax.nn.gelu(act), (PEER_SLOTS, LANES))
        rows = []
        for s in range(D_SUB):
            rows.append(jnp.sum(buf[slot, D_SUB + s] * coef, axis=0, keepdims=True))
            issue_piece(D_SUB + s)
        out_sc[t] = jnp.concatenate(rows, axis=0)

    gt_sc[...] = g_ref[...].T
    lane = lax.broadcasted_iota(jnp.int32, (PEER_SLOTS, tb), 1)
    for t in range(ahead):
        issue(t, t)

    def body(t, carry):
        consume(t, t % GATHER_DEPTH, nxt=t + ahead)
        return carry

    lax.fori_loop(0, tb - ahead, body, 0)
    for t in range(tb - ahead, tb):
        consume(t, t % GATHER_DEPTH)
    z = DN_ALPHA * h_ref[...] + out_sc[...]
    mean = lambda a: jnp.sum(jnp.sum(a, axis=2, keepdims=True), axis=1, keepdims=True) * (1.0 / D_MODEL)
    zc = z - mean(z)
    y_ref[...] = zc * lax.rsqrt(mean(zc * zc) + LN_EPS) * lng_ref[...] + lnb_ref[...]


def _peer_experts(h2d, ids, g, uv_rows, ln_g, ln_b):
    n = h2d.shape[0]
    tb = EXPERT_TOKENS
    row = lambda i: (i, 0)
    tok = lambda i: (i, 0, 0)
    y = pl.pallas_call(
        _peer_expert_kernel,
        grid=(n // tb,),
        in_specs=[pl.BlockSpec((tb, PEER_SLOTS), row, memory_space=pltpu.SMEM), pl.BlockSpec((tb, PEER_SLOTS), row),
                  pl.BlockSpec((tb, D_SUB, LANES), tok), pl.BlockSpec((D_SUB, LANES), lambda i: (0, 0)),
                  pl.BlockSpec((D_SUB, LANES), lambda i: (0, 0)), pl.BlockSpec(memory_space=pl.ANY)],
        out_specs=pl.BlockSpec((tb, D_SUB, LANES), tok),
        out_shape=jax.ShapeDtypeStruct((n, D_SUB, LANES), F32),
        scratch_shapes=[pltpu.VMEM((GATHER_DEPTH, EXPERT_ROWS, PEER_SLOTS, LANES), F32),
                        pltpu.SemaphoreType.DMA((GATHER_DEPTH,)), pltpu.VMEM((PEER_SLOTS, tb), F32),
                        pltpu.VMEM((tb, D_SUB, LANES), F32)],
        compiler_params=pltpu.CompilerParams(dimension_semantics=('arbitrary',), vmem_limit_bytes=VMEM_LIMIT),
        name='peer_experts',
    )(ids, g, h2d.reshape(n, D_SUB, LANES), ln_g.reshape(D_SUB, LANES), ln_b.reshape(D_SUB, LANES), uv_rows)
    return y.reshape(n, D_MODEL)


def _group_step(x, mixer, ws, w_up_a, w_up_b, w_out, ln1_g, ln1_b, w_pq, sub_keys, uv, ln2_g, ln2_b, tm):
    B, T, _ = x.shape
    x2d = x.reshape(B * T, D_MODEL)
    pr = _unpack_proj(_project(x2d, ws, tm), B, T)
    o_a, o_b, new_state = mixer(pr)
    h = _merge(x2d, o_a.reshape(B * T, -1), o_b.reshape(B * T, -1), pr['merge_gate'],
               w_up_a, w_up_b, w_out, ln1_g, ln1_b, tm)
    ids, g = _peer_route(h, w_pq, sub_keys)
    y = _peer_experts(h, ids, g, uv, ln2_g, ln2_b)
    return y.reshape(B, T, D_MODEL), new_state


def kernel(x_prompt, x_sample, cache_a_kv, cache_a_idx, cache_b_cmp_kv, cache_b_sel_kv, state_b_win_kv, page_table, w_in, rel_bias_table, cmp_pe, cmp_w1, cmp_w2, w_up_a, w_up_b, w_out, ln1_g, ln1_b, w_pq, peer_sub_keys, peer_u, peer_v, ln2_g, ln2_b):
    l = 0
    ws = _split_w_in(w_in[l])
    uv = jnp.concatenate([peer_u[l], peer_v[l]], axis=1).reshape(-1, LANES)
    tail = (w_up_a[l], w_up_b[l], w_out[l], ln1_g[l], ln1_b[l], w_pq[l], peer_sub_keys[l], uv, ln2_g[l], ln2_b[l])
    mix_p = functools.partial(_mix_prompt, rel_table=rel_bias_table, cmp_pe=cmp_pe[l], cmp_w1=cmp_w1[l],
                              cmp_w2=cmp_w2[l])
    y_p, st_p = _group_step(x_prompt, mix_p, ws, *tail, tm=256)
    mix_s = functools.partial(_mix_sample, cache_a_kv=cache_a_kv[l], cache_a_idx=cache_a_idx[l],
                              cache_b_cmp_kv=cache_b_cmp_kv[l], cache_b_sel_kv=cache_b_sel_kv[l],
                              state_b_win_kv=state_b_win_kv[l], page_table=page_table, rel_table=rel_bias_table,
                              cmp_pe=cmp_pe[l], cmp_w1=cmp_w1[l], cmp_w2=cmp_w2[l])
    y_s, st_s = _group_step(x_sample, mix_s, ws, *tail, tm=256)
    return (y_p, y_s) + tuple(s[None] for s in st_p) + tuple(s[None] for s in st_s)
```

```python
import functools
import math

import jax
import jax.numpy as jnp
import numpy as np
from jax import lax
from jax.experimental import pallas as pl
from jax.experimental.pallas import tpu as pltpu

D_MODEL = 1024
DEPTH = 1
PAGE_SIZE = 128
HEAD_DIM = 64
A_HEADS = 8
A_KV_HEADS = 2
IDX_HEADS = 4
IDX_DIM = 64
A_TOPK = 256
B_HEADS = 8
B_KV_HEADS = 2
NSA_BLOCK = 64
NSA_TOPN = 16
NSA_FORCE = 8.0
CMP_HIDDEN = 64
WINDOW = 512
REL_BUCKETS = 32
REL_MAX_EXACT = 16
REL_MAX_DIST = 128
PEER_HEADS = 8
PEER_NKEYS = 128
PEER_QDIM = 256
PEER_TOPK = 16
Q_BLOCK = 32
TOKEN_BLOCK = 256
LN_EPS = 1e-5
DN_ALPHA = (2 * DEPTH) ** 0.25

PROJ_LAYOUT = (
    ('a_q', A_HEADS * HEAD_DIM),
    ('a_kv', 2 * A_KV_HEADS * HEAD_DIM),
    ('a_iq', IDX_HEADS * IDX_DIM),
    ('a_iw', IDX_HEADS),
    ('a_ik', IDX_DIM),
    ('b_q', B_HEADS * HEAD_DIM),
    ('b_cmp_kv', 2 * B_KV_HEADS * HEAD_DIM),
    ('b_sel_kv', 2 * B_KV_HEADS * HEAD_DIM),
    ('b_win_kv', 2 * B_KV_HEADS * HEAD_DIM),
    ('b_gate', B_HEADS * 3),
    ('merge_gate', 2 * D_MODEL),
)

LANES = 128
VMEM_LIMIT = 56 * 1024 * 1024
MISC_GATE_OFF = IDX_HEADS

F32 = jnp.float32
BF16 = jnp.bfloat16
MXU = jnp.bfloat16


_PROJ_OUT = ('a_q', 'a_kv', 'a_iq', 'a_ik', 'misc', 'b_q', 'b_cmp_kv', 'b_sel_kv', 'b_win_kv', 'merge_gate')


def _proj_kernel(x_ref, *refs):
    n = len(_PROJ_OUT)
    w_refs, o_refs = refs[:n], refs[n:]
    x = x_ref[...].astype(BF16)
    for name, w_ref, o_ref in zip(_PROJ_OUT, w_refs, o_refs):
        r = jnp.dot(x, w_ref[...], preferred_element_type=F32)
        if name == 'misc':
            lane = lax.broadcasted_iota(jnp.int32, r.shape, 1)
            r = jnp.where(lane < MISC_GATE_OFF, r * IDX_HEADS ** -0.5, jax.nn.sigmoid(r))
        elif name == 'merge_gate':
            r = jax.nn.sigmoid(r)
        o_ref[...] = r


def _split_w_in(w_in):
    parts = {}
    off = 0
    for name, width in PROJ_LAYOUT:
        parts[name] = w_in[:, off:off + width]
        off += width
    misc = jnp.concatenate([parts['a_iw'], parts['b_gate']], axis=1)
    parts['misc'] = jnp.pad(misc, ((0, 0), (0, LANES - misc.shape[1])))
    return [parts[name].astype(BF16) for name in _PROJ_OUT]


def _project(x2d, ws, tm):
    n = x2d.shape[0]
    widths = [w.shape[1] for w in ws]
    return pl.pallas_call(
        _proj_kernel,
        grid=(n // tm,),
        in_specs=[pl.BlockSpec((tm, D_MODEL), lambda i: (i, 0))]
        + [pl.BlockSpec((D_MODEL, wd), lambda i: (0, 0)) for wd in widths],
        out_specs=[pl.BlockSpec((tm, wd), lambda i: (i, 0)) for wd in widths],
        out_shape=[jax.ShapeDtypeStruct((n, wd), F32) for wd in widths],
        compiler_params=pltpu.CompilerParams(dimension_semantics=('arbitrary',), vmem_limit_bytes=VMEM_LIMIT),
        name='project',
    )(x2d, *ws)


def _unpack_proj(outs, B, T):
    d = dict(zip(_PROJ_OUT, outs))
    misc = d['misc']
    kv = (B, T, 2, A_KV_HEADS, HEAD_DIM)
    return {
        'a_q': d['a_q'].reshape(B, T, A_HEADS, HEAD_DIM),
        'a_kv': d['a_kv'].reshape(kv),
        'a_iq': d['a_iq'].reshape(B, T, IDX_HEADS, IDX_DIM),
        'a_iw': misc[:, :IDX_HEADS].reshape(B, T, IDX_HEADS),
        'a_ik': d['a_ik'].reshape(B, T, IDX_DIM),
        'b_q': d['b_q'].reshape(B, T, B_HEADS, HEAD_DIM),
        'b_cmp_kv': d['b_cmp_kv'].reshape(kv),
        'b_sel_kv': d['b_sel_kv'].reshape(kv),
        'b_win_kv': d['b_win_kv'].reshape(kv),
        'b_gate': misc[:, MISC_GATE_OFF:MISC_GATE_OFF + 3 * B_HEADS].reshape(B, T, B_HEADS, 3),
        'merge_gate': d['merge_gate'],
        'raw': d,
    }


def _layer_norm(x, g, b):
    mu = jnp.mean(x, axis=-1, keepdims=True)
    xc = x - mu
    var = jnp.mean(xc * xc, axis=-1, keepdims=True)
    return xc * lax.rsqrt(var + LN_EPS) * g + b


def _merge_kernel(x_ref, oa_ref, ob_ref, mg_ref, wa_ref, wb_ref, wo_ref, g_ref, b_ref, h_ref):
    ya = jnp.dot(oa_ref[...].astype(BF16), wa_ref[...], preferred_element_type=F32)
    yb = jnp.dot(ob_ref[...].astype(BF16), wb_ref[...], preferred_element_type=F32)
    mg = mg_ref[...]
    mix = mg[:, :D_MODEL] * ya + mg[:, D_MODEL:] * yb
    mixed = jnp.dot(mix.astype(BF16), wo_ref[...], preferred_element_type=F32)
    h_ref[...] = _layer_norm(DN_ALPHA * x_ref[...] + mixed, g_ref[...], b_ref[...])


def _merge(x2d, o_a, o_b, mg, w_up_a, w_up_b, w_out, ln_g, ln_b, tm):
    n = x2d.shape[0]
    ca, cb = o_a.shape[1], o_b.shape[1]
    row = lambda i: (i, 0)
    fixed = lambda i: (0, 0)
    return pl.pallas_call(
        _merge_kernel,
        grid=(n // tm,),
        in_specs=[pl.BlockSpec((tm, D_MODEL), row), pl.BlockSpec((tm, ca), row), pl.BlockSpec((tm, cb), row),
                  pl.BlockSpec((tm, 2 * D_MODEL), row), pl.BlockSpec((ca, D_MODEL), fixed),
                  pl.BlockSpec((cb, D_MODEL), fixed), pl.BlockSpec((D_MODEL, D_MODEL), fixed),
                  pl.BlockSpec((1, D_MODEL), fixed), pl.BlockSpec((1, D_MODEL), fixed)],
        out_specs=pl.BlockSpec((tm, D_MODEL), row),
        out_shape=jax.ShapeDtypeStruct((n, D_MODEL), F32),
        compiler_params=pltpu.CompilerParams(dimension_semantics=('arbitrary',), vmem_limit_bytes=VMEM_LIMIT),
        name='merge',
    )(x2d, o_a, o_b, mg, w_up_a.astype(BF16), w_up_b.astype(BF16), w_out.astype(BF16),
      ln_g.reshape(1, D_MODEL), ln_b.reshape(1, D_MODEL))


TQ = 256
NEG = -1e30
INT_MIN = -2 ** 31
COUNT_ROWS = 256
_NT = (((1,), (1,)), ((), ()))
_TN = (((0,), (0,)), ((), ()))


def _bucket_np(d):
    n = np.maximum(d, 0)
    nf = np.maximum(n, 1).astype(np.float64)
    large = REL_MAX_EXACT + (np.log(nf / REL_MAX_EXACT) / math.log(REL_MAX_DIST / REL_MAX_EXACT)
                             * (REL_BUCKETS - REL_MAX_EXACT)).astype(np.int64)
    return np.where(n < REL_MAX_EXACT, n, np.minimum(large, REL_BUCKETS - 1)).astype(np.int32)


def _rel_bias_at(table, d):
    bucket = jnp.asarray(_bucket_np(d))[..., None]
    out = jnp.broadcast_to(table[REL_BUCKETS - 1], bucket.shape[:-1] + table.shape[1:])
    for b in range(REL_BUCKETS - 1):
        out = jnp.where(bucket == b, table[b], out)
    return out


def _band_bias(table, n_groups):
    d = TQ + np.arange(TQ)[None, :] - np.arange(2 * TQ)[:, None]
    band = _rel_bias_at(table, d) - table[REL_BUCKETS - 1]
    band = jnp.moveaxis(band, -1, 1)
    return jnp.moveaxis(band.reshape(2 * TQ, n_groups, -1), 1, 0)


def _block_bias(table, S):
    nb = S // NSA_BLOCK
    d = np.arange(S)[None, :] - ((np.arange(nb) + 1) * NSA_BLOCK - 1)[:, None]
    return jnp.moveaxis(_rel_bias_at(table, d), -1, 0)


def _ordered_keys(x):
    x = jnp.where(x == 0.0, 0.0, x)
    b = lax.bitcast_convert_type(x, jnp.int32)
    return b ^ ((b >> 31) & jnp.int32(0x7FFFFFFF))


def _softmax_reset(m_sc, l_sc, acc_sc):
    m_sc[...] = jnp.full(m_sc.shape, NEG, F32)
    l_sc[...] = jnp.zeros(l_sc.shape, F32)
    acc_sc[...] = jnp.zeros(acc_sc.shape, F32)


def _group_queries(q, n_groups):
    rep = q.shape[1] // HEAD_DIM // n_groups
    return [_stack_heads(q, range(g * rep, (g + 1) * rep), HEAD_DIM) for g in range(n_groups)]


def _attend_chunk(kv_c, q_groups, madd_of_group, band_ref, band_row0, m_sc, l_sc, acc_sc):
    n_g = len(q_groups)
    rep = q_groups[0].shape[0] // TQ
    for g in range(n_g):
        k_c = kv_c[:, g * HEAD_DIM:(g + 1) * HEAD_DIM]
        v_c = kv_c[:, (n_g + g) * HEAD_DIM:(n_g + g + 1) * HEAD_DIM]
        add = jnp.concatenate([madd_of_group(g)] * rep, axis=1)
        if band_row0 is not None:
            add = add + band_ref[g, band_row0:band_row0 + TQ, :]
        logits = lax.dot_general(k_c, q_groups[g], _NT, preferred_element_type=F32) + add
        m_old = m_sc[g]
        m_new = jnp.maximum(m_old, jnp.max(logits, axis=0, keepdims=True))
        alpha = jnp.exp(m_old - m_new)
        p = jnp.exp(logits - m_new)
        l_sc[g] = alpha * l_sc[g] + jnp.sum(p, axis=0, keepdims=True)
        m_sc[g] = m_new
        p = p.astype(MXU)
        for r in range(rep):
            h = g * rep + r
            lanes = slice(r * TQ, (r + 1) * TQ)
            pv = lax.dot_general(v_c, p[:, lanes], _TN, preferred_element_type=F32)
            acc_sc[h] = alpha[:, lanes] * acc_sc[h] + pv


def _softmax_scratch(n_heads, n_groups):
    stats = pltpu.VMEM((n_groups, 1, n_heads // n_groups * TQ), F32)
    return [stats, stats, pltpu.VMEM((n_heads, HEAD_DIM, TQ), F32)]


def _attend_result(h, rep, l_sc, acc_sc):
    g, r = divmod(h, rep)
    return acc_sc[h] / l_sc[g][:, r * TQ:(r + 1) * TQ]


def _dsa_kernel(iq_ref, misc_ref, q_ref, ik_ref, kv_ref, band_ref, o_ref, key_sc, m_sc, l_sc, acc_sc, out_sc,
                *, k_sel, pos_bits):
    i = pl.program_id(1)
    t0 = i * TQ
    n_chunks = i + 1
    lane_t = t0 + lax.broadcasted_iota(jnp.int32, (1, TQ), 1)
    sub_iota = lax.broadcasted_iota(jnp.int32, (TQ, TQ), 0)
    misc_t = misc_ref[...].T
    iq = iq_ref[...].astype(MXU)
    iq_heads = [iq[:, h * IDX_DIM:(h + 1) * IDX_DIM] for h in range(IDX_HEADS)]

    def score_chunk(j, carry):
        r0 = pl.multiple_of(j * TQ, TQ)
        ik_c = ik_ref[pl.ds(r0, TQ), :].astype(MXU)
        sc = jnp.zeros((TQ, TQ), F32)
        for h in range(IDX_HEADS):
            rel = lax.dot_general(ik_c, iq_heads[h], _NT, preferred_element_type=F32)
            sc = sc + jnp.maximum(rel * IDX_DIM ** -0.5, 0.0) * misc_t[h:h + 1, :]
        sc = jnp.where(r0 + sub_iota <= lane_t, sc, -jnp.inf)
        key_sc[pl.ds(r0, TQ), :] = _ordered_keys(sc)
        return carry

    lax.fori_loop(0, n_chunks, score_chunk, 0)

    def count(pred):
        def body(jj, acc):
            r0 = pl.multiple_of(jj * COUNT_ROWS, COUNT_ROWS)
            blk = key_sc[pl.ds(r0, COUNT_ROWS), :]
            pos = r0 + lax.broadcasted_iota(jnp.int32, (COUNT_ROWS, TQ), 0)
            hit = jnp.where(pred(blk, pos), 1, 0)
            return acc + jnp.sum(hit.reshape(COUNT_ROWS // 8, 8, TQ), axis=0)

        acc = lax.fori_loop(0, n_chunks * (TQ // COUNT_ROWS), body, jnp.zeros((8, TQ), jnp.int32))
        return jnp.sum(acc, axis=0, keepdims=True)

    v = jnp.full((1, TQ), INT_MIN, jnp.int32)
    v = jnp.where(count(lambda blk, pos: blk >= 0) >= k_sel, 0, v)

    def value_bit(it, v):
        cand = v | jnp.left_shift(jnp.int32(1), 30 - it)
        return jnp.where(count(lambda blk, pos: blk >= cand) >= k_sel, cand, v)

    v = lax.fori_loop(0, 31, value_bit, v)
    need = k_sel - count(lambda blk, pos: blk > v)
    n_ge = count(lambda blk, pos: blk >= v)

    def tie_search():
        def pos_bit(it, jm):
            cand = jm | jnp.left_shift(jnp.int32(1), pos_bits - 1 - it)
            return jnp.where(count(lambda blk, pos: (blk == v) & (pos < cand)) < need, cand, jm)

        return lax.fori_loop(0, pos_bits, pos_bit, jnp.zeros((1, TQ), jnp.int32))

    j_max = lax.cond(jnp.max(n_ge) > k_sel, tie_search, lambda: jnp.full((1, TQ), 2 ** 30, jnp.int32))

    q = (q_ref[...] * HEAD_DIM ** -0.5).astype(MXU)
    q_groups = _group_queries(q, A_KV_HEADS)
    _softmax_reset(m_sc, l_sc, acc_sc)

    def chunk(j, band_row0, causal):
        r0 = pl.multiple_of(j * TQ, TQ)
        key = key_sc[pl.ds(r0, TQ), :]
        pos = r0 + sub_iota
        sel = (key > v) | ((key == v) & (pos <= j_max))
        if causal:
            sel = sel & (pos <= lane_t)
        madd = jnp.where(sel, 0.0, NEG)
        kv_c = kv_ref[pl.ds(r0, TQ), :].astype(MXU)
        _attend_chunk(kv_c, q_groups, lambda g: madd, band_ref, band_row0, m_sc, l_sc, acc_sc)

    def far_chunk(j, carry):
        chunk(j, None, False)
        return carry

    lax.fori_loop(0, jnp.maximum(i - 1, 0), far_chunk, 0)

    @pl.when(i >= 1)
    def _():
        chunk(i - 1, 0, False)

    chunk(i, TQ, True)
    for h in range(A_HEADS):
        out_sc[h * HEAD_DIM:(h + 1) * HEAD_DIM, :] = _attend_result(h, A_HEADS // A_KV_HEADS, l_sc, acc_sc)
    o_ref[...] = out_sc[...].T


def _dsa_prompt(raw, band_a, B, S):
    nq = S // TQ
    k_sel = min(A_TOPK, S // 4)
    assert S % TQ == 0 and TQ >= k_sel
    tile = lambda b, i: (b * nq + i, 0)
    seq = lambda b, i: (b, 0)
    kern = functools.partial(_dsa_kernel, k_sel=k_sel, pos_bits=(S - 1).bit_length())
    return pl.pallas_call(
        kern,
        grid=(B, nq),
        in_specs=[pl.BlockSpec((TQ, IDX_HEADS * IDX_DIM), tile), pl.BlockSpec((TQ, LANES), tile),
                  pl.BlockSpec((TQ, A_HEADS * HEAD_DIM), tile), pl.BlockSpec((S, IDX_DIM), seq),
                  pl.BlockSpec((S, 2 * A_KV_HEADS * HEAD_DIM), seq),
                  pl.BlockSpec(band_a.shape, lambda b, i: (0, 0, 0))],
        out_specs=pl.BlockSpec((TQ, A_HEADS * HEAD_DIM), tile),
        out_shape=jax.ShapeDtypeStruct((B * S, A_HEADS * HEAD_DIM), F32),
        scratch_shapes=[pltpu.VMEM((S, TQ), jnp.int32)] + _softmax_scratch(A_HEADS, A_KV_HEADS)
        + [pltpu.VMEM((A_HEADS * HEAD_DIM, TQ), F32)],
        compiler_params=pltpu.CompilerParams(dimension_semantics=('arbitrary', 'arbitrary'),
                                             vmem_limit_bytes=VMEM_LIMIT),
        name='dsa_prompt',
    )(raw['a_iq'], raw['misc'], raw['a_q'], raw['a_ik'], raw['a_kv'], band_a)


def _compress_kernel(x_ref, pe_ref, w1_ref, w2_ref, o_ref):
    x = (x_ref[...] + pe_ref[...]).astype(MXU)
    h = jax.nn.gelu(jnp.dot(x, w1_ref[...], preferred_element_type=F32))
    o_ref[...] = jnp.dot(h.astype(MXU), w2_ref[...], preferred_element_type=F32)


def _compress_weights(pe, w1, w2):
    eye_c = jnp.eye(2, dtype=F32)
    eye_g = jnp.eye(B_KV_HEADS, dtype=F32)
    w1_big = jnp.einsum('lcde,cC,gG->lcgdCGe', w1, eye_c, eye_g)
    w1_big = w1_big.reshape(NSA_BLOCK * 2 * B_KV_HEADS * HEAD_DIM, 2 * B_KV_HEADS * CMP_HIDDEN)
    w2_big = jnp.einsum('cef,cC,gG->cgeCGf', w2, eye_c, eye_g)
    w2_big = w2_big.reshape(2 * B_KV_HEADS * CMP_HIDDEN, 2 * B_KV_HEADS * HEAD_DIM)
    pe_flat = jnp.broadcast_to(pe[:, :, None, :], (NSA_BLOCK, 2, B_KV_HEADS, HEAD_DIM)).reshape(1, -1)
    return pe_flat, w1_big.astype(MXU), w2_big.astype(MXU)


def _compress(blocks2d, pe_flat, w1_big, w2_big, tm):
    n, width = blocks2d.shape
    fixed = lambda i: (0, 0)
    return pl.pallas_call(
        _compress_kernel,
        grid=(n // tm,),
        in_specs=[pl.BlockSpec((tm, width), lambda i: (i, 0)), pl.BlockSpec((1, width), fixed),
                  pl.BlockSpec(w1_big.shape, fixed), pl.BlockSpec(w2_big.shape, fixed)],
        out_specs=pl.BlockSpec((tm, w2_big.shape[1]), lambda i: (i, 0)),
        out_shape=jax.ShapeDtypeStruct((n, w2_big.shape[1]), F32),
        compiler_params=pltpu.CompilerParams(dimension_semantics=('arbitrary',), vmem_limit_bytes=VMEM_LIMIT),
        name='nsa_compress',
    )(blocks2d, pe_flat, w1_big, w2_big)


def _nsa_kernel(q_ref, misc_ref, cmp_ref, selkv_ref, winkv_ref, band_ref, biasc_ref, o_ref,
                sel_sc, m_sc, l_sc, acc_sc, out_sc, *, nb, n_sel):
    i = pl.program_id(1)
    t0 = i * TQ
    n_g = B_KV_HEADS
    rep = B_HEADS // n_g
    lane_t = t0 + lax.broadcasted_iota(jnp.int32, (1, TQ), 1)
    sub_iota = lax.broadcasted_iota(jnp.int32, (TQ, TQ), 0)
    lane_iota = lax.broadcasted_iota(jnp.int32, (TQ, TQ), 1)
    misc_t = misc_ref[...].T
    gate = lambda h, k: misc_t[MISC_GATE_OFF + 3 * h + k:MISC_GATE_OFF + 3 * h + k + 1, :]
    q = (q_ref[...] * HEAD_DIM ** -0.5).astype(MXU)
    q_heads = [q[:, h * HEAD_DIM:(h + 1) * HEAD_DIM] for h in range(B_HEADS)]
    q_groups = _group_queries(q, n_g)

    cmp = cmp_ref[...].astype(MXU)
    blk = lax.broadcasted_iota(jnp.int32, (nb, TQ), 0)
    visible = (blk + 1) * NSA_BLOCK - 1 <= lane_t
    cur = lax.shift_right_logical(lane_t, NSA_BLOCK.bit_length() - 1)
    forced = (blk == 0) | (blk == cur) | (blk == cur - 1)
    for g in range(n_g):
        k_c = cmp[:, g * HEAD_DIM:(g + 1) * HEAD_DIM]
        v_c = cmp[:, (n_g + g) * HEAD_DIM:(n_g + g + 1) * HEAD_DIM]
        importance = jnp.zeros((nb, TQ), F32)
        for r in range(rep):
            h = g * rep + r
            lc = lax.dot_general(k_c, q_heads[h], _NT, preferred_element_type=F32) + biasc_ref[h]
            lc = jnp.where(visible, lc, -jnp.inf)
            m = jnp.max(lc, axis=0, keepdims=True)
            m = jnp.where(m > -jnp.inf, m, 0.0)
            e = jnp.exp(lc - m)
            s = jnp.sum(e, axis=0, keepdims=True)
            p = e / jnp.where(s > 0, s, 1.0)
            importance = importance + p
            o_cmp = lax.dot_general(v_c, p.astype(MXU), _TN, preferred_element_type=F32)
            out_sc[h * HEAD_DIM:(h + 1) * HEAD_DIM, :] = gate(h, 0) * o_cmp
        score = jnp.where(forced, NSA_FORCE, importance)
        score = jnp.where(blk <= cur, score, -1.0)
        rank = jnp.zeros((nb, TQ), jnp.int32)
        for n in range(nb):
            row = score[n:n + 1, :]
            beats = (row > score) | ((row == score) & (blk > n))
            rank = rank + jnp.where(beats, 1, 0)
        sel_sc[g] = jnp.where(rank < n_sel, 0.0, NEG)

    def finish_branch(k):
        for h in range(B_HEADS):
            rows = slice(h * HEAD_DIM, (h + 1) * HEAD_DIM)
            out_sc[rows, :] = out_sc[rows, :] + gate(h, k) * _attend_result(h, rep, l_sc, acc_sc)

    _softmax_reset(m_sc, l_sc, acc_sc)
    blocks_per_chunk = TQ // NSA_BLOCK

    def sel_chunk(j, band_row0, causal):
        r0 = pl.multiple_of(j * TQ, TQ)
        kv_c = selkv_ref[pl.ds(r0, TQ), :].astype(MXU)

        def madd_of_group(g):
            rows = [sel_sc[g, pl.ds(j * blocks_per_chunk + b, 1), :] for b in range(blocks_per_chunk)]
            madd = jnp.concatenate([jnp.broadcast_to(row, (NSA_BLOCK, TQ)) for row in rows], axis=0)
            if causal:
                madd = jnp.where(sub_iota <= lane_iota, madd, NEG)
            return madd

        _attend_chunk(kv_c, q_groups, madd_of_group, band_ref, band_row0, m_sc, l_sc, acc_sc)

    def far_chunk(j, carry):
        sel_chunk(j, None, False)
        return carry

    lax.fori_loop(0, jnp.maximum(i - 1, 0), far_chunk, 0)

    @pl.when(i >= 1)
    def _():
        sel_chunk(i - 1, 0, False)

    sel_chunk(i, TQ, True)
    finish_branch(1)

    _softmax_reset(m_sc, l_sc, acc_sc)
    for back in range(WINDOW // TQ, -1, -1):
        dist = back * TQ + lane_iota - sub_iota
        madd = jnp.where((dist >= 0) & (dist < WINDOW), 0.0, NEG)
        band_row0 = {0: TQ, 1: 0}.get(back)

        def win_chunk(back=back, madd=madd, band_row0=band_row0):
            r0 = pl.multiple_of((i - back) * TQ, TQ)
            kv_c = winkv_ref[pl.ds(r0, TQ), :].astype(MXU)
            _attend_chunk(kv_c, q_groups, lambda g: madd, band_ref, band_row0, m_sc, l_sc, acc_sc)

        if back == 0:
            win_chunk()
        else:
            pl.when(i >= back)(win_chunk)
    finish_branch(2)
    o_ref[...] = out_sc[...].T


def _nsa_prompt(raw, cmp, band_b, bias_c, B, S):
    nq = S // TQ
    nb = S // NSA_BLOCK
    assert S % TQ == 0 and TQ % NSA_BLOCK == 0
    tile = lambda b, i: (b * nq + i, 0)
    seq = lambda b, i: (b, 0)
    kv_w = 2 * B_KV_HEADS * HEAD_DIM
    kern = functools.partial(_nsa_kernel, nb=nb, n_sel=min(NSA_TOPN, nb))
    return pl.pallas_call(
        kern,
        grid=(B, nq),
        in_specs=[pl.BlockSpec((TQ, B_HEADS * HEAD_DIM), tile), pl.BlockSpec((TQ, LANES), tile),
                  pl.BlockSpec((nb, kv_w), seq), pl.BlockSpec((S, kv_w), seq), pl.BlockSpec((S, kv_w), seq),
                  pl.BlockSpec(band_b.shape, lambda b, i: (0, 0, 0)),
                  pl.BlockSpec((B_HEADS, nb, TQ), lambda b, i: (0, 0, i))],
        out_specs=pl.BlockSpec((TQ, B_HEADS * HEAD_DIM), tile),
        out_shape=jax.ShapeDtypeStruct((B * S, B_HEADS * HEAD_DIM), F32),
        scratch_shapes=[pltpu.VMEM((B_KV_HEADS, nb, TQ), F32)] + _softmax_scratch(B_HEADS, B_KV_HEADS)
        + [pltpu.VMEM((B_HEADS * HEAD_DIM, TQ), F32)],
        compiler_params=pltpu.CompilerParams(dimension_semantics=('arbitrary', 'arbitrary'),
                                             vmem_limit_bytes=VMEM_LIMIT),
        name='nsa_prompt',
    )(raw['b_q'], raw['misc'], cmp, raw['b_sel_kv'], raw['b_win_kv'], band_b, bias_c)


def _mix_prompt(pr, rel_table, cmp_pe, cmp_w1, cmp_w2):
    B, S = pr['a_q'].shape[:2]
    raw = pr['raw']
    table_a, table_b = rel_table[:, :A_HEADS], rel_table[:, A_HEADS:]
    pe_flat, w1_big, w2_big = _compress_weights(cmp_pe, cmp_w1, cmp_w2)
    n_blocks = B * S // NSA_BLOCK
    cmp = _compress(raw['b_cmp_kv'].reshape(n_blocks, -1), pe_flat, w1_big, w2_big, tm=min(128, n_blocks))
    o_a = _dsa_prompt(raw, _band_bias(table_a, A_KV_HEADS), B, S)
    o_b = _nsa_prompt(raw, cmp, _band_bias(table_b, B_KV_HEADS), _block_bias(table_b, S), B, S)
    wb = min(WINDOW, S)
    new_state = (pr['a_kv'], pr['a_ik'], pr['b_cmp_kv'], pr['b_sel_kv'], pr['b_win_kv'][:, S - wb:])
    return o_a, o_b, new_state


PAGES_PER_STEP = 8
KEY_PAD = 128


def _paged_specs(width):
    def spec(j):
        return pl.BlockSpec((1, PAGE_SIZE, width), lambda s, p, pt: (pt[s, p * PAGES_PER_STEP + j], 0, 0))
    return [spec(j) for j in range(PAGES_PER_STEP)]


def _stage_pages(page_refs, dst_sc):
    p = pl.program_id(1)
    for j, ref in enumerate(page_refs):
        r0 = pl.multiple_of((p * PAGES_PER_STEP + j) * PAGE_SIZE, PAGE_SIZE)
        dst_sc[pl.ds(r0, PAGE_SIZE), :] = ref[0]


def _stage_new_rows(new_ref, dst_sc, past):
    t = new_ref.shape[0]
    dst_sc[past:past + t, :] = new_ref[...]
    dst_sc[past + t:, :] = jnp.zeros((dst_sc.shape[0] - past - t, dst_sc.shape[1]), F32)


def _rows_softmax_attend(q_rows, k, v, bias, madd):
    logits = lax.dot_general(q_rows, k, _NT, preferred_element_type=F32) + bias + madd
    m = jnp.max(logits, axis=1, keepdims=True)
    e = jnp.exp(logits - m)
    p = e / jnp.sum(e, axis=1, keepdims=True)
    return jnp.dot(p.astype(MXU), v, preferred_element_type=F32)


def _stack_heads(x, heads, width):
    return jnp.concatenate([x[:, h * width:(h + 1) * width] for h in heads], axis=0)


def _dsa_sample_kernel(pt_ref, *refs, past, k_sel):
    n = PAGES_PER_STEP
    idx_pages, kv_pages = refs[:n], refs[n:2 * n]
    ikn_ref, kvn_ref, iq_ref, misc_ref, q_ref, bias_ref, o_ref, ik_sc, kv_sc = refs[2 * n:]
    _stage_pages(idx_pages, ik_sc)
    _stage_pages(kv_pages, kv_sc)

    @pl.when(pl.program_id(1) == pl.num_programs(1) - 1)
    def _():
        T = iq_ref.shape[0]
        L = ik_sc.shape[0]
        _stage_new_rows(ikn_ref, ik_sc, past)
        _stage_new_rows(kvn_ref, kv_sc, past)
        pos = lax.broadcasted_iota(jnp.int32, (T, L), 1)
        valid = pos <= past + lax.broadcasted_iota(jnp.int32, (T, L), 0)
        misc = misc_ref[...]
        iq = _stack_heads(iq_ref[...].astype(MXU), range(IDX_HEADS), IDX_DIM)
        rel = lax.dot_general(iq, ik_sc[...].astype(MXU), _NT, preferred_element_type=F32)
        sc = jnp.zeros((T, L), F32)
        for h in range(IDX_HEADS):
            sc = sc + jnp.maximum(rel[h * T:(h + 1) * T] * IDX_DIM ** -0.5, 0.0) * misc[:, h:h + 1]
        key = _ordered_keys(jnp.where(valid, sc, -jnp.inf))

        def count(hit):
            return jnp.sum(jnp.where(hit, 1, 0), axis=1, keepdims=True)

        v = jnp.full((T, 1), INT_MIN, jnp.int32)
        v = jnp.where(count(key >= 0) >= k_sel, 0, v)

        def value_bit(it, v):
            cand = v | jnp.left_shift(jnp.int32(1), 30 - it)
            return jnp.where(count(key >= cand) >= k_sel, cand, v)

        v = lax.fori_loop(0, 31, value_bit, v)
        need = k_sel - count(key > v)
        pos_bits = (L - 1).bit_length()

        def tie_search():
            def pos_bit(it, jm):
                cand = jm | jnp.left_shift(jnp.int32(1), pos_bits - 1 - it)
                return jnp.where(count((key == v) & (pos < cand)) < need, cand, jm)

            return lax.fori_loop(0, pos_bits, pos_bit, jnp.zeros((T, 1), jnp.int32))

        j_max = lax.cond(jnp.max(count(key >= v)) > k_sel, tie_search,
                         lambda: jnp.full((T, 1), 2 ** 30, jnp.int32))
        sel = ((key > v) | ((key == v) & (pos <= j_max))) & valid
        madd = jnp.where(sel, 0.0, NEG)
        rep = A_HEADS // A_KV_HEADS
        madd = jnp.concatenate([madd] * rep, axis=0)
        q = (q_ref[...] * HEAD_DIM ** -0.5).astype(MXU)
        kv = kv_sc[...].astype(MXU)
        for g in range(A_KV_HEADS):
            q_rows = _stack_heads(q, range(g * rep, (g + 1) * rep), HEAD_DIM)
            o = _rows_softmax_attend(q_rows, kv[:, g * HEAD_DIM:(g + 1) * HEAD_DIM],
                                     kv[:, (A_KV_HEADS + g) * HEAD_DIM:(A_KV_HEADS + g + 1) * HEAD_DIM],
                                     bias_ref[g], madd)
            for r in range(rep):
                h = g * rep + r
                o_ref[:, h * HEAD_DIM:(h + 1) * HEAD_DIM] = o[r * T:(r + 1) * T, :]


def _sample_bias(table, T, past, n_keys, key_pos0, n_groups):
    d = (past + np.arange(T))[:, None] - (key_pos0 + np.arange(n_keys))[None, :]
    b = _rel_bias_at(table, d)
    b = jnp.moveaxis(b, -1, 0)
    return b.reshape(n_groups, -1, n_keys)


def _seq_rows(width, T):
    return pl.BlockSpec((T, width), lambda s, p, pt: (s, 0))


def _dsa_sample(raw, pool_idx, pool_kv, page_table, table_a, DB, T):
    n_pages = page_table.shape[1]
    past = n_pages * PAGE_SIZE
    L = past + KEY_PAD
    assert n_pages % PAGES_PER_STEP == 0 and T <= KEY_PAD
    kv_w = 2 * A_KV_HEADS * HEAD_DIM
    bias = _sample_bias(table_a, T, past, L, 0, A_KV_HEADS)
    fixed3 = lambda s, p, pt: (0, 0, 0)
    grid_spec = pltpu.PrefetchScalarGridSpec(
        num_scalar_prefetch=1,
        grid=(DB, n_pages // PAGES_PER_STEP),
        in_specs=_paged_specs(IDX_DIM) + _paged_specs(kv_w)
        + [_seq_rows(IDX_DIM, T), _seq_rows(kv_w, T), _seq_rows(IDX_HEADS * IDX_DIM, T), _seq_rows(LANES, T),
           _seq_rows(A_HEADS * HEAD_DIM, T), pl.BlockSpec(bias.shape, fixed3)],
        out_specs=_seq_rows(A_HEADS * HEAD_DIM, T),
        scratch_shapes=[pltpu.VMEM((L, IDX_DIM), F32), pltpu.VMEM((L, kv_w), F32)],
    )
    kern = functools.partial(_dsa_sample_kernel, past=past, k_sel=min(A_TOPK, (past + T) // 4))
    return pl.pallas_call(
        kern, grid_spec=grid_spec,
        out_shape=jax.ShapeDtypeStruct((DB * T, A_HEADS * HEAD_DIM), F32),
        compiler_params=pltpu.CompilerParams(dimension_semantics=('arbitrary', 'arbitrary'),
                                             vmem_limit_bytes=VMEM_LIMIT),
        name='dsa_sample',
    )(page_table, *([pool_idx] * PAGES_PER_STEP), *([pool_kv] * PAGES_PER_STEP),
      raw['a_ik'], raw['a_kv'], raw['a_iq'], raw['misc'], raw['a_q'], bias)


def _compress_sample_kernel(pt_ref, *refs, past):
    n = PAGES_PER_STEP
    pages = refs[:n]
    new_ref, pe_ref, w1_ref, w2_ref, o_ref, xk_sc, xv_sc = refs[n:]
    halves = (xk_sc, xv_sc)
    p = pl.program_id(1)
    for j, ref in enumerate(pages):
        r0 = pl.multiple_of((p * PAGES_PER_STEP + j) * PAGE_SIZE, PAGE_SIZE)
        for c, x_sc in enumerate(halves):
            x_sc[pl.ds(r0, PAGE_SIZE), :] = ref[0, :, c * LANES:(c + 1) * LANES]

    @pl.when(p == pl.num_programs(1) - 1)
    def _():
        nbp = o_ref.shape[1]
        t = new_ref.shape[0]
        for c, x_sc in enumerate(halves):
            x_sc[past:past + t, :] = new_ref[:, c * LANES:(c + 1) * LANES]
            x_sc[past + t:, :] = jnp.zeros((x_sc.shape[0] - past - t, LANES), F32)

        def body(l, accs):
            out = []
            for c, (x_sc, acc) in enumerate(zip(halves, accs)):
                xl = x_sc[pl.ds(l, nbp, stride=NSA_BLOCK), :] + pe_ref[c, pl.ds(l, 1), :]
                out.append(acc + jnp.dot(xl.astype(MXU), w1_ref[l, c], preferred_element_type=F32))
            return tuple(out)

        zero = jnp.zeros((nbp, LANES), F32)
        h = jnp.concatenate(lax.fori_loop(0, NSA_BLOCK, body, (zero, zero)), axis=1)
        o_ref[0] = jnp.dot(jax.nn.gelu(h).astype(MXU), w2_ref[...], preferred_element_type=F32)


def _padded_blocks(past, T):
    nb = -(-(past + T) // NSA_BLOCK)
    return nb, -(-nb // 8) * 8


def _compress_sample(raw, pool_cmp, page_table, pe_flat, w1_big, w2_big, DB, T):
    n_pages = page_table.shape[1]
    past = n_pages * PAGE_SIZE
    _, nbp = _padded_blocks(past, T)
    kv_w = 2 * B_KV_HEADS * HEAD_DIM
    assert kv_w == 2 * LANES and w1_big.shape[1] == 2 * LANES
    w1_l = w1_big.reshape(NSA_BLOCK, 2, LANES, 2, LANES)
    w1_halves = jnp.stack([w1_l[:, 0, :, 0, :], w1_l[:, 1, :, 1, :]], axis=1)
    fixed2 = lambda s, p, pt: (0, 0)
    grid_spec = pltpu.PrefetchScalarGridSpec(
        num_scalar_prefetch=1,
        grid=(DB, n_pages // PAGES_PER_STEP),
        in_specs=_paged_specs(kv_w) + [_seq_rows(kv_w, T),
                                       pl.BlockSpec((2, NSA_BLOCK, LANES), lambda s, p, pt: (0, 0, 0)),
                                       pl.BlockSpec(w1_halves.shape, lambda s, p, pt: (0, 0, 0, 0)),
                                       pl.BlockSpec(w2_big.shape, fixed2)],
        out_specs=pl.BlockSpec((1, nbp, kv_w), lambda s, p, pt: (s, 0, 0)),
        scratch_shapes=[pltpu.VMEM((nbp * NSA_BLOCK, LANES), F32), pltpu.VMEM((nbp * NSA_BLOCK, LANES), F32)],
    )
    return pl.pallas_call(
        functools.partial(_compress_sample_kernel, past=past), grid_spec=grid_spec,
        out_shape=jax.ShapeDtypeStruct((DB, nbp, kv_w), F32),
        compiler_params=pltpu.CompilerParams(dimension_semantics=('arbitrary', 'arbitrary'),
                                             vmem_limit_bytes=VMEM_LIMIT),
        name='nsa_compress_sample',
    )(page_table, *([pool_cmp] * PAGES_PER_STEP), raw['b_cmp_kv'],
      pe_flat.reshape(NSA_BLOCK, 2, LANES).swapaxes(0, 1), w1_halves, w2_big)


def _nsa_sample_kernel(pt_ref, *refs, past, nb, n_sel):
    n = PAGES_PER_STEP
    pages = refs[:n]
    (seln_ref, winn_ref, q_ref, misc_ref, cmp_ref, win_ref, expand_ref, bias_c_ref, bias_s_ref, bias_w_ref,
     o_ref, kv_sc, win_sc) = refs[n:]
    _stage_pages(pages, kv_sc)

    @pl.when(pl.program_id(1) == pl.num_programs(1) - 1)
    def _():
        T = q_ref.shape[0]
        L = kv_sc.shape[0]
        n_g = B_KV_HEADS
        rep = B_HEADS // n_g
        wb = win_ref.shape[1]
        _stage_new_rows(seln_ref, kv_sc, past)
        win_sc[0:wb, :] = win_ref[0]
        _stage_new_rows(winn_ref, win_sc, wb)
        misc = misc_ref[...]
        gate = lambda h, k: misc[:, MISC_GATE_OFF + 3 * h + k:MISC_GATE_OFF + 3 * h + k + 1]
        q = (q_ref[...] * HEAD_DIM ** -0.5).astype(MXU)
        cmp = cmp_ref[0].astype(MXU)
        nbp = cmp.shape[0]
        t_col = lax.broadcasted_iota(jnp.int32, (T, 1), 0)
        q_pos = past + t_col
        blk = lax.broadcasted_iota(jnp.int32, (T, nbp), 1)
        cur = lax.shift_right_logical(q_pos, NSA_BLOCK.bit_length() - 1)
        visible = (blk + 1) * NSA_BLOCK - 1 <= q_pos
        visible_rows = jnp.concatenate([visible] * rep, axis=0)
        forced = (blk == 0) | (blk == cur) | (blk == cur - 1)
        pos = lax.broadcasted_iota(jnp.int32, (T, L), 1)
        causal = pos <= q_pos
        kv = kv_sc[...].astype(MXU)
        win = win_sc[...].astype(MXU)
        wpos = lax.broadcasted_iota(jnp.int32, (T, win.shape[0]), 1)
        dw = wb + t_col - wpos
        madd_w = jnp.where((dw >= 0) & (dw < WINDOW), 0.0, NEG)
        madd_w = jnp.concatenate([madd_w] * rep, axis=0)
        outs = []
        for g in range(n_g):
            heads = range(g * rep, (g + 1) * rep)
            ks = slice(g * HEAD_DIM, (g + 1) * HEAD_DIM)
            vs = slice((n_g + g) * HEAD_DIM, (n_g + g + 1) * HEAD_DIM)
            q_rows = _stack_heads(q, heads, HEAD_DIM)
            lc = lax.dot_general(q_rows, cmp[:, ks], _NT, preferred_element_type=F32) + bias_c_ref[g]
            lc = jnp.where(visible_rows, lc, -jnp.inf)
            m = jnp.max(lc, axis=1, keepdims=True)
            m = jnp.where(m > -jnp.inf, m, 0.0)
            e = jnp.exp(lc - m)
            s = jnp.sum(e, axis=1, keepdims=True)
            p = e / jnp.where(s > 0, s, 1.0)
            o_cmp = jnp.dot(p.astype(MXU), cmp[:, vs], preferred_element_type=F32)
            importance = p[0:T]
            for r in range(1, rep):
                importance = importance + p[r * T:(r + 1) * T]
            score = jnp.where(forced, NSA_FORCE, importance)
            score = jnp.where(blk <= cur, score, -1.0)
            rank = jnp.zeros((T, nbp), jnp.int32)
            for b in range(nb):
                col = score[:, b:b + 1]
                rank = rank + jnp.where((col > score) | ((col == score) & (blk > b)), 1, 0)
            picked = jnp.where(rank < n_sel, 1.0, 0.0).astype(MXU)
            on_keys = jnp.dot(picked, expand_ref[...], preferred_element_type=F32)
            madd_s = jnp.where((on_keys > 0.5) & causal, 0.0, NEG)
            madd_s = jnp.concatenate([madd_s] * rep, axis=0)
            o_slc = _rows_softmax_attend(q_rows, kv[:, ks], kv[:, vs], bias_s_ref[g], madd_s)
            o_win = _rows_softmax_attend(q_rows, win[:, ks], win[:, vs], bias_w_ref[g], madd_w)
            for r, h in enumerate(heads):
                rows = slice(r * T, (r + 1) * T)
                outs.append(gate(h, 0) * o_cmp[rows] + gate(h, 1) * o_slc[rows] + gate(h, 2) * o_win[rows])
        o_ref[...] = jnp.concatenate(outs, axis=1)


def _nsa_sample(raw, cmp, pool_sel, win_state, page_table, table_b, DB, T):
    n_pages = page_table.shape[1]
    past = n_pages * PAGE_SIZE
    L = past + KEY_PAD
    nb, nbp = _padded_blocks(past, T)
    wb = win_state.shape[1]
    lw = wb + KEY_PAD
    kv_w = 2 * B_KV_HEADS * HEAD_DIM
    n_g = B_KV_HEADS
    expand = jnp.asarray(np.arange(L)[None, :] // NSA_BLOCK == np.arange(nbp)[:, None], dtype=MXU)
    blk_end = (np.arange(nbp) + 1) * NSA_BLOCK - 1
    d_c = (past + np.arange(T))[:, None] - blk_end[None, :]
    bias_c = jnp.moveaxis(_rel_bias_at(table_b, d_c), -1, 0).reshape(n_g, -1, nbp)
    bias_s = _sample_bias(table_b, T, past, L, 0, n_g)
    bias_w = _sample_bias(table_b, T, past, lw, past - wb, n_g)
    fixed2 = lambda s, p, pt: (0, 0)
    fixed3 = lambda s, p, pt: (0, 0, 0)
    per_seq3 = lambda s, p, pt: (s, 0, 0)
    grid_spec = pltpu.PrefetchScalarGridSpec(
        num_scalar_prefetch=1,
        grid=(DB, n_pages // PAGES_PER_STEP),
        in_specs=_paged_specs(kv_w)
        + [_seq_rows(kv_w, T), _seq_rows(kv_w, T), _seq_rows(B_HEADS * HEAD_DIM, T), _seq_rows(LANES, T),
           pl.BlockSpec((1, nbp, kv_w), per_seq3), pl.BlockSpec((1, wb, kv_w), per_seq3),
           pl.BlockSpec(expand.shape, fixed2), pl.BlockSpec(bias_c.shape, fixed3),
           pl.BlockSpec(bias_s.shape, fixed3), pl.BlockSpec(bias_w.shape, fixed3)],
        out_specs=_seq_rows(B_HEADS * HEAD_DIM, T),
        scratch_shapes=[pltpu.VMEM((L, kv_w), F32), pltpu.VMEM((lw, kv_w), F32)],
    )
    kern = functools.partial(_nsa_sample_kernel, past=past, nb=nb, n_sel=min(NSA_TOPN, nb))
    return pl.pallas_call(
        kern, grid_spec=grid_spec,
        out_shape=jax.ShapeDtypeStruct((DB * T, B_HEADS * HEAD_DIM), F32),
        compiler_params=pltpu.CompilerParams(dimension_semantics=('arbitrary', 'arbitrary'),
                                             vmem_limit_bytes=VMEM_LIMIT),
        name='nsa_sample',
    )(page_table, *([pool_sel] * PAGES_PER_STEP), raw['b_sel_kv'], raw['b_win_kv'], raw['b_q'], raw['misc'],
      cmp, win_state, expand, bias_c, bias_s, bias_w)


def _mix_sample(pr, cache_a_kv, cache_a_idx, cache_b_cmp_kv, cache_b_sel_kv, state_b_win_kv, page_table,
                rel_table, cmp_pe, cmp_w1, cmp_w2):
    DB, T = pr['a_q'].shape[:2]
    raw = pr['raw']
    n_pool = cache_a_kv.shape[0]
    kv_w = 2 * A_KV_HEADS * HEAD_DIM
    table_a, table_b = rel_table[:, :A_HEADS], rel_table[:, A_HEADS:]
    pe_flat, w1_big, w2_big = _compress_weights(cmp_pe, cmp_w1, cmp_w2)
    win_state = state_b_win_kv.reshape(DB, -1, kv_w)
    o_a = _dsa_sample(raw, cache_a_idx, cache_a_kv.reshape(n_pool, PAGE_SIZE, kv_w), page_table, table_a, DB, T)
    cmp = _compress_sample(raw, cache_b_cmp_kv.reshape(n_pool, PAGE_SIZE, kv_w), page_table, pe_flat, w1_big,
                           w2_big, DB, T)
    o_b = _nsa_sample(raw, cmp, cache_b_sel_kv.reshape(n_pool, PAGE_SIZE, kv_w), win_state, page_table, table_b,
                      DB, T)
    win_all = jnp.concatenate([state_b_win_kv, pr['b_win_kv']], axis=1)
    new_state = (pr['a_kv'], pr['a_ik'], pr['b_cmp_kv'], pr['b_sel_kv'], win_all[:, T:])
    return o_a, o_b, new_state


PEER_SLOTS = PEER_HEADS * PEER_TOPK
ROUTE_TOKENS = 256
EXPERT_TOKENS = 256
GATHER_DEPTH = 16
D_SUB = D_MODEL // LANES


def _pack_expert_tables(u, v):
    hi = lax.bitcast_convert_type(u.astype(BF16), jnp.uint16).astype(jnp.uint32) << 16
    lo = lax.bitcast_convert_type(v.astype(BF16), jnp.uint16).astype(jnp.uint32)
    return lax.bitcast_convert_type(hi | lo, jnp.int32).reshape(-1, LANES)


def _take_topk(cur, pos_iota, k, payload=None):
    n = cur.shape[0]
    vals, picks = [], []
    for _ in range(k):
        m = jnp.max(cur, axis=0, keepdims=True)
        pos = jnp.min(jnp.where(cur == m, pos_iota, n), axis=0, keepdims=True)
        hit = pos_iota == pos
        vals.append(m)
        picks.append(pos if payload is None else jnp.sum(jnp.where(hit, payload, 0), axis=0, keepdims=True))
        cur = jnp.where(hit, -jnp.inf, cur)
    return vals, picks


def _peer_route_kernel(h_ref, wpq_ref, keys_ref, ids_ref, g_ref):
    tn = h_ref.shape[0]
    half = PEER_QDIM // 2
    q = jnp.dot(h_ref[...].astype(MXU), wpq_ref[...], preferred_element_type=F32).astype(MXU)
    key_iota = lax.broadcasted_iota(jnp.int32, (PEER_NKEYS, tn), 0)
    pairs = [(a, b) for a in range(PEER_TOPK) for b in range(PEER_TOPK // (a + 1))]
    n_pairs = -(-len(pairs) // 8) * 8
    pair_iota = lax.broadcasted_iota(jnp.int32, (n_pairs, tn), 0)
    pad_v = [jnp.full((n_pairs - len(pairs), tn), -jnp.inf, F32)]
    pad_i = [jnp.zeros((n_pairs - len(pairs), tn), jnp.int32)]
    id_rows, g_rows = [], []
    for hd in range(PEER_HEADS):
        vals, idxs = [], []
        for p in range(2):
            c0 = (hd * 2 + p) * half
            s = lax.dot_general(keys_ref[hd * 2 + p], q[:, c0:c0 + half], _NT, preferred_element_type=F32)
            v_rows, i_rows = _take_topk(s, key_iota, PEER_TOPK)
            vals.append(v_rows)
            idxs.append(i_rows)
        cand = jnp.concatenate([vals[0][a] + vals[1][b] for a, b in pairs] + pad_v, axis=0)
        expert = jnp.concatenate([idxs[0][a] * PEER_NKEYS + idxs[1][b] for a, b in pairs] + pad_i, axis=0)
        cv, ce = _take_topk(cand, pair_iota, PEER_TOPK, payload=expert)
        cv = jnp.concatenate(cv, axis=0)
        e = jnp.exp(cv - cv[0:1, :])
        g_rows.append(e / jnp.sum(e, axis=0, keepdims=True))
        id_rows.append(jnp.concatenate(ce, axis=0))
    ids_ref[...] = jnp.concatenate(id_rows, axis=0).astype(F32).T.astype(jnp.int32)
    g_ref[...] = jnp.concatenate(g_rows, axis=0).T


def _peer_route(h2d, w_pq, sub_keys):
    n = h2d.shape[0]
    tn = ROUTE_TOKENS
    wpq = w_pq.reshape(D_MODEL, PEER_HEADS * PEER_QDIM).astype(MXU)
    keys = sub_keys.reshape(PEER_HEADS * 2, PEER_NKEYS, PEER_QDIM // 2).astype(MXU)
    return pl.pallas_call(
        _peer_route_kernel,
        grid=(n // tn,),
        in_specs=[pl.BlockSpec((tn, D_MODEL), lambda i: (i, 0)), pl.BlockSpec(wpq.shape, lambda i: (0, 0)),
                  pl.BlockSpec(keys.shape, lambda i: (0, 0, 0))],
        out_specs=[pl.BlockSpec((tn, PEER_SLOTS), lambda i: (i, 0)), pl.BlockSpec((tn, PEER_SLOTS), lambda i: (i, 0))],
        out_shape=[jax.ShapeDtypeStruct((n, PEER_SLOTS), jnp.int32), jax.ShapeDtypeStruct((n, PEER_SLOTS), F32)],
        compiler_params=pltpu.CompilerParams(dimension_semantics=('arbitrary',), vmem_limit_bytes=VMEM_LIMIT),
        name='peer_route',
    )(h2d, wpq, keys)


def _peer_expert_kernel(ids_ref, g_ref, h_ref, lng_ref, lnb_ref, uv_hbm, y_ref, buf, sem, gt_sc, out_sc):
    tb = h_ref.shape[0]
    ahead = GATHER_DEPTH - 1
    pieces = 2 * D_SUB
    per_piece = PEER_SLOTS // pieces

    def row_copy(t, slot, k):
        src = pl.ds(pl.multiple_of(ids_ref[t, k] * D_SUB, D_SUB), D_SUB)
        return pltpu.make_async_copy(uv_hbm.at[src, :], buf.at[slot, :, k, :], sem.at[slot])

    def issue(t, slot, k0=0, k1=PEER_SLOTS):
        for k in range(k0, k1):
            row_copy(t, slot, k).start(priority=k % 2)

    def consume(t, nxt=None):
        slot = t % GATHER_DEPTH

        def issue_piece(i):
            if nxt is not None:
                issue(nxt, nxt % GATHER_DEPTH, i * per_piece, (i + 1) * per_piece)

        for k in range(PEER_SLOTS):
            row_copy(t, slot, k).wait()
        x = h_ref[t]
        part = jnp.zeros((PEER_SLOTS, LANES), F32)
        for s in range(D_SUB):
            u = lax.bitcast_convert_type(buf[slot, s] & jnp.int32(-65536), F32)
            part = part + u * x[s:s + 1, :]
            issue_piece(s)
        act = jnp.sum(part, axis=1, keepdims=True)
        g_col = jnp.sum(jnp.where(lane == t, gt_sc[...], 0.0), axis=1, keepdims=True)
        coef = jnp.broadcast_to(g_col * jax.nn.gelu(act), (PEER_SLOTS, LANES))
        rows = []
        for s in range(D_SUB):
            v = lax.bitcast_convert_type(buf[slot, s] << 16, F32)
            rows.append(jnp.sum(v * coef, axis=0, keepdims=True))
            issue_piece(D_SUB + s)
        out_sc[t] = jnp.concatenate(rows, axis=0)

    gt_sc[...] = g_ref[...].T
    lane = lax.broadcasted_iota(jnp.int32, (PEER_SLOTS, tb), 1)

    def prologue(t, carry):
        issue(t, t)
        return carry

    def steady(t, carry):
        consume(t, nxt=t + ahead)
        return carry

    def drain(t, carry):
        consume(t)
        return carry

    lax.fori_loop(0, ahead, prologue, 0)
    lax.fori_loop(0, tb - ahead, steady, 0)
    lax.fori_loop(tb - ahead, tb, drain, 0)
    z = DN_ALPHA * h_ref[...] + out_sc[...]
    mean = lambda a: jnp.sum(jnp.sum(a, axis=2, keepdims=True), axis=1, keepdims=True) * (1.0 / D_MODEL)
    zc = z - mean(z)
    y_ref[...] = zc * lax.rsqrt(mean(zc * zc) + LN_EPS) * lng_ref[...] + lnb_ref[...]


def _peer_experts(h2d, ids, g, uv_rows, ln_g, ln_b):
    n = h2d.shape[0]
    tb = EXPERT_TOKENS
    row = lambda i: (i, 0)
    tok = lambda i: (i, 0, 0)
    y = pl.pallas_call(
        _peer_expert_kernel,
        grid=(n // tb,),
        in_specs=[pl.BlockSpec((tb, PEER_SLOTS), row, memory_space=pltpu.SMEM), pl.BlockSpec((tb, PEER_SLOTS), row),
                  pl.BlockSpec((tb, D_SUB, LANES), tok), pl.BlockSpec((D_SUB, LANES), lambda i: (0, 0)),
                  pl.BlockSpec((D_SUB, LANES), lambda i: (0, 0)), pl.BlockSpec(memory_space=pl.ANY)],
        out_specs=pl.BlockSpec((tb, D_SUB, LANES), tok),
        out_shape=jax.ShapeDtypeStruct((n, D_SUB, LANES), F32),
        scratch_shapes=[pltpu.VMEM((GATHER_DEPTH, D_SUB, PEER_SLOTS, LANES), jnp.int32),
                        pltpu.SemaphoreType.DMA((GATHER_DEPTH,)), pltpu.VMEM((PEER_SLOTS, tb), F32),
                        pltpu.VMEM((tb, D_SUB, LANES), F32)],
        compiler_params=pltpu.CompilerParams(dimension_semantics=('arbitrary',), vmem_limit_bytes=VMEM_LIMIT),
        name='peer_experts',
    )(ids, g, h2d.reshape(n, D_SUB, LANES), ln_g.reshape(D_SUB, LANES), ln_b.reshape(D_SUB, LANES), uv_rows)
    return y.reshape(n, D_MODEL)


def _group_step(x, mixer, ws, w_up_a, w_up_b, w_out, ln1_g, ln1_b, w_pq, sub_keys, uv, ln2_g, ln2_b, tm):
    B, T, _ = x.shape
    x2d = x.reshape(B * T, D_MODEL)
    pr = _unpack_proj(_project(x2d, ws, tm), B, T)
    o_a, o_b, new_state = mixer(pr)
    h = _merge(x2d, o_a.reshape(B * T, -1), o_b.reshape(B * T, -1), pr['merge_gate'],
               w_up_a, w_up_b, w_out, ln1_g, ln1_b, tm)
    ids, g = _peer_route(h, w_pq, sub_keys)
    y = _peer_experts(h, ids, g, uv, ln2_g, ln2_b)
    return y.reshape(B, T, D_MODEL), new_state


def kernel(x_prompt, x_sample, cache_a_kv, cache_a_idx, cache_b_cmp_kv, cache_b_sel_kv, state_b_win_kv, page_table, w_in, rel_bias_table, cmp_pe, cmp_w1, cmp_w2, w_up_a, w_up_b, w_out, ln1_g, ln1_b, w_pq, peer_sub_keys, peer_u, peer_v, ln2_g, ln2_b):
    l = 0
    ws = _split_w_in(w_in[l])
    uv = _pack_expert_tables(peer_u[l], peer_v[l])
    tail = (w_up_a[l], w_up_b[l], w_out[l], ln1_g[l], ln1_b[l], w_pq[l], peer_sub_keys[l], uv, ln2_g[l], ln2_b[l])
    mix_p = functools.partial(_mix_prompt, rel_table=rel_bias_table, cmp_pe=cmp_pe[l], cmp_w1=cmp_w1[l],
                              cmp_w2=cmp_w2[l])
    y_p, st_p = _group_step(x_prompt, mix_p, ws, *tail, tm=256)
    mix_s = functools.partial(_mix_sample, cache_a_kv=cache_a_kv[l], cache_a_idx=cache_a_idx[l],
                              cache_b_cmp_kv=cache_b_cmp_kv[l], cache_b_sel_kv=cache_b_sel_kv[l],
                              state_b_win_kv=state_b_win_kv[l], page_table=page_table, rel_table=rel_bias_table,
                              cmp_pe=cmp_pe[l], cmp_w1=cmp_w1[l], cmp_w2=cmp_w2[l])
    y_s, st_s = _group_step(x_sample, mix_s, ws, *tail, tm=256)
    return (y_p, y_s) + tuple(s[None] for s in st_p) + tuple(s[None] for s in st_s)
```

```python
import functools
import math

import jax
import jax.numpy as jnp
import numpy as np
from jax import lax
from jax.experimental import pallas as pl
from jax.experimental.pallas import tpu as pltpu

D_MODEL = 1024
DEPTH = 1
PAGE_SIZE = 128
HEAD_DIM = 64
A_HEADS = 8
A_KV_HEADS = 2
IDX_HEADS = 4
IDX_DIM = 64
A_TOPK = 256
B_HEADS = 8
B_KV_HEADS = 2
NSA_BLOCK = 64
NSA_TOPN = 16
NSA_FORCE = 8.0
CMP_HIDDEN = 64
WINDOW = 512
REL_BUCKETS = 32
REL_MAX_EXACT = 16
REL_MAX_DIST = 128
PEER_HEADS = 8
PEER_NKEYS = 128
PEER_QDIM = 256
PEER_TOPK = 16
Q_BLOCK = 32
TOKEN_BLOCK = 256
LN_EPS = 1e-5
DN_ALPHA = (2 * DEPTH) ** 0.25

PROJ_LAYOUT = (
    ('a_q', A_HEADS * HEAD_DIM),
    ('a_kv', 2 * A_KV_HEADS * HEAD_DIM),
    ('a_iq', IDX_HEADS * IDX_DIM),
    ('a_iw', IDX_HEADS),
    ('a_ik', IDX_DIM),
    ('b_q', B_HEADS * HEAD_DIM),
    ('b_cmp_kv', 2 * B_KV_HEADS * HEAD_DIM),
    ('b_sel_kv', 2 * B_KV_HEADS * HEAD_DIM),
    ('b_win_kv', 2 * B_KV_HEADS * HEAD_DIM),
    ('b_gate', B_HEADS * 3),
    ('merge_gate', 2 * D_MODEL),
)

LANES = 128
VMEM_LIMIT = 56 * 1024 * 1024
MISC_GATE_OFF = IDX_HEADS

F32 = jnp.float32
BF16 = jnp.bfloat16
MXU = jnp.bfloat16


_PROJ_OUT = ('a_q', 'a_kv', 'a_iq', 'a_ik', 'misc', 'b_q', 'b_cmp_kv', 'b_sel_kv', 'b_win_kv', 'merge_gate')


_STATE_OUT = ('a_kv', 'a_ik', 'b_cmp_kv', 'b_sel_kv', 'b_win_kv')


def _proj_kernel(x_ref, *refs, feature_major_state):
    n = len(_PROJ_OUT)
    w_refs, o_refs, t_refs = refs[:n], refs[n:2 * n], refs[2 * n:]
    t_refs = dict(zip(_STATE_OUT, t_refs)) if feature_major_state else {}
    x = x_ref[...].astype(BF16)
    for name, w_ref, o_ref in zip(_PROJ_OUT, w_refs, o_refs):
        r = jnp.dot(x, w_ref[...], preferred_element_type=F32)
        if name == 'misc':
            lane = lax.broadcasted_iota(jnp.int32, r.shape, 1)
            r = jnp.where(lane < MISC_GATE_OFF, r * IDX_HEADS ** -0.5, jax.nn.sigmoid(r))
        elif name == 'merge_gate':
            r = jax.nn.sigmoid(r)
        o_ref[...] = r
        if name in t_refs:
            t_refs[name][0] = r.T


def _split_w_in(w_in):
    parts = {}
    off = 0
    for name, width in PROJ_LAYOUT:
        parts[name] = w_in[:, off:off + width]
        off += width
    misc = jnp.concatenate([parts['a_iw'], parts['b_gate']], axis=1)
    parts['misc'] = jnp.pad(misc, ((0, 0), (0, LANES - misc.shape[1])))
    return [parts[name].astype(BF16) for name in _PROJ_OUT]


def _project(x2d, ws, tm, seq_len=None):
    n = x2d.shape[0]
    widths = [w.shape[1] for w in ws]
    out_specs = [pl.BlockSpec((tm, wd), lambda i: (i, 0)) for wd in widths]
    out_shape = [jax.ShapeDtypeStruct((n, wd), F32) for wd in widths]
    if seq_len is not None:
        tiles = seq_len // tm
        for name in _STATE_OUT:
            wd = widths[_PROJ_OUT.index(name)]
            out_specs.append(pl.BlockSpec((1, wd, tm), lambda i: (i // tiles, 0, i % tiles)))
            out_shape.append(jax.ShapeDtypeStruct((n // seq_len, wd, seq_len), F32))
    return pl.pallas_call(
        functools.partial(_proj_kernel, feature_major_state=seq_len is not None),
        grid=(n // tm,),
        in_specs=[pl.BlockSpec((tm, D_MODEL), lambda i: (i, 0))]
        + [pl.BlockSpec((D_MODEL, wd), lambda i: (0, 0)) for wd in widths],
        out_specs=out_specs,
        out_shape=out_shape,
        compiler_params=pltpu.CompilerParams(dimension_semantics=('arbitrary',), vmem_limit_bytes=VMEM_LIMIT),
        name='project',
    )(x2d, *ws)


def _unpack_proj(outs, B, T):
    d = dict(zip(_PROJ_OUT, outs))
    d['state_t'] = dict(zip(_STATE_OUT, outs[len(_PROJ_OUT):]))
    misc = d['misc']
    kv = (B, T, 2, A_KV_HEADS, HEAD_DIM)
    return {
        'a_q': d['a_q'].reshape(B, T, A_HEADS, HEAD_DIM),
        'a_kv': d['a_kv'].reshape(kv),
        'a_iq': d['a_iq'].reshape(B, T, IDX_HEADS, IDX_DIM),
        'a_iw': misc[:, :IDX_HEADS].reshape(B, T, IDX_HEADS),
        'a_ik': d['a_ik'].reshape(B, T, IDX_DIM),
        'b_q': d['b_q'].reshape(B, T, B_HEADS, HEAD_DIM),
        'b_cmp_kv': d['b_cmp_kv'].reshape(kv),
        'b_sel_kv': d['b_sel_kv'].reshape(kv),
        'b_win_kv': d['b_win_kv'].reshape(kv),
        'b_gate': misc[:, MISC_GATE_OFF:MISC_GATE_OFF + 3 * B_HEADS].reshape(B, T, B_HEADS, 3),
        'merge_gate': d['merge_gate'],
        'raw': d,
    }


def _layer_norm(x, g, b):
    mu = jnp.mean(x, axis=-1, keepdims=True)
    xc = x - mu
    var = jnp.mean(xc * xc, axis=-1, keepdims=True)
    return xc * lax.rsqrt(var + LN_EPS) * g + b


def _merge_kernel(x_ref, oa_ref, ob_ref, mg_ref, wa_ref, wb_ref, wo_ref, g_ref, b_ref, h_ref):
    ya = jnp.dot(oa_ref[...].astype(BF16), wa_ref[...], preferred_element_type=F32)
    yb = jnp.dot(ob_ref[...].astype(BF16), wb_ref[...], preferred_element_type=F32)
    mg = mg_ref[...]
    mix = mg[:, :D_MODEL] * ya + mg[:, D_MODEL:] * yb
    mixed = jnp.dot(mix.astype(BF16), wo_ref[...], preferred_element_type=F32)
    h_ref[...] = _layer_norm(DN_ALPHA * x_ref[...] + mixed, g_ref[...], b_ref[...])


def _merge(x2d, o_a, o_b, mg, w_up_a, w_up_b, w_out, ln_g, ln_b, tm):
    n = x2d.shape[0]
    ca, cb = o_a.shape[1], o_b.shape[1]
    row = lambda i: (i, 0)
    fixed = lambda i: (0, 0)
    return pl.pallas_call(
        _merge_kernel,
        grid=(n // tm,),
        in_specs=[pl.BlockSpec((tm, D_MODEL), row), pl.BlockSpec((tm, ca), row), pl.BlockSpec((tm, cb), row),
                  pl.BlockSpec((tm, 2 * D_MODEL), row), pl.BlockSpec((ca, D_MODEL), fixed),
                  pl.BlockSpec((cb, D_MODEL), fixed), pl.BlockSpec((D_MODEL, D_MODEL), fixed),
                  pl.BlockSpec((1, D_MODEL), fixed), pl.BlockSpec((1, D_MODEL), fixed)],
        out_specs=pl.BlockSpec((tm, D_MODEL), row),
        out_shape=jax.ShapeDtypeStruct((n, D_MODEL), F32),
        compiler_params=pltpu.CompilerParams(dimension_semantics=('arbitrary',), vmem_limit_bytes=VMEM_LIMIT),
        name='merge',
    )(x2d, o_a, o_b, mg, w_up_a.astype(BF16), w_up_b.astype(BF16), w_out.astype(BF16),
      ln_g.reshape(1, D_MODEL), ln_b.reshape(1, D_MODEL))


TQ = 256
NEG = -1e30
INT_MIN = -2 ** 31
COUNT_ROWS = 256
_NT = (((1,), (1,)), ((), ()))
_TN = (((0,), (0,)), ((), ()))


def _bucket_np(d):
    n = np.maximum(d, 0)
    nf = np.maximum(n, 1).astype(np.float64)
    large = REL_MAX_EXACT + (np.log(nf / REL_MAX_EXACT) / math.log(REL_MAX_DIST / REL_MAX_EXACT)
                             * (REL_BUCKETS - REL_MAX_EXACT)).astype(np.int64)
    return np.where(n < REL_MAX_EXACT, n, np.minimum(large, REL_BUCKETS - 1)).astype(np.int32)


def _rel_bias_at(table, d):
    bucket = jnp.asarray(_bucket_np(d))[..., None]
    out = jnp.broadcast_to(table[REL_BUCKETS - 1], bucket.shape[:-1] + table.shape[1:])
    for b in range(REL_BUCKETS - 1):
        out = jnp.where(bucket == b, table[b], out)
    return out


def _band_bias(table, n_groups):
    d = TQ + np.arange(TQ)[None, :] - np.arange(2 * TQ)[:, None]
    band = _rel_bias_at(table, d) - table[REL_BUCKETS - 1]
    band = jnp.moveaxis(band, -1, 1)
    return jnp.moveaxis(band.reshape(2 * TQ, n_groups, -1), 1, 0)


def _block_bias(table, S):
    nb = S // NSA_BLOCK
    d = np.arange(S)[None, :] - ((np.arange(nb) + 1) * NSA_BLOCK - 1)[:, None]
    return jnp.moveaxis(_rel_bias_at(table, d), -1, 0)


def _ordered_keys(x):
    x = jnp.where(x == 0.0, 0.0, x)
    b = lax.bitcast_convert_type(x, jnp.int32)
    return b ^ ((b >> 31) & jnp.int32(0x7FFFFFFF))


def _softmax_reset(m_sc, l_sc, acc_sc):
    m_sc[...] = jnp.full(m_sc.shape, NEG, F32)
    l_sc[...] = jnp.zeros(l_sc.shape, F32)
    acc_sc[...] = jnp.zeros(acc_sc.shape, F32)


def _group_queries(q, n_groups):
    rep = q.shape[1] // HEAD_DIM // n_groups
    return [_stack_heads(q, range(g * rep, (g + 1) * rep), HEAD_DIM) for g in range(n_groups)]


def _attend_chunk(kv_c, q_groups, madd_of_group, band_ref, band_row0, m_sc, l_sc, acc_sc):
    n_g = len(q_groups)
    rep = q_groups[0].shape[0] // TQ
    for g in range(n_g):
        k_c = kv_c[:, g * HEAD_DIM:(g + 1) * HEAD_DIM]
        v_c = kv_c[:, (n_g + g) * HEAD_DIM:(n_g + g + 1) * HEAD_DIM]
        add = jnp.concatenate([madd_of_group(g)] * rep, axis=1)
        if band_row0 is not None:
            add = add + band_ref[g, band_row0:band_row0 + TQ, :]
        logits = lax.dot_general(k_c, q_groups[g], _NT, preferred_element_type=F32) + add
        m_old = m_sc[g]
        m_new = jnp.maximum(m_old, jnp.max(logits, axis=0, keepdims=True))
        alpha = jnp.exp(m_old - m_new)
        p = jnp.exp(logits - m_new)
        l_sc[g] = alpha * l_sc[g] + jnp.sum(p, axis=0, keepdims=True)
        m_sc[g] = m_new
        p = p.astype(MXU)
        for r in range(rep):
            h = g * rep + r
            lanes = slice(r * TQ, (r + 1) * TQ)
            pv = lax.dot_general(v_c, p[:, lanes], _TN, preferred_element_type=F32)
            acc_sc[h] = alpha[:, lanes] * acc_sc[h] + pv


def _softmax_scratch(n_heads, n_groups):
    stats = pltpu.VMEM((n_groups, 1, n_heads // n_groups * TQ), F32)
    return [stats, stats, pltpu.VMEM((n_heads, HEAD_DIM, TQ), F32)]


def _attend_result(h, rep, l_sc, acc_sc):
    g, r = divmod(h, rep)
    return acc_sc[h] / l_sc[g][:, r * TQ:(r + 1) * TQ]


def _dsa_kernel(iq_ref, misc_ref, q_ref, ik_ref, kv_ref, band_ref, o_ref, key_sc, m_sc, l_sc, acc_sc, out_sc,
                *, k_sel, pos_bits):
    i = pl.program_id(1)
    t0 = i * TQ
    n_chunks = i + 1
    lane_t = t0 + lax.broadcasted_iota(jnp.int32, (1, TQ), 1)
    sub_iota = lax.broadcasted_iota(jnp.int32, (TQ, TQ), 0)
    misc_t = misc_ref[...].T
    iq = iq_ref[...].astype(MXU)
    iq_heads = [iq[:, h * IDX_DIM:(h + 1) * IDX_DIM] for h in range(IDX_HEADS)]

    def score_chunk(j, carry):
        r0 = pl.multiple_of(j * TQ, TQ)
        ik_c = ik_ref[pl.ds(r0, TQ), :].astype(MXU)
        sc = jnp.zeros((TQ, TQ), F32)
        for h in range(IDX_HEADS):
            rel = lax.dot_general(ik_c, iq_heads[h], _NT, preferred_element_type=F32)
            sc = sc + jnp.maximum(rel * IDX_DIM ** -0.5, 0.0) * misc_t[h:h + 1, :]
        sc = jnp.where(r0 + sub_iota <= lane_t, sc, -jnp.inf)
        key_sc[pl.ds(r0, TQ), :] = _ordered_keys(sc)
        return carry

    lax.fori_loop(0, n_chunks, score_chunk, 0)

    def count(pred):
        def body(jj, acc):
            r0 = pl.multiple_of(jj * COUNT_ROWS, COUNT_ROWS)
            blk = key_sc[pl.ds(r0, COUNT_ROWS), :]
            pos = r0 + lax.broadcasted_iota(jnp.int32, (COUNT_ROWS, TQ), 0)
            hit = jnp.where(pred(blk, pos), 1, 0)
            return acc + jnp.sum(hit.reshape(COUNT_ROWS // 8, 8, TQ), axis=0)

        acc = lax.fori_loop(0, n_chunks * (TQ // COUNT_ROWS), body, jnp.zeros((8, TQ), jnp.int32))
        return jnp.sum(acc, axis=0, keepdims=True)

    v = jnp.full((1, TQ), INT_MIN, jnp.int32)
    v = jnp.where(count(lambda blk, pos: blk >= 0) >= k_sel, 0, v)

    def value_bit(it, v):
        cand = v | jnp.left_shift(jnp.int32(1), 30 - it)
        return jnp.where(count(lambda blk, pos: blk >= cand) >= k_sel, cand, v)

    v = lax.fori_loop(0, 31, value_bit, v)
    need = k_sel - count(lambda blk, pos: blk > v)
    n_ge = count(lambda blk, pos: blk >= v)

    def tie_search():
        def pos_bit(it, jm):
            cand = jm | jnp.left_shift(jnp.int32(1), pos_bits - 1 - it)
            return jnp.where(count(lambda blk, pos: (blk == v) & (pos < cand)) < need, cand, jm)

        return lax.fori_loop(0, pos_bits, pos_bit, jnp.zeros((1, TQ), jnp.int32))

    j_max = lax.cond(jnp.max(n_ge) > k_sel, tie_search, lambda: jnp.full((1, TQ), 2 ** 30, jnp.int32))

    q = (q_ref[...] * HEAD_DIM ** -0.5).astype(MXU)
    q_groups = _group_queries(q, A_KV_HEADS)
    _softmax_reset(m_sc, l_sc, acc_sc)

    def chunk(j, band_row0, causal):
        r0 = pl.multiple_of(j * TQ, TQ)
        key = key_sc[pl.ds(r0, TQ), :]
        pos = r0 + sub_iota
        sel = (key > v) | ((key == v) & (pos <= j_max))
        if causal:
            sel = sel & (pos <= lane_t)
        madd = jnp.where(sel, 0.0, NEG)
        kv_c = kv_ref[pl.ds(r0, TQ), :].astype(MXU)
        _attend_chunk(kv_c, q_groups, lambda g: madd, band_ref, band_row0, m_sc, l_sc, acc_sc)

    def far_chunk(j, carry):
        chunk(j, None, False)
        return carry

    lax.fori_loop(0, jnp.maximum(i - 1, 0), far_chunk, 0)

    @pl.when(i >= 1)
    def _():
        chunk(i - 1, 0, False)

    chunk(i, TQ, True)
    for h in range(A_HEADS):
        out_sc[h * HEAD_DIM:(h + 1) * HEAD_DIM, :] = _attend_result(h, A_HEADS // A_KV_HEADS, l_sc, acc_sc)
    o_ref[...] = out_sc[...].T


def _dsa_prompt(raw, band_a, B, S):
    nq = S // TQ
    k_sel = min(A_TOPK, S // 4)
    assert S % TQ == 0 and TQ >= k_sel
    tile = lambda b, i: (b * nq + i, 0)
    seq = lambda b, i: (b, 0)
    kern = functools.partial(_dsa_kernel, k_sel=k_sel, pos_bits=(S - 1).bit_length())
    return pl.pallas_call(
        kern,
        grid=(B, nq),
        in_specs=[pl.BlockSpec((TQ, IDX_HEADS * IDX_DIM), tile), pl.BlockSpec((TQ, LANES), tile),
                  pl.BlockSpec((TQ, A_HEADS * HEAD_DIM), tile), pl.BlockSpec((S, IDX_DIM), seq),
                  pl.BlockSpec((S, 2 * A_KV_HEADS * HEAD_DIM), seq),
                  pl.BlockSpec(band_a.shape, lambda b, i: (0, 0, 0))],
        out_specs=pl.BlockSpec((TQ, A_HEADS * HEAD_DIM), tile),
        out_shape=jax.ShapeDtypeStruct((B * S, A_HEADS * HEAD_DIM), F32),
        scratch_shapes=[pltpu.VMEM((S, TQ), jnp.int32)] + _softmax_scratch(A_HEADS, A_KV_HEADS)
        + [pltpu.VMEM((A_HEADS * HEAD_DIM, TQ), F32)],
        compiler_params=pltpu.CompilerParams(dimension_semantics=('arbitrary', 'arbitrary'),
                                             vmem_limit_bytes=VMEM_LIMIT),
        name='dsa_prompt',
    )(raw['a_iq'], raw['misc'], raw['a_q'], raw['a_ik'], raw['a_kv'], band_a)


def _compress_kernel(x_ref, pe_ref, w1_ref, w2_ref, o_ref):
    x = (x_ref[...] + pe_ref[...]).astype(MXU)
    h = jax.nn.gelu(jnp.dot(x, w1_ref[...], preferred_element_type=F32))
    o_ref[...] = jnp.dot(h.astype(MXU), w2_ref[...], preferred_element_type=F32)


def _compress_weights(pe, w1, w2):
    eye_c = jnp.eye(2, dtype=F32)
    eye_g = jnp.eye(B_KV_HEADS, dtype=F32)
    w1_big = jnp.einsum('lcde,cC,gG->lcgdCGe', w1, eye_c, eye_g)
    w1_big = w1_big.reshape(NSA_BLOCK * 2 * B_KV_HEADS * HEAD_DIM, 2 * B_KV_HEADS * CMP_HIDDEN)
    w2_big = jnp.einsum('cef,cC,gG->cgeCGf', w2, eye_c, eye_g)
    w2_big = w2_big.reshape(2 * B_KV_HEADS * CMP_HIDDEN, 2 * B_KV_HEADS * HEAD_DIM)
    pe_flat = jnp.broadcast_to(pe[:, :, None, :], (NSA_BLOCK, 2, B_KV_HEADS, HEAD_DIM)).reshape(1, -1)
    return pe_flat, w1_big.astype(MXU), w2_big.astype(MXU)


def _compress(blocks2d, pe_flat, w1_big, w2_big, tm):
    n, width = blocks2d.shape
    fixed = lambda i: (0, 0)
    return pl.pallas_call(
        _compress_kernel,
        grid=(n // tm,),
        in_specs=[pl.BlockSpec((tm, width), lambda i: (i, 0)), pl.BlockSpec((1, width), fixed),
                  pl.BlockSpec(w1_big.shape, fixed), pl.BlockSpec(w2_big.shape, fixed)],
        out_specs=pl.BlockSpec((tm, w2_big.shape[1]), lambda i: (i, 0)),
        out_shape=jax.ShapeDtypeStruct((n, w2_big.shape[1]), F32),
        compiler_params=pltpu.CompilerParams(dimension_semantics=('arbitrary',), vmem_limit_bytes=VMEM_LIMIT),
        name='nsa_compress',
    )(blocks2d, pe_flat, w1_big, w2_big)


def _nsa_kernel(q_ref, misc_ref, cmp_ref, selkv_ref, winkv_ref, band_ref, biasc_ref, o_ref,
                sel_sc, m_sc, l_sc, acc_sc, out_sc, *, nb, n_sel):
    i = pl.program_id(1)
    t0 = i * TQ
    n_g = B_KV_HEADS
    rep = B_HEADS // n_g
    lane_t = t0 + lax.broadcasted_iota(jnp.int32, (1, TQ), 1)
    sub_iota = lax.broadcasted_iota(jnp.int32, (TQ, TQ), 0)
    lane_iota = lax.broadcasted_iota(jnp.int32, (TQ, TQ), 1)
    misc_t = misc_ref[...].T
    gate = lambda h, k: misc_t[MISC_GATE_OFF + 3 * h + k:MISC_GATE_OFF + 3 * h + k + 1, :]
    q = (q_ref[...] * HEAD_DIM ** -0.5).astype(MXU)
    q_heads = [q[:, h * HEAD_DIM:(h + 1) * HEAD_DIM] for h in range(B_HEADS)]
    q_groups = _group_queries(q, n_g)

    cmp = cmp_ref[...].astype(MXU)
    blk = lax.broadcasted_iota(jnp.int32, (nb, TQ), 0)
    visible = (blk + 1) * NSA_BLOCK - 1 <= lane_t
    cur = lax.shift_right_logical(lane_t, NSA_BLOCK.bit_length() - 1)
    forced = (blk == 0) | (blk == cur) | (blk == cur - 1)
    for g in range(n_g):
        k_c = cmp[:, g * HEAD_DIM:(g + 1) * HEAD_DIM]
        v_c = cmp[:, (n_g + g) * HEAD_DIM:(n_g + g + 1) * HEAD_DIM]
        importance = jnp.zeros((nb, TQ), F32)
        for r in range(rep):
            h = g * rep + r
            lc = lax.dot_general(k_c, q_heads[h], _NT, preferred_element_type=F32) + biasc_ref[h]
            lc = jnp.where(visible, lc, -jnp.inf)
            m = jnp.max(lc, axis=0, keepdims=True)
            m = jnp.where(m > -jnp.inf, m, 0.0)
            e = jnp.exp(lc - m)
            s = jnp.sum(e, axis=0, keepdims=True)
            p = e / jnp.where(s > 0, s, 1.0)
            importance = importance + p
            o_cmp = lax.dot_general(v_c, p.astype(MXU), _TN, preferred_element_type=F32)
            out_sc[h * HEAD_DIM:(h + 1) * HEAD_DIM, :] = gate(h, 0) * o_cmp
        score = jnp.where(forced, NSA_FORCE, importance)
        score = jnp.where(blk <= cur, score, -1.0)
        rank = jnp.zeros((nb, TQ), jnp.int32)
        for n in range(nb):
            row = score[n:n + 1, :]
            beats = (row > score) | ((row == score) & (blk > n))
            rank = rank + jnp.where(beats, 1, 0)
        sel_sc[g] = jnp.where(rank < n_sel, 0.0, NEG)

    def finish_branch(k):
        for h in range(B_HEADS):
            rows = slice(h * HEAD_DIM, (h + 1) * HEAD_DIM)
            out_sc[rows, :] = out_sc[rows, :] + gate(h, k) * _attend_result(h, rep, l_sc, acc_sc)

    _softmax_reset(m_sc, l_sc, acc_sc)
    blocks_per_chunk = TQ // NSA_BLOCK

    def sel_chunk(j, band_row0, causal):
        r0 = pl.multiple_of(j * TQ, TQ)
        kv_c = selkv_ref[pl.ds(r0, TQ), :].astype(MXU)

        def madd_of_group(g):
            rows = [sel_sc[g, pl.ds(j * blocks_per_chunk + b, 1), :] for b in range(blocks_per_chunk)]
            madd = jnp.concatenate([jnp.broadcast_to(row, (NSA_BLOCK, TQ)) for row in rows], axis=0)
            if causal:
                madd = jnp.where(sub_iota <= lane_iota, madd, NEG)
            return madd

        _attend_chunk(kv_c, q_groups, madd_of_group, band_ref, band_row0, m_sc, l_sc, acc_sc)

    def far_chunk(j, carry):
        sel_chunk(j, None, False)
        return carry

    lax.fori_loop(0, jnp.maximum(i - 1, 0), far_chunk, 0)

    @pl.when(i >= 1)
    def _():
        sel_chunk(i - 1, 0, False)

    sel_chunk(i, TQ, True)
    finish_branch(1)

    _softmax_reset(m_sc, l_sc, acc_sc)
    for back in range(WINDOW // TQ, -1, -1):
        dist = back * TQ + lane_iota - sub_iota
        madd = jnp.where((dist >= 0) & (dist < WINDOW), 0.0, NEG)
        band_row0 = {0: TQ, 1: 0}.get(back)

        def win_chunk(back=back, madd=madd, band_row0=band_row0):
            r0 = pl.multiple_of((i - back) * TQ, TQ)
            kv_c = winkv_ref[pl.ds(r0, TQ), :].astype(MXU)
            _attend_chunk(kv_c, q_groups, lambda g: madd, band_ref, band_row0, m_sc, l_sc, acc_sc)

        if back == 0:
            win_chunk()
        else:
            pl.when(i >= back)(win_chunk)
    finish_branch(2)
    o_ref[...] = out_sc[...].T


def _nsa_prompt(raw, cmp, band_b, bias_c, B, S):
    nq = S // TQ
    nb = S // NSA_BLOCK
    assert S % TQ == 0 and TQ % NSA_BLOCK == 0
    tile = lambda b, i: (b * nq + i, 0)
    seq = lambda b, i: (b, 0)
    kv_w = 2 * B_KV_HEADS * HEAD_DIM
    kern = functools.partial(_nsa_kernel, nb=nb, n_sel=min(NSA_TOPN, nb))
    return pl.pallas_call(
        kern,
        grid=(B, nq),
        in_specs=[pl.BlockSpec((TQ, B_HEADS * HEAD_DIM), tile), pl.BlockSpec((TQ, LANES), tile),
                  pl.BlockSpec((nb, kv_w), seq), pl.BlockSpec((S, kv_w), seq), pl.BlockSpec((S, kv_w), seq),
                  pl.BlockSpec(band_b.shape, lambda b, i: (0, 0, 0)),
                  pl.BlockSpec((B_HEADS, nb, TQ), lambda b, i: (0, 0, i))],
        out_specs=pl.BlockSpec((TQ, B_HEADS * HEAD_DIM), tile),
        out_shape=jax.ShapeDtypeStruct((B * S, B_HEADS * HEAD_DIM), F32),
        scratch_shapes=[pltpu.VMEM((B_KV_HEADS, nb, TQ), F32)] + _softmax_scratch(B_HEADS, B_KV_HEADS)
        + [pltpu.VMEM((B_HEADS * HEAD_DIM, TQ), F32)],
        compiler_params=pltpu.CompilerParams(dimension_semantics=('arbitrary', 'arbitrary'),
                                             vmem_limit_bytes=VMEM_LIMIT),
        name='nsa_prompt',
    )(raw['b_q'], raw['misc'], cmp, raw['b_sel_kv'], raw['b_win_kv'], band_b, bias_c)


def _mix_prompt(pr, rel_table, cmp_pe, cmp_w1, cmp_w2):
    B, S = pr['a_q'].shape[:2]
    raw = pr['raw']
    table_a, table_b = rel_table[:, :A_HEADS], rel_table[:, A_HEADS:]
    pe_flat, w1_big, w2_big = _compress_weights(cmp_pe, cmp_w1, cmp_w2)
    n_blocks = B * S // NSA_BLOCK
    cmp = _compress(raw['b_cmp_kv'].reshape(n_blocks, -1), pe_flat, w1_big, w2_big, tm=min(128, n_blocks))
    o_a = _dsa_prompt(raw, _band_bias(table_a, A_KV_HEADS), B, S)
    o_b = _nsa_prompt(raw, cmp, _band_bias(table_b, B_KV_HEADS), _block_bias(table_b, S), B, S)
    wb = min(WINDOW, S)
    kv_tail = (2, B_KV_HEADS, HEAD_DIM)

    def rows(name, tail, first=0):
        part = raw['state_t'][name][:, :, first:]
        return jnp.moveaxis(part.reshape((B,) + tail + (S - first,)), -1, 1)

    new_state = (rows('a_kv', kv_tail), rows('a_ik', (IDX_DIM,)), rows('b_cmp_kv', kv_tail),
                 rows('b_sel_kv', kv_tail), rows('b_win_kv', kv_tail, S - wb))
    return o_a, o_b, new_state


PAGES_PER_STEP = 8
KEY_PAD = 128


def _paged_specs(width):
    def spec(j):
        return pl.BlockSpec((1, width, PAGE_SIZE), lambda s, p, pt: (pt[s, p * PAGES_PER_STEP + j], 0, 0))
    return [spec(j) for j in range(PAGES_PER_STEP)]


def _feature_major(pool, width):
    return jnp.swapaxes(pool.reshape(pool.shape[0], PAGE_SIZE, width), 1, 2)


def _stage_pages(page_refs, dst_sc):
    p = pl.program_id(1)
    for j, ref in enumerate(page_refs):
        r0 = pl.multiple_of((p * PAGES_PER_STEP + j) * PAGE_SIZE, PAGE_SIZE)
        dst_sc[pl.ds(r0, PAGE_SIZE), :] = ref[0].T


def _stage_new_rows(new_ref, dst_sc, past):
    t = new_ref.shape[0]
    dst_sc[past:past + t, :] = new_ref[...]
    dst_sc[past + t:, :] = jnp.zeros((dst_sc.shape[0] - past - t, dst_sc.shape[1]), F32)


def _rows_softmax_attend(q_rows, k, v, bias, madd):
    logits = lax.dot_general(q_rows, k, _NT, preferred_element_type=F32) + bias + madd
    m = jnp.max(logits, axis=1, keepdims=True)
    e = jnp.exp(logits - m)
    p = e / jnp.sum(e, axis=1, keepdims=True)
    return jnp.dot(p.astype(MXU), v, preferred_element_type=F32)


def _stack_heads(x, heads, width):
    return jnp.concatenate([x[:, h * width:(h + 1) * width] for h in heads], axis=0)


def _dsa_sample_kernel(pt_ref, *refs, past, k_sel):
    n = PAGES_PER_STEP
    idx_pages, kv_pages = refs[:n], refs[n:2 * n]
    ikn_ref, kvn_ref, iq_ref, misc_ref, q_ref, bias_ref, o_ref, ik_sc, kv_sc = refs[2 * n:]
    _stage_pages(idx_pages, ik_sc)
    _stage_pages(kv_pages, kv_sc)

    @pl.when(pl.program_id(1) == pl.num_programs(1) - 1)
    def _():
        T = iq_ref.shape[0]
        L = ik_sc.shape[0]
        _stage_new_rows(ikn_ref, ik_sc, past)
        _stage_new_rows(kvn_ref, kv_sc, past)
        pos = lax.broadcasted_iota(jnp.int32, (T, L), 1)
        valid = pos <= past + lax.broadcasted_iota(jnp.int32, (T, L), 0)
        misc = misc_ref[...]
        iq = _stack_heads(iq_ref[...].astype(MXU), range(IDX_HEADS), IDX_DIM)
        rel = lax.dot_general(iq, ik_sc[...].astype(MXU), _NT, preferred_element_type=F32)
        sc = jnp.zeros((T, L), F32)
        for h in range(IDX_HEADS):
            sc = sc + jnp.maximum(rel[h * T:(h + 1) * T] * IDX_DIM ** -0.5, 0.0) * misc[:, h:h + 1]
        key = _ordered_keys(jnp.where(valid, sc, -jnp.inf))

        def count(hit):
            return jnp.sum(jnp.where(hit, 1, 0), axis=1, keepdims=True)

        v = jnp.full((T, 1), INT_MIN, jnp.int32)
        v = jnp.where(count(key >= 0) >= k_sel, 0, v)

        def value_bit(it, v):
            cand = v | jnp.left_shift(jnp.int32(1), 30 - it)
            return jnp.where(count(key >= cand) >= k_sel, cand, v)

        v = lax.fori_loop(0, 31, value_bit, v)
        need = k_sel - count(key > v)
        pos_bits = (L - 1).bit_length()

        def tie_search():
            def pos_bit(it, jm):
                cand = jm | jnp.left_shift(jnp.int32(1), pos_bits - 1 - it)
                return jnp.where(count((key == v) & (pos < cand)) < need, cand, jm)

            return lax.fori_loop(0, pos_bits, pos_bit, jnp.zeros((T, 1), jnp.int32))

        j_max = lax.cond(jnp.max(count(key >= v)) > k_sel, tie_search,
                         lambda: jnp.full((T, 1), 2 ** 30, jnp.int32))
        sel = ((key > v) | ((key == v) & (pos <= j_max))) & valid
        madd = jnp.where(sel, 0.0, NEG)
        rep = A_HEADS // A_KV_HEADS
        madd = jnp.concatenate([madd] * rep, axis=0)
        q = (q_ref[...] * HEAD_DIM ** -0.5).astype(MXU)
        kv = kv_sc[...].astype(MXU)
        for g in range(A_KV_HEADS):
            q_rows = _stack_heads(q, range(g * rep, (g + 1) * rep), HEAD_DIM)
            o = _rows_softmax_attend(q_rows, kv[:, g * HEAD_DIM:(g + 1) * HEAD_DIM],
                                     kv[:, (A_KV_HEADS + g) * HEAD_DIM:(A_KV_HEADS + g + 1) * HEAD_DIM],
                                     bias_ref[g], madd)
            for r in range(rep):
                h = g * rep + r
                o_ref[:, h * HEAD_DIM:(h + 1) * HEAD_DIM] = o[r * T:(r + 1) * T, :]


def _sample_bias(table, T, past, n_keys, key_pos0, n_groups):
    d = (past + np.arange(T))[:, None] - (key_pos0 + np.arange(n_keys))[None, :]
    b = _rel_bias_at(table, d)
    b = jnp.moveaxis(b, -1, 0)
    return b.reshape(n_groups, -1, n_keys)


def _seq_rows(width, T):
    return pl.BlockSpec((T, width), lambda s, p, pt: (s, 0))


def _dsa_sample(raw, pool_idx, pool_kv, page_table, table_a, DB, T):
    n_pages = page_table.shape[1]
    past = n_pages * PAGE_SIZE
    L = past + KEY_PAD
    assert n_pages % PAGES_PER_STEP == 0 and T <= KEY_PAD
    kv_w = 2 * A_KV_HEADS * HEAD_DIM
    bias = _sample_bias(table_a, T, past, L, 0, A_KV_HEADS)
    fixed3 = lambda s, p, pt: (0, 0, 0)
    grid_spec = pltpu.PrefetchScalarGridSpec(
        num_scalar_prefetch=1,
        grid=(DB, n_pages // PAGES_PER_STEP),
        in_specs=_paged_specs(IDX_DIM) + _paged_specs(kv_w)
        + [_seq_rows(IDX_DIM, T), _seq_rows(kv_w, T), _seq_rows(IDX_HEADS * IDX_DIM, T), _seq_rows(LANES, T),
           _seq_rows(A_HEADS * HEAD_DIM, T), pl.BlockSpec(bias.shape, fixed3)],
        out_specs=_seq_rows(A_HEADS * HEAD_DIM, T),
        scratch_shapes=[pltpu.VMEM((L, IDX_DIM), F32), pltpu.VMEM((L, kv_w), F32)],
    )
    kern = functools.partial(_dsa_sample_kernel, past=past, k_sel=min(A_TOPK, (past + T) // 4))
    return pl.pallas_call(
        kern, grid_spec=grid_spec,
        out_shape=jax.ShapeDtypeStruct((DB * T, A_HEADS * HEAD_DIM), F32),
        compiler_params=pltpu.CompilerParams(dimension_semantics=('arbitrary', 'arbitrary'),
                                             vmem_limit_bytes=VMEM_LIMIT),
        name='dsa_sample',
    )(page_table, *([pool_idx] * PAGES_PER_STEP), *([pool_kv] * PAGES_PER_STEP),
      raw['a_ik'], raw['a_kv'], raw['a_iq'], raw['misc'], raw['a_q'], bias)


def _compress_sample_kernel(pt_ref, *refs, past):
    n = PAGES_PER_STEP
    pages = refs[:n]
    new_ref, pe_ref, w1_ref, w2_ref, o_ref, xk_sc, xv_sc = refs[n:]
    halves = (xk_sc, xv_sc)
    p = pl.program_id(1)
    for j, ref in enumerate(pages):
        r0 = pl.multiple_of((p * PAGES_PER_STEP + j) * PAGE_SIZE, PAGE_SIZE)
        for c, x_sc in enumerate(halves):
            x_sc[pl.ds(r0, PAGE_SIZE), :] = ref[0, c * LANES:(c + 1) * LANES, :].T

    @pl.when(p == pl.num_programs(1) - 1)
    def _():
        nbp = o_ref.shape[1]
        t = new_ref.shape[0]
        for c, x_sc in enumerate(halves):
            x_sc[past:past + t, :] = new_ref[:, c * LANES:(c + 1) * LANES]
            x_sc[past + t:, :] = jnp.zeros((x_sc.shape[0] - past - t, LANES), F32)

        def body(l, accs):
            out = []
            for c, (x_sc, acc) in enumerate(zip(halves, accs)):
                xl = x_sc[pl.ds(l, nbp, stride=NSA_BLOCK), :] + pe_ref[c, pl.ds(l, 1), :]
                out.append(acc + jnp.dot(xl.astype(MXU), w1_ref[l, c], preferred_element_type=F32))
            return tuple(out)

        zero = jnp.zeros((nbp, LANES), F32)
        h = jnp.concatenate(lax.fori_loop(0, NSA_BLOCK, body, (zero, zero)), axis=1)
        o_ref[0] = jnp.dot(jax.nn.gelu(h).astype(MXU), w2_ref[...], preferred_element_type=F32)


def _padded_blocks(past, T):
    nb = -(-(past + T) // NSA_BLOCK)
    return nb, -(-nb // 8) * 8


def _compress_sample(raw, pool_cmp, page_table, pe_flat, w1_big, w2_big, DB, T):
    n_pages = page_table.shape[1]
    past = n_pages * PAGE_SIZE
    _, nbp = _padded_blocks(past, T)
    kv_w = 2 * B_KV_HEADS * HEAD_DIM
    assert kv_w == 2 * LANES and w1_big.shape[1] == 2 * LANES
    w1_l = w1_big.reshape(NSA_BLOCK, 2, LANES, 2, LANES)
    w1_halves = jnp.stack([w1_l[:, 0, :, 0, :], w1_l[:, 1, :, 1, :]], axis=1)
    fixed2 = lambda s, p, pt: (0, 0)
    grid_spec = pltpu.PrefetchScalarGridSpec(
        num_scalar_prefetch=1,
        grid=(DB, n_pages // PAGES_PER_STEP),
        in_specs=_paged_specs(kv_w) + [_seq_rows(kv_w, T),
                                       pl.BlockSpec((2, NSA_BLOCK, LANES), lambda s, p, pt: (0, 0, 0)),
                                       pl.BlockSpec(w1_halves.shape, lambda s, p, pt: (0, 0, 0, 0)),
                                       pl.BlockSpec(w2_big.shape, fixed2)],
        out_specs=pl.BlockSpec((1, nbp, kv_w), lambda s, p, pt: (s, 0, 0)),
        scratch_shapes=[pltpu.VMEM((nbp * NSA_BLOCK, LANES), F32), pltpu.VMEM((nbp * NSA_BLOCK, LANES), F32)],
    )
    return pl.pallas_call(
        functools.partial(_compress_sample_kernel, past=past), grid_spec=grid_spec,
        out_shape=jax.ShapeDtypeStruct((DB, nbp, kv_w), F32),
        compiler_params=pltpu.CompilerParams(dimension_semantics=('arbitrary', 'arbitrary'),
                                             vmem_limit_bytes=VMEM_LIMIT),
        name='nsa_compress_sample',
    )(page_table, *([pool_cmp] * PAGES_PER_STEP), raw['b_cmp_kv'],
      pe_flat.reshape(NSA_BLOCK, 2, LANES).swapaxes(0, 1), w1_halves, w2_big)


def _nsa_sample_kernel(pt_ref, *refs, past, nb, n_sel):
    n = PAGES_PER_STEP
    pages = refs[:n]
    (seln_ref, winn_ref, q_ref, misc_ref, cmp_ref, win_ref, expand_ref, bias_c_ref, bias_s_ref, bias_w_ref,
     o_ref, kv_sc, win_sc) = refs[n:]
    _stage_pages(pages, kv_sc)

    @pl.when(pl.program_id(1) == pl.num_programs(1) - 1)
    def _():
        T = q_ref.shape[0]
        L = kv_sc.shape[0]
        n_g = B_KV_HEADS
        rep = B_HEADS // n_g
        wb = win_ref.shape[1]
        _stage_new_rows(seln_ref, kv_sc, past)
        win_sc[0:wb, :] = win_ref[0]
        _stage_new_rows(winn_ref, win_sc, wb)
        misc = misc_ref[...]
        gate = lambda h, k: misc[:, MISC_GATE_OFF + 3 * h + k:MISC_GATE_OFF + 3 * h + k + 1]
        q = (q_ref[...] * HEAD_DIM ** -0.5).astype(MXU)
        cmp = cmp_ref[0].astype(MXU)
        nbp = cmp.shape[0]
        t_col = lax.broadcasted_iota(jnp.int32, (T, 1), 0)
        q_pos = past + t_col
        blk = lax.broadcasted_iota(jnp.int32, (T, nbp), 1)
        cur = lax.shift_right_logical(q_pos, NSA_BLOCK.bit_length() - 1)
        visible = (blk + 1) * NSA_BLOCK - 1 <= q_pos
        visible_rows = jnp.concatenate([visible] * rep, axis=0)
        forced = (blk == 0) | (blk == cur) | (blk == cur - 1)
        pos = lax.broadcasted_iota(jnp.int32, (T, L), 1)
        causal = pos <= q_pos
        kv = kv_sc[...].astype(MXU)
        win = win_sc[...].astype(MXU)
        wpos = lax.broadcasted_iota(jnp.int32, (T, win.shape[0]), 1)
        dw = wb + t_col - wpos
        madd_w = jnp.where((dw >= 0) & (dw < WINDOW), 0.0, NEG)
        madd_w = jnp.concatenate([madd_w] * rep, axis=0)
        outs = []
        for g in range(n_g):
            heads = range(g * rep, (g + 1) * rep)
            ks = slice(g * HEAD_DIM, (g + 1) * HEAD_DIM)
            vs = slice((n_g + g) * HEAD_DIM, (n_g + g + 1) * HEAD_DIM)
            q_rows = _stack_heads(q, heads, HEAD_DIM)
            lc = lax.dot_general(q_rows, cmp[:, ks], _NT, preferred_element_type=F32) + bias_c_ref[g]
            lc = jnp.where(visible_rows, lc, -jnp.inf)
            m = jnp.max(lc, axis=1, keepdims=True)
            m = jnp.where(m > -jnp.inf, m, 0.0)
            e = jnp.exp(lc - m)
            s = jnp.sum(e, axis=1, keepdims=True)
            p = e / jnp.where(s > 0, s, 1.0)
            o_cmp = jnp.dot(p.astype(MXU), cmp[:, vs], preferred_element_type=F32)
            importance = p[0:T]
            for r in range(1, rep):
                importance = importance + p[r * T:(r + 1) * T]
            score = jnp.where(forced, NSA_FORCE, importance)
            score = jnp.where(blk <= cur, score, -1.0)
            rank = jnp.zeros((T, nbp), jnp.int32)
            for b in range(nb):
                col = score[:, b:b + 1]
                rank = rank + jnp.where((col > score) | ((col == score) & (blk > b)), 1, 0)
            picked = jnp.where(rank < n_sel, 1.0, 0.0).astype(MXU)
            on_keys = jnp.dot(picked, expand_ref[...], preferred_element_type=F32)
            madd_s = jnp.where((on_keys > 0.5) & causal, 0.0, NEG)
            madd_s = jnp.concatenate([madd_s] * rep, axis=0)
            o_slc = _rows_softmax_attend(q_rows, kv[:, ks], kv[:, vs], bias_s_ref[g], madd_s)
            o_win = _rows_softmax_attend(q_rows, win[:, ks], win[:, vs], bias_w_ref[g], madd_w)
            for r, h in enumerate(heads):
                rows = slice(r * T, (r + 1) * T)
                outs.append(gate(h, 0) * o_cmp[rows] + gate(h, 1) * o_slc[rows] + gate(h, 2) * o_win[rows])
        o_ref[...] = jnp.concatenate(outs, axis=1)


def _nsa_sample(raw, cmp, pool_sel, win_state, page_table, table_b, DB, T):
    n_pages = page_table.shape[1]
    past = n_pages * PAGE_SIZE
    L = past + KEY_PAD
    nb, nbp = _padded_blocks(past, T)
    wb = win_state.shape[1]
    lw = wb + KEY_PAD
    kv_w = 2 * B_KV_HEADS * HEAD_DIM
    n_g = B_KV_HEADS
    expand = jnp.asarray(np.arange(L)[None, :] // NSA_BLOCK == np.arange(nbp)[:, None], dtype=MXU)
    blk_end = (np.arange(nbp) + 1) * NSA_BLOCK - 1
    d_c = (past + np.arange(T))[:, None] - blk_end[None, :]
    bias_c = jnp.moveaxis(_rel_bias_at(table_b, d_c), -1, 0).reshape(n_g, -1, nbp)
    bias_s = _sample_bias(table_b, T, past, L, 0, n_g)
    bias_w = _sample_bias(table_b, T, past, lw, past - wb, n_g)
    fixed2 = lambda s, p, pt: (0, 0)
    fixed3 = lambda s, p, pt: (0, 0, 0)
    per_seq3 = lambda s, p, pt: (s, 0, 0)
    grid_spec = pltpu.PrefetchScalarGridSpec(
        num_scalar_prefetch=1,
        grid=(DB, n_pages // PAGES_PER_STEP),
        in_specs=_paged_specs(kv_w)
        + [_seq_rows(kv_w, T), _seq_rows(kv_w, T), _seq_rows(B_HEADS * HEAD_DIM, T), _seq_rows(LANES, T),
           pl.BlockSpec((1, nbp, kv_w), per_seq3), pl.BlockSpec((1, wb, kv_w), per_seq3),
           pl.BlockSpec(expand.shape, fixed2), pl.BlockSpec(bias_c.shape, fixed3),
           pl.BlockSpec(bias_s.shape, fixed3), pl.BlockSpec(bias_w.shape, fixed3)],
        out_specs=_seq_rows(B_HEADS * HEAD_DIM, T),
        scratch_shapes=[pltpu.VMEM((L, kv_w), F32), pltpu.VMEM((lw, kv_w), F32)],
    )
    kern = functools.partial(_nsa_sample_kernel, past=past, nb=nb, n_sel=min(NSA_TOPN, nb))
    return pl.pallas_call(
        kern, grid_spec=grid_spec,
        out_shape=jax.ShapeDtypeStruct((DB * T, B_HEADS * HEAD_DIM), F32),
        compiler_params=pltpu.CompilerParams(dimension_semantics=('arbitrary', 'arbitrary'),
                                             vmem_limit_bytes=VMEM_LIMIT),
        name='nsa_sample',
    )(page_table, *([pool_sel] * PAGES_PER_STEP), raw['b_sel_kv'], raw['b_win_kv'], raw['b_q'], raw['misc'],
      cmp, win_state, expand, bias_c, bias_s, bias_w)


def _mix_sample(pr, cache_a_kv, cache_a_idx, cache_b_cmp_kv, cache_b_sel_kv, state_b_win_kv, page_table,
                rel_table, cmp_pe, cmp_w1, cmp_w2):
    DB, T = pr['a_q'].shape[:2]
    raw = pr['raw']
    kv_w = 2 * A_KV_HEADS * HEAD_DIM
    table_a, table_b = rel_table[:, :A_HEADS], rel_table[:, A_HEADS:]
    pe_flat, w1_big, w2_big = _compress_weights(cmp_pe, cmp_w1, cmp_w2)
    win_state = state_b_win_kv.reshape(DB, -1, kv_w)
    o_a = _dsa_sample(raw, _feature_major(cache_a_idx, IDX_DIM), _feature_major(cache_a_kv, kv_w), page_table,
                      table_a, DB, T)
    cmp = _compress_sample(raw, _feature_major(cache_b_cmp_kv, kv_w), page_table, pe_flat, w1_big, w2_big, DB, T)
    o_b = _nsa_sample(raw, cmp, _feature_major(cache_b_sel_kv, kv_w), win_state, page_table, table_b, DB, T)
    win_all = jnp.concatenate([state_b_win_kv, pr['b_win_kv']], axis=1)
    new_state = (pr['a_kv'], pr['a_ik'], pr['b_cmp_kv'], pr['b_sel_kv'], win_all[:, T:])
    return o_a, o_b, new_state


PEER_SLOTS = PEER_HEADS * PEER_TOPK
ROUTE_TOKENS = 256
EXPERT_TOKENS = 128
GATHER_DEPTH = 8
D_SUB = D_MODEL // LANES
EXPERT_ROWS = 2 * D_SUB


def _take_topk(cur, pos_iota, k, payload=None):
    n = cur.shape[0]
    vals, picks = [], []
    for _ in range(k):
        m = jnp.max(cur, axis=0, keepdims=True)
        pos = jnp.min(jnp.where(cur == m, pos_iota, n), axis=0, keepdims=True)
        hit = pos_iota == pos
        vals.append(m)
        picks.append(pos if payload is None else jnp.sum(jnp.where(hit, payload, 0), axis=0, keepdims=True))
        cur = jnp.where(hit, -jnp.inf, cur)
    return vals, picks


def _peer_route_kernel(h_ref, wpq_ref, keys_ref, ids_ref, g_ref):
    tn = h_ref.shape[0]
    half = PEER_QDIM // 2
    q = jnp.dot(h_ref[...].astype(MXU), wpq_ref[...], preferred_element_type=F32).astype(MXU)
    key_iota = lax.broadcasted_iota(jnp.int32, (PEER_NKEYS, tn), 0)
    pairs = [(a, b) for a in range(PEER_TOPK) for b in range(PEER_TOPK // (a + 1))]
    n_pairs = -(-len(pairs) // 8) * 8
    pair_iota = lax.broadcasted_iota(jnp.int32, (n_pairs, tn), 0)
    pad_v = [jnp.full((n_pairs - len(pairs), tn), -jnp.inf, F32)]
    pad_i = [jnp.zeros((n_pairs - len(pairs), tn), jnp.int32)]
    id_rows, g_rows = [], []
    for hd in range(PEER_HEADS):
        vals, idxs = [], []
        for p in range(2):
            c0 = (hd * 2 + p) * half
            s = lax.dot_general(keys_ref[hd * 2 + p], q[:, c0:c0 + half], _NT, preferred_element_type=F32)
            v_rows, i_rows = _take_topk(s, key_iota, PEER_TOPK)
            vals.append(v_rows)
            idxs.append(i_rows)
        cand = jnp.concatenate([vals[0][a] + vals[1][b] for a, b in pairs] + pad_v, axis=0)
        expert = jnp.concatenate([idxs[0][a] * PEER_NKEYS + idxs[1][b] for a, b in pairs] + pad_i, axis=0)
        cv, ce = _take_topk(cand, pair_iota, PEER_TOPK, payload=expert)
        cv = jnp.concatenate(cv, axis=0)
        e = jnp.exp(cv - cv[0:1, :])
        g_rows.append(e / jnp.sum(e, axis=0, keepdims=True))
        id_rows.append(jnp.concatenate(ce, axis=0))
    ids_ref[...] = jnp.concatenate(id_rows, axis=0).astype(F32).T.astype(jnp.int32)
    g_ref[...] = jnp.concatenate(g_rows, axis=0).T


def _peer_route(h2d, w_pq, sub_keys):
    n = h2d.shape[0]
    tn = ROUTE_TOKENS
    wpq = w_pq.reshape(D_MODEL, PEER_HEADS * PEER_QDIM).astype(MXU)
    keys = sub_keys.reshape(PEER_HEADS * 2, PEER_NKEYS, PEER_QDIM // 2).astype(MXU)
    return pl.pallas_call(
        _peer_route_kernel,
        grid=(n // tn,),
        in_specs=[pl.BlockSpec((tn, D_MODEL), lambda i: (i, 0)), pl.BlockSpec(wpq.shape, lambda i: (0, 0)),
                  pl.BlockSpec(keys.shape, lambda i: (0, 0, 0))],
        out_specs=[pl.BlockSpec((tn, PEER_SLOTS), lambda i: (i, 0)), pl.BlockSpec((tn, PEER_SLOTS), lambda i: (i, 0))],
        out_shape=[jax.ShapeDtypeStruct((n, PEER_SLOTS), jnp.int32), jax.ShapeDtypeStruct((n, PEER_SLOTS), F32)],
        compiler_params=pltpu.CompilerParams(dimension_semantics=('arbitrary',), vmem_limit_bytes=VMEM_LIMIT),
        name='peer_route',
    )(h2d, wpq, keys)


def _peer_expert_kernel(ids_ref, g_ref, h_ref, lng_ref, lnb_ref, uv_hbm, y_ref, buf, sem, gt_sc, out_sc):
    tb = h_ref.shape[0]
    ahead = GATHER_DEPTH - 1
    per_piece = PEER_SLOTS // EXPERT_ROWS

    def row_copy(t, slot, k):
        src = pl.ds(pl.multiple_of(ids_ref[t, k] * EXPERT_ROWS, EXPERT_ROWS), EXPERT_ROWS)
        return pltpu.make_async_copy(uv_hbm.at[src, :], buf.at[slot, :, k, :], sem.at[slot])

    def issue(t, slot, k0=0, k1=PEER_SLOTS):
        for k in range(k0, k1):
            row_copy(t, slot, k).start(priority=k % 2)

    def consume(t, nxt=None):
        slot = t % GATHER_DEPTH

        def issue_piece(i):
            if nxt is not None:
                issue(nxt, nxt % GATHER_DEPTH, i * per_piece, (i + 1) * per_piece)

        for k in range(PEER_SLOTS):
            row_copy(t, slot, k).wait()
        x = h_ref[t]
        part = jnp.zeros((PEER_SLOTS, LANES), F32)
        for s in range(D_SUB):
            part = part + buf[slot, s] * x[s:s + 1, :]
            issue_piece(s)
        act = jnp.sum(part, axis=1, keepdims=True)
        g_col = jnp.sum(jnp.where(lane == t, gt_sc[...], 0.0), axis=1, keepdims=True)
        coef = jnp.broadcast_to(g_col * jax.nn.gelu(act), (PEER_SLOTS, LANES))
        rows = []
        for s in range(D_SUB):
            rows.append(jnp.sum(buf[slot, D_SUB + s] * coef, axis=0, keepdims=True))
            issue_piece(D_SUB + s)
        out_sc[t] = jnp.concatenate(rows, axis=0)

    gt_sc[...] = g_ref[...].T
    lane = lax.broadcasted_iota(jnp.int32, (PEER_SLOTS, tb), 1)

    def prologue(t, carry):
        issue(t, t)
        return carry

    def steady(t, carry):
        consume(t, nxt=t + ahead)
        return carry

    def drain(t, carry):
        consume(t)
        return carry

    lax.fori_loop(0, ahead, prologue, 0)
    lax.fori_loop(0, tb - ahead, steady, 0)
    lax.fori_loop(tb - ahead, tb, drain, 0)
    z = DN_ALPHA * h_ref[...] + out_sc[...]
    mean = lambda a: jnp.sum(jnp.sum(a, axis=2, keepdims=True), axis=1, keepdims=True) * (1.0 / D_MODEL)
    zc = z - mean(z)
    y_ref[...] = zc * lax.rsqrt(mean(zc * zc) + LN_EPS) * lng_ref[...] + lnb_ref[...]


def _peer_experts(h2d, ids, g, uv_rows, ln_g, ln_b):
    n = h2d.shape[0]
    tb = EXPERT_TOKENS
    row = lambda i: (i, 0)
    tok = lambda i: (i, 0, 0)
    y = pl.pallas_call(
        _peer_expert_kernel,
        grid=(n // tb,),
        in_specs=[pl.BlockSpec((tb, PEER_SLOTS), row, memory_space=pltpu.SMEM), pl.BlockSpec((tb, PEER_SLOTS), row),
                  pl.BlockSpec((tb, D_SUB, LANES), tok), pl.BlockSpec((D_SUB, LANES), lambda i: (0, 0)),
                  pl.BlockSpec((D_SUB, LANES), lambda i: (0, 0)), pl.BlockSpec(memory_space=pl.ANY)],
        out_specs=pl.BlockSpec((tb, D_SUB, LANES), tok),
        out_shape=jax.ShapeDtypeStruct((n, D_SUB, LANES), F32),
        scratch_shapes=[pltpu.VMEM((GATHER_DEPTH, EXPERT_ROWS, PEER_SLOTS, LANES), F32),
                        pltpu.SemaphoreType.DMA((GATHER_DEPTH,)), pltpu.VMEM((PEER_SLOTS, tb), F32),
                        pltpu.VMEM((tb, D_SUB, LANES), F32)],
        compiler_params=pltpu.CompilerParams(dimension_semantics=('arbitrary',), vmem_limit_bytes=VMEM_LIMIT),
        name='peer_experts',
    )(ids, g, h2d.reshape(n, D_SUB, LANES), ln_g.reshape(D_SUB, LANES), ln_b.reshape(D_SUB, LANES), uv_rows)
    return y.reshape(n, D_MODEL)


def _group_step(x, mixer, ws, w_up_a, w_up_b, w_out, ln1_g, ln1_b, w_pq, sub_keys, uv, ln2_g, ln2_b, tm,
                feature_major_state=False):
    B, T, _ = x.shape
    x2d = x.reshape(B * T, D_MODEL)
    pr = _unpack_proj(_project(x2d, ws, tm, seq_len=T if feature_major_state else None), B, T)
    o_a, o_b, new_state = mixer(pr)
    h = _merge(x2d, o_a.reshape(B * T, -1), o_b.reshape(B * T, -1), pr['merge_gate'],
               w_up_a, w_up_b, w_out, ln1_g, ln1_b, tm)
    ids, g = _peer_route(h, w_pq, sub_keys)
    y = _peer_experts(h, ids, g, uv, ln2_g, ln2_b)
    return y.reshape(B, T, D_MODEL), new_state


def kernel(x_prompt, x_sample, cache_a_kv, cache_a_idx, cache_b_cmp_kv, cache_b_sel_kv, state_b_win_kv, page_table, w_in, rel_bias_table, cmp_pe, cmp_w1, cmp_w2, w_up_a, w_up_b, w_out, ln1_g, ln1_b, w_pq, peer_sub_keys, peer_u, peer_v, ln2_g, ln2_b):
    l = 0
    ws = _split_w_in(w_in[l])
    uv = jnp.concatenate([peer_u[l], peer_v[l]], axis=1).reshape(-1, LANES)
    tail = (w_up_a[l], w_up_b[l], w_out[l], ln1_g[l], ln1_b[l], w_pq[l], peer_sub_keys[l], uv, ln2_g[l], ln2_b[l])
    mix_p = functools.partial(_mix_prompt, rel_table=rel_bias_table, cmp_pe=cmp_pe[l], cmp_w1=cmp_w1[l],
                              cmp_w2=cmp_w2[l])
    y_p, st_p = _group_step(x_prompt, mix_p, ws, *tail, tm=256, feature_major_state=True)
    mix_s = functools.partial(_mix_sample, cache_a_kv=cache_a_kv[l], cache_a_idx=cache_a_idx[l],
                              cache_b_cmp_kv=cache_b_cmp_kv[l], cache_b_sel_kv=cache_b_sel_kv[l],
                              state_b_win_kv=state_b_win_kv[l], page_table=page_table, rel_table=rel_bias_table,
                              cmp_pe=cmp_pe[l], cmp_w1=cmp_w1[l], cmp_w2=cmp_w2[l])
    y_s, st_s = _group_step(x_sample, mix_s, ws, *tail, tm=256)
    return (y_p, y_s) + tuple(s[None] for s in st_p) + tuple(s[None] for s in st_s)
```

```python
import functools
import math

import jax
import jax.numpy as jnp
import numpy as np
from jax import lax
from jax.experimental import pallas as pl
from jax.experimental.pallas import tpu as pltpu

D_MODEL = 1024
DEPTH = 1
PAGE_SIZE = 128
HEAD_DIM = 64
A_HEADS = 8
A_KV_HEADS = 2
IDX_HEADS = 4
IDX_DIM = 64
A_TOPK = 256
B_HEADS = 8
B_KV_HEADS = 2
NSA_BLOCK = 64
NSA_TOPN = 16
NSA_FORCE = 8.0
CMP_HIDDEN = 64
WINDOW = 512
REL_BUCKETS = 32
REL_MAX_EXACT = 16
REL_MAX_DIST = 128
PEER_HEADS = 8
PEER_NKEYS = 128
PEER_QDIM = 256
PEER_TOPK = 16
Q_BLOCK = 32
TOKEN_BLOCK = 256
LN_EPS = 1e-5
DN_ALPHA = (2 * DEPTH) ** 0.25

PROJ_LAYOUT = (
    ('a_q', A_HEADS * HEAD_DIM),
    ('a_kv', 2 * A_KV_HEADS * HEAD_DIM),
    ('a_iq', IDX_HEADS * IDX_DIM),
    ('a_iw', IDX_HEADS),
    ('a_ik', IDX_DIM),
    ('b_q', B_HEADS * HEAD_DIM),
    ('b_cmp_kv', 2 * B_KV_HEADS * HEAD_DIM),
    ('b_sel_kv', 2 * B_KV_HEADS * HEAD_DIM),
    ('b_win_kv', 2 * B_KV_HEADS * HEAD_DIM),
    ('b_gate', B_HEADS * 3),
    ('merge_gate', 2 * D_MODEL),
)

LANES = 128
VMEM_LIMIT = 56 * 1024 * 1024
MISC_GATE_OFF = IDX_HEADS

F32 = jnp.float32
BF16 = jnp.bfloat16
MXU = jnp.bfloat16


_PROJ_OUT = ('a_q', 'a_kv', 'a_iq', 'a_ik', 'misc', 'b_q', 'b_cmp_kv', 'b_sel_kv', 'b_win_kv', 'merge_gate')


_STATE_OUT = ('a_kv', 'a_ik', 'b_cmp_kv', 'b_sel_kv', 'b_win_kv')


def _proj_kernel(x_ref, *refs, feature_major_state):
    n = len(_PROJ_OUT)
    w_refs, o_refs, t_refs = refs[:n], refs[n:2 * n], refs[2 * n:]
    t_refs = dict(zip(_STATE_OUT, t_refs)) if feature_major_state else {}
    x = x_ref[...].astype(BF16)
    for name, w_ref, o_ref in zip(_PROJ_OUT, w_refs, o_refs):
        r = jnp.dot(x, w_ref[...], preferred_element_type=F32)
        if name == 'misc':
            lane = lax.broadcasted_iota(jnp.int32, r.shape, 1)
            r = jnp.where(lane < MISC_GATE_OFF, r * IDX_HEADS ** -0.5, jax.nn.sigmoid(r))
        elif name == 'merge_gate':
            r = jax.nn.sigmoid(r)
        o_ref[...] = r
        if name in t_refs:
            t_refs[name][0] = r.T


def _split_w_in(w_in):
    parts = {}
    off = 0
    for name, width in PROJ_LAYOUT:
        parts[name] = w_in[:, off:off + width]
        off += width
    misc = jnp.concatenate([parts['a_iw'], parts['b_gate']], axis=1)
    parts['misc'] = jnp.pad(misc, ((0, 0), (0, LANES - misc.shape[1])))
    return [parts[name].astype(BF16) for name in _PROJ_OUT]


def _project(x2d, ws, tm, seq_len=None):
    n = x2d.shape[0]
    widths = [w.shape[1] for w in ws]
    out_specs = [pl.BlockSpec((tm, wd), lambda i: (i, 0)) for wd in widths]
    out_shape = [jax.ShapeDtypeStruct((n, wd), F32) for wd in widths]
    if seq_len is not None:
        tiles = seq_len // tm
        for name in _STATE_OUT:
            wd = widths[_PROJ_OUT.index(name)]
            out_specs.append(pl.BlockSpec((1, wd, tm), lambda i: (i // tiles, 0, i % tiles)))
            out_shape.append(jax.ShapeDtypeStruct((n // seq_len, wd, seq_len), F32))
    return pl.pallas_call(
        functools.partial(_proj_kernel, feature_major_state=seq_len is not None),
        grid=(n // tm,),
        in_specs=[pl.BlockSpec((tm, D_MODEL), lambda i: (i, 0))]
        + [pl.BlockSpec((D_MODEL, wd), lambda i: (0, 0)) for wd in widths],
        out_specs=out_specs,
        out_shape=out_shape,
        compiler_params=pltpu.CompilerParams(dimension_semantics=('arbitrary',), vmem_limit_bytes=VMEM_LIMIT),
        name='project',
    )(x2d, *ws)


def _unpack_proj(outs, B, T):
    d = dict(zip(_PROJ_OUT, outs))
    d['state_t'] = dict(zip(_STATE_OUT, outs[len(_PROJ_OUT):]))
    misc = d['misc']
    kv = (B, T, 2, A_KV_HEADS, HEAD_DIM)
    return {
        'a_q': d['a_q'].reshape(B, T, A_HEADS, HEAD_DIM),
        'a_kv': d['a_kv'].reshape(kv),
        'a_iq': d['a_iq'].reshape(B, T, IDX_HEADS, IDX_DIM),
        'a_iw': misc[:, :IDX_HEADS].reshape(B, T, IDX_HEADS),
        'a_ik': d['a_ik'].reshape(B, T, IDX_DIM),
        'b_q': d['b_q'].reshape(B, T, B_HEADS, HEAD_DIM),
        'b_cmp_kv': d['b_cmp_kv'].reshape(kv),
        'b_sel_kv': d['b_sel_kv'].reshape(kv),
        'b_win_kv': d['b_win_kv'].reshape(kv),
        'b_gate': misc[:, MISC_GATE_OFF:MISC_GATE_OFF + 3 * B_HEADS].reshape(B, T, B_HEADS, 3),
        'merge_gate': d['merge_gate'],
        'raw': d,
    }


def _layer_norm(x, g, b):
    mu = jnp.mean(x, axis=-1, keepdims=True)
    xc = x - mu
    var = jnp.mean(xc * xc, axis=-1, keepdims=True)
    return xc * lax.rsqrt(var + LN_EPS) * g + b


def _merge_kernel(x_ref, oa_ref, ob_ref, mg_ref, wa_ref, wb_ref, wo_ref, g_ref, b_ref, h_ref):
    ya = jnp.dot(oa_ref[...].astype(BF16), wa_ref[...], preferred_element_type=F32)
    yb = jnp.dot(ob_ref[...].astype(BF16), wb_ref[...], preferred_element_type=F32)
    mg = mg_ref[...]
    mix = mg[:, :D_MODEL] * ya + mg[:, D_MODEL:] * yb
    mixed = jnp.dot(mix.astype(BF16), wo_ref[...], preferred_element_type=F32)
    h_ref[...] = _layer_norm(DN_ALPHA * x_ref[...] + mixed, g_ref[...], b_ref[...])


def _merge(x2d, o_a, o_b, mg, w_up_a, w_up_b, w_out, ln_g, ln_b, tm):
    n = x2d.shape[0]
    ca, cb = o_a.shape[1], o_b.shape[1]
    row = lambda i: (i, 0)
    fixed = lambda i: (0, 0)
    return pl.pallas_call(
        _merge_kernel,
        grid=(n // tm,),
        in_specs=[pl.BlockSpec((tm, D_MODEL), row), pl.BlockSpec((tm, ca), row), pl.BlockSpec((tm, cb), row),
                  pl.BlockSpec((tm, 2 * D_MODEL), row), pl.BlockSpec((ca, D_MODEL), fixed),
                  pl.BlockSpec((cb, D_MODEL), fixed), pl.BlockSpec((D_MODEL, D_MODEL), fixed),
                  pl.BlockSpec((1, D_MODEL), fixed), pl.BlockSpec((1, D_MODEL), fixed)],
        out_specs=pl.BlockSpec((tm, D_MODEL), row),
        out_shape=jax.ShapeDtypeStruct((n, D_MODEL), F32),
        compiler_params=pltpu.CompilerParams(dimension_semantics=('arbitrary',), vmem_limit_bytes=VMEM_LIMIT),
        name='merge',
    )(x2d, o_a, o_b, mg, w_up_a.astype(BF16), w_up_b.astype(BF16), w_out.astype(BF16),
      ln_g.reshape(1, D_MODEL), ln_b.reshape(1, D_MODEL))


TQ = 256
NEG = -1e30
INT_MIN = -2 ** 31
COUNT_ROWS = 256
_NT = (((1,), (1,)), ((), ()))
_TN = (((0,), (0,)), ((), ()))


def _bucket_np(d):
    n = np.maximum(d, 0)
    nf = np.maximum(n, 1).astype(np.float64)
    large = REL_MAX_EXACT + (np.log(nf / REL_MAX_EXACT) / math.log(REL_MAX_DIST / REL_MAX_EXACT)
                             * (REL_BUCKETS - REL_MAX_EXACT)).astype(np.int64)
    return np.where(n < REL_MAX_EXACT, n, np.minimum(large, REL_BUCKETS - 1)).astype(np.int32)


def _rel_bias_at(table, d):
    bucket = jnp.asarray(_bucket_np(d))[..., None]
    out = jnp.broadcast_to(table[REL_BUCKETS - 1], bucket.shape[:-1] + table.shape[1:])
    for b in range(REL_BUCKETS - 1):
        out = jnp.where(bucket == b, table[b], out)
    return out


def _band_bias(table, n_groups):
    d = TQ + np.arange(TQ)[None, :] - np.arange(2 * TQ)[:, None]
    band = _rel_bias_at(table, d) - table[REL_BUCKETS - 1]
    band = jnp.moveaxis(band, -1, 1)
    return jnp.moveaxis(band.reshape(2 * TQ, n_groups, -1), 1, 0)


def _block_bias(table, S):
    nb = S // NSA_BLOCK
    d = np.arange(S)[None, :] - ((np.arange(nb) + 1) * NSA_BLOCK - 1)[:, None]
    return jnp.moveaxis(_rel_bias_at(table, d), -1, 0)


def _ordered_keys(x):
    x = jnp.where(x == 0.0, 0.0, x)
    b = lax.bitcast_convert_type(x, jnp.int32)
    return b ^ ((b >> 31) & jnp.int32(0x7FFFFFFF))


def _softmax_reset(m_sc, l_sc, acc_sc):
    m_sc[...] = jnp.full(m_sc.shape, NEG, F32)
    l_sc[...] = jnp.zeros(l_sc.shape, F32)
    acc_sc[...] = jnp.zeros(acc_sc.shape, F32)


def _group_queries(q, n_groups):
    rep = q.shape[1] // HEAD_DIM // n_groups
    return [_stack_heads(q, range(g * rep, (g + 1) * rep), HEAD_DIM) for g in range(n_groups)]


def _attend_chunk(kv_c, q_groups, madd_of_group, band_ref, band_row0, m_sc, l_sc, acc_sc):
    n_g = len(q_groups)
    rep = q_groups[0].shape[0] // TQ
    for g in range(n_g):
        k_c = kv_c[:, g * HEAD_DIM:(g + 1) * HEAD_DIM]
        v_c = kv_c[:, (n_g + g) * HEAD_DIM:(n_g + g + 1) * HEAD_DIM]
        add = jnp.concatenate([madd_of_group(g)] * rep, axis=1)
        if band_row0 is not None:
            add = add + band_ref[g, band_row0:band_row0 + TQ, :]
        logits = lax.dot_general(k_c, q_groups[g], _NT, preferred_element_type=F32) + add
        m_old = m_sc[g]
        m_new = jnp.maximum(m_old, jnp.max(logits, axis=0, keepdims=True))
        alpha = jnp.exp(m_old - m_new)
        p = jnp.exp(logits - m_new)
        l_sc[g] = alpha * l_sc[g] + jnp.sum(p, axis=0, keepdims=True)
        m_sc[g] = m_new
        p = p.astype(MXU)
        for r in range(rep):
            h = g * rep + r
            lanes = slice(r * TQ, (r + 1) * TQ)
            pv = lax.dot_general(v_c, p[:, lanes], _TN, preferred_element_type=F32)
            acc_sc[h] = alpha[:, lanes] * acc_sc[h] + pv


def _softmax_scratch(n_heads, n_groups):
    stats = pltpu.VMEM((n_groups, 1, n_heads // n_groups * TQ), F32)
    return [stats, stats, pltpu.VMEM((n_heads, HEAD_DIM, TQ), F32)]


def _attend_result(h, rep, l_sc, acc_sc):
    g, r = divmod(h, rep)
    return acc_sc[h] / l_sc[g][:, r * TQ:(r + 1) * TQ]


def _dsa_kernel(iq_ref, misc_ref, q_ref, ik_ref, kv_ref, band_ref, o_ref, key_sc, m_sc, l_sc, acc_sc, out_sc,
                *, k_sel, pos_bits):
    i = pl.program_id(1)
    t0 = i * TQ
    n_chunks = i + 1
    lane_t = t0 + lax.broadcasted_iota(jnp.int32, (1, TQ), 1)
    sub_iota = lax.broadcasted_iota(jnp.int32, (TQ, TQ), 0)
    misc_t = misc_ref[...].T
    iq = iq_ref[...].astype(MXU)
    iq_heads = [iq[:, h * IDX_DIM:(h + 1) * IDX_DIM] for h in range(IDX_HEADS)]

    def score_chunk(j, carry):
        r0 = pl.multiple_of(j * TQ, TQ)
        ik_c = ik_ref[pl.ds(r0, TQ), :].astype(MXU)
        sc = jnp.zeros((TQ, TQ), F32)
        for h in range(IDX_HEADS):
            rel = lax.dot_general(ik_c, iq_heads[h], _NT, preferred_element_type=F32)
            sc = sc + jnp.maximum(rel * IDX_DIM ** -0.5, 0.0) * misc_t[h:h + 1, :]
        sc = jnp.where(r0 + sub_iota <= lane_t, sc, -jnp.inf)
        key_sc[pl.ds(r0, TQ), :] = _ordered_keys(sc)
        return carry

    lax.fori_loop(0, n_chunks, score_chunk, 0)

    def count(pred):
        def body(jj, acc):
            r0 = pl.multiple_of(jj * COUNT_ROWS, COUNT_ROWS)
            blk = key_sc[pl.ds(r0, COUNT_ROWS), :]
            pos = r0 + lax.broadcasted_iota(jnp.int32, (COUNT_ROWS, TQ), 0)
            hit = jnp.where(pred(blk, pos), 1, 0)
            return acc + jnp.sum(hit.reshape(COUNT_ROWS // 8, 8, TQ), axis=0)

        acc = lax.fori_loop(0, n_chunks * (TQ // COUNT_ROWS), body, jnp.zeros((8, TQ), jnp.int32))
        return jnp.sum(acc, axis=0, keepdims=True)

    v = jnp.full((1, TQ), INT_MIN, jnp.int32)
    v = jnp.where(count(lambda blk, pos: blk >= 0) >= k_sel, 0, v)

    def value_bit(it, v):
        cand = v | jnp.left_shift(jnp.int32(1), 30 - it)
        return jnp.where(count(lambda blk, pos: blk >= cand) >= k_sel, cand, v)

    v = lax.fori_loop(0, 31, value_bit, v)
    need = k_sel - count(lambda blk, pos: blk > v)
    n_ge = count(lambda blk, pos: blk >= v)

    def tie_search():
        def pos_bit(it, jm):
            cand = jm | jnp.left_shift(jnp.int32(1), pos_bits - 1 - it)
            return jnp.where(count(lambda blk, pos: (blk == v) & (pos < cand)) < need, cand, jm)

        return lax.fori_loop(0, pos_bits, pos_bit, jnp.zeros((1, TQ), jnp.int32))

    j_max = lax.cond(jnp.max(n_ge) > k_sel, tie_search, lambda: jnp.full((1, TQ), 2 ** 30, jnp.int32))

    q = (q_ref[...] * HEAD_DIM ** -0.5).astype(MXU)
    q_groups = _group_queries(q, A_KV_HEADS)
    _softmax_reset(m_sc, l_sc, acc_sc)

    def chunk(j, band_row0, causal):
        r0 = pl.multiple_of(j * TQ, TQ)
        key = key_sc[pl.ds(r0, TQ), :]
        pos = r0 + sub_iota
        sel = (key > v) | ((key == v) & (pos <= j_max))
        if causal:
            sel = sel & (pos <= lane_t)
        madd = jnp.where(sel, 0.0, NEG)
        kv_c = kv_ref[pl.ds(r0, TQ), :].astype(MXU)
        _attend_chunk(kv_c, q_groups, lambda g: madd, band_ref, band_row0, m_sc, l_sc, acc_sc)

    def far_chunk(j, carry):
        chunk(j, None, False)
        return carry

    lax.fori_loop(0, jnp.maximum(i - 1, 0), far_chunk, 0)

    @pl.when(i >= 1)
    def _():
        chunk(i - 1, 0, False)

    chunk(i, TQ, True)
    for h in range(A_HEADS):
        out_sc[h * HEAD_DIM:(h + 1) * HEAD_DIM, :] = _attend_result(h, A_HEADS // A_KV_HEADS, l_sc, acc_sc)
    o_ref[...] = out_sc[...].T


def _dsa_prompt(raw, band_a, B, S):
    nq = S // TQ
    k_sel = min(A_TOPK, S // 4)
    assert S % TQ == 0 and TQ >= k_sel
    tile = lambda b, i: (b * nq + i, 0)
    seq = lambda b, i: (b, 0)
    kern = functools.partial(_dsa_kernel, k_sel=k_sel, pos_bits=(S - 1).bit_length())
    return pl.pallas_call(
        kern,
        grid=(B, nq),
        in_specs=[pl.BlockSpec((TQ, IDX_HEADS * IDX_DIM), tile), pl.BlockSpec((TQ, LANES), tile),
                  pl.BlockSpec((TQ, A_HEADS * HEAD_DIM), tile), pl.BlockSpec((S, IDX_DIM), seq),
                  pl.BlockSpec((S, 2 * A_KV_HEADS * HEAD_DIM), seq),
                  pl.BlockSpec(band_a.shape, lambda b, i: (0, 0, 0))],
        out_specs=pl.BlockSpec((TQ, A_HEADS * HEAD_DIM), tile),
        out_shape=jax.ShapeDtypeStruct((B * S, A_HEADS * HEAD_DIM), F32),
        scratch_shapes=[pltpu.VMEM((S, TQ), jnp.int32)] + _softmax_scratch(A_HEADS, A_KV_HEADS)
        + [pltpu.VMEM((A_HEADS * HEAD_DIM, TQ), F32)],
        compiler_params=pltpu.CompilerParams(dimension_semantics=('arbitrary', 'arbitrary'),
                                             vmem_limit_bytes=VMEM_LIMIT),
        name='dsa_prompt',
    )(raw['a_iq'], raw['misc'], raw['a_q'], raw['a_ik'], raw['a_kv'], band_a)


def _compress_kernel(x_ref, pe_ref, w1_ref, w2_ref, o_ref):
    x = (x_ref[...] + pe_ref[...]).astype(MXU)
    h = jax.nn.gelu(jnp.dot(x, w1_ref[...], preferred_element_type=F32))
    o_ref[...] = jnp.dot(h.astype(MXU), w2_ref[...], preferred_element_type=F32)


def _compress_weights(pe, w1, w2):
    eye_c = jnp.eye(2, dtype=F32)
    eye_g = jnp.eye(B_KV_HEADS, dtype=F32)
    w1_big = jnp.einsum('lcde,cC,gG->lcgdCGe', w1, eye_c, eye_g)
    w1_big = w1_big.reshape(NSA_BLOCK * 2 * B_KV_HEADS * HEAD_DIM, 2 * B_KV_HEADS * CMP_HIDDEN)
    w2_big = jnp.einsum('cef,cC,gG->cgeCGf', w2, eye_c, eye_g)
    w2_big = w2_big.reshape(2 * B_KV_HEADS * CMP_HIDDEN, 2 * B_KV_HEADS * HEAD_DIM)
    pe_flat = jnp.broadcast_to(pe[:, :, None, :], (NSA_BLOCK, 2, B_KV_HEADS, HEAD_DIM)).reshape(1, -1)
    return pe_flat, w1_big.astype(MXU), w2_big.astype(MXU)


def _compress(blocks2d, pe_flat, w1_big, w2_big, tm):
    n, width = blocks2d.shape
    fixed = lambda i: (0, 0)
    return pl.pallas_call(
        _compress_kernel,
        grid=(n // tm,),
        in_specs=[pl.BlockSpec((tm, width), lambda i: (i, 0)), pl.BlockSpec((1, width), fixed),
                  pl.BlockSpec(w1_big.shape, fixed), pl.BlockSpec(w2_big.shape, fixed)],
        out_specs=pl.BlockSpec((tm, w2_big.shape[1]), lambda i: (i, 0)),
        out_shape=jax.ShapeDtypeStruct((n, w2_big.shape[1]), F32),
        compiler_params=pltpu.CompilerParams(dimension_semantics=('arbitrary',), vmem_limit_bytes=VMEM_LIMIT),
        name='nsa_compress',
    )(blocks2d, pe_flat, w1_big, w2_big)


def _nsa_kernel(q_ref, misc_ref, cmp_ref, selkv_ref, winkv_ref, band_ref, biasc_ref, o_ref,
                sel_sc, m_sc, l_sc, acc_sc, out_sc, *, nb, n_sel):
    i = pl.program_id(1)
    t0 = i * TQ
    n_g = B_KV_HEADS
    rep = B_HEADS // n_g
    lane_t = t0 + lax.broadcasted_iota(jnp.int32, (1, TQ), 1)
    sub_iota = lax.broadcasted_iota(jnp.int32, (TQ, TQ), 0)
    lane_iota = lax.broadcasted_iota(jnp.int32, (TQ, TQ), 1)
    misc_t = misc_ref[...].T
    gate = lambda h, k: misc_t[MISC_GATE_OFF + 3 * h + k:MISC_GATE_OFF + 3 * h + k + 1, :]
    q = (q_ref[...] * HEAD_DIM ** -0.5).astype(MXU)
    q_heads = [q[:, h * HEAD_DIM:(h + 1) * HEAD_DIM] for h in range(B_HEADS)]
    q_groups = _group_queries(q, n_g)

    cmp = cmp_ref[...].astype(MXU)
    blk = lax.broadcasted_iota(jnp.int32, (nb, TQ), 0)
    visible = (blk + 1) * NSA_BLOCK - 1 <= lane_t
    cur = lax.shift_right_logical(lane_t, NSA_BLOCK.bit_length() - 1)
    forced = (blk == 0) | (blk == cur) | (blk == cur - 1)
    for g in range(n_g):
        k_c = cmp[:, g * HEAD_DIM:(g + 1) * HEAD_DIM]
        v_c = cmp[:, (n_g + g) * HEAD_DIM:(n_g + g + 1) * HEAD_DIM]
        importance = jnp.zeros((nb, TQ), F32)
        for r in range(rep):
            h = g * rep + r
            lc = lax.dot_general(k_c, q_heads[h], _NT, preferred_element_type=F32) + biasc_ref[h]
            lc = jnp.where(visible, lc, -jnp.inf)
            m = jnp.max(lc, axis=0, keepdims=True)
            m = jnp.where(m > -jnp.inf, m, 0.0)
            e = jnp.exp(lc - m)
            s = jnp.sum(e, axis=0, keepdims=True)
            p = e / jnp.where(s > 0, s, 1.0)
            importance = importance + p
            o_cmp = lax.dot_general(v_c, p.astype(MXU), _TN, preferred_element_type=F32)
            out_sc[h * HEAD_DIM:(h + 1) * HEAD_DIM, :] = gate(h, 0) * o_cmp
        score = jnp.where(forced, NSA_FORCE, importance)
        score = jnp.where(blk <= cur, score, -1.0)
        rank = jnp.zeros((nb, TQ), jnp.int32)
        for n in range(nb):
            row = score[n:n + 1, :]
            beats = (row > score) | ((row == score) & (blk > n))
            rank = rank + jnp.where(beats, 1, 0)
        sel_sc[g] = jnp.where(rank < n_sel, 0.0, NEG)

    def finish_branch(k):
        for h in range(B_HEADS):
            rows = slice(h * HEAD_DIM, (h + 1) * HEAD_DIM)
            out_sc[rows, :] = out_sc[rows, :] + gate(h, k) * _attend_result(h, rep, l_sc, acc_sc)

    _softmax_reset(m_sc, l_sc, acc_sc)
    blocks_per_chunk = TQ // NSA_BLOCK

    def sel_chunk(j, band_row0, causal):
        r0 = pl.multiple_of(j * TQ, TQ)
        kv_c = selkv_ref[pl.ds(r0, TQ), :].astype(MXU)

        def madd_of_group(g):
            rows = [sel_sc[g, pl.ds(j * blocks_per_chunk + b, 1), :] for b in range(blocks_per_chunk)]
            madd = jnp.concatenate([jnp.broadcast_to(row, (NSA_BLOCK, TQ)) for row in rows], axis=0)
            if causal:
                madd = jnp.where(sub_iota <= lane_iota, madd, NEG)
            return madd

        _attend_chunk(kv_c, q_groups, madd_of_group, band_ref, band_row0, m_sc, l_sc, acc_sc)

    def far_chunk(j, carry):
        sel_chunk(j, None, False)
        return carry

    lax.fori_loop(0, jnp.maximum(i - 1, 0), far_chunk, 0)

    @pl.when(i >= 1)
    def _():
        sel_chunk(i - 1, 0, False)

    sel_chunk(i, TQ, True)
    finish_branch(1)

    _softmax_reset(m_sc, l_sc, acc_sc)
    for back in range(WINDOW // TQ, -1, -1):
        dist = back * TQ + lane_iota - sub_iota
        madd = jnp.where((dist >= 0) & (dist < WINDOW), 0.0, NEG)
        band_row0 = {0: TQ, 1: 0}.get(back)

        def win_chunk(back=back, madd=madd, band_row0=band_row0):
            r0 = pl.multiple_of((i - back) * TQ, TQ)
            kv_c = winkv_ref[pl.ds(r0, TQ), :].astype(MXU)
            _attend_chunk(kv_c, q_groups, lambda g: madd, band_ref, band_row0, m_sc, l_sc, acc_sc)

        if back == 0:
            win_chunk()
        else:
            pl.when(i >= back)(win_chunk)
    finish_branch(2)
    o_ref[...] = out_sc[...].T


def _nsa_prompt(raw, cmp, band_b, bias_c, B, S):
    nq = S // TQ
    nb = S // NSA_BLOCK
    assert S % TQ == 0 and TQ % NSA_BLOCK == 0
    tile = lambda b, i: (b * nq + i, 0)
    seq = lambda b, i: (b, 0)
    kv_w = 2 * B_KV_HEADS * HEAD_DIM
    kern = functools.partial(_nsa_kernel, nb=nb, n_sel=min(NSA_TOPN, nb))
    return pl.pallas_call(
        kern,
        grid=(B, nq),
        in_specs=[pl.BlockSpec((TQ, B_HEADS * HEAD_DIM), tile), pl.BlockSpec((TQ, LANES), tile),
                  pl.BlockSpec((nb, kv_w), seq), pl.BlockSpec((S, kv_w), seq), pl.BlockSpec((S, kv_w), seq),
                  pl.BlockSpec(band_b.shape, lambda b, i: (0, 0, 0)),
                  pl.BlockSpec((B_HEADS, nb, TQ), lambda b, i: (0, 0, i))],
        out_specs=pl.BlockSpec((TQ, B_HEADS * HEAD_DIM), tile),
        out_shape=jax.ShapeDtypeStruct((B * S, B_HEADS * HEAD_DIM), F32),
        scratch_shapes=[pltpu.VMEM((B_KV_HEADS, nb, TQ), F32)] + _softmax_scratch(B_HEADS, B_KV_HEADS)
        + [pltpu.VMEM((B_HEADS * HEAD_DIM, TQ), F32)],
        compiler_params=pltpu.CompilerParams(dimension_semantics=('arbitrary', 'arbitrary'),
                                             vmem_limit_bytes=VMEM_LIMIT),
        name='nsa_prompt',
    )(raw['b_q'], raw['misc'], cmp, raw['b_sel_kv'], raw['b_win_kv'], band_b, bias_c)


def _mix_prompt(pr, rel_table, cmp_pe, cmp_w1, cmp_w2):
    B, S = pr['a_q'].shape[:2]
    raw = pr['raw']
    table_a, table_b = rel_table[:, :A_HEADS], rel_table[:, A_HEADS:]
    pe_flat, w1_big, w2_big = _compress_weights(cmp_pe, cmp_w1, cmp_w2)
    n_blocks = B * S // NSA_BLOCK
    cmp = _compress(raw['b_cmp_kv'].reshape(n_blocks, -1), pe_flat, w1_big, w2_big, tm=min(128, n_blocks))
    o_a = _dsa_prompt(raw, _band_bias(table_a, A_KV_HEADS), B, S)
    o_b = _nsa_prompt(raw, cmp, _band_bias(table_b, B_KV_HEADS), _block_bias(table_b, S), B, S)
    wb = min(WINDOW, S)
    kv_tail = (2, B_KV_HEADS, HEAD_DIM)

    def rows(name, tail, first=0):
        part = raw['state_t'][name][:, :, first:]
        return jnp.moveaxis(part.reshape((B,) + tail + (S - first,)), -1, 1)

    new_state = (rows('a_kv', kv_tail), rows('a_ik', (IDX_DIM,)), rows('b_cmp_kv', kv_tail),
                 rows('b_sel_kv', kv_tail), rows('b_win_kv', kv_tail, S - wb))
    return o_a, o_b, new_state


PAGES_PER_STEP = 16
KEY_PAD = 128


def _paged_specs(width):
    def spec(j):
        return pl.BlockSpec((1, width, PAGE_SIZE), lambda s, p, pt: (pt[s, p * PAGES_PER_STEP + j], 0, 0))
    return [spec(j) for j in range(PAGES_PER_STEP)]


def _feature_major(pool, width):
    return jnp.swapaxes(pool.reshape(pool.shape[0], PAGE_SIZE, width), 1, 2)


def _stage_pages(page_refs, dst_sc):
    p = pl.program_id(1)
    for j, ref in enumerate(page_refs):
        r0 = pl.multiple_of((p * PAGES_PER_STEP + j) * PAGE_SIZE, PAGE_SIZE)
        dst_sc[pl.ds(r0, PAGE_SIZE), :] = ref[0].T


def _stage_new_rows(new_ref, dst_sc, past):
    t = new_ref.shape[0]
    dst_sc[past:past + t, :] = new_ref[...]
    dst_sc[past + t:, :] = jnp.zeros((dst_sc.shape[0] - past - t, dst_sc.shape[1]), F32)


def _rows_softmax_attend(q_rows, k, v, bias, madd):
    logits = lax.dot_general(q_rows, k, _NT, preferred_element_type=F32) + bias + madd
    m = jnp.max(logits, axis=1, keepdims=True)
    e = jnp.exp(logits - m)
    p = e / jnp.sum(e, axis=1, keepdims=True)
    return jnp.dot(p.astype(MXU), v, preferred_element_type=F32)


def _stack_heads(x, heads, width):
    return jnp.concatenate([x[:, h * width:(h + 1) * width] for h in heads], axis=0)


def _dsa_sample_kernel(pt_ref, *refs, past, k_sel):
    n = PAGES_PER_STEP
    idx_pages, kv_pages = refs[:n], refs[n:2 * n]
    ikn_ref, kvn_ref, iq_ref, misc_ref, q_ref, bias_ref, o_ref, ik_sc, kv_sc = refs[2 * n:]
    _stage_pages(idx_pages, ik_sc)
    _stage_pages(kv_pages, kv_sc)

    @pl.when(pl.program_id(1) == pl.num_programs(1) - 1)
    def _():
        T = iq_ref.shape[0]
        L = ik_sc.shape[0]
        _stage_new_rows(ikn_ref, ik_sc, past)
        _stage_new_rows(kvn_ref, kv_sc, past)
        pos = lax.broadcasted_iota(jnp.int32, (T, L), 1)
        valid = pos <= past + lax.broadcasted_iota(jnp.int32, (T, L), 0)
        misc = misc_ref[...]
        iq = _stack_heads(iq_ref[...].astype(MXU), range(IDX_HEADS), IDX_DIM)
        rel = lax.dot_general(iq, ik_sc[...].astype(MXU), _NT, preferred_element_type=F32)
        sc = jnp.zeros((T, L), F32)
        for h in range(IDX_HEADS):
            sc = sc + jnp.maximum(rel[h * T:(h + 1) * T] * IDX_DIM ** -0.5, 0.0) * misc[:, h:h + 1]
        key = _ordered_keys(jnp.where(valid, sc, -jnp.inf))

        def count(hit):
            return jnp.sum(jnp.where(hit, 1, 0), axis=1, keepdims=True)

        v = jnp.full((T, 1), INT_MIN, jnp.int32)
        v = jnp.where(count(key >= 0) >= k_sel, 0, v)

        def value_bit(it, v):
            cand = v | jnp.left_shift(jnp.int32(1), 30 - it)
            return jnp.where(count(key >= cand) >= k_sel, cand, v)

        v = lax.fori_loop(0, 31, value_bit, v)
        need = k_sel - count(key > v)
        pos_bits = (L - 1).bit_length()

        def tie_search():
            def pos_bit(it, jm):
                cand = jm | jnp.left_shift(jnp.int32(1), pos_bits - 1 - it)
                return jnp.where(count((key == v) & (pos < cand)) < need, cand, jm)

            return lax.fori_loop(0, pos_bits, pos_bit, jnp.zeros((T, 1), jnp.int32))

        j_max = lax.cond(jnp.max(count(key >= v)) > k_sel, tie_search,
                         lambda: jnp.full((T, 1), 2 ** 30, jnp.int32))
        sel = ((key > v) | ((key == v) & (pos <= j_max))) & valid
        madd = jnp.where(sel, 0.0, NEG)
        rep = A_HEADS // A_KV_HEADS
        madd = jnp.concatenate([madd] * rep, axis=0)
        q = (q_ref[...] * HEAD_DIM ** -0.5).astype(MXU)
        kv = kv_sc[...].astype(MXU)
        for g in range(A_KV_HEADS):
            q_rows = _stack_heads(q, range(g * rep, (g + 1) * rep), HEAD_DIM)
            o = _rows_softmax_attend(q_rows, kv[:, g * HEAD_DIM:(g + 1) * HEAD_DIM],
                                     kv[:, (A_KV_HEADS + g) * HEAD_DIM:(A_KV_HEADS + g + 1) * HEAD_DIM],
                                     bias_ref[g], madd)
            for r in range(rep):
                h = g * rep + r
                o_ref[:, h * HEAD_DIM:(h + 1) * HEAD_DIM] = o[r * T:(r + 1) * T, :]


def _sample_bias(table, T, past, n_keys, key_pos0, n_groups):
    d = (past + np.arange(T))[:, None] - (key_pos0 + np.arange(n_keys))[None, :]
    b = _rel_bias_at(table, d)
    b = jnp.moveaxis(b, -1, 0)
    return b.reshape(n_groups, -1, n_keys)


def _seq_rows(width, T):
    return pl.BlockSpec((T, width), lambda s, p, pt: (s, 0))


def _dsa_sample(raw, pool_idx, pool_kv, page_table, table_a, DB, T):
    n_pages = page_table.shape[1]
    past = n_pages * PAGE_SIZE
    L = past + KEY_PAD
    assert n_pages % PAGES_PER_STEP == 0 and T <= KEY_PAD
    kv_w = 2 * A_KV_HEADS * HEAD_DIM
    bias = _sample_bias(table_a, T, past, L, 0, A_KV_HEADS)
    fixed3 = lambda s, p, pt: (0, 0, 0)
    grid_spec = pltpu.PrefetchScalarGridSpec(
        num_scalar_prefetch=1,
        grid=(DB, n_pages // PAGES_PER_STEP),
        in_specs=_paged_specs(IDX_DIM) + _paged_specs(kv_w)
        + [_seq_rows(IDX_DIM, T), _seq_rows(kv_w, T), _seq_rows(IDX_HEADS * IDX_DIM, T), _seq_rows(LANES, T),
           _seq_rows(A_HEADS * HEAD_DIM, T), pl.BlockSpec(bias.shape, fixed3)],
        out_specs=_seq_rows(A_HEADS * HEAD_DIM, T),
        scratch_shapes=[pltpu.VMEM((L, IDX_DIM), F32), pltpu.VMEM((L, kv_w), F32)],
    )
    kern = functools.partial(_dsa_sample_kernel, past=past, k_sel=min(A_TOPK, (past + T) // 4))
    return pl.pallas_call(
        kern, grid_spec=grid_spec,
        out_shape=jax.ShapeDtypeStruct((DB * T, A_HEADS * HEAD_DIM), F32),
        compiler_params=pltpu.CompilerParams(dimension_semantics=('arbitrary', 'arbitrary'),
                                             vmem_limit_bytes=VMEM_LIMIT),
        name='dsa_sample',
    )(page_table, *([pool_idx] * PAGES_PER_STEP), *([pool_kv] * PAGES_PER_STEP),
      raw['a_ik'], raw['a_kv'], raw['a_iq'], raw['misc'], raw['a_q'], bias)


def _compress_sample_kernel(pt_ref, *refs, past):
    n = PAGES_PER_STEP
    pages = refs[:n]
    new_ref, pe_ref, w1_ref, w2_ref, o_ref, xk_sc, xv_sc = refs[n:]
    halves = (xk_sc, xv_sc)
    p = pl.program_id(1)
    for j, ref in enumerate(pages):
        r0 = pl.multiple_of((p * PAGES_PER_STEP + j) * PAGE_SIZE, PAGE_SIZE)
        for c, x_sc in enumerate(halves):
            x_sc[pl.ds(r0, PAGE_SIZE), :] = ref[0, c * LANES:(c + 1) * LANES, :].T

    @pl.when(p == pl.num_programs(1) - 1)
    def _():
        nbp = o_ref.shape[1]
        t = new_ref.shape[0]
        for c, x_sc in enumerate(halves):
            x_sc[past:past + t, :] = new_ref[:, c * LANES:(c + 1) * LANES]
            x_sc[past + t:, :] = jnp.zeros((x_sc.shape[0] - past - t, LANES), F32)

        def body(l, accs):
            out = []
            for c, (x_sc, acc) in enumerate(zip(halves, accs)):
                xl = x_sc[pl.ds(l, nbp, stride=NSA_BLOCK), :] + pe_ref[c, pl.ds(l, 1), :]
                out.append(acc + jnp.dot(xl.astype(MXU), w1_ref[l, c], preferred_element_type=F32))
            return tuple(out)

        zero = jnp.zeros((nbp, LANES), F32)
        h = jnp.concatenate(lax.fori_loop(0, NSA_BLOCK, body, (zero, zero)), axis=1)
        o_ref[0] = jnp.dot(jax.nn.gelu(h).astype(MXU), w2_ref[...], preferred_element_type=F32)


def _padded_blocks(past, T):
    nb = -(-(past + T) // NSA_BLOCK)
    return nb, -(-nb // 8) * 8


def _compress_sample(raw, pool_cmp, page_table, pe_flat, w1_big, w2_big, DB, T):
    n_pages = page_table.shape[1]
    past = n_pages * PAGE_SIZE
    _, nbp = _padded_blocks(past, T)
    kv_w = 2 * B_KV_HEADS * HEAD_DIM
    assert kv_w == 2 * LANES and w1_big.shape[1] == 2 * LANES
    w1_l = w1_big.reshape(NSA_BLOCK, 2, LANES, 2, LANES)
    w1_halves = jnp.stack([w1_l[:, 0, :, 0, :], w1_l[:, 1, :, 1, :]], axis=1)
    fixed2 = lambda s, p, pt: (0, 0)
    grid_spec = pltpu.PrefetchScalarGridSpec(
        num_scalar_prefetch=1,
        grid=(DB, n_pages // PAGES_PER_STEP),
        in_specs=_paged_specs(kv_w) + [_seq_rows(kv_w, T),
                                       pl.BlockSpec((2, NSA_BLOCK, LANES), lambda s, p, pt: (0, 0, 0)),
                                       pl.BlockSpec(w1_halves.shape, lambda s, p, pt: (0, 0, 0, 0)),
                                       pl.BlockSpec(w2_big.shape, fixed2)],
        out_specs=pl.BlockSpec((1, nbp, kv_w), lambda s, p, pt: (s, 0, 0)),
        scratch_shapes=[pltpu.VMEM((nbp * NSA_BLOCK, LANES), F32), pltpu.VMEM((nbp * NSA_BLOCK, LANES), F32)],
    )
    return pl.pallas_call(
        functools.partial(_compress_sample_kernel, past=past), grid_spec=grid_spec,
        out_shape=jax.ShapeDtypeStruct((DB, nbp, kv_w), F32),
        compiler_params=pltpu.CompilerParams(dimension_semantics=('arbitrary', 'arbitrary'),
                                             vmem_limit_bytes=VMEM_LIMIT),
        name='nsa_compress_sample',
    )(page_table, *([pool_cmp] * PAGES_PER_STEP), raw['b_cmp_kv'],
      pe_flat.reshape(NSA_BLOCK, 2, LANES).swapaxes(0, 1), w1_halves, w2_big)


def _nsa_sample_kernel(pt_ref, *refs, past, nb, n_sel):
    n = PAGES_PER_STEP
    pages = refs[:n]
    (seln_ref, winn_ref, q_ref, misc_ref, cmp_ref, win_ref, expand_ref, bias_c_ref, bias_s_ref, bias_w_ref,
     o_ref, kv_sc, win_sc) = refs[n:]
    _stage_pages(pages, kv_sc)

    @pl.when(pl.program_id(1) == pl.num_programs(1) - 1)
    def _():
        T = q_ref.shape[0]
        L = kv_sc.shape[0]
        n_g = B_KV_HEADS
        rep = B_HEADS // n_g
        wb = win_ref.shape[1]
        _stage_new_rows(seln_ref, kv_sc, past)
        win_sc[0:wb, :] = win_ref[0]
        _stage_new_rows(winn_ref, win_sc, wb)
        misc = misc_ref[...]
        gate = lambda h, k: misc[:, MISC_GATE_OFF + 3 * h + k:MISC_GATE_OFF + 3 * h + k + 1]
        q = (q_ref[...] * HEAD_DIM ** -0.5).astype(MXU)
        cmp = cmp_ref[0].astype(MXU)
        nbp = cmp.shape[0]
        t_col = lax.broadcasted_iota(jnp.int32, (T, 1), 0)
        q_pos = past + t_col
        blk = lax.broadcasted_iota(jnp.int32, (T, nbp), 1)
        cur = lax.shift_right_logical(q_pos, NSA_BLOCK.bit_length() - 1)
        visible = (blk + 1) * NSA_BLOCK - 1 <= q_pos
        visible_rows = jnp.concatenate([visible] * rep, axis=0)
        forced = (blk == 0) | (blk == cur) | (blk == cur - 1)
        pos = lax.broadcasted_iota(jnp.int32, (T, L), 1)
        causal = pos <= q_pos
        kv = kv_sc[...].astype(MXU)
        win = win_sc[...].astype(MXU)
        wpos = lax.broadcasted_iota(jnp.int32, (T, win.shape[0]), 1)
        dw = wb + t_col - wpos
        madd_w = jnp.where((dw >= 0) & (dw < WINDOW), 0.0, NEG)
        madd_w = jnp.concatenate([madd_w] * rep, axis=0)
        outs = []
        for g in range(n_g):
            heads = range(g * rep, (g + 1) * rep)
            ks = slice(g * HEAD_DIM, (g + 1) * HEAD_DIM)
            vs = slice((n_g + g) * HEAD_DIM, (n_g + g + 1) * HEAD_DIM)
            q_rows = _stack_heads(q, heads, HEAD_DIM)
            lc = lax.dot_general(q_rows, cmp[:, ks], _NT, preferred_element_type=F32) + bias_c_ref[g]
            lc = jnp.where(visible_rows, lc, -jnp.inf)
            m = jnp.max(lc, axis=1, keepdims=True)
            m = jnp.where(m > -jnp.inf, m, 0.0)
            e = jnp.exp(lc - m)
            s = jnp.sum(e, axis=1, keepdims=True)
            p = e / jnp.where(s > 0, s, 1.0)
            o_cmp = jnp.dot(p.astype(MXU), cmp[:, vs], preferred_element_type=F32)
            importance = p[0:T]
            for r in range(1, rep):
                importance = importance + p[r * T:(r + 1) * T]
            score = jnp.where(forced, NSA_FORCE, importance)
            score = jnp.where(blk <= cur, score, -1.0)
            rank = jnp.zeros((T, nbp), jnp.int32)
            for b in range(nb):
                col = score[:, b:b + 1]
                rank = rank + jnp.where((col > score) | ((col == score) & (blk > b)), 1, 0)
            picked = jnp.where(rank < n_sel, 1.0, 0.0).astype(MXU)
            on_keys = jnp.dot(picked, expand_ref[...], preferred_element_type=F32)
            madd_s = jnp.where((on_keys > 0.5) & causal, 0.0, NEG)
            madd_s = jnp.concatenate([madd_s] * rep, axis=0)
            o_slc = _rows_softmax_attend(q_rows, kv[:, ks], kv[:, vs], bias_s_ref[g], madd_s)
            o_win = _rows_softmax_attend(q_rows, win[:, ks], win[:, vs], bias_w_ref[g], madd_w)
            for r, h in enumerate(heads):
                rows = slice(r * T, (r + 1) * T)
                outs.append(gate(h, 0) * o_cmp[rows] + gate(h, 1) * o_slc[rows] + gate(h, 2) * o_win[rows])
        o_ref[...] = jnp.concatenate(outs, axis=1)


def _nsa_sample(raw, cmp, pool_sel, win_state, page_table, table_b, DB, T):
    n_pages = page_table.shape[1]
    past = n_pages * PAGE_SIZE
    L = past + KEY_PAD
    nb, nbp = _padded_blocks(past, T)
    wb = win_state.shape[1]
    lw = wb + KEY_PAD
    kv_w = 2 * B_KV_HEADS * HEAD_DIM
    n_g = B_KV_HEADS
    expand = jnp.asarray(np.arange(L)[None, :] // NSA_BLOCK == np.arange(nbp)[:, None], dtype=MXU)
    blk_end = (np.arange(nbp) + 1) * NSA_BLOCK - 1
    d_c = (past + np.arange(T))[:, None] - blk_end[None, :]
    bias_c = jnp.moveaxis(_rel_bias_at(table_b, d_c), -1, 0).reshape(n_g, -1, nbp)
    bias_s = _sample_bias(table_b, T, past, L, 0, n_g)
    bias_w = _sample_bias(table_b, T, past, lw, past - wb, n_g)
    fixed2 = lambda s, p, pt: (0, 0)
    fixed3 = lambda s, p, pt: (0, 0, 0)
    per_seq3 = lambda s, p, pt: (s, 0, 0)
    grid_spec = pltpu.PrefetchScalarGridSpec(
        num_scalar_prefetch=1,
        grid=(DB, n_pages // PAGES_PER_STEP),
        in_specs=_paged_specs(kv_w)
        + [_seq_rows(kv_w, T), _seq_rows(kv_w, T), _seq_rows(B_HEADS * HEAD_DIM, T), _seq_rows(LANES, T),
           pl.BlockSpec((1, nbp, kv_w), per_seq3), pl.BlockSpec((1, wb, kv_w), per_seq3),
           pl.BlockSpec(expand.shape, fixed2), pl.BlockSpec(bias_c.shape, fixed3),
           pl.BlockSpec(bias_s.shape, fixed3), pl.BlockSpec(bias_w.shape, fixed3)],
        out_specs=_seq_rows(B_HEADS * HEAD_DIM, T),
        scratch_shapes=[pltpu.VMEM((L, kv_w), F32), pltpu.VMEM((lw, kv_w), F32)],
    )
    kern = functools.partial(_nsa_sample_kernel, past=past, nb=nb, n_sel=min(NSA_TOPN, nb))
    return pl.pallas_call(
        kern, grid_spec=grid_spec,
        out_shape=jax.ShapeDtypeStruct((DB * T, B_HEADS * HEAD_DIM), F32),
        compiler_params=pltpu.CompilerParams(dimension_semantics=('arbitrary', 'arbitrary'),
                                             vmem_limit_bytes=VMEM_LIMIT),
        name='nsa_sample',
    )(page_table, *([pool_sel] * PAGES_PER_STEP), raw['b_sel_kv'], raw['b_win_kv'], raw['b_q'], raw['misc'],
      cmp, win_state, expand, bias_c, bias_s, bias_w)


def _mix_sample(pr, cache_a_kv, cache_a_idx, cache_b_cmp_kv, cache_b_sel_kv, state_b_win_kv, page_table,
                rel_table, cmp_pe, cmp_w1, cmp_w2):
    DB, T = pr['a_q'].shape[:2]
    raw = pr['raw']
    kv_w = 2 * A_KV_HEADS * HEAD_DIM
    table_a, table_b = rel_table[:, :A_HEADS], rel_table[:, A_HEADS:]
    pe_flat, w1_big, w2_big = _compress_weights(cmp_pe, cmp_w1, cmp_w2)
    win_state = state_b_win_kv.reshape(DB, -1, kv_w)
    o_a = _dsa_sample(raw, _feature_major(cache_a_idx, IDX_DIM), _feature_major(cache_a_kv, kv_w), page_table,
                      table_a, DB, T)
    cmp = _compress_sample(raw, _feature_major(cache_b_cmp_kv, kv_w), page_table, pe_flat, w1_big, w2_big, DB, T)
    o_b = _nsa_sample(raw, cmp, _feature_major(cache_b_sel_kv, kv_w), win_state, page_table, table_b, DB, T)
    win_all = jnp.concatenate([state_b_win_kv, pr['b_win_kv']], axis=1)
    new_state = (pr['a_kv'], pr['a_ik'], pr['b_cmp_kv'], pr['b_sel_kv'], win_all[:, T:])
    return o_a, o_b, new_state


PEER_SLOTS = PEER_HEADS * PEER_TOPK
ROUTE_TOKENS = 256
EXPERT_TOKENS = 256
GATHER_DEPTH = 8
D_SUB = D_MODEL // LANES
EXPERT_ROWS = 2 * D_SUB


def _take_topk(cur, pos_iota, k, payload=None):
    n = cur.shape[0]
    vals, picks = [], []
    for _ in range(k):
        m = jnp.max(cur, axis=0, keepdims=True)
        pos = jnp.min(jnp.where(cur == m, pos_iota, n), axis=0, keepdims=True)
        hit = pos_iota == pos
        vals.append(m)
        picks.append(pos if payload is None else jnp.sum(jnp.where(hit, payload, 0), axis=0, keepdims=True))
        cur = jnp.where(hit, -jnp.inf, cur)
    return vals, picks


def _peer_route_kernel(h_ref, wpq_ref, keys_ref, ids_ref, g_ref):
    tn = h_ref.shape[0]
    half = PEER_QDIM // 2
    q = jnp.dot(h_ref[...].astype(MXU), wpq_ref[...], preferred_element_type=F32).astype(MXU)
    key_iota = lax.broadcasted_iota(jnp.int32, (PEER_NKEYS, tn), 0)
    pairs = [(a, b) for a in range(PEER_TOPK) for b in range(PEER_TOPK // (a + 1))]
    n_pairs = -(-len(pairs) // 8) * 8
    pair_iota = lax.broadcasted_iota(jnp.int32, (n_pairs, tn), 0)
    pad_v = [jnp.full((n_pairs - len(pairs), tn), -jnp.inf, F32)]
    pad_i = [jnp.zeros((n_pairs - len(pairs), tn), jnp.int32)]
    id_rows, g_rows = [], []
    for hd in range(PEER_HEADS):
        vals, idxs = [], []
        for p in range(2):
            c0 = (hd * 2 + p) * half
            s = lax.dot_general(keys_ref[hd * 2 + p], q[:, c0:c0 + half], _NT, preferred_element_type=F32)
            v_rows, i_rows = _take_topk(s, key_iota, PEER_TOPK)
            vals.append(v_rows)
            idxs.append(i_rows)
        cand = jnp.concatenate([vals[0][a] + vals[1][b] for a, b in pairs] + pad_v, axis=0)
        expert = jnp.concatenate([idxs[0][a] * PEER_NKEYS + idxs[1][b] for a, b in pairs] + pad_i, axis=0)
        cv, ce = _take_topk(cand, pair_iota, PEER_TOPK, payload=expert)
        cv = jnp.concatenate(cv, axis=0)
        e = jnp.exp(cv - cv[0:1, :])
        g_rows.append(e / jnp.sum(e, axis=0, keepdims=True))
        id_rows.append(jnp.concatenate(ce, axis=0))
    ids_ref[...] = jnp.concatenate(id_rows, axis=0).astype(F32).T.astype(jnp.int32)
    g_ref[...] = jnp.concatenate(g_rows, axis=0).T


def _peer_route(h2d, w_pq, sub_keys):
    n = h2d.shape[0]
    tn = ROUTE_TOKENS
    wpq = w_pq.reshape(D_MODEL, PEER_HEADS * PEER_QDIM).astype(MXU)
    keys = sub_keys.reshape(PEER_HEADS * 2, PEER_NKEYS, PEER_QDIM // 2).astype(MXU)
    return pl.pallas_call(
        _peer_route_kernel,
        grid=(n // tn,),
        in_specs=[pl.BlockSpec((tn, D_MODEL), lambda i: (i, 0)), pl.BlockSpec(wpq.shape, lambda i: (0, 0)),
                  pl.BlockSpec(keys.shape, lambda i: (0, 0, 0))],
        out_specs=[pl.BlockSpec((tn, PEER_SLOTS), lambda i: (i, 0)), pl.BlockSpec((tn, PEER_SLOTS), lambda i: (i, 0))],
        out_shape=[jax.ShapeDtypeStruct((n, PEER_SLOTS), jnp.int32), jax.ShapeDtypeStruct((n, PEER_SLOTS), F32)],
        compiler_params=pltpu.CompilerParams(dimension_semantics=('arbitrary',), vmem_limit_bytes=VMEM_LIMIT),
        name='peer_route',
    )(h2d, wpq, keys)


def _peer_expert_kernel(ids_ref, next_ids_ref, g_ref, h_ref, lng_ref, lnb_ref, uv_hbm, y_ref, buf, sem, gt_sc,
                        out_sc):
    tb = h_ref.shape[0]
    ahead = GATHER_DEPTH - 1
    per_piece = PEER_SLOTS // EXPERT_ROWS

    step = pl.program_id(0)
    last_step = pl.num_programs(0) - 1

    def row_copy(ids, t, slot, k):
        src = pl.ds(pl.multiple_of(ids[t, k] * EXPERT_ROWS, EXPERT_ROWS), EXPERT_ROWS)
        return pltpu.make_async_copy(uv_hbm.at[src, :], buf.at[slot, :, k, :], sem.at[slot])

    def issue(ids, t, slot, k0=0, k1=PEER_SLOTS):
        for k in range(k0, k1):
            row_copy(ids, t, slot, k).start(priority=k % 2)

    def consume(t, nxt=None):
        slot = t % GATHER_DEPTH

        def issue_piece(i):
            if nxt is not None:
                issue(*nxt, i * per_piece, (i + 1) * per_piece)

        for k in range(PEER_SLOTS):
            row_copy(ids_ref, t, slot, k).wait()
        x = h_ref[t]
        part = jnp.zeros((PEER_SLOTS, LANES), F32)
        for s in range(D_SUB):
            part = part + buf[slot, s] * x[s:s + 1, :]
            issue_piece(s)
        act = jnp.sum(part, axis=1, keepdims=True)
        g_col = jnp.sum(jnp.where(lane == t, gt_sc[...], 0.0), axis=1, keepdims=True)
        coef = jnp.broadcast_to(g_col * jax.nn.gelu(act), (PEER_SLOTS, LANES))
        rows = []
        for s in range(D_SUB):
            rows.append(jnp.sum(buf[slot, D_SUB + s] * coef, axis=0, keepdims=True))
            issue_piece(D_SUB + s)
        out_sc[t] = jnp.concatenate(rows, axis=0)

    gt_sc[...] = g_ref[...].T
    lane = lax.broadcasted_iota(jnp.int32, (PEER_SLOTS, tb), 1)

    def prologue(t, carry):
        issue(ids_ref, t, t)
        return carry

    def steady(t, carry):
        consume(t, nxt=(ids_ref, t + ahead, (t + ahead) % GATHER_DEPTH))
        return carry

    def handover(t, carry):
        consume(t, nxt=(next_ids_ref, t + ahead - tb, (t + ahead) % GATHER_DEPTH))
        return carry

    def drain(t, carry):
        consume(t)
        return carry

    @pl.when(step == 0)
    def _():
        lax.fori_loop(0, ahead, prologue, 0)

    lax.fori_loop(0, tb - ahead, steady, 0)

    @pl.when(step < last_step)
    def _():
        lax.fori_loop(tb - ahead, tb, handover, 0)

    @pl.when(step == last_step)
    def _():
        lax.fori_loop(tb - ahead, tb, drain, 0)
    z = DN_ALPHA * h_ref[...] + out_sc[...]
    mean = lambda a: jnp.sum(jnp.sum(a, axis=2, keepdims=True), axis=1, keepdims=True) * (1.0 / D_MODEL)
    zc = z - mean(z)
    y_ref[...] = zc * lax.rsqrt(mean(zc * zc) + LN_EPS) * lng_ref[...] + lnb_ref[...]


def _peer_experts(h2d, ids, g, uv_rows, ln_g, ln_b):
    n = h2d.shape[0]
    tb = EXPERT_TOKENS
    steps = n // tb
    assert n % tb == 0 and tb % GATHER_DEPTH == 0
    row = lambda i: (i, 0)
    next_row = lambda i: (jnp.minimum(i + 1, steps - 1), 0)
    tok = lambda i: (i, 0, 0)
    y = pl.pallas_call(
        _peer_expert_kernel,
        grid=(steps,),
        in_specs=[pl.BlockSpec((tb, PEER_SLOTS), row, memory_space=pltpu.SMEM),
                  pl.BlockSpec((tb, PEER_SLOTS), next_row, memory_space=pltpu.SMEM),
                  pl.BlockSpec((tb, PEER_SLOTS), row),
                  pl.BlockSpec((tb, D_SUB, LANES), tok), pl.BlockSpec((D_SUB, LANES), lambda i: (0, 0)),
                  pl.BlockSpec((D_SUB, LANES), lambda i: (0, 0)), pl.BlockSpec(memory_space=pl.ANY)],
        out_specs=pl.BlockSpec((tb, D_SUB, LANES), tok),
        out_shape=jax.ShapeDtypeStruct((n, D_SUB, LANES), F32),
        scratch_shapes=[pltpu.VMEM((GATHER_DEPTH, EXPERT_ROWS, PEER_SLOTS, LANES), F32),
                        pltpu.SemaphoreType.DMA((GATHER_DEPTH,)), pltpu.VMEM((PEER_SLOTS, tb), F32),
                        pltpu.VMEM((tb, D_SUB, LANES), F32)],
        compiler_params=pltpu.CompilerParams(dimension_semantics=('arbitrary',), vmem_limit_bytes=VMEM_LIMIT),
        name='peer_experts',
    )(ids, ids, g, h2d.reshape(n, D_SUB, LANES), ln_g.reshape(D_SUB, LANES), ln_b.reshape(D_SUB, LANES), uv_rows)
    return y.reshape(n, D_MODEL)


def _group_step(x, mixer, ws, w_up_a, w_up_b, w_out, ln1_g, ln1_b, w_pq, sub_keys, uv, ln2_g, ln2_b, tm,
                feature_major_state=False):
    B, T, _ = x.shape
    x2d = x.reshape(B * T, D_MODEL)
    pr = _unpack_proj(_project(x2d, ws, tm, seq_len=T if feature_major_state else None), B, T)
    o_a, o_b, new_state = mixer(pr)
    h = _merge(x2d, o_a.reshape(B * T, -1), o_b.reshape(B * T, -1), pr['merge_gate'],
               w_up_a, w_up_b, w_out, ln1_g, ln1_b, tm)
    ids, g = _peer_route(h, w_pq, sub_keys)
    y = _peer_experts(h, ids, g, uv, ln2_g, ln2_b)
    return y.reshape(B, T, D_MODEL), new_state


def kernel(x_prompt, x_sample, cache_a_kv, cache_a_idx, cache_b_cmp_kv, cache_b_sel_kv, state_b_win_kv, page_table, w_in, rel_bias_table, cmp_pe, cmp_w1, cmp_w2, w_up_a, w_up_b, w_out, ln1_g, ln1_b, w_pq, peer_sub_keys, peer_u, peer_v, ln2_g, ln2_b):
    l = 0
    ws = _split_w_in(w_in[l])
    uv = jnp.concatenate([peer_u[l], peer_v[l]], axis=1).reshape(-1, LANES)
    tail = (w_up_a[l], w_up_b[l], w_out[l], ln1_g[l], ln1_b[l], w_pq[l], peer_sub_keys[l], uv, ln2_g[l], ln2_b[l])
    mix_p = functools.partial(_mix_prompt, rel_table=rel_bias_table, cmp_pe=cmp_pe[l], cmp_w1=cmp_w1[l],
                              cmp_w2=cmp_w2[l])
    y_p, st_p = _group_step(x_prompt, mix_p, ws, *tail, tm=256, feature_major_state=True)
    mix_s = functools.partial(_mix_sample, cache_a_kv=cache_a_kv[l], cache_a_idx=cache_a_idx[l],
                              cache_b_cmp_kv=cache_b_cmp_kv[l], cache_b_sel_kv=cache_b_sel_kv[l],
                              state_b_win_kv=state_b_win_kv[l], page_table=page_table, rel_table=rel_bias_table,
                              cmp_pe=cmp_pe[l], cmp_w1=cmp_w1[l], cmp_w2=cmp_w2[l])
    y_s, st_s = _group_step(x_sample, mix_s, ws, *tail, tm=256)
    return (y_p, y_s) + tuple(s[None] for s in st_p) + tuple(s[None] for s in st_s)
```

```python
import functools
import math

import jax
import jax.numpy as jnp
import numpy as np
from jax import lax
from jax.experimental import pallas as pl
from jax.experimental.pallas import tpu as pltpu

D_MODEL = 1024
DEPTH = 1
PAGE_SIZE = 128
HEAD_DIM = 64
A_HEADS = 8
A_KV_HEADS = 2
IDX_HEADS = 4
IDX_DIM = 64
A_TOPK = 256
B_HEADS = 8
B_KV_HEADS = 2
NSA_BLOCK = 64
NSA_TOPN = 16
NSA_FORCE = 8.0
CMP_HIDDEN = 64
WINDOW = 512
REL_BUCKETS = 32
REL_MAX_EXACT = 16
REL_MAX_DIST = 128
PEER_HEADS = 8
PEER_NKEYS = 128
PEER_QDIM = 256
PEER_TOPK = 16
Q_BLOCK = 32
TOKEN_BLOCK = 256
LN_EPS = 1e-5
DN_ALPHA = (2 * DEPTH) ** 0.25

PROJ_LAYOUT = (
    ('a_q', A_HEADS * HEAD_DIM),
    ('a_kv', 2 * A_KV_HEADS * HEAD_DIM),
    ('a_iq', IDX_HEADS * IDX_DIM),
    ('a_iw', IDX_HEADS),
    ('a_ik', IDX_DIM),
    ('b_q', B_HEADS * HEAD_DIM),
    ('b_cmp_kv', 2 * B_KV_HEADS * HEAD_DIM),
    ('b_sel_kv', 2 * B_KV_HEADS * HEAD_DIM),
    ('b_win_kv', 2 * B_KV_HEADS * HEAD_DIM),
    ('b_gate', B_HEADS * 3),
    ('merge_gate', 2 * D_MODEL),
)

LANES = 128
VMEM_LIMIT = 56 * 1024 * 1024
MISC_GATE_OFF = IDX_HEADS

F32 = jnp.float32
BF16 = jnp.bfloat16
MXU = jnp.bfloat16


_PROJ_OUT = ('a_q', 'a_kv', 'a_iq', 'a_ik', 'misc', 'b_q', 'b_cmp_kv', 'b_sel_kv', 'b_win_kv', 'merge_gate')


_STATE_OUT = ('a_kv', 'a_ik', 'b_cmp_kv', 'b_sel_kv', 'b_win_kv')


def _proj_kernel(x_ref, *refs, feature_major_state):
    n = len(_PROJ_OUT)
    w_refs, o_refs, t_refs = refs[:n], refs[n:2 * n], refs[2 * n:]
    t_refs = dict(zip(_STATE_OUT, t_refs)) if feature_major_state else {}
    x = x_ref[...].astype(BF16)
    for name, w_ref, o_ref in zip(_PROJ_OUT, w_refs, o_refs):
        r = jnp.dot(x, w_ref[...], preferred_element_type=F32)
        if name == 'misc':
            lane = lax.broadcasted_iota(jnp.int32, r.shape, 1)
            r = jnp.where(lane < MISC_GATE_OFF, r * IDX_HEADS ** -0.5, jax.nn.sigmoid(r))
        elif name == 'merge_gate':
            r = jax.nn.sigmoid(r)
        o_ref[...] = r
        if name in t_refs:
            t_refs[name][0] = r.T


def _split_w_in(w_in):
    parts = {}
    off = 0
    for name, width in PROJ_LAYOUT:
        parts[name] = w_in[:, off:off + width]
        off += width
    misc = jnp.concatenate([parts['a_iw'], parts['b_gate']], axis=1)
    parts['misc'] = jnp.pad(misc, ((0, 0), (0, LANES - misc.shape[1])))
    return [parts[name].astype(BF16) for name in _PROJ_OUT]


def _project(x2d, ws, tm, seq_len=None):
    n = x2d.shape[0]
    widths = [w.shape[1] for w in ws]
    out_specs = [pl.BlockSpec((tm, wd), lambda i: (i, 0)) for wd in widths]
    out_shape = [jax.ShapeDtypeStruct((n, wd), F32) for wd in widths]
    if seq_len is not None:
        tiles = seq_len // tm
        for name in _STATE_OUT:
            wd = widths[_PROJ_OUT.index(name)]
            out_specs.append(pl.BlockSpec((1, wd, tm), lambda i: (i // tiles, 0, i % tiles)))
            out_shape.append(jax.ShapeDtypeStruct((n // seq_len, wd, seq_len), F32))
    return pl.pallas_call(
        functools.partial(_proj_kernel, feature_major_state=seq_len is not None),
        grid=(n // tm,),
        in_specs=[pl.BlockSpec((tm, D_MODEL), lambda i: (i, 0))]
        + [pl.BlockSpec((D_MODEL, wd), lambda i: (0, 0)) for wd in widths],
        out_specs=out_specs,
        out_shape=out_shape,
        compiler_params=pltpu.CompilerParams(dimension_semantics=('arbitrary',), vmem_limit_bytes=VMEM_LIMIT),
        name='project',
    )(x2d, *ws)


def _unpack_proj(outs, B, T):
    d = dict(zip(_PROJ_OUT, outs))
    d['state_t'] = dict(zip(_STATE_OUT, outs[len(_PROJ_OUT):]))
    misc = d['misc']
    kv = (B, T, 2, A_KV_HEADS, HEAD_DIM)
    return {
        'a_q': d['a_q'].reshape(B, T, A_HEADS, HEAD_DIM),
        'a_kv': d['a_kv'].reshape(kv),
        'a_iq': d['a_iq'].reshape(B, T, IDX_HEADS, IDX_DIM),
        'a_iw': misc[:, :IDX_HEADS].reshape(B, T, IDX_HEADS),
        'a_ik': d['a_ik'].reshape(B, T, IDX_DIM),
        'b_q': d['b_q'].reshape(B, T, B_HEADS, HEAD_DIM),
        'b_cmp_kv': d['b_cmp_kv'].reshape(kv),
        'b_sel_kv': d['b_sel_kv'].reshape(kv),
        'b_win_kv': d['b_win_kv'].reshape(kv),
        'b_gate': misc[:, MISC_GATE_OFF:MISC_GATE_OFF + 3 * B_HEADS].reshape(B, T, B_HEADS, 3),
        'merge_gate': d['merge_gate'],
        'raw': d,
    }


def _layer_norm(x, g, b):
    mu = jnp.mean(x, axis=-1, keepdims=True)
    xc = x - mu
    var = jnp.mean(xc * xc, axis=-1, keepdims=True)
    return xc * lax.rsqrt(var + LN_EPS) * g + b


def _merge_kernel(x_ref, oa_ref, ob_ref, mg_ref, wa_ref, wb_ref, wo_ref, g_ref, b_ref, h_ref):
    ya = jnp.dot(oa_ref[...].astype(BF16), wa_ref[...], preferred_element_type=F32)
    yb = jnp.dot(ob_ref[...].astype(BF16), wb_ref[...], preferred_element_type=F32)
    mg = mg_ref[...]
    mix = mg[:, :D_MODEL] * ya + mg[:, D_MODEL:] * yb
    mixed = jnp.dot(mix.astype(BF16), wo_ref[...], preferred_element_type=F32)
    h_ref[...] = _layer_norm(DN_ALPHA * x_ref[...] + mixed, g_ref[...], b_ref[...])


def _merge(x2d, o_a, o_b, mg, w_up_a, w_up_b, w_out, ln_g, ln_b, tm):
    n = x2d.shape[0]
    ca, cb = o_a.shape[1], o_b.shape[1]
    row = lambda i: (i, 0)
    fixed = lambda i: (0, 0)
    return pl.pallas_call(
        _merge_kernel,
        grid=(n // tm,),
        in_specs=[pl.BlockSpec((tm, D_MODEL), row), pl.BlockSpec((tm, ca), row), pl.BlockSpec((tm, cb), row),
                  pl.BlockSpec((tm, 2 * D_MODEL), row), pl.BlockSpec((ca, D_MODEL), fixed),
                  pl.BlockSpec((cb, D_MODEL), fixed), pl.BlockSpec((D_MODEL, D_MODEL), fixed),
                  pl.BlockSpec((1, D_MODEL), fixed), pl.BlockSpec((1, D_MODEL), fixed)],
        out_specs=pl.BlockSpec((tm, D_MODEL), row),
        out_shape=jax.ShapeDtypeStruct((n, D_MODEL), F32),
        compiler_params=pltpu.CompilerParams(dimension_semantics=('arbitrary',), vmem_limit_bytes=VMEM_LIMIT),
        name='merge',
    )(x2d, o_a, o_b, mg, w_up_a.astype(BF16), w_up_b.astype(BF16), w_out.astype(BF16),
      ln_g.reshape(1, D_MODEL), ln_b.reshape(1, D_MODEL))


TQ = 256
NEG = -1e30
INT_MIN = -2 ** 31
COUNT_ROWS = 256
_NT = (((1,), (1,)), ((), ()))
_TN = (((0,), (0,)), ((), ()))


def _bucket_np(d):
    n = np.maximum(d, 0)
    nf = np.maximum(n, 1).astype(np.float64)
    large = REL_MAX_EXACT + (np.log(nf / REL_MAX_EXACT) / math.log(REL_MAX_DIST / REL_MAX_EXACT)
                             * (REL_BUCKETS - REL_MAX_EXACT)).astype(np.int64)
    return np.where(n < REL_MAX_EXACT, n, np.minimum(large, REL_BUCKETS - 1)).astype(np.int32)


def _rel_bias_at(table, d):
    bucket = jnp.asarray(_bucket_np(d))[..., None]
    out = jnp.broadcast_to(table[REL_BUCKETS - 1], bucket.shape[:-1] + table.shape[1:])
    for b in range(REL_BUCKETS - 1):
        out = jnp.where(bucket == b, table[b], out)
    return out


def _band_bias(table, n_groups):
    d = TQ + np.arange(TQ)[None, :] - np.arange(2 * TQ)[:, None]
    band = _rel_bias_at(table, d) - table[REL_BUCKETS - 1]
    band = jnp.moveaxis(band, -1, 1)
    return jnp.moveaxis(band.reshape(2 * TQ, n_groups, -1), 1, 0)


def _block_bias(table, S):
    nb = S // NSA_BLOCK
    d = np.arange(S)[None, :] - ((np.arange(nb) + 1) * NSA_BLOCK - 1)[:, None]
    return jnp.moveaxis(_rel_bias_at(table, d), -1, 0)


def _ordered_keys(x):
    x = jnp.where(x == 0.0, 0.0, x)
    b = lax.bitcast_convert_type(x, jnp.int32)
    return b ^ ((b >> 31) & jnp.int32(0x7FFFFFFF))


def _softmax_reset(m_sc, l_sc, acc_sc):
    m_sc[...] = jnp.full(m_sc.shape, NEG, F32)
    l_sc[...] = jnp.zeros(l_sc.shape, F32)
    acc_sc[...] = jnp.zeros(acc_sc.shape, F32)


def _group_queries(q, n_groups):
    rep = q.shape[1] // HEAD_DIM // n_groups
    return [_stack_heads(q, range(g * rep, (g + 1) * rep), HEAD_DIM) for g in range(n_groups)]


def _attend_chunk(kv_c, q_groups, madd_of_group, band_ref, band_row0, m_sc, l_sc, acc_sc):
    n_g = len(q_groups)
    rep = q_groups[0].shape[0] // TQ
    for g in range(n_g):
        k_c = kv_c[:, g * HEAD_DIM:(g + 1) * HEAD_DIM]
        v_c = kv_c[:, (n_g + g) * HEAD_DIM:(n_g + g + 1) * HEAD_DIM]
        add = jnp.concatenate([madd_of_group(g)] * rep, axis=1)
        if band_row0 is not None:
            add = add + band_ref[g, band_row0:band_row0 + TQ, :]
        logits = lax.dot_general(k_c, q_groups[g], _NT, preferred_element_type=F32) + add
        m_old = m_sc[g]
        m_new = jnp.maximum(m_old, jnp.max(logits, axis=0, keepdims=True))
        alpha = jnp.exp(m_old - m_new)
        p = jnp.exp(logits - m_new)
        l_sc[g] = alpha * l_sc[g] + jnp.sum(p, axis=0, keepdims=True)
        m_sc[g] = m_new
        p = p.astype(MXU)
        for r in range(rep):
            h = g * rep + r
            lanes = slice(r * TQ, (r + 1) * TQ)
            pv = lax.dot_general(v_c, p[:, lanes], _TN, preferred_element_type=F32)
            acc_sc[h] = alpha[:, lanes] * acc_sc[h] + pv


def _softmax_scratch(n_heads, n_groups):
    stats = pltpu.VMEM((n_groups, 1, n_heads // n_groups * TQ), F32)
    return [stats, stats, pltpu.VMEM((n_heads, HEAD_DIM, TQ), F32)]


def _attend_result(h, rep, l_sc, acc_sc):
    g, r = divmod(h, rep)
    return acc_sc[h] / l_sc[g][:, r * TQ:(r + 1) * TQ]


def _dsa_kernel(iq_ref, misc_ref, q_ref, ik_ref, kv_ref, band_ref, o_ref, key_sc, m_sc, l_sc, acc_sc, out_sc,
                *, k_sel, pos_bits):
    i = pl.program_id(1)
    t0 = i * TQ
    n_chunks = i + 1
    lane_t = t0 + lax.broadcasted_iota(jnp.int32, (1, TQ), 1)
    sub_iota = lax.broadcasted_iota(jnp.int32, (TQ, TQ), 0)
    misc_t = misc_ref[...].T
    iq = iq_ref[...].astype(MXU)
    iq_heads = [iq[:, h * IDX_DIM:(h + 1) * IDX_DIM] for h in range(IDX_HEADS)]

    def score_chunk(j, carry):
        r0 = pl.multiple_of(j * TQ, TQ)
        ik_c = ik_ref[pl.ds(r0, TQ), :].astype(MXU)
        sc = jnp.zeros((TQ, TQ), F32)
        for h in range(IDX_HEADS):
            rel = lax.dot_general(ik_c, iq_heads[h], _NT, preferred_element_type=F32)
            sc = sc + jnp.maximum(rel * IDX_DIM ** -0.5, 0.0) * misc_t[h:h + 1, :]
        sc = jnp.where(r0 + sub_iota <= lane_t, sc, -jnp.inf)
        key_sc[pl.ds(r0, TQ), :] = _ordered_keys(sc)
        return carry

    lax.fori_loop(0, n_chunks, score_chunk, 0)

    def count(pred):
        def body(jj, acc):
            r0 = pl.multiple_of(jj * COUNT_ROWS, COUNT_ROWS)
            blk = key_sc[pl.ds(r0, COUNT_ROWS), :]
            pos = r0 + lax.broadcasted_iota(jnp.int32, (COUNT_ROWS, TQ), 0)
            hit = jnp.where(pred(blk, pos), 1, 0)
            return acc + jnp.sum(hit.reshape(COUNT_ROWS // 8, 8, TQ), axis=0)

        acc = lax.fori_loop(0, n_chunks * (TQ // COUNT_ROWS), body, jnp.zeros((8, TQ), jnp.int32))
        return jnp.sum(acc, axis=0, keepdims=True)

    v = jnp.full((1, TQ), INT_MIN, jnp.int32)
    v = jnp.where(count(lambda blk, pos: blk >= 0) >= k_sel, 0, v)

    def value_bit(it, v):
        cand = v | jnp.left_shift(jnp.int32(1), 30 - it)
        return jnp.where(count(lambda blk, pos: blk >= cand) >= k_sel, cand, v)

    v = lax.fori_loop(0, 31, value_bit, v)
    need = k_sel - count(lambda blk, pos: blk > v)
    n_ge = count(lambda blk, pos: blk >= v)

    def tie_search():
        def pos_bit(it, jm):
            cand = jm | jnp.left_shift(jnp.int32(1), pos_bits - 1 - it)
            return jnp.where(count(lambda blk, pos: (blk == v) & (pos < cand)) < need, cand, jm)

        return lax.fori_loop(0, pos_bits, pos_bit, jnp.zeros((1, TQ), jnp.int32))

    j_max = lax.cond(jnp.max(n_ge) > k_sel, tie_search, lambda: jnp.full((1, TQ), 2 ** 30, jnp.int32))

    q = (q_ref[...] * HEAD_DIM ** -0.5).astype(MXU)
    q_groups = _group_queries(q, A_KV_HEADS)
    _softmax_reset(m_sc, l_sc, acc_sc)

    def chunk(j, band_row0, causal):
        r0 = pl.multiple_of(j * TQ, TQ)
        key = key_sc[pl.ds(r0, TQ), :]
        pos = r0 + sub_iota
        sel = (key > v) | ((key == v) & (pos <= j_max))
        if causal:
            sel = sel & (pos <= lane_t)
        madd = jnp.where(sel, 0.0, NEG)
        kv_c = kv_ref[pl.ds(r0, TQ), :].astype(MXU)
        _attend_chunk(kv_c, q_groups, lambda g: madd, band_ref, band_row0, m_sc, l_sc, acc_sc)

    def far_chunk(j, carry):
        chunk(j, None, False)
        return carry

    lax.fori_loop(0, jnp.maximum(i - 1, 0), far_chunk, 0)

    @pl.when(i >= 1)
    def _():
        chunk(i - 1, 0, False)

    chunk(i, TQ, True)
    for h in range(A_HEADS):
        out_sc[h * HEAD_DIM:(h + 1) * HEAD_DIM, :] = _attend_result(h, A_HEADS // A_KV_HEADS, l_sc, acc_sc)
    o_ref[...] = out_sc[...].T


def _dsa_prompt(raw, band_a, B, S):
    nq = S // TQ
    k_sel = min(A_TOPK, S // 4)
    assert S % TQ == 0 and TQ >= k_sel
    tile = lambda b, i: (b * nq + i, 0)
    seq = lambda b, i: (b, 0)
    kern = functools.partial(_dsa_kernel, k_sel=k_sel, pos_bits=(S - 1).bit_length())
    return pl.pallas_call(
        kern,
        grid=(B, nq),
        in_specs=[pl.BlockSpec((TQ, IDX_HEADS * IDX_DIM), tile), pl.BlockSpec((TQ, LANES), tile),
                  pl.BlockSpec((TQ, A_HEADS * HEAD_DIM), tile), pl.BlockSpec((S, IDX_DIM), seq),
                  pl.BlockSpec((S, 2 * A_KV_HEADS * HEAD_DIM), seq),
                  pl.BlockSpec(band_a.shape, lambda b, i: (0, 0, 0))],
        out_specs=pl.BlockSpec((TQ, A_HEADS * HEAD_DIM), tile),
        out_shape=jax.ShapeDtypeStruct((B * S, A_HEADS * HEAD_DIM), F32),
        scratch_shapes=[pltpu.VMEM((S, TQ), jnp.int32)] + _softmax_scratch(A_HEADS, A_KV_HEADS)
        + [pltpu.VMEM((A_HEADS * HEAD_DIM, TQ), F32)],
        compiler_params=pltpu.CompilerParams(dimension_semantics=('arbitrary', 'arbitrary'),
                                             vmem_limit_bytes=VMEM_LIMIT),
        name='dsa_prompt',
    )(raw['a_iq'], raw['misc'], raw['a_q'], raw['a_ik'], raw['a_kv'], band_a)


def _compress_kernel(x_ref, pe_ref, w1_ref, w2_ref, o_ref):
    x = (x_ref[...] + pe_ref[...]).astype(MXU)
    h = jax.nn.gelu(jnp.dot(x, w1_ref[...], preferred_element_type=F32))
    o_ref[...] = jnp.dot(h.astype(MXU), w2_ref[...], preferred_element_type=F32)


def _compress_weights(pe, w1, w2):
    eye_c = jnp.eye(2, dtype=F32)
    eye_g = jnp.eye(B_KV_HEADS, dtype=F32)
    w1_big = jnp.einsum('lcde,cC,gG->lcgdCGe', w1, eye_c, eye_g)
    w1_big = w1_big.reshape(NSA_BLOCK * 2 * B_KV_HEADS * HEAD_DIM, 2 * B_KV_HEADS * CMP_HIDDEN)
    w2_big = jnp.einsum('cef,cC,gG->cgeCGf', w2, eye_c, eye_g)
    w2_big = w2_big.reshape(2 * B_KV_HEADS * CMP_HIDDEN, 2 * B_KV_HEADS * HEAD_DIM)
    pe_flat = jnp.broadcast_to(pe[:, :, None, :], (NSA_BLOCK, 2, B_KV_HEADS, HEAD_DIM)).reshape(1, -1)
    return pe_flat, w1_big.astype(MXU), w2_big.astype(MXU)


def _compress(blocks2d, pe_flat, w1_big, w2_big, tm):
    n, width = blocks2d.shape
    fixed = lambda i: (0, 0)
    return pl.pallas_call(
        _compress_kernel,
        grid=(n // tm,),
        in_specs=[pl.BlockSpec((tm, width), lambda i: (i, 0)), pl.BlockSpec((1, width), fixed),
                  pl.BlockSpec(w1_big.shape, fixed), pl.BlockSpec(w2_big.shape, fixed)],
        out_specs=pl.BlockSpec((tm, w2_big.shape[1]), lambda i: (i, 0)),
        out_shape=jax.ShapeDtypeStruct((n, w2_big.shape[1]), F32),
        compiler_params=pltpu.CompilerParams(dimension_semantics=('arbitrary',), vmem_limit_bytes=VMEM_LIMIT),
        name='nsa_compress',
    )(blocks2d, pe_flat, w1_big, w2_big)


def _nsa_kernel(q_ref, misc_ref, cmp_ref, selkv_ref, winkv_ref, band_ref, biasc_ref, o_ref,
                sel_sc, m_sc, l_sc, acc_sc, out_sc, *, nb, n_sel):
    i = pl.program_id(1)
    t0 = i * TQ
    n_g = B_KV_HEADS
    rep = B_HEADS // n_g
    lane_t = t0 + lax.broadcasted_iota(jnp.int32, (1, TQ), 1)
    sub_iota = lax.broadcasted_iota(jnp.int32, (TQ, TQ), 0)
    lane_iota = lax.broadcasted_iota(jnp.int32, (TQ, TQ), 1)
    misc_t = misc_ref[...].T
    gate = lambda h, k: misc_t[MISC_GATE_OFF + 3 * h + k:MISC_GATE_OFF + 3 * h + k + 1, :]
    q = (q_ref[...] * HEAD_DIM ** -0.5).astype(MXU)
    q_heads = [q[:, h * HEAD_DIM:(h + 1) * HEAD_DIM] for h in range(B_HEADS)]
    q_groups = _group_queries(q, n_g)

    cmp = cmp_ref[...].astype(MXU)
    blk = lax.broadcasted_iota(jnp.int32, (nb, TQ), 0)
    visible = (blk + 1) * NSA_BLOCK - 1 <= lane_t
    cur = lax.shift_right_logical(lane_t, NSA_BLOCK.bit_length() - 1)
    forced = (blk == 0) | (blk == cur) | (blk == cur - 1)
    for g in range(n_g):
        k_c = cmp[:, g * HEAD_DIM:(g + 1) * HEAD_DIM]
        v_c = cmp[:, (n_g + g) * HEAD_DIM:(n_g + g + 1) * HEAD_DIM]
        importance = jnp.zeros((nb, TQ), F32)
        for r in range(rep):
            h = g * rep + r
            lc = lax.dot_general(k_c, q_heads[h], _NT, preferred_element_type=F32) + biasc_ref[h]
            lc = jnp.where(visible, lc, -jnp.inf)
            m = jnp.max(lc, axis=0, keepdims=True)
            m = jnp.where(m > -jnp.inf, m, 0.0)
            e = jnp.exp(lc - m)
            s = jnp.sum(e, axis=0, keepdims=True)
            p = e / jnp.where(s > 0, s, 1.0)
            importance = importance + p
            o_cmp = lax.dot_general(v_c, p.astype(MXU), _TN, preferred_element_type=F32)
            out_sc[h * HEAD_DIM:(h + 1) * HEAD_DIM, :] = gate(h, 0) * o_cmp
        score = jnp.where(forced, NSA_FORCE, importance)
        score = jnp.where(blk <= cur, score, -1.0)
        rank = jnp.zeros((nb, TQ), jnp.int32)
        for n in range(nb):
            row = score[n:n + 1, :]
            beats = (row > score) | ((row == score) & (blk > n))
            rank = rank + jnp.where(beats, 1, 0)
        sel_sc[g] = jnp.where(rank < n_sel, 0.0, NEG)

    def finish_branch(k):
        for h in range(B_HEADS):
            rows = slice(h * HEAD_DIM, (h + 1) * HEAD_DIM)
            out_sc[rows, :] = out_sc[rows, :] + gate(h, k) * _attend_result(h, rep, l_sc, acc_sc)

    _softmax_reset(m_sc, l_sc, acc_sc)
    blocks_per_chunk = TQ // NSA_BLOCK

    def sel_chunk(j, band_row0, causal):
        r0 = pl.multiple_of(j * TQ, TQ)
        kv_c = selkv_ref[pl.ds(r0, TQ), :].astype(MXU)

        def madd_of_group(g):
            rows = [sel_sc[g, pl.ds(j * blocks_per_chunk + b, 1), :] for b in range(blocks_per_chunk)]
            madd = jnp.concatenate([jnp.broadcast_to(row, (NSA_BLOCK, TQ)) for row in rows], axis=0)
            if causal:
                madd = jnp.where(sub_iota <= lane_iota, madd, NEG)
            return madd

        _attend_chunk(kv_c, q_groups, madd_of_group, band_ref, band_row0, m_sc, l_sc, acc_sc)

    def far_chunk(j, carry):
        sel_chunk(j, None, False)
        return carry

    lax.fori_loop(0, jnp.maximum(i - 1, 0), far_chunk, 0)

    @pl.when(i >= 1)
    def _():
        sel_chunk(i - 1, 0, False)

    sel_chunk(i, TQ, True)
    finish_branch(1)

    _softmax_reset(m_sc, l_sc, acc_sc)
    for back in range(WINDOW // TQ, -1, -1):
        dist = back * TQ + lane_iota - sub_iota
        madd = jnp.where((dist >= 0) & (dist < WINDOW), 0.0, NEG)
        band_row0 = {0: TQ, 1: 0}.get(back)

        def win_chunk(back=back, madd=madd, band_row0=band_row0):
            r0 = pl.multiple_of((i - back) * TQ, TQ)
            kv_c = winkv_ref[pl.ds(r0, TQ), :].astype(MXU)
            _attend_chunk(kv_c, q_groups, lambda g: madd, band_ref, band_row0, m_sc, l_sc, acc_sc)

        if back == 0:
            win_chunk()
        else:
            pl.when(i >= back)(win_chunk)
    finish_branch(2)
    o_ref[...] = out_sc[...].T


def _nsa_prompt(raw, cmp, band_b, bias_c, B, S):
    nq = S // TQ
    nb = S // NSA_BLOCK
    assert S % TQ == 0 and TQ % NSA_BLOCK == 0
    tile = lambda b, i: (b * nq + i, 0)
    seq = lambda b, i: (b, 0)
    kv_w = 2 * B_KV_HEADS * HEAD_DIM
    kern = functools.partial(_nsa_kernel, nb=nb, n_sel=min(NSA_TOPN, nb))
    return pl.pallas_call(
        kern,
        grid=(B, nq),
        in_specs=[pl.BlockSpec((TQ, B_HEADS * HEAD_DIM), tile), pl.BlockSpec((TQ, LANES), tile),
                  pl.BlockSpec((nb, kv_w), seq), pl.BlockSpec((S, kv_w), seq), pl.BlockSpec((S, kv_w), seq),
                  pl.BlockSpec(band_b.shape, lambda b, i: (0, 0, 0)),
                  pl.BlockSpec((B_HEADS, nb, TQ), lambda b, i: (0, 0, i))],
        out_specs=pl.BlockSpec((TQ, B_HEADS * HEAD_DIM), tile),
        out_shape=jax.ShapeDtypeStruct((B * S, B_HEADS * HEAD_DIM), F32),
        scratch_shapes=[pltpu.VMEM((B_KV_HEADS, nb, TQ), F32)] + _softmax_scratch(B_HEADS, B_KV_HEADS)
        + [pltpu.VMEM((B_HEADS * HEAD_DIM, TQ), F32)],
        compiler_params=pltpu.CompilerParams(dimension_semantics=('arbitrary', 'arbitrary'),
                                             vmem_limit_bytes=VMEM_LIMIT),
        name='nsa_prompt',
    )(raw['b_q'], raw['misc'], cmp, raw['b_sel_kv'], raw['b_win_kv'], band_b, bias_c)


def _mix_prompt(pr, rel_table, cmp_pe, cmp_w1, cmp_w2):
    B, S = pr['a_q'].shape[:2]
    raw = pr['raw']
    table_a, table_b = rel_table[:, :A_HEADS], rel_table[:, A_HEADS:]
    pe_flat, w1_big, w2_big = _compress_weights(cmp_pe, cmp_w1, cmp_w2)
    n_blocks = B * S // NSA_BLOCK
    cmp = _compress(raw['b_cmp_kv'].reshape(n_blocks, -1), pe_flat, w1_big, w2_big, tm=min(128, n_blocks))
    o_a = _dsa_prompt(raw, _band_bias(table_a, A_KV_HEADS), B, S)
    o_b = _nsa_prompt(raw, cmp, _band_bias(table_b, B_KV_HEADS), _block_bias(table_b, S), B, S)
    wb = min(WINDOW, S)
    kv_tail = (2, B_KV_HEADS, HEAD_DIM)

    def rows(name, tail, first=0):
        part = raw['state_t'][name][:, :, first:]
        return jnp.moveaxis(part.reshape((B,) + tail + (S - first,)), -1, 1)

    new_state = (rows('a_kv', kv_tail), rows('a_ik', (IDX_DIM,)), rows('b_cmp_kv', kv_tail),
                 rows('b_sel_kv', kv_tail), rows('b_win_kv', kv_tail, S - wb))
    return o_a, o_b, new_state


PAGES_PER_STEP = 16
KEY_PAD = 128


def _paged_specs(width):
    def spec(j):
        return pl.BlockSpec((1, width, PAGE_SIZE), lambda s, p, pt: (pt[s, p * PAGES_PER_STEP + j], 0, 0))
    return [spec(j) for j in range(PAGES_PER_STEP)]


def _feature_major(pool, width):
    return jnp.swapaxes(pool.reshape(pool.shape[0], PAGE_SIZE, width), 1, 2)


def _stage_pages(page_refs, dst_sc):
    p = pl.program_id(1)
    for j, ref in enumerate(page_refs):
        r0 = pl.multiple_of((p * PAGES_PER_STEP + j) * PAGE_SIZE, PAGE_SIZE)
        dst_sc[pl.ds(r0, PAGE_SIZE), :] = ref[0].T


def _stage_new_rows(new_ref, dst_sc, past):
    t = new_ref.shape[0]
    dst_sc[past:past + t, :] = new_ref[...]
    dst_sc[past + t:, :] = jnp.zeros((dst_sc.shape[0] - past - t, dst_sc.shape[1]), F32)


def _rows_softmax_attend(q_rows, k, v, bias, madd):
    logits = lax.dot_general(q_rows, k, _NT, preferred_element_type=F32) + bias + madd
    m = jnp.max(logits, axis=1, keepdims=True)
    e = jnp.exp(logits - m)
    p = e / jnp.sum(e, axis=1, keepdims=True)
    return jnp.dot(p.astype(MXU), v, preferred_element_type=F32)


def _stack_heads(x, heads, width):
    return jnp.concatenate([x[:, h * width:(h + 1) * width] for h in heads], axis=0)


def _dsa_sample_kernel(pt_ref, *refs, past, k_sel):
    n = PAGES_PER_STEP
    idx_pages, kv_pages = refs[:n], refs[n:2 * n]
    ikn_ref, kvn_ref, iq_ref, misc_ref, q_ref, bias_ref, o_ref, ik_sc, kv_sc = refs[2 * n:]
    _stage_pages(idx_pages, ik_sc)
    _stage_pages(kv_pages, kv_sc)

    @pl.when(pl.program_id(1) == pl.num_programs(1) - 1)
    def _():
        T = iq_ref.shape[0]
        L = ik_sc.shape[0]
        _stage_new_rows(ikn_ref, ik_sc, past)
        _stage_new_rows(kvn_ref, kv_sc, past)
        pos = lax.broadcasted_iota(jnp.int32, (T, L), 1)
        valid = pos <= past + lax.broadcasted_iota(jnp.int32, (T, L), 0)
        misc = misc_ref[...]
        iq = _stack_heads(iq_ref[...].astype(MXU), range(IDX_HEADS), IDX_DIM)
        rel = lax.dot_general(iq, ik_sc[...].astype(MXU), _NT, preferred_element_type=F32)
        sc = jnp.zeros((T, L), F32)
        for h in range(IDX_HEADS):
            sc = sc + jnp.maximum(rel[h * T:(h + 1) * T] * IDX_DIM ** -0.5, 0.0) * misc[:, h:h + 1]
        key = _ordered_keys(jnp.where(valid, sc, -jnp.inf))

        def count(hit):
            return jnp.sum(jnp.where(hit, 1, 0), axis=1, keepdims=True)

        v = jnp.full((T, 1), INT_MIN, jnp.int32)
        v = jnp.where(count(key >= 0) >= k_sel, 0, v)

        def value_bit(it, v):
            cand = v | jnp.left_shift(jnp.int32(1), 30 - it)
            return jnp.where(count(key >= cand) >= k_sel, cand, v)

        v = lax.fori_loop(0, 31, value_bit, v)
        need = k_sel - count(key > v)
        pos_bits = (L - 1).bit_length()

        def tie_search():
            def pos_bit(it, jm):
                cand = jm | jnp.left_shift(jnp.int32(1), pos_bits - 1 - it)
                return jnp.where(count((key == v) & (pos < cand)) < need, cand, jm)

            return lax.fori_loop(0, pos_bits, pos_bit, jnp.zeros((T, 1), jnp.int32))

        j_max = lax.cond(jnp.max(count(key >= v)) > k_sel, tie_search,
                         lambda: jnp.full((T, 1), 2 ** 30, jnp.int32))
        sel = ((key > v) | ((key == v) & (pos <= j_max))) & valid
        madd = jnp.where(sel, 0.0, NEG)
        rep = A_HEADS // A_KV_HEADS
        madd = jnp.concatenate([madd] * rep, axis=0)
        q = (q_ref[...] * HEAD_DIM ** -0.5).astype(MXU)
        kv = kv_sc[...].astype(MXU)
        for g in range(A_KV_HEADS):
            q_rows = _stack_heads(q, range(g * rep, (g + 1) * rep), HEAD_DIM)
            o = _rows_softmax_attend(q_rows, kv[:, g * HEAD_DIM:(g + 1) * HEAD_DIM],
                                     kv[:, (A_KV_HEADS + g) * HEAD_DIM:(A_KV_HEADS + g + 1) * HEAD_DIM],
                                     bias_ref[g], madd)
            for r in range(rep):
                h = g * rep + r
                o_ref[:, h * HEAD_DIM:(h + 1) * HEAD_DIM] = o[r * T:(r + 1) * T, :]


def _sample_bias(table, T, past, n_keys, key_pos0, n_groups):
    d = (past + np.arange(T))[:, None] - (key_pos0 + np.arange(n_keys))[None, :]
    b = _rel_bias_at(table, d)
    b = jnp.moveaxis(b, -1, 0)
    return b.reshape(n_groups, -1, n_keys)


def _seq_rows(width, T):
    return pl.BlockSpec((T, width), lambda s, p, pt: (s, 0))


def _dsa_sample(raw, pool_idx, pool_kv, page_table, table_a, DB, T):
    n_pages = page_table.shape[1]
    past = n_pages * PAGE_SIZE
    L = past + KEY_PAD
    assert n_pages % PAGES_PER_STEP == 0 and T <= KEY_PAD
    kv_w = 2 * A_KV_HEADS * HEAD_DIM
    bias = _sample_bias(table_a, T, past, L, 0, A_KV_HEADS)
    fixed3 = lambda s, p, pt: (0, 0, 0)
    grid_spec = pltpu.PrefetchScalarGridSpec(
        num_scalar_prefetch=1,
        grid=(DB, n_pages // PAGES_PER_STEP),
        in_specs=_paged_specs(IDX_DIM) + _paged_specs(kv_w)
        + [_seq_rows(IDX_DIM, T), _seq_rows(kv_w, T), _seq_rows(IDX_HEADS * IDX_DIM, T), _seq_rows(LANES, T),
           _seq_rows(A_HEADS * HEAD_DIM, T), pl.BlockSpec(bias.shape, fixed3)],
        out_specs=_seq_rows(A_HEADS * HEAD_DIM, T),
        scratch_shapes=[pltpu.VMEM((L, IDX_DIM), F32), pltpu.VMEM((L, kv_w), F32)],
    )
    kern = functools.partial(_dsa_sample_kernel, past=past, k_sel=min(A_TOPK, (past + T) // 4))
    return pl.pallas_call(
        kern, grid_spec=grid_spec,
        out_shape=jax.ShapeDtypeStruct((DB * T, A_HEADS * HEAD_DIM), F32),
        compiler_params=pltpu.CompilerParams(dimension_semantics=('arbitrary', 'arbitrary'),
                                             vmem_limit_bytes=VMEM_LIMIT),
        name='dsa_sample',
    )(page_table, *([pool_idx] * PAGES_PER_STEP), *([pool_kv] * PAGES_PER_STEP),
      raw['a_ik'], raw['a_kv'], raw['a_iq'], raw['misc'], raw['a_q'], bias)


def _compress_sample_kernel(pt_ref, *refs, past):
    n = PAGES_PER_STEP
    pages = refs[:n]
    new_ref, pe_ref, w1_ref, w2_ref, o_ref, xk_sc, xv_sc = refs[n:]
    halves = (xk_sc, xv_sc)
    p = pl.program_id(1)
    for j, ref in enumerate(pages):
        r0 = pl.multiple_of((p * PAGES_PER_STEP + j) * PAGE_SIZE, PAGE_SIZE)
        for c, x_sc in enumerate(halves):
            x_sc[pl.ds(r0, PAGE_SIZE), :] = ref[0, c * LANES:(c + 1) * LANES, :].T

    @pl.when(p == pl.num_programs(1) - 1)
    def _():
        nbp = o_ref.shape[1]
        t = new_ref.shape[0]
        for c, x_sc in enumerate(halves):
            x_sc[past:past + t, :] = new_ref[:, c * LANES:(c + 1) * LANES]
            x_sc[past + t:, :] = jnp.zeros((x_sc.shape[0] - past - t, LANES), F32)

        def body(l, accs):
            out = []
            for c, (x_sc, acc) in enumerate(zip(halves, accs)):
                xl = x_sc[pl.ds(l, nbp, stride=NSA_BLOCK), :] + pe_ref[c, pl.ds(l, 1), :]
                out.append(acc + jnp.dot(xl.astype(MXU), w1_ref[l, c], preferred_element_type=F32))
            return tuple(out)

        zero = jnp.zeros((nbp, LANES), F32)
        h = jnp.concatenate(lax.fori_loop(0, NSA_BLOCK, body, (zero, zero), unroll=8), axis=1)
        o_ref[0] = jnp.dot(jax.nn.gelu(h).astype(MXU), w2_ref[...], preferred_element_type=F32)


def _padded_blocks(past, T):
    nb = -(-(past + T) // NSA_BLOCK)
    return nb, -(-nb // 8) * 8


def _compress_sample(raw, pool_cmp, page_table, pe_flat, w1_big, w2_big, DB, T):
    n_pages = page_table.shape[1]
    past = n_pages * PAGE_SIZE
    _, nbp = _padded_blocks(past, T)
    kv_w = 2 * B_KV_HEADS * HEAD_DIM
    assert kv_w == 2 * LANES and w1_big.shape[1] == 2 * LANES
    w1_l = w1_big.reshape(NSA_BLOCK, 2, LANES, 2, LANES)
    w1_halves = jnp.stack([w1_l[:, 0, :, 0, :], w1_l[:, 1, :, 1, :]], axis=1)
    fixed2 = lambda s, p, pt: (0, 0)
    grid_spec = pltpu.PrefetchScalarGridSpec(
        num_scalar_prefetch=1,
        grid=(DB, n_pages // PAGES_PER_STEP),
        in_specs=_paged_specs(kv_w) + [_seq_rows(kv_w, T),
                                       pl.BlockSpec((2, NSA_BLOCK, LANES), lambda s, p, pt: (0, 0, 0)),
                                       pl.BlockSpec(w1_halves.shape, lambda s, p, pt: (0, 0, 0, 0)),
                                       pl.BlockSpec(w2_big.shape, fixed2)],
        out_specs=pl.BlockSpec((1, nbp, kv_w), lambda s, p, pt: (s, 0, 0)),
        scratch_shapes=[pltpu.VMEM((nbp * NSA_BLOCK, LANES), F32), pltpu.VMEM((nbp * NSA_BLOCK, LANES), F32)],
    )
    return pl.pallas_call(
        functools.partial(_compress_sample_kernel, past=past), grid_spec=grid_spec,
        out_shape=jax.ShapeDtypeStruct((DB, nbp, kv_w), F32),
        compiler_params=pltpu.CompilerParams(dimension_semantics=('arbitrary', 'arbitrary'),
                                             vmem_limit_bytes=VMEM_LIMIT),
        name='nsa_compress_sample',
    )(page_table, *([pool_cmp] * PAGES_PER_STEP), raw['b_cmp_kv'],
      pe_flat.reshape(NSA_BLOCK, 2, LANES).swapaxes(0, 1), w1_halves, w2_big)


def _nsa_sample_kernel(pt_ref, *refs, past, nb, n_sel):
    n = PAGES_PER_STEP
    pages = refs[:n]
    (seln_ref, winn_ref, q_ref, misc_ref, cmp_ref, win_ref, expand_ref, bias_c_ref, bias_s_ref, bias_w_ref,
     o_ref, kv_sc, win_sc) = refs[n:]
    _stage_pages(pages, kv_sc)

    @pl.when(pl.program_id(1) == pl.num_programs(1) - 1)
    def _():
        T = q_ref.shape[0]
        L = kv_sc.shape[0]
        n_g = B_KV_HEADS
        rep = B_HEADS // n_g
        wb = win_ref.shape[1]
        _stage_new_rows(seln_ref, kv_sc, past)
        win_sc[0:wb, :] = win_ref[0]
        _stage_new_rows(winn_ref, win_sc, wb)
        misc = misc_ref[...]
        gate = lambda h, k: misc[:, MISC_GATE_OFF + 3 * h + k:MISC_GATE_OFF + 3 * h + k + 1]
        q = (q_ref[...] * HEAD_DIM ** -0.5).astype(MXU)
        cmp = cmp_ref[0].astype(MXU)
        nbp = cmp.shape[0]
        t_col = lax.broadcasted_iota(jnp.int32, (T, 1), 0)
        q_pos = past + t_col
        blk = lax.broadcasted_iota(jnp.int32, (T, nbp), 1)
        cur = lax.shift_right_logical(q_pos, NSA_BLOCK.bit_length() - 1)
        visible = (blk + 1) * NSA_BLOCK - 1 <= q_pos
        visible_rows = jnp.concatenate([visible] * rep, axis=0)
        forced = (blk == 0) | (blk == cur) | (blk == cur - 1)
        pos = lax.broadcasted_iota(jnp.int32, (T, L), 1)
        causal = pos <= q_pos
        kv = kv_sc[...].astype(MXU)
        win = win_sc[...].astype(MXU)
        wpos = lax.broadcasted_iota(jnp.int32, (T, win.shape[0]), 1)
        dw = wb + t_col - wpos
        madd_w = jnp.where((dw >= 0) & (dw < WINDOW), 0.0, NEG)
        madd_w = jnp.concatenate([madd_w] * rep, axis=0)
        outs = []
        for g in range(n_g):
            heads = range(g * rep, (g + 1) * rep)
            ks = slice(g * HEAD_DIM, (g + 1) * HEAD_DIM)
            vs = slice((n_g + g) * HEAD_DIM, (n_g + g + 1) * HEAD_DIM)
            q_rows = _stack_heads(q, heads, HEAD_DIM)
            lc = lax.dot_general(q_rows, cmp[:, ks], _NT, preferred_element_type=F32) + bias_c_ref[g]
            lc = jnp.where(visible_rows, lc, -jnp.inf)
            m = jnp.max(lc, axis=1, keepdims=True)
            m = jnp.where(m > -jnp.inf, m, 0.0)
            e = jnp.exp(lc - m)
            s = jnp.sum(e, axis=1, keepdims=True)
            p = e / jnp.where(s > 0, s, 1.0)
            o_cmp = jnp.dot(p.astype(MXU), cmp[:, vs], preferred_element_type=F32)
            importance = p[0:T]
            for r in range(1, rep):
                importance = importance + p[r * T:(r + 1) * T]
            score = jnp.where(forced, NSA_FORCE, importance)
            score = jnp.where(blk <= cur, score, -1.0)
            rank = jnp.zeros((T, nbp), jnp.int32)
            for b in range(nb):
                col = score[:, b:b + 1]
                rank = rank + jnp.where((col > score) | ((col == score) & (blk > b)), 1, 0)
            picked = jnp.where(rank < n_sel, 1.0, 0.0).astype(MXU)
            on_keys = jnp.dot(picked, expand_ref[...], preferred_element_type=F32)
            madd_s = jnp.where((on_keys > 0.5) & causal, 0.0, NEG)
            madd_s = jnp.concatenate([madd_s] * rep, axis=0)
            o_slc = _rows_softmax_attend(q_rows, kv[:, ks], kv[:, vs], bias_s_ref[g], madd_s)
            o_win = _rows_softmax_attend(q_rows, win[:, ks], win[:, vs], bias_w_ref[g], madd_w)
            for r, h in enumerate(heads):
                rows = slice(r * T, (r + 1) * T)
                outs.append(gate(h, 0) * o_cmp[rows] + gate(h, 1) * o_slc[rows] + gate(h, 2) * o_win[rows])
        o_ref[...] = jnp.concatenate(outs, axis=1)


def _nsa_sample(raw, cmp, pool_sel, win_state, page_table, table_b, DB, T):
    n_pages = page_table.shape[1]
    past = n_pages * PAGE_SIZE
    L = past + KEY_PAD
    nb, nbp = _padded_blocks(past, T)
    wb = win_state.shape[1]
    lw = wb + KEY_PAD
    kv_w = 2 * B_KV_HEADS * HEAD_DIM
    n_g = B_KV_HEADS
    expand = jnp.asarray(np.arange(L)[None, :] // NSA_BLOCK == np.arange(nbp)[:, None], dtype=MXU)
    blk_end = (np.arange(nbp) + 1) * NSA_BLOCK - 1
    d_c = (past + np.arange(T))[:, None] - blk_end[None, :]
    bias_c = jnp.moveaxis(_rel_bias_at(table_b, d_c), -1, 0).reshape(n_g, -1, nbp)
    bias_s = _sample_bias(table_b, T, past, L, 0, n_g)
    bias_w = _sample_bias(table_b, T, past, lw, past - wb, n_g)
    fixed2 = lambda s, p, pt: (0, 0)
    fixed3 = lambda s, p, pt: (0, 0, 0)
    per_seq3 = lambda s, p, pt: (s, 0, 0)
    grid_spec = pltpu.PrefetchScalarGridSpec(
        num_scalar_prefetch=1,
        grid=(DB, n_pages // PAGES_PER_STEP),
        in_specs=_paged_specs(kv_w)
        + [_seq_rows(kv_w, T), _seq_rows(kv_w, T), _seq_rows(B_HEADS * HEAD_DIM, T), _seq_rows(LANES, T),
           pl.BlockSpec((1, nbp, kv_w), per_seq3), pl.BlockSpec((1, wb, kv_w), per_seq3),
           pl.BlockSpec(expand.shape, fixed2), pl.BlockSpec(bias_c.shape, fixed3),
           pl.BlockSpec(bias_s.shape, fixed3), pl.BlockSpec(bias_w.shape, fixed3)],
        out_specs=_seq_rows(B_HEADS * HEAD_DIM, T),
        scratch_shapes=[pltpu.VMEM((L, kv_w), F32), pltpu.VMEM((lw, kv_w), F32)],
    )
    kern = functools.partial(_nsa_sample_kernel, past=past, nb=nb, n_sel=min(NSA_TOPN, nb))
    return pl.pallas_call(
        kern, grid_spec=grid_spec,
        out_shape=jax.ShapeDtypeStruct((DB * T, B_HEADS * HEAD_DIM), F32),
        compiler_params=pltpu.CompilerParams(dimension_semantics=('arbitrary', 'arbitrary'),
                                             vmem_limit_bytes=VMEM_LIMIT),
        name='nsa_sample',
    )(page_table, *([pool_sel] * PAGES_PER_STEP), raw['b_sel_kv'], raw['b_win_kv'], raw['b_q'], raw['misc'],
      cmp, win_state, expand, bias_c, bias_s, bias_w)


def _mix_sample(pr, cache_a_kv, cache_a_idx, cache_b_cmp_kv, cache_b_sel_kv, state_b_win_kv, page_table,
                rel_table, cmp_pe, cmp_w1, cmp_w2):
    DB, T = pr['a_q'].shape[:2]
    raw = pr['raw']
    kv_w = 2 * A_KV_HEADS * HEAD_DIM
    table_a, table_b = rel_table[:, :A_HEADS], rel_table[:, A_HEADS:]
    pe_flat, w1_big, w2_big = _compress_weights(cmp_pe, cmp_w1, cmp_w2)
    win_state = state_b_win_kv.reshape(DB, -1, kv_w)
    o_a = _dsa_sample(raw, _feature_major(cache_a_idx, IDX_DIM), _feature_major(cache_a_kv, kv_w), page_table,
                      table_a, DB, T)
    cmp = _compress_sample(raw, _feature_major(cache_b_cmp_kv, kv_w), page_table, pe_flat, w1_big, w2_big, DB, T)
    o_b = _nsa_sample(raw, cmp, _feature_major(cache_b_sel_kv, kv_w), win_state, page_table, table_b, DB, T)
    win_all = jnp.concatenate([state_b_win_kv, pr['b_win_kv']], axis=1)
    new_state = (pr['a_kv'], pr['a_ik'], pr['b_cmp_kv'], pr['b_sel_kv'], win_all[:, T:])
    return o_a, o_b, new_state


PEER_SLOTS = PEER_HEADS * PEER_TOPK
ROUTE_TOKENS = 256
EXPERT_TOKENS = 256
GATHER_DEPTH = 8
D_SUB = D_MODEL // LANES
EXPERT_ROWS = 2 * D_SUB


def _take_topk(cur, pos_iota, k, payload=None):
    n = cur.shape[0]
    vals, picks = [], []
    for _ in range(k):
        m = jnp.max(cur, axis=0, keepdims=True)
        pos = jnp.min(jnp.where(cur == m, pos_iota, n), axis=0, keepdims=True)
        hit = pos_iota == pos
        vals.append(m)
        picks.append(pos if payload is None else jnp.sum(jnp.where(hit, payload, 0), axis=0, keepdims=True))
        cur = jnp.where(hit, -jnp.inf, cur)
    return vals, picks


def _peer_route_kernel(h_ref, wpq_ref, keys_ref, ids_ref, g_ref):
    tn = h_ref.shape[0]
    half = PEER_QDIM // 2
    q = jnp.dot(h_ref[...].astype(MXU), wpq_ref[...], preferred_element_type=F32).astype(MXU)
    key_iota = lax.broadcasted_iota(jnp.int32, (PEER_NKEYS, tn), 0)
    pairs = [(a, b) for a in range(PEER_TOPK) for b in range(PEER_TOPK // (a + 1))]
    n_pairs = -(-len(pairs) // 8) * 8
    pair_iota = lax.broadcasted_iota(jnp.int32, (n_pairs, tn), 0)
    pad_v = [jnp.full((n_pairs - len(pairs), tn), -jnp.inf, F32)]
    pad_i = [jnp.zeros((n_pairs - len(pairs), tn), jnp.int32)]
    id_rows, g_rows = [], []
    for hd in range(PEER_HEADS):
        vals, idxs = [], []
        for p in range(2):
            c0 = (hd * 2 + p) * half
            s = lax.dot_general(keys_ref[hd * 2 + p], q[:, c0:c0 + half], _NT, preferred_element_type=F32)
            v_rows, i_rows = _take_topk(s, key_iota, PEER_TOPK)
            vals.append(v_rows)
            idxs.append(i_rows)
        cand = jnp.concatenate([vals[0][a] + vals[1][b] for a, b in pairs] + pad_v, axis=0)
        expert = jnp.concatenate([idxs[0][a] * PEER_NKEYS + idxs[1][b] for a, b in pairs] + pad_i, axis=0)
        cv, ce = _take_topk(cand, pair_iota, PEER_TOPK, payload=expert)
        cv = jnp.concatenate(cv, axis=0)
        e = jnp.exp(cv - cv[0:1, :])
        g_rows.append(e / jnp.sum(e, axis=0, keepdims=True))
        id_rows.append(jnp.concatenate(ce, axis=0))
    ids_ref[...] = jnp.concatenate(id_rows, axis=0).astype(F32).T.astype(jnp.int32)
    g_ref[...] = jnp.concatenate(g_rows, axis=0).T


def _peer_route(h2d, w_pq, sub_keys):
    n = h2d.shape[0]
    tn = ROUTE_TOKENS
    wpq = w_pq.reshape(D_MODEL, PEER_HEADS * PEER_QDIM).astype(MXU)
    keys = sub_keys.reshape(PEER_HEADS * 2, PEER_NKEYS, PEER_QDIM // 2).astype(MXU)
    return pl.pallas_call(
        _peer_route_kernel,
        grid=(n // tn,),
        in_specs=[pl.BlockSpec((tn, D_MODEL), lambda i: (i, 0)), pl.BlockSpec(wpq.shape, lambda i: (0, 0)),
                  pl.BlockSpec(keys.shape, lambda i: (0, 0, 0))],
        out_specs=[pl.BlockSpec((tn, PEER_SLOTS), lambda i: (i, 0)), pl.BlockSpec((tn, PEER_SLOTS), lambda i: (i, 0))],
        out_shape=[jax.ShapeDtypeStruct((n, PEER_SLOTS), jnp.int32), jax.ShapeDtypeStruct((n, PEER_SLOTS), F32)],
        compiler_params=pltpu.CompilerParams(dimension_semantics=('arbitrary',), vmem_limit_bytes=VMEM_LIMIT),
        name='peer_route',
    )(h2d, wpq, keys)


def _peer_expert_kernel(ids_ref, next_ids_ref, g_ref, h_ref, lng_ref, lnb_ref, uv_hbm, y_ref, buf, sem, gt_sc,
                        out_sc):
    tb = h_ref.shape[0]
    ahead = GATHER_DEPTH - 1
    per_piece = PEER_SLOTS // EXPERT_ROWS

    step = pl.program_id(0)
    last_step = pl.num_programs(0) - 1

    def row_copy(ids, t, slot, k):
        src = pl.ds(pl.multiple_of(ids[t, k] * EXPERT_ROWS, EXPERT_ROWS), EXPERT_ROWS)
        return pltpu.make_async_copy(uv_hbm.at[src, :], buf.at[slot, :, k, :], sem.at[slot])

    def issue(ids, t, slot, k0=0, k1=PEER_SLOTS):
        for k in range(k0, k1):
            row_copy(ids, t, slot, k).start(priority=k % 2)

    def consume(t, nxt=None):
        slot = t % GATHER_DEPTH

        def issue_piece(i):
            if nxt is not None:
                issue(*nxt, i * per_piece, (i + 1) * per_piece)

        for k in range(PEER_SLOTS):
            row_copy(ids_ref, t, slot, k).wait()
        x = h_ref[t]
        part = jnp.zeros((PEER_SLOTS, LANES), F32)
        for s in range(D_SUB):
            part = part + buf[slot, s] * x[s:s + 1, :]
            issue_piece(s)
        act = jnp.sum(part, axis=1, keepdims=True)
        g_col = jnp.sum(jnp.where(lane == t, gt_sc[...], 0.0), axis=1, keepdims=True)
        coef = jnp.broadcast_to(g_col * jax.nn.gelu(act), (PEER_SLOTS, LANES))
        rows = []
        for s in range(D_SUB):
            rows.append(jnp.sum(buf[slot, D_SUB + s] * coef, axis=0, keepdims=True))
            issue_piece(D_SUB + s)
        out_sc[t] = jnp.concatenate(rows, axis=0)

    gt_sc[...] = g_ref[...].T
    lane = lax.broadcasted_iota(jnp.int32, (PEER_SLOTS, tb), 1)

    def prologue(t, carry):
        issue(ids_ref, t, t)
        return carry

    def steady(t, carry):
        consume(t, nxt=(ids_ref, t + ahead, (t + ahead) % GATHER_DEPTH))
        return carry

    def handover(t, carry):
        consume(t, nxt=(next_ids_ref, t + ahead - tb, (t + ahead) % GATHER_DEPTH))
        return carry

    def drain(t, carry):
        consume(t)
        return carry

    @pl.when(step == 0)
    def _():
        lax.fori_loop(0, ahead, prologue, 0)

    lax.fori_loop(0, tb - ahead, steady, 0)

    @pl.when(step < last_step)
    def _():
        lax.fori_loop(tb - ahead, tb, handover, 0)

    @pl.when(step == last_step)
    def _():
        lax.fori_loop(tb - ahead, tb, drain, 0)
    z = DN_ALPHA * h_ref[...] + out_sc[...]
    mean = lambda a: jnp.sum(jnp.sum(a, axis=2, keepdims=True), axis=1, keepdims=True) * (1.0 / D_MODEL)
    zc = z - mean(z)
    y_ref[...] = zc * lax.rsqrt(mean(zc * zc) + LN_EPS) * lng_ref[...] + lnb_ref[...]


def _peer_experts(h2d, ids, g, uv_rows, ln_g, ln_b):
    n = h2d.shape[0]
    tb = EXPERT_TOKENS
    steps = n // tb
    assert n % tb == 0 and tb % GATHER_DEPTH == 0
    row = lambda i: (i, 0)
    next_row = lambda i: (jnp.minimum(i + 1, steps - 1), 0)
    tok = lambda i: (i, 0, 0)
    y = pl.pallas_call(
        _peer_expert_kernel,
        grid=(steps,),
        in_specs=[pl.BlockSpec((tb, PEER_SLOTS), row, memory_space=pltpu.SMEM),
                  pl.BlockSpec((tb, PEER_SLOTS), next_row, memory_space=pltpu.SMEM),
                  pl.BlockSpec((tb, PEER_SLOTS), row),
                  pl.BlockSpec((tb, D_SUB, LANES), tok), pl.BlockSpec((D_SUB, LANES), lambda i: (0, 0)),
                  pl.BlockSpec((D_SUB, LANES), lambda i: (0, 0)), pl.BlockSpec(memory_space=pl.ANY)],
        out_specs=pl.BlockSpec((tb, D_SUB, LANES), tok),
        out_shape=jax.ShapeDtypeStruct((n, D_SUB, LANES), F32),
        scratch_shapes=[pltpu.VMEM((GATHER_DEPTH, EXPERT_ROWS, PEER_SLOTS, LANES), F32),
                        pltpu.SemaphoreType.DMA((GATHER_DEPTH,)), pltpu.VMEM((PEER_SLOTS, tb), F32),
                        pltpu.VMEM((tb, D_SUB, LANES), F32)],
        compiler_params=pltpu.CompilerParams(dimension_semantics=('arbitrary',), vmem_limit_bytes=VMEM_LIMIT),
        name='peer_experts',
    )(ids, ids, g, h2d.reshape(n, D_SUB, LANES), ln_g.reshape(D_SUB, LANES), ln_b.reshape(D_SUB, LANES), uv_rows)
    return y.reshape(n, D_MODEL)


def _group_step(x, mixer, ws, w_up_a, w_up_b, w_out, ln1_g, ln1_b, w_pq, sub_keys, uv, ln2_g, ln2_b, tm,
                feature_major_state=False):
    B, T, _ = x.shape
    x2d = x.reshape(B * T, D_MODEL)
    pr = _unpack_proj(_project(x2d, ws, tm, seq_len=T if feature_major_state else None), B, T)
    o_a, o_b, new_state = mixer(pr)
    h = _merge(x2d, o_a.reshape(B * T, -1), o_b.reshape(B * T, -1), pr['merge_gate'],
               w_up_a, w_up_b, w_out, ln1_g, ln1_b, tm)
    ids, g = _peer_route(h, w_pq, sub_keys)
    y = _peer_experts(h, ids, g, uv, ln2_g, ln2_b)
    return y.reshape(B, T, D_MODEL), new_state


def kernel(x_prompt, x_sample, cache_a_kv, cache_a_idx, cache_b_cmp_kv, cache_b_sel_kv, state_b_win_kv, page_table, w_in, rel_bias_table, cmp_pe, cmp_w1, cmp_w2, w_up_a, w_up_b, w_out, ln1_g, ln1_b, w_pq, peer_sub_keys, peer_u, peer_v, ln2_g, ln2_b):
    l = 0
    ws = _split_w_in(w_in[l])
    uv = jnp.concatenate([peer_u[l], peer_v[l]], axis=1).reshape(-1, LANES)
    tail = (w_up_a[l], w_up_b[l], w_out[l], ln1_g[l], ln1_b[l], w_pq[l], peer_sub_keys[l], uv, ln2_g[l], ln2_b[l])
    mix_p = functools.partial(_mix_prompt, rel_table=rel_bias_table, cmp_pe=cmp_pe[l], cmp_w1=cmp_w1[l],
                              cmp_w2=cmp_w2[l])
    y_p, st_p = _group_step(x_prompt, mix_p, ws, *tail, tm=256, feature_major_state=True)
    mix_s = functools.partial(_mix_sample, cache_a_kv=cache_a_kv[l], cache_a_idx=cache_a_idx[l],
                              cache_b_cmp_kv=cache_b_cmp_kv[l], cache_b_sel_kv=cache_b_sel_kv[l],
                              state_b_win_kv=state_b_win_kv[l], page_table=page_table, rel_table=rel_bias_table,
                              cmp_pe=cmp_pe[l], cmp_w1=cmp_w1[l], cmp_w2=cmp_w2[l])
    y_s, st_s = _group_step(x_sample, mix_s, ws, *tail, tm=256)
    return (y_p, y_s) + tuple(s[None] for s in st_p) + tuple(s[None] for s in st_s)
```

```python
import functools
import math

import jax
import jax.numpy as jnp
import numpy as np
from jax import lax
from jax.experimental import pallas as pl
from jax.experimental.pallas import tpu as pltpu

D_MODEL = 1024
DEPTH = 1
PAGE_SIZE = 128
HEAD_DIM = 64
A_HEADS = 8
A_KV_HEADS = 2
IDX_HEADS = 4
IDX_DIM = 64
A_TOPK = 256
B_HEADS = 8
B_KV_HEADS = 2
NSA_BLOCK = 64
NSA_TOPN = 16
NSA_FORCE = 8.0
CMP_HIDDEN = 64
WINDOW = 512
REL_BUCKETS = 32
REL_MAX_EXACT = 16
REL_MAX_DIST = 128
PEER_HEADS = 8
PEER_NKEYS = 128
PEER_QDIM = 256
PEER_TOPK = 16
Q_BLOCK = 32
TOKEN_BLOCK = 256
LN_EPS = 1e-5
DN_ALPHA = (2 * DEPTH) ** 0.25

PROJ_LAYOUT = (
    ('a_q', A_HEADS * HEAD_DIM),
    ('a_kv', 2 * A_KV_HEADS * HEAD_DIM),
    ('a_iq', IDX_HEADS * IDX_DIM),
    ('a_iw', IDX_HEADS),
    ('a_ik', IDX_DIM),
    ('b_q', B_HEADS * HEAD_DIM),
    ('b_cmp_kv', 2 * B_KV_HEADS * HEAD_DIM),
    ('b_sel_kv', 2 * B_KV_HEADS * HEAD_DIM),
    ('b_win_kv', 2 * B_KV_HEADS * HEAD_DIM),
    ('b_gate', B_HEADS * 3),
    ('merge_gate', 2 * D_MODEL),
)

LANES = 128
VMEM_LIMIT = 56 * 1024 * 1024
MISC_GATE_OFF = IDX_HEADS

F32 = jnp.float32
BF16 = jnp.bfloat16
MXU = jnp.bfloat16


_PROJ_OUT = ('a_q', 'a_kv', 'a_iq', 'a_ik', 'misc', 'b_q', 'b_cmp_kv', 'b_sel_kv', 'b_win_kv', 'merge_gate')


_STATE_OUT = ('a_kv', 'a_ik', 'b_cmp_kv', 'b_sel_kv', 'b_win_kv')


def _proj_kernel(x_ref, *refs, feature_major_state):
    n = len(_PROJ_OUT)
    w_refs, o_refs, t_refs = refs[:n], refs[n:2 * n], refs[2 * n:]
    t_refs = dict(zip(_STATE_OUT, t_refs)) if feature_major_state else {}
    x = x_ref[...].astype(BF16)
    for name, w_ref, o_ref in zip(_PROJ_OUT, w_refs, o_refs):
        r = jnp.dot(x, w_ref[...], preferred_element_type=F32)
        if name == 'misc':
            lane = lax.broadcasted_iota(jnp.int32, r.shape, 1)
            r = jnp.where(lane < MISC_GATE_OFF, r * IDX_HEADS ** -0.5, jax.nn.sigmoid(r))
        elif name == 'merge_gate':
            r = jax.nn.sigmoid(r)
        o_ref[...] = r
        if name in t_refs:
            t_refs[name][0] = r.T


def _split_w_in(w_in):
    parts = {}
    off = 0
    for name, width in PROJ_LAYOUT:
        parts[name] = w_in[:, off:off + width]
        off += width
    misc = jnp.concatenate([parts['a_iw'], parts['b_gate']], axis=1)
    parts['misc'] = jnp.pad(misc, ((0, 0), (0, LANES - misc.shape[1])))
    return [parts[name].astype(BF16) for name in _PROJ_OUT]


def _project(x2d, ws, tm, seq_len=None):
    n = x2d.shape[0]
    widths = [w.shape[1] for w in ws]
    out_specs = [pl.BlockSpec((tm, wd), lambda i: (i, 0)) for wd in widths]
    out_shape = [jax.ShapeDtypeStruct((n, wd), F32) for wd in widths]
    if seq_len is not None:
        tiles = seq_len // tm
        for name in _STATE_OUT:
            wd = widths[_PROJ_OUT.index(name)]
            out_specs.append(pl.BlockSpec((1, wd, tm), lambda i: (i // tiles, 0, i % tiles)))
            out_shape.append(jax.ShapeDtypeStruct((n // seq_len, wd, seq_len), F32))
    return pl.pallas_call(
        functools.partial(_proj_kernel, feature_major_state=seq_len is not None),
        grid=(n // tm,),
        in_specs=[pl.BlockSpec((tm, D_MODEL), lambda i: (i, 0))]
        + [pl.BlockSpec((D_MODEL, wd), lambda i: (0, 0)) for wd in widths],
        out_specs=out_specs,
        out_shape=out_shape,
        compiler_params=pltpu.CompilerParams(dimension_semantics=('arbitrary',), vmem_limit_bytes=VMEM_LIMIT),
        name='project',
    )(x2d, *ws)


def _unpack_proj(outs, B, T):
    d = dict(zip(_PROJ_OUT, outs))
    d['state_t'] = dict(zip(_STATE_OUT, outs[len(_PROJ_OUT):]))
    misc = d['misc']
    kv = (B, T, 2, A_KV_HEADS, HEAD_DIM)
    return {
        'a_q': d['a_q'].reshape(B, T, A_HEADS, HEAD_DIM),
        'a_kv': d['a_kv'].reshape(kv),
        'a_iq': d['a_iq'].reshape(B, T, IDX_HEADS, IDX_DIM),
        'a_iw': misc[:, :IDX_HEADS].reshape(B, T, IDX_HEADS),
        'a_ik': d['a_ik'].reshape(B, T, IDX_DIM),
        'b_q': d['b_q'].reshape(B, T, B_HEADS, HEAD_DIM),
        'b_cmp_kv': d['b_cmp_kv'].reshape(kv),
        'b_sel_kv': d['b_sel_kv'].reshape(kv),
        'b_win_kv': d['b_win_kv'].reshape(kv),
        'b_gate': misc[:, MISC_GATE_OFF:MISC_GATE_OFF + 3 * B_HEADS].reshape(B, T, B_HEADS, 3),
        'merge_gate': d['merge_gate'],
        'raw': d,
    }


def _layer_norm(x, g, b):
    mu = jnp.mean(x, axis=-1, keepdims=True)
    xc = x - mu
    var = jnp.mean(xc * xc, axis=-1, keepdims=True)
    return xc * lax.rsqrt(var + LN_EPS) * g + b


def _merge_kernel(x_ref, oa_ref, ob_ref, mg_ref, wa_ref, wb_ref, wo_ref, g_ref, b_ref, h_ref):
    ya = jnp.dot(oa_ref[...].astype(BF16), wa_ref[...], preferred_element_type=F32)
    yb = jnp.dot(ob_ref[...].astype(BF16), wb_ref[...], preferred_element_type=F32)
    mg = mg_ref[...]
    mix = mg[:, :D_MODEL] * ya + mg[:, D_MODEL:] * yb
    mixed = jnp.dot(mix.astype(BF16), wo_ref[...], preferred_element_type=F32)
    h_ref[...] = _layer_norm(DN_ALPHA * x_ref[...] + mixed, g_ref[...], b_ref[...])


def _merge(x2d, o_a, o_b, mg, w_up_a, w_up_b, w_out, ln_g, ln_b, tm):
    n = x2d.shape[0]
    ca, cb = o_a.shape[1], o_b.shape[1]
    row = lambda i: (i, 0)
    fixed = lambda i: (0, 0)
    return pl.pallas_call(
        _merge_kernel,
        grid=(n // tm,),
        in_specs=[pl.BlockSpec((tm, D_MODEL), row), pl.BlockSpec((tm, ca), row), pl.BlockSpec((tm, cb), row),
                  pl.BlockSpec((tm, 2 * D_MODEL), row), pl.BlockSpec((ca, D_MODEL), fixed),
                  pl.BlockSpec((cb, D_MODEL), fixed), pl.BlockSpec((D_MODEL, D_MODEL), fixed),
                  pl.BlockSpec((1, D_MODEL), fixed), pl.BlockSpec((1, D_MODEL), fixed)],
        out_specs=pl.BlockSpec((tm, D_MODEL), row),
        out_shape=jax.ShapeDtypeStruct((n, D_MODEL), F32),
        compiler_params=pltpu.CompilerParams(dimension_semantics=('arbitrary',), vmem_limit_bytes=VMEM_LIMIT),
        name='merge',
    )(x2d, o_a, o_b, mg, w_up_a.astype(BF16), w_up_b.astype(BF16), w_out.astype(BF16),
      ln_g.reshape(1, D_MODEL), ln_b.reshape(1, D_MODEL))


TQ = 256
NEG = -1e30
INT_MIN = -2 ** 31
COUNT_ROWS = 256
_NT = (((1,), (1,)), ((), ()))
_TN = (((0,), (0,)), ((), ()))


def _bucket_np(d):
    n = np.maximum(d, 0)
    nf = np.maximum(n, 1).astype(np.float64)
    large = REL_MAX_EXACT + (np.log(nf / REL_MAX_EXACT) / math.log(REL_MAX_DIST / REL_MAX_EXACT)
                             * (REL_BUCKETS - REL_MAX_EXACT)).astype(np.int64)
    return np.where(n < REL_MAX_EXACT, n, np.minimum(large, REL_BUCKETS - 1)).astype(np.int32)


def _rel_bias_at(table, d):
    bucket = jnp.asarray(_bucket_np(d))[..., None]
    out = jnp.broadcast_to(table[REL_BUCKETS - 1], bucket.shape[:-1] + table.shape[1:])
    for b in range(REL_BUCKETS - 1):
        out = jnp.where(bucket == b, table[b], out)
    return out


def _band_bias(table, n_groups):
    d = TQ + np.arange(TQ)[None, :] - np.arange(2 * TQ)[:, None]
    band = _rel_bias_at(table, d) - table[REL_BUCKETS - 1]
    band = jnp.moveaxis(band, -1, 1)
    return jnp.moveaxis(band.reshape(2 * TQ, n_groups, -1), 1, 0)


def _block_bias(table, S):
    nb = S // NSA_BLOCK
    d = np.arange(S)[None, :] - ((np.arange(nb) + 1) * NSA_BLOCK - 1)[:, None]
    return jnp.moveaxis(_rel_bias_at(table, d), -1, 0)


def _ordered_keys(x):
    x = jnp.where(x == 0.0, 0.0, x)
    b = lax.bitcast_convert_type(x, jnp.int32)
    return b ^ ((b >> 31) & jnp.int32(0x7FFFFFFF))


def _softmax_reset(m_sc, l_sc, acc_sc):
    m_sc[...] = jnp.full(m_sc.shape, NEG, F32)
    l_sc[...] = jnp.zeros(l_sc.shape, F32)
    acc_sc[...] = jnp.zeros(acc_sc.shape, F32)


def _group_queries(q, n_groups):
    rep = q.shape[1] // HEAD_DIM // n_groups
    return [_stack_heads(q, range(g * rep, (g + 1) * rep), HEAD_DIM) for g in range(n_groups)]


def _attend_chunk(kv_c, q_groups, madd_of_group, band_ref, band_row0, m_sc, l_sc, acc_sc):
    n_g = len(q_groups)
    rep = q_groups[0].shape[0] // TQ
    for g in range(n_g):
        k_c = kv_c[:, g * HEAD_DIM:(g + 1) * HEAD_DIM]
        v_c = kv_c[:, (n_g + g) * HEAD_DIM:(n_g + g + 1) * HEAD_DIM]
        add = jnp.concatenate([madd_of_group(g)] * rep, axis=1)
        if band_row0 is not None:
            add = add + band_ref[g, band_row0:band_row0 + TQ, :]
        logits = lax.dot_general(k_c, q_groups[g], _NT, preferred_element_type=F32) + add
        m_old = m_sc[g]
        m_new = jnp.maximum(m_old, jnp.max(logits, axis=0, keepdims=True))
        alpha = jnp.exp(m_old - m_new)
        p = jnp.exp(logits - m_new)
        l_sc[g] = alpha * l_sc[g] + jnp.sum(p, axis=0, keepdims=True)
        m_sc[g] = m_new
        p = p.astype(MXU)
        for r in range(rep):
            h = g * rep + r
            lanes = slice(r * TQ, (r + 1) * TQ)
            pv = lax.dot_general(v_c, p[:, lanes], _TN, preferred_element_type=F32)
            acc_sc[h] = alpha[:, lanes] * acc_sc[h] + pv


def _softmax_scratch(n_heads, n_groups):
    stats = pltpu.VMEM((n_groups, 1, n_heads // n_groups * TQ), F32)
    return [stats, stats, pltpu.VMEM((n_heads, HEAD_DIM, TQ), F32)]


def _attend_result(h, rep, l_sc, acc_sc):
    g, r = divmod(h, rep)
    return acc_sc[h] / l_sc[g][:, r * TQ:(r + 1) * TQ]


def _dsa_kernel(iq_ref, misc_ref, q_ref, ik_ref, kv_ref, band_ref, o_ref, key_sc, m_sc, l_sc, acc_sc, out_sc,
                *, k_sel, pos_bits):
    i = pl.program_id(1)
    t0 = i * TQ
    n_chunks = i + 1
    lane_t = t0 + lax.broadcasted_iota(jnp.int32, (1, TQ), 1)
    sub_iota = lax.broadcasted_iota(jnp.int32, (TQ, TQ), 0)
    misc_t = misc_ref[...].T
    iq = iq_ref[...].astype(MXU)
    iq_heads = [iq[:, h * IDX_DIM:(h + 1) * IDX_DIM] for h in range(IDX_HEADS)]

    def score_chunk(j, carry):
        r0 = pl.multiple_of(j * TQ, TQ)
        ik_c = ik_ref[pl.ds(r0, TQ), :].astype(MXU)
        sc = jnp.zeros((TQ, TQ), F32)
        for h in range(IDX_HEADS):
            rel = lax.dot_general(ik_c, iq_heads[h], _NT, preferred_element_type=F32)
            sc = sc + jnp.maximum(rel * IDX_DIM ** -0.5, 0.0) * misc_t[h:h + 1, :]
        sc = jnp.where(r0 + sub_iota <= lane_t, sc, -jnp.inf)
        key_sc[pl.ds(r0, TQ), :] = _ordered_keys(sc)
        return carry

    lax.fori_loop(0, n_chunks, score_chunk, 0)

    def count(pred):
        def body(jj, acc):
            r0 = pl.multiple_of(jj * COUNT_ROWS, COUNT_ROWS)
            blk = key_sc[pl.ds(r0, COUNT_ROWS), :]
            pos = r0 + lax.broadcasted_iota(jnp.int32, (COUNT_ROWS, TQ), 0)
            hit = jnp.where(pred(blk, pos), 1, 0)
            return acc + jnp.sum(hit.reshape(COUNT_ROWS // 8, 8, TQ), axis=0)

        acc = lax.fori_loop(0, n_chunks * (TQ // COUNT_ROWS), body, jnp.zeros((8, TQ), jnp.int32))
        return jnp.sum(acc, axis=0, keepdims=True)

    v = jnp.full((1, TQ), INT_MIN, jnp.int32)
    v = jnp.where(count(lambda blk, pos: blk >= 0) >= k_sel, 0, v)

    def value_bit(it, v):
        cand = v | jnp.left_shift(jnp.int32(1), 30 - it)
        return jnp.where(count(lambda blk, pos: blk >= cand) >= k_sel, cand, v)

    v = lax.fori_loop(0, 31, value_bit, v)
    need = k_sel - count(lambda blk, pos: blk > v)
    n_ge = count(lambda blk, pos: blk >= v)

    def tie_search():
        def pos_bit(it, jm):
            cand = jm | jnp.left_shift(jnp.int32(1), pos_bits - 1 - it)
            return jnp.where(count(lambda blk, pos: (blk == v) & (pos < cand)) < need, cand, jm)

        return lax.fori_loop(0, pos_bits, pos_bit, jnp.zeros((1, TQ), jnp.int32))

    j_max = lax.cond(jnp.max(n_ge) > k_sel, tie_search, lambda: jnp.full((1, TQ), 2 ** 30, jnp.int32))

    q = (q_ref[...] * HEAD_DIM ** -0.5).astype(MXU)
    q_groups = _group_queries(q, A_KV_HEADS)
    _softmax_reset(m_sc, l_sc, acc_sc)

    def chunk(j, band_row0, causal):
        r0 = pl.multiple_of(j * TQ, TQ)
        key = key_sc[pl.ds(r0, TQ), :]
        pos = r0 + sub_iota
        sel = (key > v) | ((key == v) & (pos <= j_max))
        if causal:
            sel = sel & (pos <= lane_t)
        madd = jnp.where(sel, 0.0, NEG)
        kv_c = kv_ref[pl.ds(r0, TQ), :].astype(MXU)
        _attend_chunk(kv_c, q_groups, lambda g: madd, band_ref, band_row0, m_sc, l_sc, acc_sc)

    def far_chunk(j, carry):
        chunk(j, None, False)
        return carry

    lax.fori_loop(0, jnp.maximum(i - 1, 0), far_chunk, 0)

    @pl.when(i >= 1)
    def _():
        chunk(i - 1, 0, False)

    chunk(i, TQ, True)
    for h in range(A_HEADS):
        out_sc[h * HEAD_DIM:(h + 1) * HEAD_DIM, :] = _attend_result(h, A_HEADS // A_KV_HEADS, l_sc, acc_sc)
    o_ref[...] = out_sc[...].T


def _dsa_prompt(raw, band_a, B, S):
    nq = S // TQ
    k_sel = min(A_TOPK, S // 4)
    assert S % TQ == 0 and TQ >= k_sel
    tile = lambda b, i: (b * nq + i, 0)
    seq = lambda b, i: (b, 0)
    kern = functools.partial(_dsa_kernel, k_sel=k_sel, pos_bits=(S - 1).bit_length())
    return pl.pallas_call(
        kern,
        grid=(B, nq),
        in_specs=[pl.BlockSpec((TQ, IDX_HEADS * IDX_DIM), tile), pl.BlockSpec((TQ, LANES), tile),
                  pl.BlockSpec((TQ, A_HEADS * HEAD_DIM), tile), pl.BlockSpec((S, IDX_DIM), seq),
                  pl.BlockSpec((S, 2 * A_KV_HEADS * HEAD_DIM), seq),
                  pl.BlockSpec(band_a.shape, lambda b, i: (0, 0, 0))],
        out_specs=pl.BlockSpec((TQ, A_HEADS * HEAD_DIM), tile),
        out_shape=jax.ShapeDtypeStruct((B * S, A_HEADS * HEAD_DIM), F32),
        scratch_shapes=[pltpu.VMEM((S, TQ), jnp.int32)] + _softmax_scratch(A_HEADS, A_KV_HEADS)
        + [pltpu.VMEM((A_HEADS * HEAD_DIM, TQ), F32)],
        compiler_params=pltpu.CompilerParams(dimension_semantics=('arbitrary', 'arbitrary'),
                                             vmem_limit_bytes=VMEM_LIMIT),
        name='dsa_prompt',
    )(raw['a_iq'], raw['misc'], raw['a_q'], raw['a_ik'], raw['a_kv'], band_a)


def _compress_kernel(x_ref, pe_ref, w1_ref, w2_ref, o_ref):
    x = (x_ref[...] + pe_ref[...]).astype(MXU)
    h = jax.nn.gelu(jnp.dot(x, w1_ref[...], preferred_element_type=F32))
    o_ref[...] = jnp.dot(h.astype(MXU), w2_ref[...], preferred_element_type=F32)


def _compress_weights(pe, w1, w2):
    eye_c = jnp.eye(2, dtype=F32)
    eye_g = jnp.eye(B_KV_HEADS, dtype=F32)
    w1_big = jnp.einsum('lcde,cC,gG->lcgdCGe', w1, eye_c, eye_g)
    w1_big = w1_big.reshape(NSA_BLOCK * 2 * B_KV_HEADS * HEAD_DIM, 2 * B_KV_HEADS * CMP_HIDDEN)
    w2_big = jnp.einsum('cef,cC,gG->cgeCGf', w2, eye_c, eye_g)
    w2_big = w2_big.reshape(2 * B_KV_HEADS * CMP_HIDDEN, 2 * B_KV_HEADS * HEAD_DIM)
    pe_flat = jnp.broadcast_to(pe[:, :, None, :], (NSA_BLOCK, 2, B_KV_HEADS, HEAD_DIM)).reshape(1, -1)
    return pe_flat, w1_big.astype(MXU), w2_big.astype(MXU)


def _compress(blocks2d, pe_flat, w1_big, w2_big, tm):
    n, width = blocks2d.shape
    fixed = lambda i: (0, 0)
    return pl.pallas_call(
        _compress_kernel,
        grid=(n // tm,),
        in_specs=[pl.BlockSpec((tm, width), lambda i: (i, 0)), pl.BlockSpec((1, width), fixed),
                  pl.BlockSpec(w1_big.shape, fixed), pl.BlockSpec(w2_big.shape, fixed)],
        out_specs=pl.BlockSpec((tm, w2_big.shape[1]), lambda i: (i, 0)),
        out_shape=jax.ShapeDtypeStruct((n, w2_big.shape[1]), F32),
        compiler_params=pltpu.CompilerParams(dimension_semantics=('arbitrary',), vmem_limit_bytes=VMEM_LIMIT),
        name='nsa_compress',
    )(blocks2d, pe_flat, w1_big, w2_big)


def _nsa_kernel(q_ref, misc_ref, cmp_ref, selkv_ref, winkv_ref, band_ref, biasc_ref, o_ref,
                sel_sc, m_sc, l_sc, acc_sc, out_sc, *, nb, n_sel):
    i = pl.program_id(1)
    t0 = i * TQ
    n_g = B_KV_HEADS
    rep = B_HEADS // n_g
    lane_t = t0 + lax.broadcasted_iota(jnp.int32, (1, TQ), 1)
    sub_iota = lax.broadcasted_iota(jnp.int32, (TQ, TQ), 0)
    lane_iota = lax.broadcasted_iota(jnp.int32, (TQ, TQ), 1)
    misc_t = misc_ref[...].T
    gate = lambda h, k: misc_t[MISC_GATE_OFF + 3 * h + k:MISC_GATE_OFF + 3 * h + k + 1, :]
    q = (q_ref[...] * HEAD_DIM ** -0.5).astype(MXU)
    q_heads = [q[:, h * HEAD_DIM:(h + 1) * HEAD_DIM] for h in range(B_HEADS)]
    q_groups = _group_queries(q, n_g)

    cmp = cmp_ref[...].astype(MXU)
    blk = lax.broadcasted_iota(jnp.int32, (nb, TQ), 0)
    visible = (blk + 1) * NSA_BLOCK - 1 <= lane_t
    cur = lax.shift_right_logical(lane_t, NSA_BLOCK.bit_length() - 1)
    forced = (blk == 0) | (blk == cur) | (blk == cur - 1)
    for g in range(n_g):
        k_c = cmp[:, g * HEAD_DIM:(g + 1) * HEAD_DIM]
        v_c = cmp[:, (n_g + g) * HEAD_DIM:(n_g + g + 1) * HEAD_DIM]
        importance = jnp.zeros((nb, TQ), F32)
        for r in range(rep):
            h = g * rep + r
            lc = lax.dot_general(k_c, q_heads[h], _NT, preferred_element_type=F32) + biasc_ref[h]
            lc = jnp.where(visible, lc, -jnp.inf)
            m = jnp.max(lc, axis=0, keepdims=True)
            m = jnp.where(m > -jnp.inf, m, 0.0)
            e = jnp.exp(lc - m)
            s = jnp.sum(e, axis=0, keepdims=True)
            p = e / jnp.where(s > 0, s, 1.0)
            importance = importance + p
            o_cmp = lax.dot_general(v_c, p.astype(MXU), _TN, preferred_element_type=F32)
            out_sc[h * HEAD_DIM:(h + 1) * HEAD_DIM, :] = gate(h, 0) * o_cmp
        score = jnp.where(forced, NSA_FORCE, importance)
        score = jnp.where(blk <= cur, score, -1.0)
        rank = jnp.zeros((nb, TQ), jnp.int32)
        for n in range(nb):
            row = score[n:n + 1, :]
            beats = (row > score) | ((row == score) & (blk > n))
            rank = rank + jnp.where(beats, 1, 0)
        sel_sc[g] = jnp.where(rank < n_sel, 0.0, NEG)

    def finish_branch(k):
        for h in range(B_HEADS):
            rows = slice(h * HEAD_DIM, (h + 1) * HEAD_DIM)
            out_sc[rows, :] = out_sc[rows, :] + gate(h, k) * _attend_result(h, rep, l_sc, acc_sc)

    _softmax_reset(m_sc, l_sc, acc_sc)
    blocks_per_chunk = TQ // NSA_BLOCK

    def sel_chunk(j, band_row0, causal):
        r0 = pl.multiple_of(j * TQ, TQ)
        kv_c = selkv_ref[pl.ds(r0, TQ), :].astype(MXU)

        def madd_of_group(g):
            rows = [sel_sc[g, pl.ds(j * blocks_per_chunk + b, 1), :] for b in range(blocks_per_chunk)]
            madd = jnp.concatenate([jnp.broadcast_to(row, (NSA_BLOCK, TQ)) for row in rows], axis=0)
            if causal:
                madd = jnp.where(sub_iota <= lane_iota, madd, NEG)
            return madd

        _attend_chunk(kv_c, q_groups, madd_of_group, band_ref, band_row0, m_sc, l_sc, acc_sc)

    def far_chunk(j, carry):
        sel_chunk(j, None, False)
        return carry

    lax.fori_loop(0, jnp.maximum(i - 1, 0), far_chunk, 0)

    @pl.when(i >= 1)
    def _():
        sel_chunk(i - 1, 0, False)

    sel_chunk(i, TQ, True)
    finish_branch(1)

    _softmax_reset(m_sc, l_sc, acc_sc)
    for back in range(WINDOW // TQ, -1, -1):
        dist = back * TQ + lane_iota - sub_iota
        madd = jnp.where((dist >= 0) & (dist < WINDOW), 0.0, NEG)
        band_row0 = {0: TQ, 1: 0}.get(back)

        def win_chunk(back=back, madd=madd, band_row0=band_row0):
            r0 = pl.multiple_of((i - back) * TQ, TQ)
            kv_c = winkv_ref[pl.ds(r0, TQ), :].astype(MXU)
            _attend_chunk(kv_c, q_groups, lambda g: madd, band_ref, band_row0, m_sc, l_sc, acc_sc)

        if back == 0:
            win_chunk()
        else:
            pl.when(i >= back)(win_chunk)
    finish_branch(2)
    o_ref[...] = out_sc[...].T


def _nsa_prompt(raw, cmp, band_b, bias_c, B, S):
    nq = S // TQ
    nb = S // NSA_BLOCK
    assert S % TQ == 0 and TQ % NSA_BLOCK == 0
    tile = lambda b, i: (b * nq + i, 0)
    seq = lambda b, i: (b, 0)
    kv_w = 2 * B_KV_HEADS * HEAD_DIM
    kern = functools.partial(_nsa_kernel, nb=nb, n_sel=min(NSA_TOPN, nb))
    return pl.pallas_call(
        kern,
        grid=(B, nq),
        in_specs=[pl.BlockSpec((TQ, B_HEADS * HEAD_DIM), tile), pl.BlockSpec((TQ, LANES), tile),
                  pl.BlockSpec((nb, kv_w), seq), pl.BlockSpec((S, kv_w), seq), pl.BlockSpec((S, kv_w), seq),
                  pl.BlockSpec(band_b.shape, lambda b, i: (0, 0, 0)),
                  pl.BlockSpec((B_HEADS, nb, TQ), lambda b, i: (0, 0, i))],
        out_specs=pl.BlockSpec((TQ, B_HEADS * HEAD_DIM), tile),
        out_shape=jax.ShapeDtypeStruct((B * S, B_HEADS * HEAD_DIM), F32),
        scratch_shapes=[pltpu.VMEM((B_KV_HEADS, nb, TQ), F32)] + _softmax_scratch(B_HEADS, B_KV_HEADS)
        + [pltpu.VMEM((B_HEADS * HEAD_DIM, TQ), F32)],
        compiler_params=pltpu.CompilerParams(dimension_semantics=('arbitrary', 'arbitrary'),
                                             vmem_limit_bytes=VMEM_LIMIT),
        name='nsa_prompt',
    )(raw['b_q'], raw['misc'], cmp, raw['b_sel_kv'], raw['b_win_kv'], band_b, bias_c)


def _mix_prompt(pr, rel_table, cmp_pe, cmp_w1, cmp_w2):
    B, S = pr['a_q'].shape[:2]
    raw = pr['raw']
    table_a, table_b = rel_table[:, :A_HEADS], rel_table[:, A_HEADS:]
    pe_flat, w1_big, w2_big = _compress_weights(cmp_pe, cmp_w1, cmp_w2)
    n_blocks = B * S // NSA_BLOCK
    cmp = _compress(raw['b_cmp_kv'].reshape(n_blocks, -1), pe_flat, w1_big, w2_big, tm=min(128, n_blocks))
    o_a = _dsa_prompt(raw, _band_bias(table_a, A_KV_HEADS), B, S)
    o_b = _nsa_prompt(raw, cmp, _band_bias(table_b, B_KV_HEADS), _block_bias(table_b, S), B, S)
    wb = min(WINDOW, S)
    kv_tail = (2, B_KV_HEADS, HEAD_DIM)

    def rows(name, tail, first=0):
        part = raw['state_t'][name][:, :, first:]
        return jnp.moveaxis(part.reshape((B,) + tail + (S - first,)), -1, 1)

    new_state = (rows('a_kv', kv_tail), rows('a_ik', (IDX_DIM,)), rows('b_cmp_kv', kv_tail),
                 rows('b_sel_kv', kv_tail), rows('b_win_kv', kv_tail, S - wb))
    return o_a, o_b, new_state


PAGES_PER_STEP = 32
KEY_PAD = 128


def _paged_specs(width):
    def spec(j):
        return pl.BlockSpec((1, width, PAGE_SIZE), lambda s, p, pt: (pt[s, p * PAGES_PER_STEP + j], 0, 0))
    return [spec(j) for j in range(PAGES_PER_STEP)]


def _feature_major(pool, width):
    return jnp.swapaxes(pool.reshape(pool.shape[0], PAGE_SIZE, width), 1, 2)


def _stage_pages(page_refs, dst_sc):
    p = pl.program_id(1)
    for j, ref in enumerate(page_refs):
        r0 = pl.multiple_of((p * PAGES_PER_STEP + j) * PAGE_SIZE, PAGE_SIZE)
        dst_sc[pl.ds(r0, PAGE_SIZE), :] = ref[0].T


def _stage_new_rows(new_ref, dst_sc, past):
    t = new_ref.shape[0]
    dst_sc[past:past + t, :] = new_ref[...]
    dst_sc[past + t:, :] = jnp.zeros((dst_sc.shape[0] - past - t, dst_sc.shape[1]), F32)


def _rows_softmax_attend(q_rows, k, v, bias, madd):
    logits = lax.dot_general(q_rows, k, _NT, preferred_element_type=F32) + bias + madd
    m = jnp.max(logits, axis=1, keepdims=True)
    e = jnp.exp(logits - m)
    p = e / jnp.sum(e, axis=1, keepdims=True)
    return jnp.dot(p.astype(MXU), v, preferred_element_type=F32)


def _stack_heads(x, heads, width):
    return jnp.concatenate([x[:, h * width:(h + 1) * width] for h in heads], axis=0)


def _dsa_sample_kernel(pt_ref, *refs, past, k_sel):
    n = PAGES_PER_STEP
    idx_pages, kv_pages = refs[:n], refs[n:2 * n]
    ikn_ref, kvn_ref, iq_ref, misc_ref, q_ref, bias_ref, o_ref, ik_sc, kv_sc = refs[2 * n:]
    _stage_pages(idx_pages, ik_sc)
    _stage_pages(kv_pages, kv_sc)

    @pl.when(pl.program_id(1) == pl.num_programs(1) - 1)
    def _():
        T = iq_ref.shape[0]
        L = ik_sc.shape[0]
        _stage_new_rows(ikn_ref, ik_sc, past)
        _stage_new_rows(kvn_ref, kv_sc, past)
        pos = lax.broadcasted_iota(jnp.int32, (T, L), 1)
        valid = pos <= past + lax.broadcasted_iota(jnp.int32, (T, L), 0)
        misc = misc_ref[...]
        iq = _stack_heads(iq_ref[...].astype(MXU), range(IDX_HEADS), IDX_DIM)
        rel = lax.dot_general(iq, ik_sc[...].astype(MXU), _NT, preferred_element_type=F32)
        sc = jnp.zeros((T, L), F32)
        for h in range(IDX_HEADS):
            sc = sc + jnp.maximum(rel[h * T:(h + 1) * T] * IDX_DIM ** -0.5, 0.0) * misc[:, h:h + 1]
        key = _ordered_keys(jnp.where(valid, sc, -jnp.inf))

        def count(hit):
            return jnp.sum(jnp.where(hit, 1, 0), axis=1, keepdims=True)

        v = jnp.full((T, 1), INT_MIN, jnp.int32)
        v = jnp.where(count(key >= 0) >= k_sel, 0, v)

        def value_bit(it, v):
            cand = v | jnp.left_shift(jnp.int32(1), 30 - it)
            return jnp.where(count(key >= cand) >= k_sel, cand, v)

        v = lax.fori_loop(0, 31, value_bit, v)
        need = k_sel - count(key > v)
        pos_bits = (L - 1).bit_length()

        def tie_search():
            def pos_bit(it, jm):
                cand = jm | jnp.left_shift(jnp.int32(1), pos_bits - 1 - it)
                return jnp.where(count((key == v) & (pos < cand)) < need, cand, jm)

            return lax.fori_loop(0, pos_bits, pos_bit, jnp.zeros((T, 1), jnp.int32))

        j_max = lax.cond(jnp.max(count(key >= v)) > k_sel, tie_search,
                         lambda: jnp.full((T, 1), 2 ** 30, jnp.int32))
        sel = ((key > v) | ((key == v) & (pos <= j_max))) & valid
        madd = jnp.where(sel, 0.0, NEG)
        rep = A_HEADS // A_KV_HEADS
        madd = jnp.concatenate([madd] * rep, axis=0)
        q = (q_ref[...] * HEAD_DIM ** -0.5).astype(MXU)
        kv = kv_sc[...].astype(MXU)
        for g in range(A_KV_HEADS):
            q_rows = _stack_heads(q, range(g * rep, (g + 1) * rep), HEAD_DIM)
            o = _rows_softmax_attend(q_rows, kv[:, g * HEAD_DIM:(g + 1) * HEAD_DIM],
                                     kv[:, (A_KV_HEADS + g) * HEAD_DIM:(A_KV_HEADS + g + 1) * HEAD_DIM],
                                     bias_ref[g], madd)
            for r in range(rep):
                h = g * rep + r
                o_ref[:, h * HEAD_DIM:(h + 1) * HEAD_DIM] = o[r * T:(r + 1) * T, :]


def _sample_bias(table, T, past, n_keys, key_pos0, n_groups):
    d = (past + np.arange(T))[:, None] - (key_pos0 + np.arange(n_keys))[None, :]
    b = _rel_bias_at(table, d)
    b = jnp.moveaxis(b, -1, 0)
    return b.reshape(n_groups, -1, n_keys)


def _seq_rows(width, T):
    return pl.BlockSpec((T, width), lambda s, p, pt: (s, 0))


def _dsa_sample(raw, pool_idx, pool_kv, page_table, table_a, DB, T):
    n_pages = page_table.shape[1]
    past = n_pages * PAGE_SIZE
    L = past + KEY_PAD
    assert n_pages % PAGES_PER_STEP == 0 and T <= KEY_PAD
    kv_w = 2 * A_KV_HEADS * HEAD_DIM
    bias = _sample_bias(table_a, T, past, L, 0, A_KV_HEADS)
    fixed3 = lambda s, p, pt: (0, 0, 0)
    grid_spec = pltpu.PrefetchScalarGridSpec(
        num_scalar_prefetch=1,
        grid=(DB, n_pages // PAGES_PER_STEP),
        in_specs=_paged_specs(IDX_DIM) + _paged_specs(kv_w)
        + [_seq_rows(IDX_DIM, T), _seq_rows(kv_w, T), _seq_rows(IDX_HEADS * IDX_DIM, T), _seq_rows(LANES, T),
           _seq_rows(A_HEADS * HEAD_DIM, T), pl.BlockSpec(bias.shape, fixed3)],
        out_specs=_seq_rows(A_HEADS * HEAD_DIM, T),
        scratch_shapes=[pltpu.VMEM((L, IDX_DIM), F32), pltpu.VMEM((L, kv_w), F32)],
    )
    kern = functools.partial(_dsa_sample_kernel, past=past, k_sel=min(A_TOPK, (past + T) // 4))
    return pl.pallas_call(
        kern, grid_spec=grid_spec,
        out_shape=jax.ShapeDtypeStruct((DB * T, A_HEADS * HEAD_DIM), F32),
        compiler_params=pltpu.CompilerParams(dimension_semantics=('arbitrary', 'arbitrary'),
                                             vmem_limit_bytes=VMEM_LIMIT),
        name='dsa_sample',
    )(page_table, *([pool_idx] * PAGES_PER_STEP), *([pool_kv] * PAGES_PER_STEP),
      raw['a_ik'], raw['a_kv'], raw['a_iq'], raw['misc'], raw['a_q'], bias)


def _compress_sample_kernel(pt_ref, *refs, past):
    n = PAGES_PER_STEP
    pages = refs[:n]
    new_ref, pe_ref, w1_ref, w2_ref, o_ref, xk_sc, xv_sc = refs[n:]
    halves = (xk_sc, xv_sc)
    p = pl.program_id(1)
    for j, ref in enumerate(pages):
        r0 = pl.multiple_of((p * PAGES_PER_STEP + j) * PAGE_SIZE, PAGE_SIZE)
        for c, x_sc in enumerate(halves):
            x_sc[pl.ds(r0, PAGE_SIZE), :] = ref[0, c * LANES:(c + 1) * LANES, :].T

    @pl.when(p == pl.num_programs(1) - 1)
    def _():
        nbp = o_ref.shape[1]
        t = new_ref.shape[0]
        for c, x_sc in enumerate(halves):
            x_sc[past:past + t, :] = new_ref[:, c * LANES:(c + 1) * LANES]
            x_sc[past + t:, :] = jnp.zeros((x_sc.shape[0] - past - t, LANES), F32)

        def body(l, accs):
            out = []
            for c, (x_sc, acc) in enumerate(zip(halves, accs)):
                xl = x_sc[pl.ds(l, nbp, stride=NSA_BLOCK), :] + pe_ref[c, pl.ds(l, 1), :]
                out.append(acc + jnp.dot(xl.astype(MXU), w1_ref[l, c], preferred_element_type=F32))
            return tuple(out)

        zero = jnp.zeros((nbp, LANES), F32)
        h = jnp.concatenate(lax.fori_loop(0, NSA_BLOCK, body, (zero, zero), unroll=16), axis=1)
        o_ref[0] = jnp.dot(jax.nn.gelu(h).astype(MXU), w2_ref[...], preferred_element_type=F32)


def _padded_blocks(past, T):
    nb = -(-(past + T) // NSA_BLOCK)
    return nb, -(-nb // 8) * 8


def _compress_sample(raw, pool_cmp, page_table, pe_flat, w1_big, w2_big, DB, T):
    n_pages = page_table.shape[1]
    past = n_pages * PAGE_SIZE
    _, nbp = _padded_blocks(past, T)
    kv_w = 2 * B_KV_HEADS * HEAD_DIM
    assert kv_w == 2 * LANES and w1_big.shape[1] == 2 * LANES
    w1_l = w1_big.reshape(NSA_BLOCK, 2, LANES, 2, LANES)
    w1_halves = jnp.stack([w1_l[:, 0, :, 0, :], w1_l[:, 1, :, 1, :]], axis=1)
    fixed2 = lambda s, p, pt: (0, 0)
    grid_spec = pltpu.PrefetchScalarGridSpec(
        num_scalar_prefetch=1,
        grid=(DB, n_pages // PAGES_PER_STEP),
        in_specs=_paged_specs(kv_w) + [_seq_rows(kv_w, T),
                                       pl.BlockSpec((2, NSA_BLOCK, LANES), lambda s, p, pt: (0, 0, 0)),
                                       pl.BlockSpec(w1_halves.shape, lambda s, p, pt: (0, 0, 0, 0)),
                                       pl.BlockSpec(w2_big.shape, fixed2)],
        out_specs=pl.BlockSpec((1, nbp, kv_w), lambda s, p, pt: (s, 0, 0)),
        scratch_shapes=[pltpu.VMEM((nbp * NSA_BLOCK, LANES), F32), pltpu.VMEM((nbp * NSA_BLOCK, LANES), F32)],
    )
    return pl.pallas_call(
        functools.partial(_compress_sample_kernel, past=past), grid_spec=grid_spec,
        out_shape=jax.ShapeDtypeStruct((DB, nbp, kv_w), F32),
        compiler_params=pltpu.CompilerParams(dimension_semantics=('arbitrary', 'arbitrary'),
                                             vmem_limit_bytes=VMEM_LIMIT),
        name='nsa_compress_sample',
    )(page_table, *([pool_cmp] * PAGES_PER_STEP), raw['b_cmp_kv'],
      pe_flat.reshape(NSA_BLOCK, 2, LANES).swapaxes(0, 1), w1_halves, w2_big)


def _nsa_sample_kernel(pt_ref, *refs, past, nb, n_sel):
    n = PAGES_PER_STEP
    pages = refs[:n]
    (seln_ref, winn_ref, q_ref, misc_ref, cmp_ref, win_ref, expand_ref, bias_c_ref, bias_s_ref, bias_w_ref,
     o_ref, kv_sc, win_sc) = refs[n:]
    _stage_pages(pages, kv_sc)

    @pl.when(pl.program_id(1) == pl.num_programs(1) - 1)
    def _():
        T = q_ref.shape[0]
        L = kv_sc.shape[0]
        n_g = B_KV_HEADS
        rep = B_HEADS // n_g
        wb = win_ref.shape[1]
        _stage_new_rows(seln_ref, kv_sc, past)
        win_sc[0:wb, :] = win_ref[0]
        _stage_new_rows(winn_ref, win_sc, wb)
        misc = misc_ref[...]
        gate = lambda h, k: misc[:, MISC_GATE_OFF + 3 * h + k:MISC_GATE_OFF + 3 * h + k + 1]
        q = (q_ref[...] * HEAD_DIM ** -0.5).astype(MXU)
        cmp = cmp_ref[0].astype(MXU)
        nbp = cmp.shape[0]
        t_col = lax.broadcasted_iota(jnp.int32, (T, 1), 0)
        q_pos = past + t_col
        blk = lax.broadcasted_iota(jnp.int32, (T, nbp), 1)
        cur = lax.shift_right_logical(q_pos, NSA_BLOCK.bit_length() - 1)
        visible = (blk + 1) * NSA_BLOCK - 1 <= q_pos
        visible_rows = jnp.concatenate([visible] * rep, axis=0)
        forced = (blk == 0) | (blk == cur) | (blk == cur - 1)
        pos = lax.broadcasted_iota(jnp.int32, (T, L), 1)
        causal = pos <= q_pos
        kv = kv_sc[...].astype(MXU)
        win = win_sc[...].astype(MXU)
        wpos = lax.broadcasted_iota(jnp.int32, (T, win.shape[0]), 1)
        dw = wb + t_col - wpos
        madd_w = jnp.where((dw >= 0) & (dw < WINDOW), 0.0, NEG)
        madd_w = jnp.concatenate([madd_w] * rep, axis=0)
        outs = []
        for g in range(n_g):
            heads = range(g * rep, (g + 1) * rep)
            ks = slice(g * HEAD_DIM, (g + 1) * HEAD_DIM)
            vs = slice((n_g + g) * HEAD_DIM, (n_g + g + 1) * HEAD_DIM)
            q_rows = _stack_heads(q, heads, HEAD_DIM)
            lc = lax.dot_general(q_rows, cmp[:, ks], _NT, preferred_element_type=F32) + bias_c_ref[g]
            lc = jnp.where(visible_rows, lc, -jnp.inf)
            m = jnp.max(lc, axis=1, keepdims=True)
            m = jnp.where(m > -jnp.inf, m, 0.0)
            e = jnp.exp(lc - m)
            s = jnp.sum(e, axis=1, keepdims=True)
            p = e / jnp.where(s > 0, s, 1.0)
            o_cmp = jnp.dot(p.astype(MXU), cmp[:, vs], preferred_element_type=F32)
            importance = p[0:T]
            for r in range(1, rep):
                importance = importance + p[r * T:(r + 1) * T]
            score = jnp.where(forced, NSA_FORCE, importance)
            score = jnp.where(blk <= cur, score, -1.0)
            rank = jnp.zeros((T, nbp), jnp.int32)
            for b in range(nb):
                col = score[:, b:b + 1]
                rank = rank + jnp.where((col > score) | ((col == score) & (blk > b)), 1, 0)
            picked = jnp.where(rank < n_sel, 1.0, 0.0).astype(MXU)
            on_keys = jnp.dot(picked, expand_ref[...], preferred_element_type=F32)
            madd_s = jnp.where((on_keys > 0.5) & causal, 0.0, NEG)
            madd_s = jnp.concatenate([madd_s] * rep, axis=0)
            o_slc = _rows_softmax_attend(q_rows, kv[:, ks], kv[:, vs], bias_s_ref[g], madd_s)
            o_win = _rows_softmax_attend(q_rows, win[:, ks], win[:, vs], bias_w_ref[g], madd_w)
            for r, h in enumerate(heads):
                rows = slice(r * T, (r + 1) * T)
                outs.append(gate(h, 0) * o_cmp[rows] + gate(h, 1) * o_slc[rows] + gate(h, 2) * o_win[rows])
        o_ref[...] = jnp.concatenate(outs, axis=1)


def _nsa_sample(raw, cmp, pool_sel, win_state, page_table, table_b, DB, T):
    n_pages = page_table.shape[1]
    past = n_pages * PAGE_SIZE
    L = past + KEY_PAD
    nb, nbp = _padded_blocks(past, T)
    wb = win_state.shape[1]
    lw = wb + KEY_PAD
    kv_w = 2 * B_KV_HEADS * HEAD_DIM
    n_g = B_KV_HEADS
    expand = jnp.asarray(np.arange(L)[None, :] // NSA_BLOCK == np.arange(nbp)[:, None], dtype=MXU)
    blk_end = (np.arange(nbp) + 1) * NSA_BLOCK - 1
    d_c = (past + np.arange(T))[:, None] - blk_end[None, :]
    bias_c = jnp.moveaxis(_rel_bias_at(table_b, d_c), -1, 0).reshape(n_g, -1, nbp)
    bias_s = _sample_bias(table_b, T, past, L, 0, n_g)
    bias_w = _sample_bias(table_b, T, past, lw, past - wb, n_g)
    fixed2 = lambda s, p, pt: (0, 0)
    fixed3 = lambda s, p, pt: (0, 0, 0)
    per_seq3 = lambda s, p, pt: (s, 0, 0)
    grid_spec = pltpu.PrefetchScalarGridSpec(
        num_scalar_prefetch=1,
        grid=(DB, n_pages // PAGES_PER_STEP),
        in_specs=_paged_specs(kv_w)
        + [_seq_rows(kv_w, T), _seq_rows(kv_w, T), _seq_rows(B_HEADS * HEAD_DIM, T), _seq_rows(LANES, T),
           pl.BlockSpec((1, nbp, kv_w), per_seq3), pl.BlockSpec((1, wb, kv_w), per_seq3),
           pl.BlockSpec(expand.shape, fixed2), pl.BlockSpec(bias_c.shape, fixed3),
           pl.BlockSpec(bias_s.shape, fixed3), pl.BlockSpec(bias_w.shape, fixed3)],
        out_specs=_seq_rows(B_HEADS * HEAD_DIM, T),
        scratch_shapes=[pltpu.VMEM((L, kv_w), F32), pltpu.VMEM((lw, kv_w), F32)],
    )
    kern = functools.partial(_nsa_sample_kernel, past=past, nb=nb, n_sel=min(NSA_TOPN, nb))
    return pl.pallas_call(
        kern, grid_spec=grid_spec,
        out_shape=jax.ShapeDtypeStruct((DB * T, B_HEADS * HEAD_DIM), F32),
        compiler_params=pltpu.CompilerParams(dimension_semantics=('arbitrary', 'arbitrary'),
                                             vmem_limit_bytes=VMEM_LIMIT),
        name='nsa_sample',
    )(page_table, *([pool_sel] * PAGES_PER_STEP), raw['b_sel_kv'], raw['b_win_kv'], raw['b_q'], raw['misc'],
      cmp, win_state, expand, bias_c, bias_s, bias_w)


def _mix_sample(pr, cache_a_kv, cache_a_idx, cache_b_cmp_kv, cache_b_sel_kv, state_b_win_kv, page_table,
                rel_table, cmp_pe, cmp_w1, cmp_w2):
    DB, T = pr['a_q'].shape[:2]
    raw = pr['raw']
    kv_w = 2 * A_KV_HEADS * HEAD_DIM
    table_a, table_b = rel_table[:, :A_HEADS], rel_table[:, A_HEADS:]
    pe_flat, w1_big, w2_big = _compress_weights(cmp_pe, cmp_w1, cmp_w2)
    win_state = state_b_win_kv.reshape(DB, -1, kv_w)
    o_a = _dsa_sample(raw, _feature_major(cache_a_idx, IDX_DIM), _feature_major(cache_a_kv, kv_w), page_table,
                      table_a, DB, T)
    cmp = _compress_sample(raw, _feature_major(cache_b_cmp_kv, kv_w), page_table, pe_flat, w1_big, w2_big, DB, T)
    o_b = _nsa_sample(raw, cmp, _feature_major(cache_b_sel_kv, kv_w), win_state, page_table, table_b, DB, T)
    win_all = jnp.concatenate([state_b_win_kv, pr['b_win_kv']], axis=1)
    new_state = (pr['a_kv'], pr['a_ik'], pr['b_cmp_kv'], pr['b_sel_kv'], win_all[:, T:])
    return o_a, o_b, new_state


PEER_SLOTS = PEER_HEADS * PEER_TOPK
ROUTE_TOKENS = 256
EXPERT_TOKENS = 256
GATHER_DEPTH = 8
D_SUB = D_MODEL // LANES
EXPERT_ROWS = 2 * D_SUB


def _take_topk(cur, pos_iota, k, payload=None):
    n = cur.shape[0]
    vals, picks = [], []
    for _ in range(k):
        m = jnp.max(cur, axis=0, keepdims=True)
        pos = jnp.min(jnp.where(cur == m, pos_iota, n), axis=0, keepdims=True)
        hit = pos_iota == pos
        vals.append(m)
        picks.append(pos if payload is None else jnp.sum(jnp.where(hit, payload, 0), axis=0, keepdims=True))
        cur = jnp.where(hit, -jnp.inf, cur)
    return vals, picks


def _peer_route_kernel(h_ref, wpq_ref, keys_ref, ids_ref, g_ref):
    tn = h_ref.shape[0]
    half = PEER_QDIM // 2
    q = jnp.dot(h_ref[...].astype(MXU), wpq_ref[...], preferred_element_type=F32).astype(MXU)
    key_iota = lax.broadcasted_iota(jnp.int32, (PEER_NKEYS, tn), 0)
    pairs = [(a, b) for a in range(PEER_TOPK) for b in range(PEER_TOPK // (a + 1))]
    n_pairs = -(-len(pairs) // 8) * 8
    pair_iota = lax.broadcasted_iota(jnp.int32, (n_pairs, tn), 0)
    pad_v = [jnp.full((n_pairs - len(pairs), tn), -jnp.inf, F32)]
    pad_i = [jnp.zeros((n_pairs - len(pairs), tn), jnp.int32)]
    id_rows, g_rows = [], []
    for hd in range(PEER_HEADS):
        vals, idxs = [], []
        for p in range(2):
            c0 = (hd * 2 + p) * half
            s = lax.dot_general(keys_ref[hd * 2 + p], q[:, c0:c0 + half], _NT, preferred_element_type=F32)
            v_rows, i_rows = _take_topk(s, key_iota, PEER_TOPK)
            vals.append(v_rows)
            idxs.append(i_rows)
        cand = jnp.concatenate([vals[0][a] + vals[1][b] for a, b in pairs] + pad_v, axis=0)
        expert = jnp.concatenate([idxs[0][a] * PEER_NKEYS + idxs[1][b] for a, b in pairs] + pad_i, axis=0)
        cv, ce = _take_topk(cand, pair_iota, PEER_TOPK, payload=expert)
        cv = jnp.concatenate(cv, axis=0)
        e = jnp.exp(cv - cv[0:1, :])
        g_rows.append(e / jnp.sum(e, axis=0, keepdims=True))
        id_rows.append(jnp.concatenate(ce, axis=0))
    ids_ref[...] = jnp.concatenate(id_rows, axis=0).astype(F32).T.astype(jnp.int32)
    g_ref[...] = jnp.concatenate(g_rows, axis=0).T


def _peer_route(h2d, w_pq, sub_keys):
    n = h2d.shape[0]
    tn = ROUTE_TOKENS
    wpq = w_pq.reshape(D_MODEL, PEER_HEADS * PEER_QDIM).astype(MXU)
    keys = sub_keys.reshape(PEER_HEADS * 2, PEER_NKEYS, PEER_QDIM // 2).astype(MXU)
    return pl.pallas_call(
        _peer_route_kernel,
        grid=(n // tn,),
        in_specs=[pl.BlockSpec((tn, D_MODEL), lambda i: (i, 0)), pl.BlockSpec(wpq.shape, lambda i: (0, 0)),
                  pl.BlockSpec(keys.shape, lambda i: (0, 0, 0))],
        out_specs=[pl.BlockSpec((tn, PEER_SLOTS), lambda i: (i, 0)), pl.BlockSpec((tn, PEER_SLOTS), lambda i: (i, 0))],
        out_shape=[jax.ShapeDtypeStruct((n, PEER_SLOTS), jnp.int32), jax.ShapeDtypeStruct((n, PEER_SLOTS), F32)],
        compiler_params=pltpu.CompilerParams(dimension_semantics=('arbitrary',), vmem_limit_bytes=VMEM_LIMIT),
        name='peer_route',
    )(h2d, wpq, keys)


def _peer_expert_kernel(ids_ref, next_ids_ref, g_ref, h_ref, lng_ref, lnb_ref, uv_hbm, y_ref, buf, sem, gt_sc,
                        out_sc):
    tb = h_ref.shape[0]
    ahead = GATHER_DEPTH - 1
    per_piece = PEER_SLOTS // EXPERT_ROWS

    step = pl.program_id(0)
    last_step = pl.num_programs(0) - 1

    def row_copy(ids, t, slot, k):
        src = pl.ds(pl.multiple_of(ids[t, k] * EXPERT_ROWS, EXPERT_ROWS), EXPERT_ROWS)
        return pltpu.make_async_copy(uv_hbm.at[src, :], buf.at[slot, :, k, :], sem.at[slot])

    def issue(ids, t, slot, k0=0, k1=PEER_SLOTS):
        for k in range(k0, k1):
            row_copy(ids, t, slot, k).start(priority=k % 2)

    def consume(t, nxt=None):
        slot = t % GATHER_DEPTH

        def issue_piece(i):
            if nxt is not None:
                issue(*nxt, i * per_piece, (i + 1) * per_piece)

        for k in range(PEER_SLOTS):
            row_copy(ids_ref, t, slot, k).wait()
        x = h_ref[t]
        part = jnp.zeros((PEER_SLOTS, LANES), F32)
        for s in range(D_SUB):
            part = part + buf[slot, s] * x[s:s + 1, :]
            issue_piece(s)
        act = jnp.sum(part, axis=1, keepdims=True)
        g_col = jnp.sum(jnp.where(lane == t, gt_sc[...], 0.0), axis=1, keepdims=True)
        coef = jnp.broadcast_to(g_col * jax.nn.gelu(act), (PEER_SLOTS, LANES))
        rows = []
        for s in range(D_SUB):
            rows.append(jnp.sum(buf[slot, D_SUB + s] * coef, axis=0, keepdims=True))
            issue_piece(D_SUB + s)
        out_sc[t] = jnp.concatenate(rows, axis=0)

    gt_sc[...] = g_ref[...].T
    lane = lax.broadcasted_iota(jnp.int32, (PEER_SLOTS, tb), 1)

    def prologue(t, carry):
        issue(ids_ref, t, t)
        return carry

    def steady(t, carry):
        consume(t, nxt=(ids_ref, t + ahead, (t + ahead) % GATHER_DEPTH))
        return carry

    def handover(t, carry):
        consume(t, nxt=(next_ids_ref, t + ahead - tb, (t + ahead) % GATHER_DEPTH))
        return carry

    def drain(t, carry):
        consume(t)
        return carry

    @pl.when(step == 0)
    def _():
        lax.fori_loop(0, ahead, prologue, 0)

    lax.fori_loop(0, tb - ahead, steady, 0)

    @pl.when(step < last_step)
    def _():
        lax.fori_loop(tb - ahead, tb, handover, 0)

    @pl.when(step == last_step)
    def _():
        lax.fori_loop(tb - ahead, tb, drain, 0)
    z = DN_ALPHA * h_ref[...] + out_sc[...]
    mean = lambda a: jnp.sum(jnp.sum(a, axis=2, keepdims=True), axis=1, keepdims=True) * (1.0 / D_MODEL)
    zc = z - mean(z)
    y_ref[...] = zc * lax.rsqrt(mean(zc * zc) + LN_EPS) * lng_ref[...] + lnb_ref[...]


def _peer_experts(h2d, ids, g, uv_rows, ln_g, ln_b):
    n = h2d.shape[0]
    tb = EXPERT_TOKENS
    steps = n // tb
    assert n % tb == 0 and tb % GATHER_DEPTH == 0
    row = lambda i: (i, 0)
    next_row = lambda i: (jnp.minimum(i + 1, steps - 1), 0)
    tok = lambda i: (i, 0, 0)
    y = pl.pallas_call(
        _peer_expert_kernel,
        grid=(steps,),
        in_specs=[pl.BlockSpec((tb, PEER_SLOTS), row, memory_space=pltpu.SMEM),
                  pl.BlockSpec((tb, PEER_SLOTS), next_row, memory_space=pltpu.SMEM),
                  pl.BlockSpec((tb, PEER_SLOTS), row),
                  pl.BlockSpec((tb, D_SUB, LANES), tok), pl.BlockSpec((D_SUB, LANES), lambda i: (0, 0)),
                  pl.BlockSpec((D_SUB, LANES), lambda i: (0, 0)), pl.BlockSpec(memory_space=pl.ANY)],
        out_specs=pl.BlockSpec((tb, D_SUB, LANES), tok),
        out_shape=jax.ShapeDtypeStruct((n, D_SUB, LANES), F32),
        scratch_shapes=[pltpu.VMEM((GATHER_DEPTH, EXPERT_ROWS, PEER_SLOTS, LANES), F32),
                        pltpu.SemaphoreType.DMA((GATHER_DEPTH,)), pltpu.VMEM((PEER_SLOTS, tb), F32),
                        pltpu.VMEM((tb, D_SUB, LANES), F32)],
        compiler_params=pltpu.CompilerParams(dimension_semantics=('arbitrary',), vmem_limit_bytes=VMEM_LIMIT),
        name='peer_experts',
    )(ids, ids, g, h2d.reshape(n, D_SUB, LANES), ln_g.reshape(D_SUB, LANES), ln_b.reshape(D_SUB, LANES), uv_rows)
    return y.reshape(n, D_MODEL)


def _group_step(x, mixer, ws, w_up_a, w_up_b, w_out, ln1_g, ln1_b, w_pq, sub_keys, uv, ln2_g, ln2_b, tm,
                feature_major_state=False):
    B, T, _ = x.shape
    x2d = x.reshape(B * T, D_MODEL)
    pr = _unpack_proj(_project(x2d, ws, tm, seq_len=T if feature_major_state else None), B, T)
    o_a, o_b, new_state = mixer(pr)
    h = _merge(x2d, o_a.reshape(B * T, -1), o_b.reshape(B * T, -1), pr['merge_gate'],
               w_up_a, w_up_b, w_out, ln1_g, ln1_b, tm)
    ids, g = _peer_route(h, w_pq, sub_keys)
    y = _peer_experts(h, ids, g, uv, ln2_g, ln2_b)
    return y.reshape(B, T, D_MODEL), new_state


def kernel(x_prompt, x_sample, cache_a_kv, cache_a_idx, cache_b_cmp_kv, cache_b_sel_kv, state_b_win_kv, page_table, w_in, rel_bias_table, cmp_pe, cmp_w1, cmp_w2, w_up_a, w_up_b, w_out, ln1_g, ln1_b, w_pq, peer_sub_keys, peer_u, peer_v, ln2_g, ln2_b):
    l = 0
    ws = _split_w_in(w_in[l])
    uv = jnp.concatenate([peer_u[l], peer_v[l]], axis=1).reshape(-1, LANES)
    tail = (w_up_a[l], w_up_b[l], w_out[l], ln1_g[l], ln1_b[l], w_pq[l], peer_sub_keys[l], uv, ln2_g[l], ln2_b[l])
    mix_p = functools.partial(_mix_prompt, rel_table=rel_bias_table, cmp_pe=cmp_pe[l], cmp_w1=cmp_w1[l],
                              cmp_w2=cmp_w2[l])
    y_p, st_p = _group_step(x_prompt, mix_p, ws, *tail, tm=256, feature_major_state=True)
    mix_s = functools.partial(_mix_sample, cache_a_kv=cache_a_kv[l], cache_a_idx=cache_a_idx[l],
                              cache_b_cmp_kv=cache_b_cmp_kv[l], cache_b_sel_kv=cache_b_sel_kv[l],
                              state_b_win_kv=state_b_win_kv[l], page_table=page_table, rel_table=rel_bias_table,
                              cmp_pe=cmp_pe[l], cmp_w1=cmp_w1[l], cmp_w2=cmp_w2[l])
    y_s, st_s = _group_step(x_sample, mix_s, ws, *tail, tm=256)
    return (y_p, y_s) + tuple(s[None] for s in st_p) + tuple(s[None] for s in st_s)
```

```python
import functools
import math

import jax
import jax.numpy as jnp
import numpy as np
from jax import lax
from jax.experimental import pallas as pl
from jax.experimental.pallas import tpu as pltpu

D_MODEL = 1024
DEPTH = 1
PAGE_SIZE = 128
HEAD_DIM = 64
A_HEADS = 8
A_KV_HEADS = 2
IDX_HEADS = 4
IDX_DIM = 64
A_TOPK = 256
B_HEADS = 8
B_KV_HEADS = 2
NSA_BLOCK = 64
NSA_TOPN = 16
NSA_FORCE = 8.0
CMP_HIDDEN = 64
WINDOW = 512
REL_BUCKETS = 32
REL_MAX_EXACT = 16
REL_MAX_DIST = 128
PEER_HEADS = 8
PEER_NKEYS = 128
PEER_QDIM = 256
PEER_TOPK = 16
Q_BLOCK = 32
TOKEN_BLOCK = 256
LN_EPS = 1e-5
DN_ALPHA = (2 * DEPTH) ** 0.25

PROJ_LAYOUT = (
    ('a_q', A_HEADS * HEAD_DIM),
    ('a_kv', 2 * A_KV_HEADS * HEAD_DIM),
    ('a_iq', IDX_HEADS * IDX_DIM),
    ('a_iw', IDX_HEADS),
    ('a_ik', IDX_DIM),
    ('b_q', B_HEADS * HEAD_DIM),
    ('b_cmp_kv', 2 * B_KV_HEADS * HEAD_DIM),
    ('b_sel_kv', 2 * B_KV_HEADS * HEAD_DIM),
    ('b_win_kv', 2 * B_KV_HEADS * HEAD_DIM),
    ('b_gate', B_HEADS * 3),
    ('merge_gate', 2 * D_MODEL),
)

LANES = 128
VMEM_LIMIT = 56 * 1024 * 1024
MISC_GATE_OFF = IDX_HEADS

F32 = jnp.float32
BF16 = jnp.bfloat16
MXU = jnp.bfloat16


_PROJ_OUT = ('a_q', 'a_kv', 'a_iq', 'a_ik', 'misc', 'b_q', 'b_cmp_kv', 'b_sel_kv', 'b_win_kv', 'merge_gate')


_STATE_OUT = ('a_kv', 'a_ik', 'b_cmp_kv', 'b_sel_kv', 'b_win_kv')


def _proj_kernel(x_ref, *refs, feature_major_state):
    n = len(_PROJ_OUT)
    w_refs, o_refs, t_refs = refs[:n], refs[n:2 * n], refs[2 * n:]
    t_refs = dict(zip(_STATE_OUT, t_refs)) if feature_major_state else {}
    x = x_ref[...].astype(BF16)
    for name, w_ref, o_ref in zip(_PROJ_OUT, w_refs, o_refs):
        r = jnp.dot(x, w_ref[...], preferred_element_type=F32)
        if name == 'misc':
            lane = lax.broadcasted_iota(jnp.int32, r.shape, 1)
            r = jnp.where(lane < MISC_GATE_OFF, r * IDX_HEADS ** -0.5, jax.nn.sigmoid(r))
        elif name == 'merge_gate':
            r = jax.nn.sigmoid(r)
        o_ref[...] = r
        if name in t_refs:
            t_refs[name][0] = r.T


def _split_w_in(w_in):
    parts = {}
    off = 0
    for name, width in PROJ_LAYOUT:
        parts[name] = w_in[:, off:off + width]
        off += width
    misc = jnp.concatenate([parts['a_iw'], parts['b_gate']], axis=1)
    parts['misc'] = jnp.pad(misc, ((0, 0), (0, LANES - misc.shape[1])))
    return [parts[name].astype(BF16) for name in _PROJ_OUT]


def _project(x2d, ws, tm, seq_len=None):
    n = x2d.shape[0]
    widths = [w.shape[1] for w in ws]
    out_specs = [pl.BlockSpec((tm, wd), lambda i: (i, 0)) for wd in widths]
    out_shape = [jax.ShapeDtypeStruct((n, wd), F32) for wd in widths]
    if seq_len is not None:
        tiles = seq_len // tm
        for name in _STATE_OUT:
            wd = widths[_PROJ_OUT.index(name)]
            out_specs.append(pl.BlockSpec((1, wd, tm), lambda i: (i // tiles, 0, i % tiles)))
            out_shape.append(jax.ShapeDtypeStruct((n // seq_len, wd, seq_len), F32))
    return pl.pallas_call(
        functools.partial(_proj_kernel, feature_major_state=seq_len is not None),
        grid=(n // tm,),
        in_specs=[pl.BlockSpec((tm, D_MODEL), lambda i: (i, 0))]
        + [pl.BlockSpec((D_MODEL, wd), lambda i: (0, 0)) for wd in widths],
        out_specs=out_specs,
        out_shape=out_shape,
        compiler_params=pltpu.CompilerParams(dimension_semantics=('arbitrary',), vmem_limit_bytes=VMEM_LIMIT),
        name='project',
    )(x2d, *ws)


def _unpack_proj(outs, B, T):
    d = dict(zip(_PROJ_OUT, outs))
    d['state_t'] = dict(zip(_STATE_OUT, outs[len(_PROJ_OUT):]))
    misc = d['misc']
    kv = (B, T, 2, A_KV_HEADS, HEAD_DIM)
    return {
        'a_q': d['a_q'].reshape(B, T, A_HEADS, HEAD_DIM),
        'a_kv': d['a_kv'].reshape(kv),
        'a_iq': d['a_iq'].reshape(B, T, IDX_HEADS, IDX_DIM),
        'a_iw': misc[:, :IDX_HEADS].reshape(B, T, IDX_HEADS),
        'a_ik': d['a_ik'].reshape(B, T, IDX_DIM),
        'b_q': d['b_q'].reshape(B, T, B_HEADS, HEAD_DIM),
        'b_cmp_kv': d['b_cmp_kv'].reshape(kv),
        'b_sel_kv': d['b_sel_kv'].reshape(kv),
        'b_win_kv': d['b_win_kv'].reshape(kv),
        'b_gate': misc[:, MISC_GATE_OFF:MISC_GATE_OFF + 3 * B_HEADS].reshape(B, T, B_HEADS, 3),
        'merge_gate': d['merge_gate'],
        'raw': d,
    }


def _layer_norm(x, g, b):
    mu = jnp.mean(x, axis=-1, keepdims=True)
    xc = x - mu
    var = jnp.mean(xc * xc, axis=-1, keepdims=True)
    return xc * lax.rsqrt(var + LN_EPS) * g + b


def _merge_kernel(x_ref, oa_ref, ob_ref, mg_ref, wa_ref, wb_ref, wo_ref, g_ref, b_ref, h_ref):
    ya = jnp.dot(oa_ref[...].astype(BF16), wa_ref[...], preferred_element_type=F32)
    yb = jnp.dot(ob_ref[...].astype(BF16), wb_ref[...], preferred_element_type=F32)
    mg = mg_ref[...]
    mix = mg[:, :D_MODEL] * ya + mg[:, D_MODEL:] * yb
    mixed = jnp.dot(mix.astype(BF16), wo_ref[...], preferred_element_type=F32)
    h_ref[...] = _layer_norm(DN_ALPHA * x_ref[...] + mixed, g_ref[...], b_ref[...])


def _merge(x2d, o_a, o_b, mg, w_up_a, w_up_b, w_out, ln_g, ln_b, tm):
    n = x2d.shape[0]
    ca, cb = o_a.shape[1], o_b.shape[1]
    row = lambda i: (i, 0)
    fixed = lambda i: (0, 0)
    return pl.pallas_call(
        _merge_kernel,
        grid=(n // tm,),
        in_specs=[pl.BlockSpec((tm, D_MODEL), row), pl.BlockSpec((tm, ca), row), pl.BlockSpec((tm, cb), row),
                  pl.BlockSpec((tm, 2 * D_MODEL), row), pl.BlockSpec((ca, D_MODEL), fixed),
                  pl.BlockSpec((cb, D_MODEL), fixed), pl.BlockSpec((D_MODEL, D_MODEL), fixed),
                  pl.BlockSpec((1, D_MODEL), fixed), pl.BlockSpec((1, D_MODEL), fixed)],
        out_specs=pl.BlockSpec((tm, D_MODEL), row),
        out_shape=jax.ShapeDtypeStruct((n, D_MODEL), F32),
        compiler_params=pltpu.CompilerParams(dimension_semantics=('arbitrary',), vmem_limit_bytes=VMEM_LIMIT),
        name='merge',
    )(x2d, o_a, o_b, mg, w_up_a.astype(BF16), w_up_b.astype(BF16), w_out.astype(BF16),
      ln_g.reshape(1, D_MODEL), ln_b.reshape(1, D_MODEL))


TQ = 256
NEG = -1e30
INT_MIN = -2 ** 31
COUNT_ROWS = 256
_NT = (((1,), (1,)), ((), ()))
_TN = (((0,), (0,)), ((), ()))


def _bucket_np(d):
    n = np.maximum(d, 0)
    nf = np.maximum(n, 1).astype(np.float64)
    large = REL_MAX_EXACT + (np.log(nf / REL_MAX_EXACT) / math.log(REL_MAX_DIST / REL_MAX_EXACT)
                             * (REL_BUCKETS - REL_MAX_EXACT)).astype(np.int64)
    return np.where(n < REL_MAX_EXACT, n, np.minimum(large, REL_BUCKETS - 1)).astype(np.int32)


def _rel_bias_at(table, d):
    bucket = jnp.asarray(_bucket_np(d))[..., None]
    out = jnp.broadcast_to(table[REL_BUCKETS - 1], bucket.shape[:-1] + table.shape[1:])
    for b in range(REL_BUCKETS - 1):
        out = jnp.where(bucket == b, table[b], out)
    return out


def _band_bias(table, n_groups):
    d = TQ + np.arange(TQ)[None, :] - np.arange(2 * TQ)[:, None]
    band = _rel_bias_at(table, d) - table[REL_BUCKETS - 1]
    band = jnp.moveaxis(band, -1, 1)
    return jnp.moveaxis(band.reshape(2 * TQ, n_groups, -1), 1, 0)


def _block_bias(table, S):
    nb = S // NSA_BLOCK
    d = np.arange(S)[None, :] - ((np.arange(nb) + 1) * NSA_BLOCK - 1)[:, None]
    return jnp.moveaxis(_rel_bias_at(table, d), -1, 0)


def _ordered_keys(x):
    x = jnp.where(x == 0.0, 0.0, x)
    b = lax.bitcast_convert_type(x, jnp.int32)
    return b ^ ((b >> 31) & jnp.int32(0x7FFFFFFF))


def _softmax_reset(m_sc, l_sc, acc_sc):
    m_sc[...] = jnp.full(m_sc.shape, NEG, F32)
    l_sc[...] = jnp.zeros(l_sc.shape, F32)
    acc_sc[...] = jnp.zeros(acc_sc.shape, F32)


def _group_queries(q, n_groups):
    rep = q.shape[1] // HEAD_DIM // n_groups
    return [_stack_heads(q, range(g * rep, (g + 1) * rep), HEAD_DIM) for g in range(n_groups)]


def _attend_chunk(kv_c, q_groups, madd_of_group, band_ref, band_row0, m_sc, l_sc, acc_sc):
    n_g = len(q_groups)
    rep = q_groups[0].shape[0] // TQ
    for g in range(n_g):
        k_c = kv_c[:, g * HEAD_DIM:(g + 1) * HEAD_DIM]
        v_c = kv_c[:, (n_g + g) * HEAD_DIM:(n_g + g + 1) * HEAD_DIM]
        add = jnp.concatenate([madd_of_group(g)] * rep, axis=1)
        if band_row0 is not None:
            add = add + band_ref[g, band_row0:band_row0 + TQ, :]
        logits = lax.dot_general(k_c, q_groups[g], _NT, preferred_element_type=F32) + add
        m_old = m_sc[g]
        m_new = jnp.maximum(m_old, jnp.max(logits, axis=0, keepdims=True))
        alpha = jnp.exp(m_old - m_new)
        p = jnp.exp(logits - m_new)
        l_sc[g] = alpha * l_sc[g] + jnp.sum(p, axis=0, keepdims=True)
        m_sc[g] = m_new
        p = p.astype(MXU)
        for r in range(rep):
            h = g * rep + r
            lanes = slice(r * TQ, (r + 1) * TQ)
            pv = lax.dot_general(v_c, p[:, lanes], _TN, preferred_element_type=F32)
            acc_sc[h] = alpha[:, lanes] * acc_sc[h] + pv


def _softmax_scratch(n_heads, n_groups):
    stats = pltpu.VMEM((n_groups, 1, n_heads // n_groups * TQ), F32)
    return [stats, stats, pltpu.VMEM((n_heads, HEAD_DIM, TQ), F32)]


def _attend_result(h, rep, l_sc, acc_sc):
    g, r = divmod(h, rep)
    return acc_sc[h] / l_sc[g][:, r * TQ:(r + 1) * TQ]


def _dsa_kernel(iq_ref, misc_ref, q_ref, ik_ref, kv_ref, band_ref, o_ref, key_sc, m_sc, l_sc, acc_sc, out_sc,
                *, k_sel, pos_bits):
    i = pl.program_id(1)
    t0 = i * TQ
    n_chunks = i + 1
    lane_t = t0 + lax.broadcasted_iota(jnp.int32, (1, TQ), 1)
    sub_iota = lax.broadcasted_iota(jnp.int32, (TQ, TQ), 0)
    misc_t = misc_ref[...].T
    iq = iq_ref[...].astype(MXU)
    iq_heads = [iq[:, h * IDX_DIM:(h + 1) * IDX_DIM] for h in range(IDX_HEADS)]

    def score_chunk(j, carry):
        r0 = pl.multiple_of(j * TQ, TQ)
        ik_c = ik_ref[pl.ds(r0, TQ), :].astype(MXU)
        sc = jnp.zeros((TQ, TQ), F32)
        for h in range(IDX_HEADS):
            rel = lax.dot_general(ik_c, iq_heads[h], _NT, preferred_element_type=F32)
            sc = sc + jnp.maximum(rel * IDX_DIM ** -0.5, 0.0) * misc_t[h:h + 1, :]
        sc = jnp.where(r0 + sub_iota <= lane_t, sc, -jnp.inf)
        key_sc[pl.ds(r0, TQ), :] = _ordered_keys(sc)
        return carry

    def score_pair(jj, carry):
        score_chunk(2 * jj, carry)
        return score_chunk(2 * jj + 1, carry)

    lax.fori_loop(0, n_chunks // 2, score_pair, 0)

    @pl.when(n_chunks % 2 == 1)
    def _():
        score_chunk(n_chunks - 1, 0)

    def count(pred):
        def body(jj, acc):
            r0 = pl.multiple_of(jj * COUNT_ROWS, COUNT_ROWS)
            blk = key_sc[pl.ds(r0, COUNT_ROWS), :]
            pos = r0 + lax.broadcasted_iota(jnp.int32, (COUNT_ROWS, TQ), 0)
            hit = jnp.where(pred(blk, pos), 1, 0)
            return acc + jnp.sum(hit.reshape(COUNT_ROWS // 8, 8, TQ), axis=0)

        acc = lax.fori_loop(0, n_chunks * (TQ // COUNT_ROWS), body, jnp.zeros((8, TQ), jnp.int32))
        return jnp.sum(acc, axis=0, keepdims=True)

    v = jnp.full((1, TQ), INT_MIN, jnp.int32)
    v = jnp.where(count(lambda blk, pos: blk >= 0) >= k_sel, 0, v)

    def value_bit(it, v):
        cand = v | jnp.left_shift(jnp.int32(1), 30 - it)
        return jnp.where(count(lambda blk, pos: blk >= cand) >= k_sel, cand, v)

    v = lax.fori_loop(0, 31, value_bit, v)
    need = k_sel - count(lambda blk, pos: blk > v)
    n_ge = count(lambda blk, pos: blk >= v)

    def tie_search():
        def pos_bit(it, jm):
            cand = jm | jnp.left_shift(jnp.int32(1), pos_bits - 1 - it)
            return jnp.where(count(lambda blk, pos: (blk == v) & (pos < cand)) < need, cand, jm)

        return lax.fori_loop(0, pos_bits, pos_bit, jnp.zeros((1, TQ), jnp.int32))

    j_max = lax.cond(jnp.max(n_ge) > k_sel, tie_search, lambda: jnp.full((1, TQ), 2 ** 30, jnp.int32))

    q = (q_ref[...] * HEAD_DIM ** -0.5).astype(MXU)
    q_groups = _group_queries(q, A_KV_HEADS)
    _softmax_reset(m_sc, l_sc, acc_sc)

    def chunk(j, band_row0, causal):
        r0 = pl.multiple_of(j * TQ, TQ)
        key = key_sc[pl.ds(r0, TQ), :]
        pos = r0 + sub_iota
        sel = (key > v) | ((key == v) & (pos <= j_max))
        if causal:
            sel = sel & (pos <= lane_t)
        madd = jnp.where(sel, 0.0, NEG)
        kv_c = kv_ref[pl.ds(r0, TQ), :].astype(MXU)
        _attend_chunk(kv_c, q_groups, lambda g: madd, band_ref, band_row0, m_sc, l_sc, acc_sc)

    def far_chunk(j, carry):
        chunk(j, None, False)
        return carry

    lax.fori_loop(0, jnp.maximum(i - 1, 0), far_chunk, 0)

    @pl.when(i >= 1)
    def _():
        chunk(i - 1, 0, False)

    chunk(i, TQ, True)
    for h in range(A_HEADS):
        out_sc[h * HEAD_DIM:(h + 1) * HEAD_DIM, :] = _attend_result(h, A_HEADS // A_KV_HEADS, l_sc, acc_sc)
    o_ref[...] = out_sc[...].T


def _dsa_prompt(raw, band_a, B, S):
    nq = S // TQ
    k_sel = min(A_TOPK, S // 4)
    assert S % TQ == 0 and TQ >= k_sel
    tile = lambda b, i: (b * nq + i, 0)
    seq = lambda b, i: (b, 0)
    kern = functools.partial(_dsa_kernel, k_sel=k_sel, pos_bits=(S - 1).bit_length())
    return pl.pallas_call(
        kern,
        grid=(B, nq),
        in_specs=[pl.BlockSpec((TQ, IDX_HEADS * IDX_DIM), tile), pl.BlockSpec((TQ, LANES), tile),
                  pl.BlockSpec((TQ, A_HEADS * HEAD_DIM), tile), pl.BlockSpec((S, IDX_DIM), seq),
                  pl.BlockSpec((S, 2 * A_KV_HEADS * HEAD_DIM), seq),
                  pl.BlockSpec(band_a.shape, lambda b, i: (0, 0, 0))],
        out_specs=pl.BlockSpec((TQ, A_HEADS * HEAD_DIM), tile),
        out_shape=jax.ShapeDtypeStruct((B * S, A_HEADS * HEAD_DIM), F32),
        scratch_shapes=[pltpu.VMEM((S, TQ), jnp.int32)] + _softmax_scratch(A_HEADS, A_KV_HEADS)
        + [pltpu.VMEM((A_HEADS * HEAD_DIM, TQ), F32)],
        compiler_params=pltpu.CompilerParams(dimension_semantics=('arbitrary', 'arbitrary'),
                                             vmem_limit_bytes=VMEM_LIMIT),
        name='dsa_prompt',
    )(raw['a_iq'], raw['misc'], raw['a_q'], raw['a_ik'], raw['a_kv'], band_a)


def _compress_kernel(x_ref, pe_ref, w1_ref, w2_ref, o_ref):
    x = (x_ref[...] + pe_ref[...]).astype(MXU)
    h = jax.nn.gelu(jnp.dot(x, w1_ref[...], preferred_element_type=F32))
    o_ref[...] = jnp.dot(h.astype(MXU), w2_ref[...], preferred_element_type=F32)


def _compress_weights(pe, w1, w2):
    eye_c = jnp.eye(2, dtype=F32)
    eye_g = jnp.eye(B_KV_HEADS, dtype=F32)
    w1_big = jnp.einsum('lcde,cC,gG->lcgdCGe', w1, eye_c, eye_g)
    w1_big = w1_big.reshape(NSA_BLOCK * 2 * B_KV_HEADS * HEAD_DIM, 2 * B_KV_HEADS * CMP_HIDDEN)
    w2_big = jnp.einsum('cef,cC,gG->cgeCGf', w2, eye_c, eye_g)
    w2_big = w2_big.reshape(2 * B_KV_HEADS * CMP_HIDDEN, 2 * B_KV_HEADS * HEAD_DIM)
    pe_flat = jnp.broadcast_to(pe[:, :, None, :], (NSA_BLOCK, 2, B_KV_HEADS, HEAD_DIM)).reshape(1, -1)
    return pe_flat, w1_big.astype(MXU), w2_big.astype(MXU)


def _compress(blocks2d, pe_flat, w1_big, w2_big, tm):
    n, width = blocks2d.shape
    fixed = lambda i: (0, 0)
    return pl.pallas_call(
        _compress_kernel,
        grid=(n // tm,),
        in_specs=[pl.BlockSpec((tm, width), lambda i: (i, 0)), pl.BlockSpec((1, width), fixed),
                  pl.BlockSpec(w1_big.shape, fixed), pl.BlockSpec(w2_big.shape, fixed)],
        out_specs=pl.BlockSpec((tm, w2_big.shape[1]), lambda i: (i, 0)),
        out_shape=jax.ShapeDtypeStruct((n, w2_big.shape[1]), F32),
        compiler_params=pltpu.CompilerParams(dimension_semantics=('arbitrary',), vmem_limit_bytes=VMEM_LIMIT),
        name='nsa_compress',
    )(blocks2d, pe_flat, w1_big, w2_big)


def _nsa_kernel(q_ref, misc_ref, cmp_ref, selkv_ref, winkv_ref, band_ref, biasc_ref, o_ref,
                sel_sc, m_sc, l_sc, acc_sc, out_sc, *, nb, n_sel):
    i = pl.program_id(1)
    t0 = i * TQ
    n_g = B_KV_HEADS
    rep = B_HEADS // n_g
    lane_t = t0 + lax.broadcasted_iota(jnp.int32, (1, TQ), 1)
    sub_iota = lax.broadcasted_iota(jnp.int32, (TQ, TQ), 0)
    lane_iota = lax.broadcasted_iota(jnp.int32, (TQ, TQ), 1)
    misc_t = misc_ref[...].T
    gate = lambda h, k: misc_t[MISC_GATE_OFF + 3 * h + k:MISC_GATE_OFF + 3 * h + k + 1, :]
    q = (q_ref[...] * HEAD_DIM ** -0.5).astype(MXU)
    q_heads = [q[:, h * HEAD_DIM:(h + 1) * HEAD_DIM] for h in range(B_HEADS)]
    q_groups = _group_queries(q, n_g)

    cmp = cmp_ref[...].astype(MXU)
    blk = lax.broadcasted_iota(jnp.int32, (nb, TQ), 0)
    visible = (blk + 1) * NSA_BLOCK - 1 <= lane_t
    cur = lax.shift_right_logical(lane_t, NSA_BLOCK.bit_length() - 1)
    forced = (blk == 0) | (blk == cur) | (blk == cur - 1)
    for g in range(n_g):
        k_c = cmp[:, g * HEAD_DIM:(g + 1) * HEAD_DIM]
        v_c = cmp[:, (n_g + g) * HEAD_DIM:(n_g + g + 1) * HEAD_DIM]
        importance = jnp.zeros((nb, TQ), F32)
        for r in range(rep):
            h = g * rep + r
            lc = lax.dot_general(k_c, q_heads[h], _NT, preferred_element_type=F32) + biasc_ref[h]
            lc = jnp.where(visible, lc, -jnp.inf)
            m = jnp.max(lc, axis=0, keepdims=True)
            m = jnp.where(m > -jnp.inf, m, 0.0)
            e = jnp.exp(lc - m)
            s = jnp.sum(e, axis=0, keepdims=True)
            p = e / jnp.where(s > 0, s, 1.0)
            importance = importance + p
            o_cmp = lax.dot_general(v_c, p.astype(MXU), _TN, preferred_element_type=F32)
            out_sc[h * HEAD_DIM:(h + 1) * HEAD_DIM, :] = gate(h, 0) * o_cmp
        score = jnp.where(forced, NSA_FORCE, importance)
        score = jnp.where(blk <= cur, score, -1.0)
        rank = jnp.zeros((nb, TQ), jnp.int32)
        for n in range(nb):
            row = score[n:n + 1, :]
            beats = (row > score) | ((row == score) & (blk > n))
            rank = rank + jnp.where(beats, 1, 0)
        sel_sc[g] = jnp.where(rank < n_sel, 0.0, NEG)

    def finish_branch(k):
        for h in range(B_HEADS):
            rows = slice(h * HEAD_DIM, (h + 1) * HEAD_DIM)
            out_sc[rows, :] = out_sc[rows, :] + gate(h, k) * _attend_result(h, rep, l_sc, acc_sc)

    _softmax_reset(m_sc, l_sc, acc_sc)
    blocks_per_chunk = TQ // NSA_BLOCK

    def sel_chunk(j, band_row0, causal):
        r0 = pl.multiple_of(j * TQ, TQ)
        kv_c = selkv_ref[pl.ds(r0, TQ), :].astype(MXU)

        def madd_of_group(g):
            rows = [sel_sc[g, pl.ds(j * blocks_per_chunk + b, 1), :] for b in range(blocks_per_chunk)]
            madd = jnp.concatenate([jnp.broadcast_to(row, (NSA_BLOCK, TQ)) for row in rows], axis=0)
            if causal:
                madd = jnp.where(sub_iota <= lane_iota, madd, NEG)
            return madd

        _attend_chunk(kv_c, q_groups, madd_of_group, band_ref, band_row0, m_sc, l_sc, acc_sc)

    def far_chunk(j, carry):
        sel_chunk(j, None, False)
        return carry

    lax.fori_loop(0, jnp.maximum(i - 1, 0), far_chunk, 0)

    @pl.when(i >= 1)
    def _():
        sel_chunk(i - 1, 0, False)

    sel_chunk(i, TQ, True)
    finish_branch(1)

    _softmax_reset(m_sc, l_sc, acc_sc)
    for back in range(WINDOW // TQ, -1, -1):
        dist = back * TQ + lane_iota - sub_iota
        madd = jnp.where((dist >= 0) & (dist < WINDOW), 0.0, NEG)
        band_row0 = {0: TQ, 1: 0}.get(back)

        def win_chunk(back=back, madd=madd, band_row0=band_row0):
            r0 = pl.multiple_of((i - back) * TQ, TQ)
            kv_c = winkv_ref[pl.ds(r0, TQ), :].astype(MXU)
            _attend_chunk(kv_c, q_groups, lambda g: madd, band_ref, band_row0, m_sc, l_sc, acc_sc)

        if back == 0:
            win_chunk()
        else:
            pl.when(i >= back)(win_chunk)
    finish_branch(2)
    o_ref[...] = out_sc[...].T


def _nsa_prompt(raw, cmp, band_b, bias_c, B, S):
    nq = S // TQ
    nb = S // NSA_BLOCK
    assert S % TQ == 0 and TQ % NSA_BLOCK == 0
    tile = lambda b, i: (b * nq + i, 0)
    seq = lambda b, i: (b, 0)
    kv_w = 2 * B_KV_HEADS * HEAD_DIM
    kern = functools.partial(_nsa_kernel, nb=nb, n_sel=min(NSA_TOPN, nb))
    return pl.pallas_call(
        kern,
        grid=(B, nq),
        in_specs=[pl.BlockSpec((TQ, B_HEADS * HEAD_DIM), tile), pl.BlockSpec((TQ, LANES), tile),
                  pl.BlockSpec((nb, kv_w), seq), pl.BlockSpec((S, kv_w), seq), pl.BlockSpec((S, kv_w), seq),
                  pl.BlockSpec(band_b.shape, lambda b, i: (0, 0, 0)),
                  pl.BlockSpec((B_HEADS, nb, TQ), lambda b, i: (0, 0, i))],
        out_specs=pl.BlockSpec((TQ, B_HEADS * HEAD_DIM), tile),
        out_shape=jax.ShapeDtypeStruct((B * S, B_HEADS * HEAD_DIM), F32),
        scratch_shapes=[pltpu.VMEM((B_KV_HEADS, nb, TQ), F32)] + _softmax_scratch(B_HEADS, B_KV_HEADS)
        + [pltpu.VMEM((B_HEADS * HEAD_DIM, TQ), F32)],
        compiler_params=pltpu.CompilerParams(dimension_semantics=('arbitrary', 'arbitrary'),
                                             vmem_limit_bytes=VMEM_LIMIT),
        name='nsa_prompt',
    )(raw['b_q'], raw['misc'], cmp, raw['b_sel_kv'], raw['b_win_kv'], band_b, bias_c)


def _mix_prompt(pr, rel_table, cmp_pe, cmp_w1, cmp_w2):
    B, S = pr['a_q'].shape[:2]
    raw = pr['raw']
    table_a, table_b = rel_table[:, :A_HEADS], rel_table[:, A_HEADS:]
    pe_flat, w1_big, w2_big = _compress_weights(cmp_pe, cmp_w1, cmp_w2)
    n_blocks = B * S // NSA_BLOCK
    cmp = _compress(raw['b_cmp_kv'].reshape(n_blocks, -1), pe_flat, w1_big, w2_big, tm=min(128, n_blocks))
    o_a = _dsa_prompt(raw, _band_bias(table_a, A_KV_HEADS), B, S)
    o_b = _nsa_prompt(raw, cmp, _band_bias(table_b, B_KV_HEADS), _block_bias(table_b, S), B, S)
    wb = min(WINDOW, S)
    kv_tail = (2, B_KV_HEADS, HEAD_DIM)

    def rows(name, tail, first=0):
        part = raw['state_t'][name][:, :, first:]
        return jnp.moveaxis(part.reshape((B,) + tail + (S - first,)), -1, 1)

    new_state = (rows('a_kv', kv_tail), rows('a_ik', (IDX_DIM,)), rows('b_cmp_kv', kv_tail),
                 rows('b_sel_kv', kv_tail), rows('b_win_kv', kv_tail, S - wb))
    return o_a, o_b, new_state


PAGES_PER_STEP = 32
KEY_PAD = 128


def _paged_specs(width):
    def spec(j):
        return pl.BlockSpec((1, width, PAGE_SIZE), lambda s, p, pt: (pt[s, p * PAGES_PER_STEP + j], 0, 0))
    return [spec(j) for j in range(PAGES_PER_STEP)]


def _feature_major(pool, width):
    return jnp.swapaxes(pool.reshape(pool.shape[0], PAGE_SIZE, width), 1, 2)


def _stage_pages(page_refs, dst_sc):
    p = pl.program_id(1)
    for j, ref in enumerate(page_refs):
        r0 = pl.multiple_of((p * PAGES_PER_STEP + j) * PAGE_SIZE, PAGE_SIZE)
        dst_sc[pl.ds(r0, PAGE_SIZE), :] = ref[0].T


def _stage_new_rows(new_ref, dst_sc, past):
    t = new_ref.shape[0]
    dst_sc[past:past + t, :] = new_ref[...]
    dst_sc[past + t:, :] = jnp.zeros((dst_sc.shape[0] - past - t, dst_sc.shape[1]), F32)


def _rows_softmax_attend(q_rows, k, v, bias, madd):
    logits = lax.dot_general(q_rows, k, _NT, preferred_element_type=F32) + bias + madd
    m = jnp.max(logits, axis=1, keepdims=True)
    e = jnp.exp(logits - m)
    p = e / jnp.sum(e, axis=1, keepdims=True)
    return jnp.dot(p.astype(MXU), v, preferred_element_type=F32)


def _stack_heads(x, heads, width):
    return jnp.concatenate([x[:, h * width:(h + 1) * width] for h in heads], axis=0)


def _dsa_sample_kernel(pt_ref, *refs, past, k_sel):
    n = PAGES_PER_STEP
    idx_pages, kv_pages = refs[:n], refs[n:2 * n]
    ikn_ref, kvn_ref, iq_ref, misc_ref, q_ref, bias_ref, o_ref, ik_sc, kv_sc = refs[2 * n:]
    _stage_pages(idx_pages, ik_sc)
    _stage_pages(kv_pages, kv_sc)

    @pl.when(pl.program_id(1) == pl.num_programs(1) - 1)
    def _():
        T = iq_ref.shape[0]
        L = ik_sc.shape[0]
        _stage_new_rows(ikn_ref, ik_sc, past)
        _stage_new_rows(kvn_ref, kv_sc, past)
        pos = lax.broadcasted_iota(jnp.int32, (T, L), 1)
        valid = pos <= past + lax.broadcasted_iota(jnp.int32, (T, L), 0)
        misc = misc_ref[...]
        iq = _stack_heads(iq_ref[...].astype(MXU), range(IDX_HEADS), IDX_DIM)
        rel = lax.dot_general(iq, ik_sc[...].astype(MXU), _NT, preferred_element_type=F32)
        sc = jnp.zeros((T, L), F32)
        for h in range(IDX_HEADS):
            sc = sc + jnp.maximum(rel[h * T:(h + 1) * T] * IDX_DIM ** -0.5, 0.0) * misc[:, h:h + 1]
        key = _ordered_keys(jnp.where(valid, sc, -jnp.inf))

        def count(hit):
            return jnp.sum(jnp.where(hit, 1, 0), axis=1, keepdims=True)

        v = jnp.full((T, 1), INT_MIN, jnp.int32)
        v = jnp.where(count(key >= 0) >= k_sel, 0, v)

        def value_bit(it, v):
            cand = v | jnp.left_shift(jnp.int32(1), 30 - it)
            return jnp.where(count(key >= cand) >= k_sel, cand, v)

        v = lax.fori_loop(0, 31, value_bit, v)
        need = k_sel - count(key > v)
        pos_bits = (L - 1).bit_length()

        def tie_search():
            def pos_bit(it, jm):
                cand = jm | jnp.left_shift(jnp.int32(1), pos_bits - 1 - it)
                return jnp.where(count((key == v) & (pos < cand)) < need, cand, jm)

            return lax.fori_loop(0, pos_bits, pos_bit, jnp.zeros((T, 1), jnp.int32))

        j_max = lax.cond(jnp.max(count(key >= v)) > k_sel, tie_search,
                         lambda: jnp.full((T, 1), 2 ** 30, jnp.int32))
        sel = ((key > v) | ((key == v) & (pos <= j_max))) & valid
        madd = jnp.where(sel, 0.0, NEG)
        rep = A_HEADS // A_KV_HEADS
        madd = jnp.concatenate([madd] * rep, axis=0)
        q = (q_ref[...] * HEAD_DIM ** -0.5).astype(MXU)
        kv = kv_sc[...].astype(MXU)
        for g in range(A_KV_HEADS):
            q_rows = _stack_heads(q, range(g * rep, (g + 1) * rep), HEAD_DIM)
            o = _rows_softmax_attend(q_rows, kv[:, g * HEAD_DIM:(g + 1) * HEAD_DIM],
                                     kv[:, (A_KV_HEADS + g) * HEAD_DIM:(A_KV_HEADS + g + 1) * HEAD_DIM],
                                     bias_ref[g], madd)
            for r in range(rep):
                h = g * rep + r
                o_ref[:, h * HEAD_DIM:(h + 1) * HEAD_DIM] = o[r * T:(r + 1) * T, :]


def _sample_bias(table, T, past, n_keys, key_pos0, n_groups):
    d = (past + np.arange(T))[:, None] - (key_pos0 + np.arange(n_keys))[None, :]
    b = _rel_bias_at(table, d)
    b = jnp.moveaxis(b, -1, 0)
    return b.reshape(n_groups, -1, n_keys)


def _seq_rows(width, T):
    return pl.BlockSpec((T, width), lambda s, p, pt: (s, 0))


def _dsa_sample(raw, pool_idx, pool_kv, page_table, table_a, DB, T):
    n_pages = page_table.shape[1]
    past = n_pages * PAGE_SIZE
    L = past + KEY_PAD
    assert n_pages % PAGES_PER_STEP == 0 and T <= KEY_PAD
    kv_w = 2 * A_KV_HEADS * HEAD_DIM
    bias = _sample_bias(table_a, T, past, L, 0, A_KV_HEADS)
    fixed3 = lambda s, p, pt: (0, 0, 0)
    grid_spec = pltpu.PrefetchScalarGridSpec(
        num_scalar_prefetch=1,
        grid=(DB, n_pages // PAGES_PER_STEP),
        in_specs=_paged_specs(IDX_DIM) + _paged_specs(kv_w)
        + [_seq_rows(IDX_DIM, T), _seq_rows(kv_w, T), _seq_rows(IDX_HEADS * IDX_DIM, T), _seq_rows(LANES, T),
           _seq_rows(A_HEADS * HEAD_DIM, T), pl.BlockSpec(bias.shape, fixed3)],
        out_specs=_seq_rows(A_HEADS * HEAD_DIM, T),
        scratch_shapes=[pltpu.VMEM((L, IDX_DIM), F32), pltpu.VMEM((L, kv_w), F32)],
    )
    kern = functools.partial(_dsa_sample_kernel, past=past, k_sel=min(A_TOPK, (past + T) // 4))
    return pl.pallas_call(
        kern, grid_spec=grid_spec,
        out_shape=jax.ShapeDtypeStruct((DB * T, A_HEADS * HEAD_DIM), F32),
        compiler_params=pltpu.CompilerParams(dimension_semantics=('arbitrary', 'arbitrary'),
                                             vmem_limit_bytes=VMEM_LIMIT),
        name='dsa_sample',
    )(page_table, *([pool_idx] * PAGES_PER_STEP), *([pool_kv] * PAGES_PER_STEP),
      raw['a_ik'], raw['a_kv'], raw['a_iq'], raw['misc'], raw['a_q'], bias)


def _compress_sample_kernel(pt_ref, *refs, past):
    n = PAGES_PER_STEP
    pages = refs[:n]
    new_ref, pe_ref, w1_ref, w2_ref, o_ref, xk_sc, xv_sc = refs[n:]
    halves = (xk_sc, xv_sc)
    p = pl.program_id(1)
    for j, ref in enumerate(pages):
        r0 = pl.multiple_of((p * PAGES_PER_STEP + j) * PAGE_SIZE, PAGE_SIZE)
        for c, x_sc in enumerate(halves):
            x_sc[pl.ds(r0, PAGE_SIZE), :] = ref[0, c * LANES:(c + 1) * LANES, :].T

    @pl.when(p == pl.num_programs(1) - 1)
    def _():
        nbp = o_ref.shape[1]
        t = new_ref.shape[0]
        for c, x_sc in enumerate(halves):
            x_sc[past:past + t, :] = new_ref[:, c * LANES:(c + 1) * LANES]
            x_sc[past + t:, :] = jnp.zeros((x_sc.shape[0] - past - t, LANES), F32)

        def body(l, accs):
            out = []
            for c, (x_sc, acc) in enumerate(zip(halves, accs)):
                xl = x_sc[pl.ds(l, nbp, stride=NSA_BLOCK), :] + pe_ref[c, pl.ds(l, 1), :]
                out.append(acc + jnp.dot(xl.astype(MXU), w1_ref[l, c], preferred_element_type=F32))
            return tuple(out)

        zero = jnp.zeros((nbp, LANES), F32)
        h = jnp.concatenate(lax.fori_loop(0, NSA_BLOCK, body, (zero, zero), unroll=16), axis=1)
        o_ref[0] = jnp.dot(jax.nn.gelu(h).astype(MXU), w2_ref[...], preferred_element_type=F32)


def _padded_blocks(past, T):
    nb = -(-(past + T) // NSA_BLOCK)
    return nb, -(-nb // 8) * 8


def _compress_sample(raw, pool_cmp, page_table, pe_flat, w1_big, w2_big, DB, T):
    n_pages = page_table.shape[1]
    past = n_pages * PAGE_SIZE
    _, nbp = _padded_blocks(past, T)
    kv_w = 2 * B_KV_HEADS * HEAD_DIM
    assert kv_w == 2 * LANES and w1_big.shape[1] == 2 * LANES
    w1_l = w1_big.reshape(NSA_BLOCK, 2, LANES, 2, LANES)
    w1_halves = jnp.stack([w1_l[:, 0, :, 0, :], w1_l[:, 1, :, 1, :]], axis=1)
    fixed2 = lambda s, p, pt: (0, 0)
    grid_spec = pltpu.PrefetchScalarGridSpec(
        num_scalar_prefetch=1,
        grid=(DB, n_pages // PAGES_PER_STEP),
        in_specs=_paged_specs(kv_w) + [_seq_rows(kv_w, T),
                                       pl.BlockSpec((2, NSA_BLOCK, LANES), lambda s, p, pt: (0, 0, 0)),
                                       pl.BlockSpec(w1_halves.shape, lambda s, p, pt: (0, 0, 0, 0)),
                                       pl.BlockSpec(w2_big.shape, fixed2)],
        out_specs=pl.BlockSpec((1, nbp, kv_w), lambda s, p, pt: (s, 0, 0)),
        scratch_shapes=[pltpu.VMEM((nbp * NSA_BLOCK, LANES), F32), pltpu.VMEM((nbp * NSA_BLOCK, LANES), F32)],
    )
    return pl.pallas_call(
        functools.partial(_compress_sample_kernel, past=past), grid_spec=grid_spec,
        out_shape=jax.ShapeDtypeStruct((DB, nbp, kv_w), F32),
        compiler_params=pltpu.CompilerParams(dimension_semantics=('arbitrary', 'arbitrary'),
                                             vmem_limit_bytes=VMEM_LIMIT),
        name='nsa_compress_sample',
    )(page_table, *([pool_cmp] * PAGES_PER_STEP), raw['b_cmp_kv'],
      pe_flat.reshape(NSA_BLOCK, 2, LANES).swapaxes(0, 1), w1_halves, w2_big)


def _nsa_sample_kernel(pt_ref, *refs, past, nb, n_sel):
    n = PAGES_PER_STEP
    pages = refs[:n]
    (seln_ref, winn_ref, q_ref, misc_ref, cmp_ref, win_ref, expand_ref, bias_c_ref, bias_s_ref, bias_w_ref,
     o_ref, kv_sc, win_sc) = refs[n:]
    _stage_pages(pages, kv_sc)

    @pl.when(pl.program_id(1) == pl.num_programs(1) - 1)
    def _():
        T = q_ref.shape[0]
        L = kv_sc.shape[0]
        n_g = B_KV_HEADS
        rep = B_HEADS // n_g
        wb = win_ref.shape[1]
        _stage_new_rows(seln_ref, kv_sc, past)
        win_sc[0:wb, :] = win_ref[0]
        _stage_new_rows(winn_ref, win_sc, wb)
        misc = misc_ref[...]
        gate = lambda h, k: misc[:, MISC_GATE_OFF + 3 * h + k:MISC_GATE_OFF + 3 * h + k + 1]
        q = (q_ref[...] * HEAD_DIM ** -0.5).astype(MXU)
        cmp = cmp_ref[0].astype(MXU)
        nbp = cmp.shape[0]
        t_col = lax.broadcasted_iota(jnp.int32, (T, 1), 0)
        q_pos = past + t_col
        blk = lax.broadcasted_iota(jnp.int32, (T, nbp), 1)
        cur = lax.shift_right_logical(q_pos, NSA_BLOCK.bit_length() - 1)
        visible = (blk + 1) * NSA_BLOCK - 1 <= q_pos
        visible_rows = jnp.concatenate([visible] * rep, axis=0)
        forced = (blk == 0) | (blk == cur) | (blk == cur - 1)
        pos = lax.broadcasted_iota(jnp.int32, (T, L), 1)
        causal = pos <= q_pos
        kv = kv_sc[...].astype(MXU)
        win = win_sc[...].astype(MXU)
        wpos = lax.broadcasted_iota(jnp.int32, (T, win.shape[0]), 1)
        dw = wb + t_col - wpos
        madd_w = jnp.where((dw >= 0) & (dw < WINDOW), 0.0, NEG)
        madd_w = jnp.concatenate([madd_w] * rep, axis=0)
        outs = []
        for g in range(n_g):
            heads = range(g * rep, (g + 1) * rep)
            ks = slice(g * HEAD_DIM, (g + 1) * HEAD_DIM)
            vs = slice((n_g + g) * HEAD_DIM, (n_g + g + 1) * HEAD_DIM)
            q_rows = _stack_heads(q, heads, HEAD_DIM)
            lc = lax.dot_general(q_rows, cmp[:, ks], _NT, preferred_element_type=F32) + bias_c_ref[g]
            lc = jnp.where(visible_rows, lc, -jnp.inf)
            m = jnp.max(lc, axis=1, keepdims=True)
            m = jnp.where(m > -jnp.inf, m, 0.0)
            e = jnp.exp(lc - m)
            s = jnp.sum(e, axis=1, keepdims=True)
            p = e / jnp.where(s > 0, s, 1.0)
            o_cmp = jnp.dot(p.astype(MXU), cmp[:, vs], preferred_element_type=F32)
            importance = p[0:T]
            for r in range(1, rep):
                importance = importance + p[r * T:(r + 1) * T]
            score = jnp.where(forced, NSA_FORCE, importance)
            score = jnp.where(blk <= cur, score, -1.0)
            rank = jnp.zeros((T, nbp), jnp.int32)
            for b in range(nb):
                col = score[:, b:b + 1]
                rank = rank + jnp.where((col > score) | ((col == score) & (blk > b)), 1, 0)
            picked = jnp.where(rank < n_sel, 1.0, 0.0).astype(MXU)
            on_keys = jnp.dot(picked, expand_ref[...], preferred_element_type=F32)
            madd_s = jnp.where((on_keys > 0.5) & causal, 0.0, NEG)
            madd_s = jnp.concatenate([madd_s] * rep, axis=0)
            o_slc = _rows_softmax_attend(q_rows, kv[:, ks], kv[:, vs], bias_s_ref[g], madd_s)
            o_win = _rows_softmax_attend(q_rows, win[:, ks], win[:, vs], bias_w_ref[g], madd_w)
            for r, h in enumerate(heads):
                rows = slice(r * T, (r + 1) * T)
                outs.append(gate(h, 0) * o_cmp[rows] + gate(h, 1) * o_slc[rows] + gate(h, 2) * o_win[rows])
        o_ref[...] = jnp.concatenate(outs, axis=1)


def _nsa_sample(raw, cmp, pool_sel, win_state, page_table, table_b, DB, T):
    n_pages = page_table.shape[1]
    past = n_pages * PAGE_SIZE
    L = past + KEY_PAD
    nb, nbp = _padded_blocks(past, T)
    wb = win_state.shape[1]
    lw = wb + KEY_PAD
    kv_w = 2 * B_KV_HEADS * HEAD_DIM
    n_g = B_KV_HEADS
    expand = jnp.asarray(np.arange(L)[None, :] // NSA_BLOCK == np.arange(nbp)[:, None], dtype=MXU)
    blk_end = (np.arange(nbp) + 1) * NSA_BLOCK - 1
    d_c = (past + np.arange(T))[:, None] - blk_end[None, :]
    bias_c = jnp.moveaxis(_rel_bias_at(table_b, d_c), -1, 0).reshape(n_g, -1, nbp)
    bias_s = _sample_bias(table_b, T, past, L, 0, n_g)
    bias_w = _sample_bias(table_b, T, past, lw, past - wb, n_g)
    fixed2 = lambda s, p, pt: (0, 0)
    fixed3 = lambda s, p, pt: (0, 0, 0)
    per_seq3 = lambda s, p, pt: (s, 0, 0)
    grid_spec = pltpu.PrefetchScalarGridSpec(
        num_scalar_prefetch=1,
        grid=(DB, n_pages // PAGES_PER_STEP),
        in_specs=_paged_specs(kv_w)
        + [_seq_rows(kv_w, T), _seq_rows(kv_w, T), _seq_rows(B_HEADS * HEAD_DIM, T), _seq_rows(LANES, T),
           pl.BlockSpec((1, nbp, kv_w), per_seq3), pl.BlockSpec((1, wb, kv_w), per_seq3),
           pl.BlockSpec(expand.shape, fixed2), pl.BlockSpec(bias_c.shape, fixed3),
           pl.BlockSpec(bias_s.shape, fixed3), pl.BlockSpec(bias_w.shape, fixed3)],
        out_specs=_seq_rows(B_HEADS * HEAD_DIM, T),
        scratch_shapes=[pltpu.VMEM((L, kv_w), F32), pltpu.VMEM((lw, kv_w), F32)],
    )
    kern = functools.partial(_nsa_sample_kernel, past=past, nb=nb, n_sel=min(NSA_TOPN, nb))
    return pl.pallas_call(
        kern, grid_spec=grid_spec,
        out_shape=jax.ShapeDtypeStruct((DB * T, B_HEADS * HEAD_DIM), F32),
        compiler_params=pltpu.CompilerParams(dimension_semantics=('arbitrary', 'arbitrary'),
                                             vmem_limit_bytes=VMEM_LIMIT),
        name='nsa_sample',
    )(page_table, *([pool_sel] * PAGES_PER_STEP), raw['b_sel_kv'], raw['b_win_kv'], raw['b_q'], raw['misc'],
      cmp, win_state, expand, bias_c, bias_s, bias_w)


def _mix_sample(pr, cache_a_kv, cache_a_idx, cache_b_cmp_kv, cache_b_sel_kv, state_b_win_kv, page_table,
                rel_table, cmp_pe, cmp_w1, cmp_w2):
    DB, T = pr['a_q'].shape[:2]
    raw = pr['raw']
    kv_w = 2 * A_KV_HEADS * HEAD_DIM
    table_a, table_b = rel_table[:, :A_HEADS], rel_table[:, A_HEADS:]
    pe_flat, w1_big, w2_big = _compress_weights(cmp_pe, cmp_w1, cmp_w2)
    win_state = state_b_win_kv.reshape(DB, -1, kv_w)
    o_a = _dsa_sample(raw, _feature_major(cache_a_idx, IDX_DIM), _feature_major(cache_a_kv, kv_w), page_table,
                      table_a, DB, T)
    cmp = _compress_sample(raw, _feature_major(cache_b_cmp_kv, kv_w), page_table, pe_flat, w1_big, w2_big, DB, T)
    o_b = _nsa_sample(raw, cmp, _feature_major(cache_b_sel_kv, kv_w), win_state, page_table, table_b, DB, T)
    win_all = jnp.concatenate([state_b_win_kv, pr['b_win_kv']], axis=1)
    new_state = (pr['a_kv'], pr['a_ik'], pr['b_cmp_kv'], pr['b_sel_kv'], win_all[:, T:])
    return o_a, o_b, new_state


PEER_SLOTS = PEER_HEADS * PEER_TOPK
ROUTE_TOKENS = 256
EXPERT_TOKENS = 256
GATHER_DEPTH = 8
D_SUB = D_MODEL // LANES
EXPERT_ROWS = 2 * D_SUB


def _take_topk(cur, pos_iota, k, payload=None):
    n = cur.shape[0]
    vals, picks = [], []
    for _ in range(k):
        m = jnp.max(cur, axis=0, keepdims=True)
        pos = jnp.min(jnp.where(cur == m, pos_iota, n), axis=0, keepdims=True)
        hit = pos_iota == pos
        vals.append(m)
        picks.append(pos if payload is None else jnp.sum(jnp.where(hit, payload, 0), axis=0, keepdims=True))
        cur = jnp.where(hit, -jnp.inf, cur)
    return vals, picks


def _peer_route_kernel(h_ref, wpq_ref, keys_ref, ids_ref, g_ref):
    tn = h_ref.shape[0]
    half = PEER_QDIM // 2
    q = jnp.dot(h_ref[...].astype(MXU), wpq_ref[...], preferred_element_type=F32).astype(MXU)
    key_iota = lax.broadcasted_iota(jnp.int32, (PEER_NKEYS, tn), 0)
    pairs = [(a, b) for a in range(PEER_TOPK) for b in range(PEER_TOPK // (a + 1))]
    n_pairs = -(-len(pairs) // 8) * 8
    pair_iota = lax.broadcasted_iota(jnp.int32, (n_pairs, tn), 0)
    pad_v = [jnp.full((n_pairs - len(pairs), tn), -jnp.inf, F32)]
    pad_i = [jnp.zeros((n_pairs - len(pairs), tn), jnp.int32)]
    id_rows, g_rows = [], []
    for hd in range(PEER_HEADS):
        vals, idxs = [], []
        for p in range(2):
            c0 = (hd * 2 + p) * half
            s = lax.dot_general(keys_ref[hd * 2 + p], q[:, c0:c0 + half], _NT, preferred_element_type=F32)
            v_rows, i_rows = _take_topk(s, key_iota, PEER_TOPK)
            vals.append(v_rows)
            idxs.append(i_rows)
        cand = jnp.concatenate([vals[0][a] + vals[1][b] for a, b in pairs] + pad_v, axis=0)
        expert = jnp.concatenate([idxs[0][a] * PEER_NKEYS + idxs[1][b] for a, b in pairs] + pad_i, axis=0)
        cv, ce = _take_topk(cand, pair_iota, PEER_TOPK, payload=expert)
        cv = jnp.concatenate(cv, axis=0)
        e = jnp.exp(cv - cv[0:1, :])
        g_rows.append(e / jnp.sum(e, axis=0, keepdims=True))
        id_rows.append(jnp.concatenate(ce, axis=0))
    ids_ref[...] = jnp.concatenate(id_rows, axis=0).astype(F32).T.astype(jnp.int32)
    g_ref[...] = jnp.concatenate(g_rows, axis=0).T


def _peer_route(h2d, w_pq, sub_keys):
    n = h2d.shape[0]
    tn = ROUTE_TOKENS
    wpq = w_pq.reshape(D_MODEL, PEER_HEADS * PEER_QDIM).astype(MXU)
    keys = sub_keys.reshape(PEER_HEADS * 2, PEER_NKEYS, PEER_QDIM // 2).astype(MXU)
    return pl.pallas_call(
        _peer_route_kernel,
        grid=(n // tn,),
        in_specs=[pl.BlockSpec((tn, D_MODEL), lambda i: (i, 0)), pl.BlockSpec(wpq.shape, lambda i: (0, 0)),
                  pl.BlockSpec(keys.shape, lambda i: (0, 0, 0))],
        out_specs=[pl.BlockSpec((tn, PEER_SLOTS), lambda i: (i, 0)), pl.BlockSpec((tn, PEER_SLOTS), lambda i: (i, 0))],
        out_shape=[jax.ShapeDtypeStruct((n, PEER_SLOTS), jnp.int32), jax.ShapeDtypeStruct((n, PEER_SLOTS), F32)],
        compiler_params=pltpu.CompilerParams(dimension_semantics=('arbitrary',), vmem_limit_bytes=VMEM_LIMIT),
        name='peer_route',
    )(h2d, wpq, keys)


def _peer_expert_kernel(ids_ref, next_ids_ref, g_ref, h_ref, lng_ref, lnb_ref, uv_hbm, y_ref, buf, sem, gt_sc,
                        out_sc):
    tb = h_ref.shape[0]
    ahead = GATHER_DEPTH - 1
    per_piece = PEER_SLOTS // EXPERT_ROWS

    step = pl.program_id(0)
    last_step = pl.num_programs(0) - 1

    def row_copy(ids, t, slot, k):
        src = pl.ds(pl.multiple_of(ids[t, k] * EXPERT_ROWS, EXPERT_ROWS), EXPERT_ROWS)
        return pltpu.make_async_copy(uv_hbm.at[src, :], buf.at[slot, :, k, :], sem.at[slot])

    def issue(ids, t, slot, k0=0, k1=PEER_SLOTS):
        for k in range(k0, k1):
            row_copy(ids, t, slot, k).start(priority=k % 2)

    def consume(t, nxt=None):
        slot = t % GATHER_DEPTH

        def issue_piece(i):
            if nxt is not None:
                issue(*nxt, i * per_piece, (i + 1) * per_piece)

        for k in range(PEER_SLOTS):
            row_copy(ids_ref, t, slot, k).wait()
        x = h_ref[t]
        part = jnp.zeros((PEER_SLOTS, LANES), F32)
        for s in range(D_SUB):
            part = part + buf[slot, s] * x[s:s + 1, :]
            issue_piece(s)
        act = jnp.sum(part, axis=1, keepdims=True)
        g_col = jnp.sum(jnp.where(lane == t, gt_sc[...], 0.0), axis=1, keepdims=True)
        coef = jnp.broadcast_to(g_col * jax.nn.gelu(act), (PEER_SLOTS, LANES))
        rows = []
        for s in range(D_SUB):
            rows.append(jnp.sum(buf[slot, D_SUB + s] * coef, axis=0, keepdims=True))
            issue_piece(D_SUB + s)
        out_sc[t] = jnp.concatenate(rows, axis=0)

    gt_sc[...] = g_ref[...].T
    lane = lax.broadcasted_iota(jnp.int32, (PEER_SLOTS, tb), 1)

    def prologue(t, carry):
        issue(ids_ref, t, t)
        return carry

    def steady(t, carry):
        consume(t, nxt=(ids_ref, t + ahead, (t + ahead) % GATHER_DEPTH))
        return carry

    def handover(t, carry):
        consume(t, nxt=(next_ids_ref, t + ahead - tb, (t + ahead) % GATHER_DEPTH))
        return carry

    def drain(t, carry):
        consume(t)
        return carry

    @pl.when(step == 0)
    def _():
        lax.fori_loop(0, ahead, prologue, 0)

    lax.fori_loop(0, tb - ahead, steady, 0)

    @pl.when(step < last_step)
    def _():
        lax.fori_loop(tb - ahead, tb, handover, 0)

    @pl.when(step == last_step)
    def _():
        lax.fori_loop(tb - ahead, tb, drain, 0)
    z = DN_ALPHA * h_ref[...] + out_sc[...]
    mean = lambda a: jnp.sum(jnp.sum(a, axis=2, keepdims=True), axis=1, keepdims=True) * (1.0 / D_MODEL)
    zc = z - mean(z)
    y_ref[...] = zc * lax.rsqrt(mean(zc * zc) + LN_EPS) * lng_ref[...] + lnb_ref[...]


def _peer_experts(h2d, ids, g, uv_rows, ln_g, ln_b):
    n = h2d.shape[0]
    tb = EXPERT_TOKENS
    steps = n // tb
    assert n % tb == 0 and tb % GATHER_DEPTH == 0
    row = lambda i: (i, 0)
    next_row = lambda i: (jnp.minimum(i + 1, steps - 1), 0)
    tok = lambda i: (i, 0, 0)
    y = pl.pallas_call(
        _peer_expert_kernel,
        grid=(steps,),
        in_specs=[pl.BlockSpec((tb, PEER_SLOTS), row, memory_space=pltpu.SMEM),
                  pl.BlockSpec((tb, PEER_SLOTS), next_row, memory_space=pltpu.SMEM),
                  pl.BlockSpec((tb, PEER_SLOTS), row),
                  pl.BlockSpec((tb, D_SUB, LANES), tok), pl.BlockSpec((D_SUB, LANES), lambda i: (0, 0)),
                  pl.BlockSpec((D_SUB, LANES), lambda i: (0, 0)), pl.BlockSpec(memory_space=pl.ANY)],
        out_specs=pl.BlockSpec((tb, D_SUB, LANES), tok),
        out_shape=jax.ShapeDtypeStruct((n, D_SUB, LANES), F32),
        scratch_shapes=[pltpu.VMEM((GATHER_DEPTH, EXPERT_ROWS, PEER_SLOTS, LANES), F32),
                        pltpu.SemaphoreType.DMA((GATHER_DEPTH,)), pltpu.VMEM((PEER_SLOTS, tb), F32),
                        pltpu.VMEM((tb, D_SUB, LANES), F32)],
        compiler_params=pltpu.CompilerParams(dimension_semantics=('arbitrary',), vmem_limit_bytes=VMEM_LIMIT),
        name='peer_experts',
    )(ids, ids, g, h2d.reshape(n, D_SUB, LANES), ln_g.reshape(D_SUB, LANES), ln_b.reshape(D_SUB, LANES), uv_rows)
    return y.reshape(n, D_MODEL)


def _group_step(x, mixer, ws, w_up_a, w_up_b, w_out, ln1_g, ln1_b, w_pq, sub_keys, uv, ln2_g, ln2_b, tm,
                feature_major_state=False):
    B, T, _ = x.shape
    x2d = x.reshape(B * T, D_MODEL)
    pr = _unpack_proj(_project(x2d, ws, tm, seq_len=T if feature_major_state else None), B, T)
    o_a, o_b, new_state = mixer(pr)
    h = _merge(x2d, o_a.reshape(B * T, -1), o_b.reshape(B * T, -1), pr['merge_gate'],
               w_up_a, w_up_b, w_out, ln1_g, ln1_b, tm)
    ids, g = _peer_route(h, w_pq, sub_keys)
    y = _peer_experts(h, ids, g, uv, ln2_g, ln2_b)
    return y.reshape(B, T, D_MODEL), new_state


def kernel(x_prompt, x_sample, cache_a_kv, cache_a_idx, cache_b_cmp_kv, cache_b_sel_kv, state_b_win_kv, page_table, w_in, rel_bias_table, cmp_pe, cmp_w1, cmp_w2, w_up_a, w_up_b, w_out, ln1_g, ln1_b, w_pq, peer_sub_keys, peer_u, peer_v, ln2_g, ln2_b):
    l = 0
    ws = _split_w_in(w_in[l])
    uv = jnp.concatenate([peer_u[l], peer_v[l]], axis=1).reshape(-1, LANES)
    tail = (w_up_a[l], w_up_b[l], w_out[l], ln1_g[l], ln1_b[l], w_pq[l], peer_sub_keys[l], uv, ln2_g[l], ln2_b[l])
    mix_p = functools.partial(_mix_prompt, rel_table=rel_bias_table, cmp_pe=cmp_pe[l], cmp_w1=cmp_w1[l],
                              cmp_w2=cmp_w2[l])
    y_p, st_p = _group_step(x_prompt, mix_p, ws, *tail, tm=256, feature_major_state=True)
    mix_s = functools.partial(_mix_sample, cache_a_kv=cache_a_kv[l], cache_a_idx=cache_a_idx[l],
                              cache_b_cmp_kv=cache_b_cmp_kv[l], cache_b_sel_kv=cache_b_sel_kv[l],
                              state_b_win_kv=state_b_win_kv[l], page_table=page_table, rel_table=rel_bias_table,
                              cmp_pe=cmp_pe[l], cmp_w1=cmp_w1[l], cmp_w2=cmp_w2[l])
    y_s, st_s = _group_step(x_sample, mix_s, ws, *tail, tm=256)
    return (y_p, y_s) + tuple(s[None] for s in st_p) + tuple(s[None] for s in st_s)
```
